```python
import math
import jax, jax.numpy as jnp
from jax import lax
import numpy as np

D_MODEL = 1024
BATCH = 8
SEQ = 2048
DEPTH = 1

N_DIFF_HEADS = 8
DIFF_HEAD_DIM = 64
DIFF_V_DIM = 2 * DIFF_HEAD_DIM
DIFF_WIDTH = N_DIFF_HEADS * DIFF_V_DIM
POOL_WINDOWS = (2, 4, 8, 16)
N_POOL_GROUPS = len(POOL_WINDOWS)
POOL_GROUP_DIM = 128
POOL_WIDTH = N_POOL_GROUPS * POOL_GROUP_DIM
N_BRANCHES = 2
IN_COLS = 3 * DIFF_WIDTH + POOL_WIDTH + N_BRANCHES * D_MODEL
ROPE_THETA = 500000.0
ROT_DIM = DIFF_HEAD_DIM // 4
Q_BLOCK = 128
N_EXPERTS = 32
TOP_K = 4
D_FF_EXPERT = D_MODEL
SWIGLU_ALPHA = 1.702
SWIGLU_LIMIT = 7.0
NORM_EPS = 1e-6
SUBLN_EPS = 1e-5
N_MOD = 6
NEG_BIG = -1e30

kernel_name = "hybrid_diffattn_pool_moe_block"


def _lambda_init(layer):
    return 0.8 - 0.6 * math.exp(-0.3 * layer)


def _rmsnorm(x, g, eps=NORM_EPS):
    xf = x.astype(jnp.float32)
    y = xf * lax.rsqrt(jnp.mean(xf * xf, axis=-1, keepdims=True) + eps)
    return (y * g.astype(jnp.float32)).astype(x.dtype)


def _partial_rope(t, positions):
    half = ROT_DIM // 2
    inv_freq = ROPE_THETA ** (-jnp.arange(half, dtype=jnp.float32) * 2.0 / ROT_DIM)
    ang = positions.astype(jnp.float32)[:, :, None] * inv_freq
    cos = jnp.cos(ang)[:, :, None, None, :]
    sin = jnp.sin(ang)[:, :, None, None, :]
    tf = t.astype(jnp.float32)
    t1, t2, rest = tf[..., :half], tf[..., half:ROT_DIM], tf[..., ROT_DIM:]
    out = jnp.concatenate([t1 * cos - t2 * sin, t2 * cos + t1 * sin, rest], axis=-1)
    return out.astype(t.dtype)


def _diff_attention(q, k, v, lam):
    B, S, H = q.shape[:3]
    nb = S // Q_BLOCK
    scale = DIFF_HEAD_DIM ** -0.5
    qt = q.transpose(3, 0, 2, 1, 4)
    kt = k.transpose(3, 0, 2, 1, 4)
    vt = v.transpose(0, 2, 1, 3)
    qb = qt.reshape(2, B, H, nb, Q_BLOCK, DIFF_HEAD_DIM).transpose(3, 0, 1, 2, 4, 5)
    kpos = jnp.arange(S)

    def block(args):
        qi, i = args
        qpos = i * Q_BLOCK + jnp.arange(Q_BLOCK)
        mask = kpos[None, :] <= qpos[:, None]
        s = jnp.einsum('cbhqd,cbhkd->cbhqk', qi, kt, preferred_element_type=jnp.float32) * scale
        s = jnp.where(mask, s, NEG_BIG)
        p = jax.nn.softmax(s, axis=-1)
        a = p[0] - lam * p[1]
        return jnp.einsum('bhqk,bhkd->bhqd', a.astype(vt.dtype), vt)

    o = lax.map(block, (qb, jnp.arange(nb)))
    return o.transpose(1, 0, 3, 2, 4).reshape(B, S, H, DIFF_V_DIM)


def _multiscale_pool(u, w_pool, pool_scale):
    B, S, _ = u.shape
    uf = u.astype(jnp.float32)
    cs = jnp.cumsum(uf, axis=1)
    t = jnp.arange(S)
    outs = []
    for g, w in enumerate(POOL_WINDOWS):
        sl = slice(g * POOL_GROUP_DIM, (g + 1) * POOL_GROUP_DIM)
        csg = cs[:, :, sl]
        lower = jnp.pad(csg[:, :S - w], ((0, 0), (w, 0), (0, 0)))
        count = jnp.minimum(t + 1, w).astype(jnp.float32)[None, :, None]
        outs.append((csg - lower) / count - uf[:, :, sl])
    d = jnp.stack(outs, axis=2)
    y = jnp.einsum('bsgc,gcd->bsgd', d, w_pool.astype(jnp.float32)).reshape(B, S, POOL_WIDTH)
    return (y * pool_scale.astype(jnp.float32)).astype(u.dtype)


def _clamped_swiglu(z):
    gate, up = z[..., ::2], z[..., 1::2]
    gate = jnp.minimum(gate, SWIGLU_LIMIT)
    up = jnp.clip(up, -SWIGLU_LIMIT, SWIGLU_LIMIT)
    return gate * jax.nn.sigmoid(SWIGLU_ALPHA * gate) * (up + 1.0)


def _moe(h, w_router, b_router, w1, b1, w2, b2):
    B, S, D = h.shape
    ht = h.reshape(B * S, D)
    logits = (ht @ w_router + b_router).astype(jnp.float32)
    top_v, top_i = lax.top_k(logits, TOP_K)
    top_w = jax.nn.softmax(top_v, axis=-1)
    comb = jnp.einsum('tk,tke->te', top_w, jax.nn.one_hot(top_i, N_EXPERTS, dtype=jnp.float32)).astype(h.dtype)
    y = jnp.zeros_like(ht)
    for e in range(N_EXPERTS):
        a = _clamped_swiglu(ht @ w1[e] + b1[e])
        y = y + comb[:, e:e + 1] * (a @ w2[e] + b2[e])
    return y.reshape(B, S, D)


def setup_inputs(seed: int = 0) -> dict:
    key = jax.random.key(seed)
    ks = jax.random.split(key, 26)
    L, D, E, F = DEPTH, D_MODEL, N_EXPERTS, D_FF_EXPERT

    def nrm(k, shape, s):
        return jax.random.normal(k, shape, jnp.float32) * s

    x = nrm(ks[0], (BATCH, SEQ, D), 1.0)
    c = nrm(ks[1], (BATCH, D), 1.0)
    positions = (jax.random.randint(ks[2], (BATCH, 1), 0, 1024) + jnp.arange(SEQ)[None, :]).astype(jnp.int32)
    return {
        "x": x,
        "c": c,
        "positions": positions,
        "w_ada": nrm(ks[3], (L, D, N_MOD * D), D ** -0.5),
        "b_ada": nrm(ks[4], (L, N_MOD * D), 0.01),
        "g_pre_mix": 1.0 + nrm(ks[5], (L, D), 0.05),
        "w_in": nrm(ks[6], (L, D, IN_COLS), D ** -0.5),
        "lambda_q1": nrm(ks[7], (L, DIFF_HEAD_DIM), 0.1),
        "lambda_k1": nrm(ks[8], (L, DIFF_HEAD_DIM), 0.1),
        "lambda_q2": nrm(ks[9], (L, DIFF_HEAD_DIM), 0.1),
        "lambda_k2": nrm(ks[10], (L, DIFF_HEAD_DIM), 0.1),
        "g_subln": 1.0 + nrm(ks[11], (L, DIFF_V_DIM), 0.05),
        "w_pool": nrm(ks[12], (L, N_POOL_GROUPS, POOL_GROUP_DIM, POOL_GROUP_DIM), POOL_GROUP_DIM ** -0.5),
        "pool_scale": 1.0 + nrm(ks[13], (L, POOL_WIDTH), 0.1),
        "w_proj_a": nrm(ks[14], (L, DIFF_WIDTH, D), DIFF_WIDTH ** -0.5),
        "w_proj_b": nrm(ks[15], (L, POOL_WIDTH, D), POOL_WIDTH ** -0.5),
        "w_out": nrm(ks[16], (L, D, D), D ** -0.5),
        "g_post_mix": 1.0 + nrm(ks[17], (L, D), 0.05),
        "g_pre_ffn": 1.0 + nrm(ks[18], (L, D), 0.05),
        "w_router": nrm(ks[19], (L, D, E), D ** -0.5),
        "b_router": nrm(ks[20], (L, E), 0.01),
        "w_exp1": nrm(ks[21], (L, E, D, 2 * F), D ** -0.5),
        "b_exp1": nrm(ks[22], (L, E, 2 * F), 0.01),
        "w_exp2": nrm(ks[23], (L, E, F, D), F ** -0.5),
        "b_exp2": nrm(ks[24], (L, E, D), 0.01),
        "g_post_ffn": 1.0 + nrm(ks[25], (L, D), 0.05),
    }


def reference(x, c, positions, w_ada, b_ada, g_pre_mix, w_in, lambda_q1, lambda_k1, lambda_q2, lambda_k2,
              g_subln, w_pool, pool_scale, w_proj_a, w_proj_b, w_out, g_post_mix, g_pre_ffn,
              w_router, b_router, w_exp1, b_exp1, w_exp2, b_exp2, g_post_ffn):
    B, S, D = x.shape
    H = N_DIFF_HEADS
    o_k = DIFF_WIDTH
    o_v = 2 * DIFF_WIDTH
    o_u = 3 * DIFF_WIDTH
    o_g = 3 * DIFF_WIDTH + POOL_WIDTH
    for l in range(DEPTH):
        mod = jax.nn.silu(c) @ w_ada[l] + b_ada[l]
        sh1, sc1, gt1, sh2, sc2, gt2 = [m[:, None, :] for m in jnp.split(mod, N_MOD, axis=-1)]

        h = _rmsnorm(x, g_pre_mix[l]) * (1.0 + sc1) + sh1
        z = h @ w_in[l]
        q = z[..., :o_k].reshape(B, S, H, 2, DIFF_HEAD_DIM)
        k = z[..., o_k:o_v].reshape(B, S, H, 2, DIFF_HEAD_DIM)
        v = z[..., o_v:o_u].reshape(B, S, H, DIFF_V_DIM)
        u = z[..., o_u:o_g]
        gates = jax.nn.sigmoid(z[..., o_g:].astype(jnp.float32)).astype(x.dtype)
        gate_a, gate_b = gates[..., :D], gates[..., D:]

        q = _partial_rope(q, positions)
        k = _partial_rope(k, positions)
        lam_init = _lambda_init(l)
        lam = (jnp.exp(jnp.sum(lambda_q1[l].astype(jnp.float32) * lambda_k1[l].astype(jnp.float32)))
               - jnp.exp(jnp.sum(lambda_q2[l].astype(jnp.float32) * lambda_k2[l].astype(jnp.float32)))
               + lam_init)
        o = _diff_attention(q, k, v, lam)
        o = _rmsnorm(o, g_subln[l], SUBLN_EPS) * (1.0 - lam_init)
        y_a = o.reshape(B, S, DIFF_WIDTH) @ w_proj_a[l]

        y_b = _multiscale_pool(u, w_pool[l], pool_scale[l]) @ w_proj_b[l]

        mixed = (gate_a * y_a + gate_b * y_b) @ w_out[l]
        x = x + gt1 * _rmsnorm(mixed, g_post_mix[l])

        h = _rmsnorm(x, g_pre_ffn[l]) * (1.0 + sc2) + sh2
        f = _moe(h, w_router[l], b_router[l], w_exp1[l], b_exp1[l], w_exp2[l], b_exp2[l])
        x = x + gt2 * _rmsnorm(f, g_post_ffn[l])
    return x
```

```python
import functools
import math

import numpy as np
import jax
import jax.numpy as jnp
from jax import lax
from jax.experimental import pallas as pl
from jax.experimental.pallas import tpu as pltpu

D_MODEL = 1024
N_HEADS = 8
HEAD_DIM = 64
V_DIM = 2 * HEAD_DIM
DIFF_WIDTH = N_HEADS * V_DIM
POOL_WINDOWS = (2, 4, 8, 16)
POOL_GROUP_DIM = 128
POOL_WIDTH = len(POOL_WINDOWS) * POOL_GROUP_DIM
IN_COLS = 3 * DIFF_WIDTH + POOL_WIDTH + 2 * D_MODEL
ROPE_THETA = 500000.0
ROT_DIM = HEAD_DIM // 4
ROT_HALF = ROT_DIM // 2
N_EXPERTS = 32
TOP_K = 4
D_FF = D_MODEL
SWIGLU_ALPHA = 1.702
SWIGLU_LIMIT = 7.0
NORM_EPS = 1e-6
SUBLN_EPS = 1e-5
N_MOD = 6
NEG_BIG = -1e30
LAMBDA_INIT = 0.8 - 0.6 * math.exp(-0.3 * 0)

LANES = 128
MXU_DIM = 256
VMEM_LIMIT = 56 * 1024 * 1024

COL_TILE = 512
COL_K = DIFF_WIDTH // COL_TILE
COL_V = 2 * DIFF_WIDTH // COL_TILE
COL_G = (3 * DIFF_WIDTH + POOL_WIDTH) // COL_TILE
ATT_BLOCK = 256
MOE_TILE = 256

_HI = lax.Precision.HIGHEST


def _params(sem, vmem=VMEM_LIMIT):
    return pltpu.CompilerParams(dimension_semantics=sem, vmem_limit_bytes=vmem)


def _rms(x, eps):
    return x * lax.rsqrt(jnp.mean(x * x, axis=-1, keepdims=True) + eps)


def _ada_kernel(c_ref, w_ref, b_ref, o_ref):
    c = c_ref[...]
    s = c * jax.nn.sigmoid(c)
    o_ref[...] = jnp.dot(s, w_ref[...], precision=_HI, preferred_element_type=jnp.float32) + b_ref[...]


def _ada(c, w_ada, b_ada):
    B, D = c.shape
    N = w_ada.shape[1]
    tn = 1536
    return pl.pallas_call(
        _ada_kernel,
        grid=(N // tn,),
        in_specs=[pl.BlockSpec((B, D), lambda j: (0, 0)),
                  pl.BlockSpec((D, tn), lambda j: (0, j)),
                  pl.BlockSpec((1, tn), lambda j: (0, j))],
        out_specs=pl.BlockSpec((B, tn), lambda j: (0, j)),
        out_shape=jax.ShapeDtypeStruct((B, N), jnp.float32),
        compiler_params=_params(("parallel",)),
        name="ada",
    )(c, w_ada, b_ada.reshape(1, N))


def _rope_tab_kernel(pos_ref, invf_ref, phase_ref, c_ref, sa_ref, sb_ref):
    pos = pos_ref[...].astype(jnp.float32)
    cs = jnp.cos(pos * invf_ref[...] - phase_ref[...])
    lane = lax.broadcasted_iota(jnp.int32, cs.shape, 1) % HEAD_DIM
    first = lane < ROT_HALF
    second = (lane >= ROT_HALF) & (lane < ROT_DIM)
    cos_t = jnp.where(first, cs, pltpu.roll(cs, ROT_HALF, 1))
    sin_t = jnp.where(first, pltpu.roll(cs, LANES - ROT_HALF, 1), cs)
    rot = first | second
    c_ref[...] = jnp.where(rot, cos_t, 1.0)
    sa_ref[...] = jnp.where(first, -sin_t, 0.0)
    sb_ref[...] = jnp.where(second, sin_t, 0.0)


def _rope_tables(positions):
    T = positions.size
    tm = min(T, 2048)
    lane = np.arange(LANES) % HEAD_DIM
    inv = ROPE_THETA ** (-(np.arange(ROT_HALF, dtype=np.float64) * 2.0 / ROT_DIM))
    invf = np.where(lane < ROT_DIM, inv[lane % ROT_HALF], 0.0).astype(np.float32)[None, :]
    phase = np.where((lane >= ROT_HALF) & (lane < ROT_DIM), np.pi / 2, 0.0).astype(np.float32)[None, :]
    row = pl.BlockSpec((1, LANES), lambda i: (0, 0))
    tab = pl.BlockSpec((tm, LANES), lambda i: (i, 0))
    sds = jax.ShapeDtypeStruct((T, LANES), jnp.float32)
    return pl.pallas_call(
        _rope_tab_kernel,
        grid=(T // tm,),
        in_specs=[pl.BlockSpec((tm, 1), lambda i: (i, 0)), row, row],
        out_specs=[tab, tab, tab],
        out_shape=[sds, sds, sds],
        compiler_params=_params(("parallel",)),
        name="rope_tables",
    )(positions.reshape(T, 1), jnp.asarray(invf), jnp.asarray(phase))


def _in_proj_kernel(x_ref, g_ref, sc_ref, sh_ref, w_ref, c_ref, sa_ref, sb_ref, z_ref, h_ref):
    j = pl.program_id(1)

    @pl.when(j == 0)
    def _():
        h = _rms(x_ref[...], NORM_EPS) * g_ref[...] * (1.0 + sc_ref[0]) + sh_ref[0]
        h_ref[...] = h.astype(h_ref.dtype)

    z = jnp.dot(h_ref[...], w_ref[...], preferred_element_type=jnp.float32)

    @pl.when(j < COL_V)
    def _():
        reps = COL_TILE // LANES
        c = jnp.concatenate([c_ref[...]] * reps, axis=1)
        sa = jnp.concatenate([sa_ref[...]] * reps, axis=1)
        sb = jnp.concatenate([sb_ref[...]] * reps, axis=1)
        r = z * c + pltpu.roll(z, COL_TILE - ROT_HALF, 1) * sa + pltpu.roll(z, ROT_HALF, 1) * sb
        r = r * jnp.where(j < COL_K, HEAD_DIM ** -0.5, 1.0)
        z_ref[...] = r.astype(z_ref.dtype)

    @pl.when((j >= COL_V) & (j < COL_G))
    def _():
        z_ref[...] = z.astype(z_ref.dtype)

    @pl.when(j >= COL_G)
    def _():
        z_ref[...] = jax.nn.sigmoid(z).astype(z_ref.dtype)


def _in_proj(x2, g_pre, mod3, w_in_bf, tabs, seq):
    T, D = x2.shape
    tm = min(seq, 1024)
    per_b = seq // tm
    mod_spec = lambda col: pl.BlockSpec((1, 1, D), lambda i, j: (i // per_b, 0, col))
    tab = pl.BlockSpec((tm, LANES), lambda i, j: (i, 0))
    return pl.pallas_call(
        _in_proj_kernel,
        grid=(T // tm, IN_COLS // COL_TILE),
        in_specs=[pl.BlockSpec((tm, D), lambda i, j: (i, 0)),
                  pl.BlockSpec((1, D), lambda i, j: (0, 0)),
                  mod_spec(1), mod_spec(0),
                  pl.BlockSpec((D, COL_TILE), lambda i, j: (0, j)),
                  tab, tab, tab],
        out_specs=pl.BlockSpec((tm, COL_TILE), lambda i, j: (i, j)),
        out_shape=jax.ShapeDtypeStruct((T, IN_COLS), jnp.bfloat16),
        scratch_shapes=[pltpu.VMEM((tm, D), jnp.bfloat16)],
        compiler_params=_params(("parallel", "arbitrary")),
        name="in_proj",
    )(x2, g_pre, mod3, mod3, w_in_bf, *tabs)


def _attn_kernel(lq1_ref, lk1_ref, lq2_ref, lk2_ref, gs_ref, q_ref, k_ref, v_ref, o_ref):
    i = pl.program_id(2)
    blk = ATT_BLOCK
    lam = (jnp.exp(jnp.sum(lq1_ref[...] * lk1_ref[...], axis=-1, keepdims=True))
           - jnp.exp(jnp.sum(lq2_ref[...] * lk2_ref[...], axis=-1, keepdims=True))
           + LAMBDA_INIT)
    q = q_ref[...]
    lane = lax.broadcasted_iota(jnp.int32, q.shape, 1)
    zero = jnp.zeros_like(q)
    qs = (jnp.where(lane < HEAD_DIM, q, zero), jnp.where(lane >= HEAD_DIM, q, zero))
    nt = (((1,), (1,)), ((), ()))

    def step(k, v, carry, mask):
        out = []
        for comp in range(2):
            m, l, acc = carry[comp]
            s = lax.dot_general(qs[comp], k, nt, preferred_element_type=jnp.float32)
            if mask is not None:
                s = jnp.where(mask, s, NEG_BIG)
            m_new = jnp.maximum(m, jnp.max(s, axis=-1, keepdims=True))
            alpha = jnp.exp(m - m_new)
            p = jnp.exp(s - m_new)
            l = alpha * l + jnp.sum(p, axis=-1, keepdims=True)
            acc = alpha * acc + jnp.dot(p.astype(v.dtype), v, preferred_element_type=jnp.float32)
            out.append((m_new, l, acc))
        return tuple(out)

    def body(c, carry):
        off = pl.multiple_of(c * blk, blk)
        return step(k_ref[pl.ds(off, blk), :], v_ref[pl.ds(off, blk), :], carry, None)

    init = tuple((jnp.full((blk, 1), NEG_BIG, jnp.float32), jnp.zeros((blk, 1), jnp.float32),
                  jnp.zeros((blk, V_DIM), jnp.float32)) for _ in range(2))
    carry = lax.fori_loop(0, i, body, init)
    off = pl.multiple_of(i * blk, blk)
    row = lax.broadcasted_iota(jnp.int32, (blk, blk), 0)
    col = lax.broadcasted_iota(jnp.int32, (blk, blk), 1)
    (_, l1, a1), (_, l2, a2) = step(k_ref[pl.ds(off, blk), :], v_ref[pl.ds(off, blk), :], carry, col <= row)
    o = a1 / l1 - lam * (a2 / l2)
    o = _rms(o, SUBLN_EPS) * gs_ref[...] * (1.0 - LAMBDA_INIT)
    o_ref[...] = o.astype(o_ref.dtype)


def _diff_attention(z, lams, g_subln, batch, seq):
    T = z.shape[0]
    blk = ATT_BLOCK
    nq = seq // blk
    vec = pl.BlockSpec((1, HEAD_DIM), lambda b, h, i: (0, 0))
    kcol = DIFF_WIDTH // V_DIM
    return pl.pallas_call(
        _attn_kernel,
        grid=(batch, N_HEADS, nq),
        in_specs=[vec, vec, vec, vec,
                  pl.BlockSpec((1, V_DIM), lambda b, h, i: (0, 0)),
                  pl.BlockSpec((blk, V_DIM), lambda b, h, i: (b * nq + i, h)),
                  pl.BlockSpec((seq, V_DIM), lambda b, h, i: (b, kcol + h)),
                  pl.BlockSpec((seq, V_DIM), lambda b, h, i: (b, 2 * kcol + h))],
        out_specs=pl.BlockSpec((blk, V_DIM), lambda b, h, i: (b * nq + i, h)),
        out_shape=jax.ShapeDtypeStruct((T, DIFF_WIDTH), jnp.bfloat16),
        compiler_params=_params(("parallel", "parallel", "arbitrary")),
        name="diff_attn",
    )(*lams, g_subln, z, z, z)


def _pool_kernel(u_ref, w_ref, ps_ref, o_ref):
    g = pl.program_id(1)
    u = u_ref[...].astype(jnp.float32)
    t = lax.broadcasted_iota(jnp.int32, u.shape, 0)

    def shifted(x, k):
        return jnp.where(t >= k, pltpu.roll(x, k, 0), 0.0)

    s = u
    d = jnp.zeros_like(u)
    for gi, w in enumerate(POOL_WINDOWS):
        s = s + shifted(s, w // 2)
        cnt = jnp.minimum(t + 1, w).astype(jnp.float32)
        d = jnp.where(g == gi, s / cnt - u, d)
    y = jnp.dot(d.astype(jnp.bfloat16), w_ref[0].astype(jnp.bfloat16), preferred_element_type=jnp.float32)
    o_ref[...] = (y * ps_ref[...]).astype(o_ref.dtype)


def _pool(z, w_pool, pool_scale, batch, seq):
    T = z.shape[0]
    G = len(POOL_WINDOWS)
    ucol = 3 * DIFF_WIDTH // POOL_GROUP_DIM
    return pl.pallas_call(
        _pool_kernel,
        grid=(batch, G),
        in_specs=[pl.BlockSpec((seq, POOL_GROUP_DIM), lambda b, g: (b, ucol + g)),
                  pl.BlockSpec((1, POOL_GROUP_DIM, POOL_GROUP_DIM), lambda b, g: (g, 0, 0)),
                  pl.BlockSpec((1, POOL_GROUP_DIM), lambda b, g: (0, g))],
        out_specs=pl.BlockSpec((seq, POOL_GROUP_DIM), lambda b, g: (b, g)),
        out_shape=jax.ShapeDtypeStruct((T, POOL_WIDTH), jnp.bfloat16),
        compiler_params=_params(("parallel", "parallel")),
        name="pool",
    )(z, w_pool, pool_scale)


def _mix_kernel(x_ref, o_ref, p_ref, ga0, ga1, gb0, gb1, wa_ref, wb_ref, wo_ref, gpm_ref, gt1_ref,
                gpf_ref, sc2_ref, sh2_ref, wr_ref, br_ref,
                x1_ref, h2_ref, ei_ref, rk_ref, tw_ref, cnt_ref, run_ref):
    step = pl.program_id(0)
    tm = x_ref.shape[0]

    @pl.when(step == 0)
    def _():
        run_ref[...] = jnp.zeros_like(run_ref)

    ya = jnp.dot(o_ref[...], wa_ref[...], preferred_element_type=jnp.float32)
    yb = jnp.dot(p_ref[...], wb_ref[...], preferred_element_type=jnp.float32)
    ga = jnp.concatenate([ga0[...], ga1[...]], axis=1).astype(jnp.float32)
    gb = jnp.concatenate([gb0[...], gb1[...]], axis=1).astype(jnp.float32)
    merged = (ga * ya + gb * yb).astype(jnp.bfloat16)
    mixed = jnp.dot(merged, wo_ref[...], preferred_element_type=jnp.float32)
    x1 = x_ref[...] + gt1_ref[0] * (_rms(mixed, NORM_EPS) * gpm_ref[...])
    x1_ref[...] = x1
    h2 = _rms(x1, NORM_EPS) * gpf_ref[...] * (1.0 + sc2_ref[0]) + sh2_ref[0]
    h2_ref[...] = h2

    logits = lax.dot_general(wr_ref[...], h2, (((1,), (1,)), ((), ())), precision=_HI,
                             preferred_element_type=jnp.float32) + br_ref[...]
    eid = lax.broadcasted_iota(jnp.int32, logits.shape, 0)
    work = logits
    sels, vals = [], []
    for _ in range(TOP_K):
        mx = jnp.max(work, axis=0, keepdims=True)
        idx = jnp.min(jnp.where(work == mx, eid, N_EXPERTS), axis=0, keepdims=True)
        sel = eid == idx
        work = jnp.where(sel, -jnp.inf, work)
        sels.append(sel)
        vals.append(mx)
    ex = [jnp.exp(v - vals[0]) for v in vals]
    den = ex[0] + ex[1] + ex[2] + ex[3]
    onehot = jnp.zeros(logits.shape, jnp.float32)
    for sel in sels:
        onehot = jnp.where(sel, 1.0, onehot)
    r = lax.broadcasted_iota(jnp.int32, (tm, tm), 0)
    c = lax.broadcasted_iota(jnp.int32, (tm, tm), 1)
    tri = jnp.where(r < c, 1.0, 0.0).astype(jnp.bfloat16)
    rank = jnp.dot(onehot.astype(jnp.bfloat16), tri, preferred_element_type=jnp.float32) + run_ref[:, 0:1]
    for kk in range(TOP_K):
        ei_ref[kk:kk + 1, :] = jnp.sum(jnp.where(sels[kk], eid, 0), axis=0, keepdims=True)
        rk_ref[kk:kk + 1, :] = jnp.sum(jnp.where(sels[kk], rank, 0.0), axis=0, keepdims=True).astype(jnp.int32)
        tw_ref[kk:kk + 1, :] = ex[kk] / den
    run_ref[...] = run_ref[...] + jnp.sum(onehot, axis=1, keepdims=True)
    cnt_ref[...] = run_ref[...]


def _mix(x2, o, p, z, wa, wb, wo, g_post_mix, mod3, g_pre_ffn, w_router_t, b_router, seq):
    T, D = x2.shape
    tm = 256
    per_b = seq // tm
    E = N_EXPERTS
    row = lambda n: pl.BlockSpec((1, n), lambda i: (0, 0))
    mod_spec = lambda col: pl.BlockSpec((1, 1, D), lambda i: (i // per_b, 0, col))
    gate = lambda cb: pl.BlockSpec((tm, COL_TILE), lambda i: (i, cb))
    full = lambda a: pl.BlockSpec(a.shape, lambda i: (0,) * a.ndim)
    tile = pl.BlockSpec((tm, D), lambda i: (i, 0))
    small = pl.BlockSpec((TOP_K, tm), lambda i: (0, i))
    return pl.pallas_call(
        _mix_kernel,
        grid=(T // tm,),
        in_specs=[tile, tile,
                  pl.BlockSpec((tm, POOL_WIDTH), lambda i: (i, 0)),
                  gate(COL_G), gate(COL_G + 1), gate(COL_G + 2), gate(COL_G + 3),
                  full(wa), full(wb), full(wo), row(D), mod_spec(2),
                  row(D), mod_spec(4), mod_spec(3), full(w_router_t),
                  pl.BlockSpec((E, 1), lambda i: (0, 0))],
        out_specs=[tile, tile, small, small, small, pl.BlockSpec((E, LANES), lambda i: (0, 0))],
        out_shape=[jax.ShapeDtypeStruct((T, D), jnp.float32),
                   jax.ShapeDtypeStruct((T, D), jnp.float32),
                   jax.ShapeDtypeStruct((TOP_K, T), jnp.int32),
                   jax.ShapeDtypeStruct((TOP_K, T), jnp.int32),
                   jax.ShapeDtypeStruct((TOP_K, T), jnp.float32),
                   jax.ShapeDtypeStruct((E, LANES), jnp.float32)],
        scratch_shapes=[pltpu.VMEM((E, LANES), jnp.float32)],
        compiler_params=_params(("arbitrary",)),
        name="mix_tail",
    )(x2, o, p, z, z, z, z, wa, wb, wo, g_post_mix, mod3, g_pre_ffn, mod3, mod3, w_router_t,
      b_router.reshape(E, 1))


def _dispatch_kernel(dest_ref, gend_ref, cnt_ref, h_ref, xs_ref, zrow_ref, sem, zsem):
    i = pl.program_id(0)
    td = h_ref.shape[0]
    T = dest_ref.shape[0] // TOP_K

    def row_copy(r, k):
        slot = dest_ref[k * T + i * td + r]
        return pltpu.make_async_copy(h_ref.at[pl.ds(r, 1), :], xs_ref.at[pl.ds(slot, 1), :], sem)

    def issue(r, _):
        for k in range(TOP_K):
            row_copy(r, k).start()
        return 0

    lax.fori_loop(0, td, issue, 0)

    @pl.when(i == 0)
    def _():
        zrow_ref[...] = jnp.zeros_like(zrow_ref)

        def pad_copy(s):
            return pltpu.make_async_copy(zrow_ref.at[pl.ds(0, 1), :], xs_ref.at[pl.ds(s, 1), :], zsem)

        def tail_copy(s):
            off = pl.multiple_of(s * MOE_TILE, MOE_TILE)
            return pltpu.make_async_copy(zrow_ref, xs_ref.at[pl.ds(off, MOE_TILE), :], zsem)

        def run(lo, hi, copy):
            lax.fori_loop(lo, hi, lambda s, _: (copy(s).start(), 0)[1], 0)
            lax.fori_loop(lo, hi, lambda s, _: (copy(s).wait(), 0)[1], 0)

        for e in range(N_EXPERTS):
            run((gend_ref[e - 1] if e else 0) + cnt_ref[e], gend_ref[e], pad_copy)
        run(gend_ref[N_EXPERTS - 1] // MOE_TILE, xs_ref.shape[0] // MOE_TILE, tail_copy)

    def drain(r, _):
        for k in range(TOP_K):
            row_copy(r, k).wait()
        return 0

    lax.fori_loop(0, td, drain, 0)


def _dispatch(dest, gend, counts, h2, n_slots):
    T, D = h2.shape
    td = 512
    return pl.pallas_call(
        _dispatch_kernel,
        grid_spec=pltpu.PrefetchScalarGridSpec(
            num_scalar_prefetch=3,
            grid=(T // td,),
            in_specs=[pl.BlockSpec((td, D), lambda i, *_: (i, 0))],
            out_specs=pl.BlockSpec(memory_space=pl.ANY),
            scratch_shapes=[pltpu.VMEM((MOE_TILE, D), jnp.float32),
                            pltpu.SemaphoreType.DMA, pltpu.SemaphoreType.DMA]),
        out_shape=jax.ShapeDtypeStruct((n_slots, D), jnp.float32),
        compiler_params=_params(("arbitrary",)),
        name="dispatch",
    )(dest, gend, counts, h2)


def _moe_kernel(te_ref, nt_ref, x_ref, w1_ref, b1_ref, w2_ref, b2_ref, perm_ref, y_ref, w1s_ref, w2s_ref):
    j = pl.program_id(0)
    prev = te_ref[jnp.maximum(j - 1, 0)]
    fresh = (j == 0) | (te_ref[j] != prev)
    nblk = w1_ref.shape[2] // MXU_DIM

    @pl.when(fresh & (j < nt_ref[0]))
    def _():
        for b in range(nblk):
            cols = slice(b * MXU_DIM, (b + 1) * MXU_DIM)
            blk = w1_ref[0, :, cols].astype(jnp.bfloat16)
            w1s_ref[:, cols] = jnp.dot(blk, perm_ref[...],
                                       preferred_element_type=jnp.float32).astype(jnp.bfloat16)
        w2s_ref[...] = w2_ref[0].astype(jnp.bfloat16)

    @pl.when(j < nt_ref[0])
    def _():
        x = x_ref[...].astype(jnp.bfloat16)
        z = jnp.dot(x, w1s_ref[...], preferred_element_type=jnp.float32) + b1_ref[0]
        acts = []
        for b in range(nblk):
            gate = jnp.minimum(z[:, b * MXU_DIM:b * MXU_DIM + LANES], SWIGLU_LIMIT)
            up = jnp.clip(z[:, b * MXU_DIM + LANES:(b + 1) * MXU_DIM], -SWIGLU_LIMIT, SWIGLU_LIMIT)
            acts.append(gate * jax.nn.sigmoid(SWIGLU_ALPHA * gate) * (up + 1.0))
        a = jnp.concatenate(acts, axis=1).astype(jnp.bfloat16)
        y_ref[...] = jnp.dot(a, w2s_ref[...], preferred_element_type=jnp.float32) + b2_ref[0]

    @pl.when(j >= nt_ref[0])
    def _():
        y_ref[...] = jnp.zeros_like(y_ref)


def _regroup_perm():
    src = np.arange(MXU_DIM)
    dst = np.where(src % 2 == 0, src // 2, LANES + src // 2)
    perm = np.zeros((MXU_DIM, MXU_DIM), np.float32)
    perm[src, dst] = 1.0
    return jnp.asarray(perm, jnp.bfloat16)


def _moe(tile_expert, n_tiles, xs, w1, b1g, w2, b2):
    n_slots, D = xs.shape
    tm = MOE_TILE
    E, _, F2 = w1.shape
    F = w2.shape[1]
    xmap = lambda j, te, nt: (jnp.minimum(j, nt[0] - 1), 0)
    emap = lambda j, te, nt: (te[j], 0, 0)
    return pl.pallas_call(
        _moe_kernel,
        grid_spec=pltpu.PrefetchScalarGridSpec(
            num_scalar_prefetch=2,
            grid=(n_slots // tm,),
            in_specs=[pl.BlockSpec((tm, D), xmap),
                      pl.BlockSpec((1, D, F2), emap),
                      pl.BlockSpec((1, 1, F2), emap),
                      pl.BlockSpec((1, F, D), emap),
                      pl.BlockSpec((1, 1, D), emap),
                      pl.BlockSpec((MXU_DIM, MXU_DIM), lambda j, te, nt: (0, 0))],
            out_specs=pl.BlockSpec((tm, D), lambda j, te, nt: (j, 0)),
            scratch_shapes=[pltpu.VMEM((D, F2), jnp.bfloat16), pltpu.VMEM((F, D), jnp.bfloat16)]),
        out_shape=jax.ShapeDtypeStruct((n_slots, D), jnp.float32),
        compiler_params=_params(("arbitrary",)),
        name="moe_experts",
    )(tile_expert, n_tiles, xs, w1, b1g, w2, b2.reshape(E, 1, D), _regroup_perm())


def _combine_kernel(dest_ref, ys_ref, w_ref, x1_ref, gt2_ref, g_ref, o_ref, buf_ref, sem):
    i = pl.program_id(0)
    tc = x1_ref.shape[0]
    T = dest_ref.shape[0] // TOP_K

    def row_copy(r, k):
        slot = dest_ref[k * T + i * tc + r]
        return pltpu.make_async_copy(ys_ref.at[pl.ds(slot, 1), :], buf_ref.at[k, pl.ds(r, 1), :], sem)

    def issue(r, _):
        for k in range(TOP_K):
            row_copy(r, k).start()
        return 0

    def drain(r, _):
        for k in range(TOP_K):
            row_copy(r, k).wait()
        return 0

    lax.fori_loop(0, tc, issue, 0)
    lax.fori_loop(0, tc, drain, 0)
    w = w_ref[...]
    f = w[:, 0:1] * buf_ref[0]
    for k in range(1, TOP_K):
        f = f + w[:, k:k + 1] * buf_ref[k]
    o_ref[...] = x1_ref[...] + gt2_ref[0] * (_rms(f, NORM_EPS) * g_ref[...])


def _combine(dest, ys, topw_t, x1, mod3, g_post_ffn, seq):
    T, D = x1.shape
    tc = 256
    per_b = seq // tc
    tile = pl.BlockSpec((tc, D), lambda i, *_: (i, 0))
    return pl.pallas_call(
        _combine_kernel,
        grid_spec=pltpu.PrefetchScalarGridSpec(
            num_scalar_prefetch=1,
            grid=(T // tc,),
            in_specs=[pl.BlockSpec(memory_space=pl.ANY),
                      pl.BlockSpec((tc, TOP_K), lambda i, *_: (i, 0)),
                      tile,
                      pl.BlockSpec((1, 1, D), lambda i, *_: (i // per_b, 0, 5)),
                      pl.BlockSpec((1, D), lambda i, *_: (0, 0))],
            out_specs=tile,
            scratch_shapes=[pltpu.VMEM((TOP_K, tc, D), jnp.float32), pltpu.SemaphoreType.DMA]),
        out_shape=jax.ShapeDtypeStruct((T, D), jnp.float32),
        compiler_params=_params(("arbitrary",)),
        name="combine",
    )(dest, ys, topw_t, x1, mod3, g_post_ffn)


def kernel(x, c, positions, w_ada, b_ada, g_pre_mix, w_in, lambda_q1, lambda_k1, lambda_q2, lambda_k2,
           g_subln, w_pool, pool_scale, w_proj_a, w_proj_b, w_out, g_post_mix, g_pre_ffn,
           w_router, b_router, w_exp1, b_exp1, w_exp2, b_exp2, g_post_ffn):
    B, S, D = x.shape
    T = B * S
    bf = jnp.bfloat16
    tabs = _rope_tables(positions)
    for l in range(w_ada.shape[0]):
        x2 = x.reshape(T, D)
        mod3 = _ada(c, w_ada[l], b_ada[l]).reshape(B, 1, N_MOD * D)
        z = _in_proj(x2, g_pre_mix[l][None], mod3, w_in[l].astype(bf), tabs, S)
        lams = [v[l][None] for v in (lambda_q1, lambda_k1, lambda_q2, lambda_k2)]
        o = _diff_attention(z, lams, g_subln[l][None], B, S)
        p = _pool(z, w_pool[l], pool_scale[l][None], B, S)
        x1, h2, eidx, rnk, topw, cnt = _mix(
            x2, o, p, z, w_proj_a[l].astype(bf), w_proj_b[l].astype(bf), w_out[l].astype(bf),
            g_post_mix[l][None], mod3, g_pre_ffn[l][None], w_router[l].T, b_router[l], S)

        counts = cnt[:, 0].astype(jnp.int32)
        padded = (counts + MOE_TILE - 1) // MOE_TILE * MOE_TILE
        gend = jnp.cumsum(padded)
        gstart = gend - padded
        onehot = eidx[:, :, None] == jnp.arange(N_EXPERTS, dtype=jnp.int32)
        dest = (rnk + jnp.sum(jnp.where(onehot, gstart, 0), axis=-1)).reshape(-1)
        n_slots = T * TOP_K + N_EXPERTS * MOE_TILE
        tile_start = jnp.arange(n_slots // MOE_TILE, dtype=jnp.int32) * MOE_TILE
        tile_expert = jnp.minimum(jnp.sum(tile_start[:, None] >= gend[None, :], axis=1), N_EXPERTS - 1)
        n_tiles = (gend[-1:] // MOE_TILE).astype(jnp.int32)

        xs = _dispatch(dest, gend.astype(jnp.int32), counts, h2, n_slots)
        half = D_FF
        b1g = b_exp1[l].reshape(N_EXPERTS, half // LANES, LANES, 2).transpose(0, 1, 3, 2)
        b1g = b1g.reshape(N_EXPERTS, 1, 2 * half)
        ys = _moe(tile_expert.astype(jnp.int32), n_tiles, xs, w_exp1[l], b1g, w_exp2[l], b_exp2[l])
        x = _combine(dest, ys, topw.T, x1, mod3, g_post_ffn[l][None], S).reshape(B, S, D)
    return x
```

```python
import functools
import math

import numpy as np
import jax
import jax.numpy as jnp
from jax import lax
from jax.experimental import pallas as pl
from jax.experimental.pallas import tpu as pltpu

D_MODEL = 1024
N_HEADS = 8
HEAD_DIM = 64
V_DIM = 2 * HEAD_DIM
DIFF_WIDTH = N_HEADS * V_DIM
POOL_WINDOWS = (2, 4, 8, 16)
POOL_GROUP_DIM = 128
POOL_WIDTH = len(POOL_WINDOWS) * POOL_GROUP_DIM
IN_COLS = 3 * DIFF_WIDTH + POOL_WIDTH + 2 * D_MODEL
ROPE_THETA = 500000.0
ROT_DIM = HEAD_DIM // 4
ROT_HALF = ROT_DIM // 2
N_EXPERTS = 32
TOP_K = 4
D_FF = D_MODEL
SWIGLU_ALPHA = 1.702
SWIGLU_LIMIT = 7.0
NORM_EPS = 1e-6
SUBLN_EPS = 1e-5
N_MOD = 6
NEG_BIG = -1e30
LAMBDA_INIT = 0.8 - 0.6 * math.exp(-0.3 * 0)

LANES = 128
MXU_DIM = 256
VMEM_LIMIT = 56 * 1024 * 1024

COL_TILE = 512
IN_ROW_CHUNK = 256
COL_K = DIFF_WIDTH // COL_TILE
COL_V = 2 * DIFF_WIDTH // COL_TILE
COL_G = (3 * DIFF_WIDTH + POOL_WIDTH) // COL_TILE
ATT_BLOCK = 512
ATT_HEADS = 2
MOE_TILE = 256
TOKEN_TILE = 256
MIX_TILES = 2
SUBLANES = 8
LOCAL_ROWS = TOP_K * TOKEN_TILE + N_EXPERTS * SUBLANES
PIECE_SIZES = tuple(1 << b for b in range(max(TOKEN_TILE, MOE_TILE).bit_length() - 1,
                                          SUBLANES.bit_length() - 2, -1))

_HI = lax.Precision.HIGHEST


def _params(sem, vmem=VMEM_LIMIT):
    return pltpu.CompilerParams(dimension_semantics=sem, vmem_limit_bytes=vmem)


def _rms(x, eps):
    return x * lax.rsqrt(jnp.mean(x * x, axis=-1, keepdims=True) + eps)


def _ada_kernel(c_ref, w_ref, b_ref, o_ref):
    c = c_ref[...]
    s = c * jax.nn.sigmoid(c)
    o_ref[...] = jnp.dot(s, w_ref[...], precision=_HI, preferred_element_type=jnp.float32) + b_ref[...]


def _ada(c, w_ada, b_ada):
    B, D = c.shape
    N = w_ada.shape[1]
    tn = 1536
    return pl.pallas_call(
        _ada_kernel,
        grid=(N // tn,),
        in_specs=[pl.BlockSpec((B, D), lambda j: (0, 0)),
                  pl.BlockSpec((D, tn), lambda j: (0, j)),
                  pl.BlockSpec((1, tn), lambda j: (0, j))],
        out_specs=pl.BlockSpec((B, tn), lambda j: (0, j)),
        out_shape=jax.ShapeDtypeStruct((B, N), jnp.float32),
        compiler_params=_params(("parallel",)),
        name="ada",
    )(c, w_ada, b_ada.reshape(1, N))


def _rope_tab_kernel(pos_ref, invf_ref, phase_ref, c_ref, s_ref):
    pos = pos_ref[...].astype(jnp.float32)
    cs = jnp.cos(pos * invf_ref[...] - phase_ref[...])
    rolled = pltpu.roll(cs, LANES // 2, 1)
    lane = lax.broadcasted_iota(jnp.int32, cs.shape, 1)
    lower = lane < LANES // 2
    rot = lane % (LANES // 2) < ROT_DIM
    c_ref[...] = jnp.where(lower, cs, rolled)
    s_ref[...] = jnp.where(rot, jnp.where(lower, -rolled, cs), 0.0)


def _rope_tables(positions):
    T = positions.size
    tm = min(T, 2048)
    lane = np.arange(LANES)
    rot = lane % (LANES // 2) < ROT_DIM
    inv = ROPE_THETA ** (-(np.arange(ROT_HALF, dtype=np.float64) * 2.0 / ROT_DIM))
    invf = np.where(rot, inv[lane % ROT_HALF], 0.0).astype(np.float32)[None, :]
    phase = np.where(rot & (lane >= LANES // 2), np.pi / 2, 0.0).astype(np.float32)[None, :]
    row = pl.BlockSpec((1, LANES), lambda i: (0, 0))
    tab = pl.BlockSpec((tm, LANES), lambda i: (i, 0))
    sds = jax.ShapeDtypeStruct((T, LANES), jnp.float32)
    return pl.pallas_call(
        _rope_tab_kernel,
        grid=(T // tm,),
        in_specs=[pl.BlockSpec((tm, 1), lambda i: (i, 0)), row, row],
        out_specs=[tab, tab],
        out_shape=[sds, sds],
        compiler_params=_params(("parallel",)),
        name="rope_tables",
    )(positions.reshape(T, 1), jnp.asarray(invf), jnp.asarray(phase))


def _permute_qk_columns(w):
    D, n = w.shape
    w = w.reshape(D, n // V_DIM, 2, 4, 2, ROT_HALF)
    return w.transpose(0, 1, 4, 3, 2, 5).reshape(D, n)


def _in_proj_kernel(x_ref, g_ref, sc_ref, sh_ref, w_ref, c_ref, s_ref, z_ref, h_ref):
    j = pl.program_id(1)
    tm = x_ref.shape[0]
    rows = min(tm, IN_ROW_CHUNK)

    @pl.when(j == 0)
    def _():
        h = _rms(x_ref[...], NORM_EPS) * g_ref[...] * (1.0 + sc_ref[0]) + sh_ref[0]
        h_ref[...] = h.astype(h_ref.dtype)

    def chunks(epilogue):
        for r in range(tm // rows):
            sl = slice(r * rows, (r + 1) * rows)
            z = jnp.dot(h_ref[sl, :], w_ref[...], preferred_element_type=jnp.float32)
            z_ref[sl, :] = epilogue(z, sl).astype(z_ref.dtype)

    @pl.when(j < COL_V)
    def _():
        scale = jnp.where(j < COL_K, HEAD_DIM ** -0.5 * math.log2(math.e), 1.0)

        def rope(z, sl):
            c = c_ref[sl, :] * scale
            s = s_ref[sl, :] * scale
            parts = []
            for g in range(COL_TILE // LANES):
                zg = z[:, g * LANES:(g + 1) * LANES]
                parts.append(zg * c + pltpu.roll(zg, LANES // 2, 1) * s)
            return jnp.concatenate(parts, axis=1)

        chunks(rope)

    @pl.when((j >= COL_V) & (j < COL_G))
    def _():
        chunks(lambda z, sl: z)

    @pl.when(j >= COL_G)
    def _():
        chunks(lambda z, sl: 0.5 * jnp.tanh(0.5 * z) + 0.5)


def _in_proj(x2, g_pre, mod3, w_in_bf, tabs, seq):
    T, D = x2.shape
    tm = min(seq, 1024)
    per_b = seq // tm
    mod_spec = lambda col: pl.BlockSpec((1, 1, D), lambda i, j: (i // per_b, 0, col))
    tab = pl.BlockSpec((tm, LANES), lambda i, j: (i, 0))
    return pl.pallas_call(
        _in_proj_kernel,
        grid=(T // tm, IN_COLS // COL_TILE),
        in_specs=[pl.BlockSpec((tm, D), lambda i, j: (i, 0)),
                  pl.BlockSpec((1, D), lambda i, j: (0, 0)),
                  mod_spec(1), mod_spec(0),
                  pl.BlockSpec((D, COL_TILE), lambda i, j: (0, j)),
                  tab, tab],
        out_specs=pl.BlockSpec((tm, COL_TILE), lambda i, j: (i, j)),
        out_shape=jax.ShapeDtypeStruct((T, IN_COLS), jnp.bfloat16),
        scratch_shapes=[pltpu.VMEM((tm, D), jnp.bfloat16)],
        compiler_params=_params(("parallel", "arbitrary")),
        name="in_proj",
    )(x2, g_pre, mod3, mod3, w_in_bf, *tabs)


def _attn_kernel(lq1_ref, lk1_ref, lq2_ref, lk2_ref, gs_ref, q_ref, k_ref, v_ref, o_ref, vt_ref, *acc_refs):
    i = pl.program_id(2)
    blk = q_ref.shape[0]
    nblk = v_ref.shape[0] // blk
    chains = [(h, comp) for h in range(ATT_HEADS) for comp in range(2)]

    @pl.when(i == 0)
    def _():
        for c in range(nblk):
            for h in range(ATT_HEADS):
                v = v_ref[c * blk:(c + 1) * blk, h * V_DIM:(h + 1) * V_DIM]
                vt_ref[c, h] = v.astype(jnp.float32).T.astype(vt_ref.dtype)

    lam = (jnp.exp(jnp.sum(lq1_ref[...] * lk1_ref[...], axis=-1, keepdims=True))
           - jnp.exp(jnp.sum(lq2_ref[...] * lk2_ref[...], axis=-1, keepdims=True))
           + LAMBDA_INIT)
    lane_comp = lax.broadcasted_iota(jnp.int32, (blk, V_DIM), 1) // ROT_HALF % 2
    qs = []
    for h, comp in chains:
        q = q_ref[:, h * V_DIM:(h + 1) * V_DIM]
        qs.append(jnp.where(lane_comp == comp, q, jnp.zeros_like(q)))
    nt = (((1,), (1,)), ((), ()))
    for acc_ref in acc_refs:
        acc_ref[...] = jnp.zeros_like(acc_ref)

    def step(c, carry, mask):
        off = pl.multiple_of(c * blk, blk)
        out = []
        for n, (h, comp) in enumerate(chains):
            m, l = carry[n]
            k = k_ref[pl.ds(off, blk), h * V_DIM:(h + 1) * V_DIM]
            vt = vt_ref[c, h]
            s = lax.dot_general(k, qs[n], nt, preferred_element_type=jnp.float32)
            if mask is not None:
                s = jnp.where(mask, s, NEG_BIG)
            m_new = jnp.maximum(m, jnp.max(s, axis=0, keepdims=True))
            alpha = jnp.exp2(m - m_new)
            p = jnp.exp2(s - m_new)
            l = alpha * l + jnp.sum(p, axis=0, keepdims=True)
            acc_refs[n][...] = alpha * acc_refs[n][...] + jnp.dot(vt, p.astype(vt.dtype),
                                                                  preferred_element_type=jnp.float32)
            out.append((m_new, l))
        return tuple(out)

    init = tuple((jnp.full((1, blk), NEG_BIG, jnp.float32), jnp.zeros((1, blk), jnp.float32))
                 for _ in chains)
    carry = lax.fori_loop(0, i, lambda c, carry: step(c, carry, None), init)
    key = lax.broadcasted_iota(jnp.int32, (blk, blk), 0)
    qry = lax.broadcasted_iota(jnp.int32, (blk, blk), 1)
    carry = step(i, carry, key <= qry)
    for h in range(ATT_HEADS):
        (_, l1), (_, l2) = carry[2 * h], carry[2 * h + 1]
        ot = acc_refs[2 * h][...] / l1 - lam * (acc_refs[2 * h + 1][...] / l2)
        o = _rms(ot.T, SUBLN_EPS) * gs_ref[...] * (1.0 - LAMBDA_INIT)
        o_ref[:, h * V_DIM:(h + 1) * V_DIM] = o.astype(o_ref.dtype)


def _diff_attention(z, lams, g_subln, batch, seq):
    T = z.shape[0]
    blk = min(ATT_BLOCK, seq)
    nq = seq // blk
    width = ATT_HEADS * V_DIM
    vec = pl.BlockSpec((1, HEAD_DIM), lambda b, h, i: (0, 0))
    kcol = DIFF_WIDTH // width
    return pl.pallas_call(
        _attn_kernel,
        grid=(batch, N_HEADS // ATT_HEADS, nq),
        in_specs=[vec, vec, vec, vec,
                  pl.BlockSpec((1, V_DIM), lambda b, h, i: (0, 0)),
                  pl.BlockSpec((blk, width), lambda b, h, i: (b * nq + i, h)),
                  pl.BlockSpec((seq, width), lambda b, h, i: (b, kcol + h)),
                  pl.BlockSpec((seq, width), lambda b, h, i: (b, 2 * kcol + h))],
        out_specs=pl.BlockSpec((blk, width), lambda b, h, i: (b * nq + i, h)),
        out_shape=jax.ShapeDtypeStruct((T, DIFF_WIDTH), jnp.bfloat16),
        scratch_shapes=[pltpu.VMEM((nq, ATT_HEADS, V_DIM, blk), jnp.bfloat16)]
        + [pltpu.VMEM((V_DIM, blk), jnp.float32)] * (2 * ATT_HEADS),
        compiler_params=_params(("parallel", "parallel", "arbitrary")),
        name="diff_attn",
    )(*lams, g_subln, z, z, z)


def _pool_kernel(u_ref, w_ref, ps_ref, o_ref):
    g = pl.program_id(1)
    u = u_ref[...].astype(jnp.float32)
    t = lax.broadcasted_iota(jnp.int32, u.shape, 0)

    def shifted(x, k):
        return jnp.where(t >= k, pltpu.roll(x, k, 0), 0.0)

    s = u
    d = jnp.zeros_like(u)
    for gi, w in enumerate(POOL_WINDOWS):
        s = s + shifted(s, w // 2)
        cnt = jnp.minimum(t + 1, w).astype(jnp.float32)
        d = jnp.where(g == gi, s / cnt - u, d)
    y = jnp.dot(d.astype(jnp.bfloat16), w_ref[0].astype(jnp.bfloat16), preferred_element_type=jnp.float32)
    o_ref[...] = (y * ps_ref[...]).astype(o_ref.dtype)


def _pool(z, w_pool, pool_scale, batch, seq):
    T = z.shape[0]
    G = len(POOL_WINDOWS)
    ucol = 3 * DIFF_WIDTH // POOL_GROUP_DIM
    return pl.pallas_call(
        _pool_kernel,
        grid=(batch, G),
        in_specs=[pl.BlockSpec((seq, POOL_GROUP_DIM), lambda b, g: (b, ucol + g)),
                  pl.BlockSpec((1, POOL_GROUP_DIM, POOL_GROUP_DIM), lambda b, g: (g, 0, 0)),
                  pl.BlockSpec((1, POOL_GROUP_DIM), lambda b, g: (0, g))],
        out_specs=pl.BlockSpec((seq, POOL_GROUP_DIM), lambda b, g: (b, g)),
        out_shape=jax.ShapeDtypeStruct((T, POOL_WIDTH), jnp.bfloat16),
        compiler_params=_params(("parallel", "parallel")),
        name="pool",
    )(z, w_pool, pool_scale)


def _mix_kernel(x_ref, o_ref, p_ref, ga0, ga1, gb0, gb1, wa_ref, wb_ref, wo_ref, gpm_ref, gt1_ref,
                gpf_ref, sc2_ref, sh2_ref, wr_ref, br_ref,
                x1_ref, xl_ref, ls_ref, tw_ref, cnt_ref):
    for t in range(x_ref.shape[0] // TOKEN_TILE):
        _mix_tile(t, x_ref, o_ref, p_ref, ga0, ga1, gb0, gb1, wa_ref, wb_ref, wo_ref, gpm_ref, gt1_ref,
                  gpf_ref, sc2_ref, sh2_ref, wr_ref, br_ref, x1_ref, xl_ref, ls_ref, tw_ref, cnt_ref)


def _mix_tile(t, x_ref, o_ref, p_ref, ga0, ga1, gb0, gb1, wa_ref, wb_ref, wo_ref, gpm_ref, gt1_ref,
              gpf_ref, sc2_ref, sh2_ref, wr_ref, br_ref, x1_ref, xl_ref, ls_ref, tw_ref, cnt_ref):
    tm = TOKEN_TILE
    rows = slice(t * tm, (t + 1) * tm)
    ya = jnp.dot(o_ref[rows, :], wa_ref[...], preferred_element_type=jnp.float32)
    yb = jnp.dot(p_ref[rows, :], wb_ref[...], preferred_element_type=jnp.float32)
    ga = jnp.concatenate([ga0[rows, :], ga1[rows, :]], axis=1).astype(jnp.float32)
    gb = jnp.concatenate([gb0[rows, :], gb1[rows, :]], axis=1).astype(jnp.float32)
    merged = (ga * ya + gb * yb).astype(jnp.bfloat16)
    mixed = jnp.dot(merged, wo_ref[...], preferred_element_type=jnp.float32)
    x1 = x_ref[rows, :] + gt1_ref[0] * (_rms(mixed, NORM_EPS) * gpm_ref[...])
    x1_ref[rows, :] = x1
    h2 = _rms(x1, NORM_EPS) * gpf_ref[...] * (1.0 + sc2_ref[0]) + sh2_ref[0]

    logits = lax.dot_general(wr_ref[...], h2, (((1,), (1,)), ((), ())), precision=_HI,
                             preferred_element_type=jnp.float32) + br_ref[...]
    eid = lax.broadcasted_iota(jnp.int32, logits.shape, 0)
    work = logits
    sels, vals = [], []
    for _ in range(TOP_K):
        mx = jnp.max(work, axis=0, keepdims=True)
        idx = jnp.min(jnp.where(work == mx, eid, N_EXPERTS), axis=0, keepdims=True)
        sel = eid == idx
        work = jnp.where(sel, -jnp.inf, work)
        sels.append(sel)
        vals.append(mx)
    ex = [jnp.exp(v - vals[0]) for v in vals]
    den = ex[0] + ex[1] + ex[2] + ex[3]
    onehot = jnp.zeros(logits.shape, jnp.float32)
    for sel in sels:
        onehot = jnp.where(sel, 1.0, onehot)
    r = lax.broadcasted_iota(jnp.int32, (tm, tm), 0)
    c = lax.broadcasted_iota(jnp.int32, (tm, tm), 1)
    tri = jnp.where(r < c, 1.0, 0.0).astype(jnp.bfloat16)
    rank = jnp.dot(onehot.astype(jnp.bfloat16), tri, preferred_element_type=jnp.float32)
    counts = jnp.broadcast_to(jnp.sum(onehot, axis=1, keepdims=True), (N_EXPERTS, LANES))
    rounded = jnp.floor((counts + (SUBLANES - 1)) * (1.0 / SUBLANES)) * SUBLANES
    er = lax.broadcasted_iota(jnp.int32, (N_EXPERTS, N_EXPERTS), 0)
    ec = lax.broadcasted_iota(jnp.int32, (N_EXPERTS, N_EXPERTS), 1)
    below = jnp.where(ec < er, 1.0, 0.0).astype(jnp.bfloat16)
    offset = jnp.dot(below, rounded.astype(jnp.bfloat16), preferred_element_type=jnp.float32)
    slot_of = rank + offset[:, 0:1]
    cnt_ref[t] = rounded
    slot_id = lax.broadcasted_iota(jnp.int32, (LOCAL_ROWS, tm), 0)
    place = jnp.zeros((LOCAL_ROWS, tm), jnp.float32)
    for kk in range(TOP_K):
        ls = jnp.sum(jnp.where(sels[kk], slot_of, 0.0), axis=0, keepdims=True).astype(jnp.int32)
        ls_ref[kk:kk + 1, rows] = ls
        tw_ref[kk:kk + 1, rows] = ex[kk] / den
        place = jnp.where(slot_id == ls, 1.0, place)
    xl_ref[t * LOCAL_ROWS:(t + 1) * LOCAL_ROWS, :] = jnp.dot(
        place.astype(jnp.bfloat16), h2.astype(jnp.bfloat16), preferred_element_type=jnp.float32)


def _mix(x2, o, p, z, wa, wb, wo, g_post_mix, mod3, g_pre_ffn, w_router_t, b_router, seq):
    T, D = x2.shape
    sub = min(MIX_TILES, seq // TOKEN_TILE)
    tm = sub * TOKEN_TILE
    per_b = seq // tm
    E = N_EXPERTS
    row = lambda n: pl.BlockSpec((1, n), lambda i: (0, 0))
    mod_spec = lambda col: pl.BlockSpec((1, 1, D), lambda i: (i // per_b, 0, col))
    gate = lambda cb: pl.BlockSpec((tm, COL_TILE), lambda i: (i, cb))
    full = lambda a: pl.BlockSpec(a.shape, lambda i: (0,) * a.ndim)
    tile = pl.BlockSpec((tm, D), lambda i: (i, 0))
    small = pl.BlockSpec((TOP_K, tm), lambda i: (0, i))
    return pl.pallas_call(
        _mix_kernel,
        grid=(T // tm,),
        in_specs=[tile, tile,
                  pl.BlockSpec((tm, POOL_WIDTH), lambda i: (i, 0)),
                  gate(COL_G), gate(COL_G + 1), gate(COL_G + 2), gate(COL_G + 3),
                  full(wa), full(wb), full(wo), row(D), mod_spec(2),
                  row(D), mod_spec(4), mod_spec(3), full(w_router_t),
                  pl.BlockSpec((E, 1), lambda i: (0, 0))],
        out_specs=[tile, pl.BlockSpec((sub * LOCAL_ROWS, D), lambda i: (i, 0)), small, small,
                   pl.BlockSpec((sub, E, LANES), lambda i: (i, 0, 0))],
        out_shape=[jax.ShapeDtypeStruct((T, D), jnp.float32),
                   jax.ShapeDtypeStruct((T // TOKEN_TILE * LOCAL_ROWS, D), jnp.float32),
                   jax.ShapeDtypeStruct((TOP_K, T), jnp.int32),
                   jax.ShapeDtypeStruct((TOP_K, T), jnp.float32),
                   jax.ShapeDtypeStruct((T // TOKEN_TILE, E, LANES), jnp.float32)],
        compiler_params=_params(("parallel",)),
        name="mix_tail",
    )(x2, o, p, z, z, z, z, wa, wb, wo, g_post_mix, mod3, g_pre_ffn, mod3, mod3, w_router_t,
      b_router.reshape(E, 1))


def _split_copy(src_ref, src_row, dst_ref, dst_row, n, sem, act):
    off = 0
    for p in PIECE_SIZES:
        take = (n & p) != 0

        @pl.when(take)
        def _(off=off, p=p):
            src_at = 0 if src_row is None else pl.multiple_of(src_row + off, SUBLANES)
            dst_at = pl.multiple_of(dst_row + off, SUBLANES)
            act(pltpu.make_async_copy(src_ref.at[pl.ds(src_at, p), :], dst_ref.at[pl.ds(dst_at, p), :], sem))

        off = off + jnp.where(take, p, 0)


def _regroup_kernel(src_off_ref, dst_off_ref, len_ref, zero_off_ref, zero_len_ref, ntile_ref,
                    src_ref, dst_ref, zero_ref, sem, zsem, *, fill_tail_tiles, zero_per_step):
    i = pl.program_id(0)

    @pl.when(i == 0)
    def _():
        zero_ref[...] = jnp.zeros_like(zero_ref)

    def pieces(act):
        def body(e, _):
            idx = i * N_EXPERTS + e
            _split_copy(src_ref, src_off_ref[idx], dst_ref, dst_off_ref[idx], len_ref[idx], sem, act)
            return 0
        lax.fori_loop(0, N_EXPERTS, body, 0)
        for r in range(zero_per_step):
            idx = i * zero_per_step + r
            _split_copy(zero_ref, None, dst_ref, zero_off_ref[idx], zero_len_ref[idx], zsem, act)

    def tails(act):
        def body(t, _):
            off = pl.multiple_of(t * MOE_TILE, MOE_TILE)
            act(pltpu.make_async_copy(zero_ref.at[pl.ds(0, MOE_TILE), :],
                                      dst_ref.at[pl.ds(off, MOE_TILE), :], zsem))
            return 0
        lax.fori_loop(ntile_ref[0], dst_ref.shape[0] // MOE_TILE, body, 0)

    for act in (lambda cp: cp.start(), lambda cp: cp.wait()):
        pieces(act)
        if fill_tail_tiles:
            pl.when(i == 0)(functools.partial(tails, act))


def _regroup(src_off, dst_off, lens, zero_off, zero_len, n_tiles, src, n_out, fill_tail_tiles, name):
    D = src.shape[1]
    steps = lens.shape[0] // N_EXPERTS
    return pl.pallas_call(
        functools.partial(_regroup_kernel, fill_tail_tiles=fill_tail_tiles,
                          zero_per_step=zero_len.shape[0] // steps),
        grid_spec=pltpu.PrefetchScalarGridSpec(
            num_scalar_prefetch=6,
            grid=(steps,),
            in_specs=[pl.BlockSpec(memory_space=pl.ANY)],
            out_specs=pl.BlockSpec(memory_space=pl.ANY),
            scratch_shapes=[pltpu.VMEM((PIECE_SIZES[0], D), jnp.float32),
                            pltpu.SemaphoreType.DMA, pltpu.SemaphoreType.DMA]),
        out_shape=jax.ShapeDtypeStruct((n_out, D), jnp.float32),
        compiler_params=_params(("arbitrary",)),
        name=name,
    )(src_off, dst_off, lens, zero_off, zero_len, n_tiles, src)


def _moe_kernel(te_ref, nt_ref, x_ref, w1_ref, b1_ref, w2_ref, b2_ref, perm_ref, y_ref, w1s_ref, w2s_ref):
    j = pl.program_id(0)
    prev = te_ref[jnp.maximum(j - 1, 0)]
    fresh = (j == 0) | (te_ref[j] != prev)
    nblk = w1_ref.shape[2] // MXU_DIM

    @pl.when(fresh & (j < nt_ref[0]))
    def _():
        for b in range(nblk):
            cols = slice(b * MXU_DIM, (b + 1) * MXU_DIM)
            blk = w1_ref[0, :, cols].astype(jnp.bfloat16)
            w1s_ref[:, cols] = jnp.dot(blk, perm_ref[...],
                                       preferred_element_type=jnp.float32).astype(jnp.bfloat16)
        w2s_ref[...] = w2_ref[0].astype(jnp.bfloat16)

    @pl.when(j < nt_ref[0])
    def _():
        x = x_ref[...].astype(jnp.bfloat16)
        z = jnp.dot(x, w1s_ref[...], preferred_element_type=jnp.float32) + b1_ref[0]
        acts = []
        for b in range(nblk):
            gate = jnp.minimum(z[:, b * MXU_DIM:b * MXU_DIM + LANES], SWIGLU_LIMIT)
            up = jnp.clip(z[:, b * MXU_DIM + LANES:(b + 1) * MXU_DIM], -SWIGLU_LIMIT, SWIGLU_LIMIT)
            acts.append(gate * jax.nn.sigmoid(SWIGLU_ALPHA * gate) * (up + 1.0))
        a = jnp.concatenate(acts, axis=1).astype(jnp.bfloat16)
        y_ref[...] = jnp.dot(a, w2s_ref[...], preferred_element_type=jnp.float32) + b2_ref[0]

    @pl.when(j >= nt_ref[0])
    def _():
        y_ref[...] = jnp.zeros_like(y_ref)


def _regroup_perm():
    src = np.arange(MXU_DIM)
    dst = np.where(src % 2 == 0, src // 2, LANES + src // 2)
    perm = np.zeros((MXU_DIM, MXU_DIM), np.float32)
    perm[src, dst] = 1.0
    return jnp.asarray(perm, jnp.bfloat16)


def _moe(tile_expert, n_tiles, xs, w1, b1g, w2, b2):
    n_slots, D = xs.shape
    tm = MOE_TILE
    E, _, F2 = w1.shape
    F = w2.shape[1]
    xmap = lambda j, te, nt: (jnp.minimum(j, nt[0] - 1), 0)
    emap = lambda j, te, nt: (te[j], 0, 0)
    return pl.pallas_call(
        _moe_kernel,
        grid_spec=pltpu.PrefetchScalarGridSpec(
            num_scalar_prefetch=2,
            grid=(n_slots // tm,),
            in_specs=[pl.BlockSpec((tm, D), xmap),
                      pl.BlockSpec((1, D, F2), emap),
                      pl.BlockSpec((1, 1, F2), emap),
                      pl.BlockSpec((1, F, D), emap),
                      pl.BlockSpec((1, 1, D), emap),
                      pl.BlockSpec((MXU_DIM, MXU_DIM), lambda j, te, nt: (0, 0))],
            out_specs=pl.BlockSpec((tm, D), lambda j, te, nt: (j, 0)),
            scratch_shapes=[pltpu.VMEM((D, F2), jnp.bfloat16), pltpu.VMEM((F, D), jnp.bfloat16)]),
        out_shape=jax.ShapeDtypeStruct((n_slots, D), jnp.float32),
        compiler_params=_params(("arbitrary",)),
        name="moe_experts",
    )(tile_expert, n_tiles, xs, w1, b1g, w2, b2.reshape(E, 1, D), _regroup_perm())


def _combine_kernel(yl_ref, ls_ref, w_ref, x1_ref, gt2_ref, g_ref, o_ref):
    tc = x1_ref.shape[0]
    slot_id = lax.broadcasted_iota(jnp.int32, (tc, LOCAL_ROWS), 1)
    ls = ls_ref[...]
    w = w_ref[...]
    mix = jnp.zeros((tc, LOCAL_ROWS), jnp.float32)
    for k in range(TOP_K):
        mix = jnp.where(slot_id == ls[:, k:k + 1], w[:, k:k + 1], mix)
    hi = mix.astype(jnp.bfloat16)
    lo = (mix - hi.astype(jnp.float32)).astype(jnp.bfloat16)
    y = yl_ref[...].astype(jnp.bfloat16)
    f = (jnp.dot(hi, y, preferred_element_type=jnp.float32)
         + jnp.dot(lo, y, preferred_element_type=jnp.float32))
    o_ref[...] = x1_ref[...] + gt2_ref[0] * (_rms(f, NORM_EPS) * g_ref[...])


def _combine(ys_local, ls_t, topw_t, x1, mod3, g_post_ffn, seq):
    T, D = x1.shape
    tc = TOKEN_TILE
    per_b = seq // tc
    tile = pl.BlockSpec((tc, D), lambda i: (i, 0))
    small = pl.BlockSpec((tc, TOP_K), lambda i: (i, 0))
    return pl.pallas_call(
        _combine_kernel,
        grid=(T // tc,),
        in_specs=[pl.BlockSpec((LOCAL_ROWS, D), lambda i: (i, 0)), small, small, tile,
                  pl.BlockSpec((1, 1, D), lambda i: (i // per_b, 0, 5)),
                  pl.BlockSpec((1, D), lambda i: (0, 0))],
        out_specs=tile,
        out_shape=jax.ShapeDtypeStruct((T, D), jnp.float32),
        compiler_params=_params(("parallel",)),
        name="combine",
    )(ys_local, ls_t, topw_t, x1, mod3, g_post_ffn)


def kernel(x, c, positions, w_ada, b_ada, g_pre_mix, w_in, lambda_q1, lambda_k1, lambda_q2, lambda_k2,
           g_subln, w_pool, pool_scale, w_proj_a, w_proj_b, w_out, g_post_mix, g_pre_ffn,
           w_router, b_router, w_exp1, b_exp1, w_exp2, b_exp2, g_post_ffn):
    B, S, D = x.shape
    T = B * S
    bf = jnp.bfloat16
    tabs = _rope_tables(positions)
    for l in range(w_ada.shape[0]):
        x2 = x.reshape(T, D)
        mod3 = _ada(c, w_ada[l], b_ada[l]).reshape(B, 1, N_MOD * D)
        n_qk = 2 * DIFF_WIDTH
        w_in_bf = jnp.concatenate([_permute_qk_columns(w_in[l][:, :n_qk]), w_in[l][:, n_qk:]], axis=1).astype(bf)
        z = _in_proj(x2, g_pre_mix[l][None], mod3, w_in_bf, tabs, S)
        lams = [v[l][None] for v in (lambda_q1, lambda_k1, lambda_q2, lambda_k2)]
        o = _diff_attention(z, lams, g_subln[l][None], B, S)
        p = _pool(z, w_pool[l], pool_scale[l][None], B, S)
        x1, xs_local, ls, topw, cnt = _mix(
            x2, o, p, z, w_proj_a[l].astype(bf), w_proj_b[l].astype(bf), w_out[l].astype(bf),
            g_post_mix[l][None], mod3, g_pre_ffn[l][None], w_router[l].T, b_router[l], S)

        i32 = jnp.int32
        n = cnt[:, :, 0].astype(i32)
        n_tok_tiles = n.shape[0]
        counts = jnp.sum(n, axis=0)
        padded = (counts + MOE_TILE - 1) // MOE_TILE * MOE_TILE
        gend = jnp.cumsum(padded).astype(i32)
        gstart = gend - padded
        tile_base = jnp.arange(n_tok_tiles, dtype=i32) * LOCAL_ROWS
        local_off = (jnp.cumsum(n, axis=1) - n + tile_base[:, None]).reshape(-1).astype(i32)
        slot_off = (gstart[None, :] + jnp.cumsum(n, axis=0) - n).reshape(-1).astype(i32)
        lens = n.reshape(-1)
        steps_pad = -N_EXPERTS % n_tok_tiles
        group_pad_off = jnp.pad(gstart + counts, (0, steps_pad)).astype(i32)
        group_pad_len = jnp.pad(padded - counts, (0, steps_pad)).astype(i32)
        used = jnp.sum(n, axis=1)
        block_pad_off, block_pad_len = (tile_base + used).astype(i32), (LOCAL_ROWS - used).astype(i32)
        n_slots = -(-(T * TOP_K + n_tok_tiles * N_EXPERTS * SUBLANES) // MOE_TILE) * MOE_TILE + N_EXPERTS * MOE_TILE
        tile_start = jnp.arange(n_slots // MOE_TILE, dtype=i32) * MOE_TILE
        tile_expert = jnp.minimum(jnp.sum(tile_start[:, None] >= gend[None, :], axis=1), N_EXPERTS - 1)
        n_tiles = gend[-1:] // MOE_TILE

        xs = _regroup(local_off, slot_off, lens, group_pad_off, group_pad_len, n_tiles, xs_local, n_slots,
                      True, "dispatch")
        half = D_FF
        b1g = b_exp1[l].reshape(N_EXPERTS, half // LANES, LANES, 2).transpose(0, 1, 3, 2)
        b1g = b1g.reshape(N_EXPERTS, 1, 2 * half)
        ys = _moe(tile_expert.astype(i32), n_tiles, xs, w_exp1[l], b1g, w_exp2[l], b_exp2[l])
        ys_local = _regroup(slot_off, local_off, lens, block_pad_off, block_pad_len, n_tiles, ys,
                            n_tok_tiles * LOCAL_ROWS, False, "undispatch")
        x = _combine(ys_local, ls.T, topw.T, x1, mod3, g_post_ffn[l][None], S).reshape(B, S, D)
    return x
```

```python
import functools
import math

import numpy as np
import jax
import jax.numpy as jnp
from jax import lax
from jax.experimental import pallas as pl
from jax.experimental.pallas import tpu as pltpu

D_MODEL = 1024
N_HEADS = 8
HEAD_DIM = 64
V_DIM = 2 * HEAD_DIM
DIFF_WIDTH = N_HEADS * V_DIM
POOL_WINDOWS = (2, 4, 8, 16)
POOL_GROUP_DIM = 128
POOL_WIDTH = len(POOL_WINDOWS) * POOL_GROUP_DIM
IN_COLS = 3 * DIFF_WIDTH + POOL_WIDTH + 2 * D_MODEL
ROPE_THETA = 500000.0
ROT_DIM = HEAD_DIM // 4
ROT_HALF = ROT_DIM // 2
N_EXPERTS = 32
TOP_K = 4
D_FF = D_MODEL
SWIGLU_ALPHA = 1.702
SWIGLU_LIMIT = 7.0
NORM_EPS = 1e-6
SUBLN_EPS = 1e-5
N_MOD = 6
NEG_BIG = -1e30
LAMBDA_INIT = 0.8 - 0.6 * math.exp(-0.3 * 0)

LANES = 128
MXU_DIM = 256
VMEM_LIMIT = 56 * 1024 * 1024

COL_TILE = 512
IN_ROW_CHUNK = 256
COL_K = DIFF_WIDTH // COL_TILE
COL_V = 2 * DIFF_WIDTH // COL_TILE
COL_G = (3 * DIFF_WIDTH + POOL_WIDTH) // COL_TILE
ATT_BLOCK = 512
ATT_HEADS = 2
MOE_TILE = 256
TOKEN_TILE = 256
MIX_TILES = 2
ROW_ALIGN = 16
LOCAL_ROWS = TOP_K * TOKEN_TILE + N_EXPERTS * ROW_ALIGN
PIECE_SIZES = tuple(1 << b for b in range(max(TOKEN_TILE, MOE_TILE).bit_length() - 1,
                                          ROW_ALIGN.bit_length() - 2, -1))

_HI = lax.Precision.HIGHEST


def _params(sem, vmem=VMEM_LIMIT):
    return pltpu.CompilerParams(dimension_semantics=sem, vmem_limit_bytes=vmem)


def _rms(x, eps):
    return x * lax.rsqrt(jnp.mean(x * x, axis=-1, keepdims=True) + eps)


def _ada_kernel(c_ref, w_ref, b_ref, o_ref):
    c = c_ref[...]
    s = c * jax.nn.sigmoid(c)
    o_ref[...] = jnp.dot(s, w_ref[...], precision=_HI, preferred_element_type=jnp.float32) + b_ref[...]


def _ada(c, w_ada, b_ada):
    B, D = c.shape
    N = w_ada.shape[1]
    tn = 1536
    return pl.pallas_call(
        _ada_kernel,
        grid=(N // tn,),
        in_specs=[pl.BlockSpec((B, D), lambda j: (0, 0)),
                  pl.BlockSpec((D, tn), lambda j: (0, j)),
                  pl.BlockSpec((1, tn), lambda j: (0, j))],
        out_specs=pl.BlockSpec((B, tn), lambda j: (0, j)),
        out_shape=jax.ShapeDtypeStruct((B, N), jnp.float32),
        compiler_params=_params(("parallel",)),
        name="ada",
    )(c, w_ada, b_ada.reshape(1, N))


def _rope_tab_kernel(pos_ref, invf_ref, phase_ref, c_ref, s_ref):
    pos = pos_ref[...].astype(jnp.float32)
    cs = jnp.cos(pos * invf_ref[...] - phase_ref[...])
    rolled = pltpu.roll(cs, LANES // 2, 1)
    lane = lax.broadcasted_iota(jnp.int32, cs.shape, 1)
    lower = lane < LANES // 2
    rot = lane % (LANES // 2) < ROT_DIM
    c_ref[...] = jnp.where(lower, cs, rolled)
    s_ref[...] = jnp.where(rot, jnp.where(lower, -rolled, cs), 0.0)


def _rope_tables(positions):
    T = positions.size
    tm = min(T, 2048)
    lane = np.arange(LANES)
    rot = lane % (LANES // 2) < ROT_DIM
    inv = ROPE_THETA ** (-(np.arange(ROT_HALF, dtype=np.float64) * 2.0 / ROT_DIM))
    invf = np.where(rot, inv[lane % ROT_HALF], 0.0).astype(np.float32)[None, :]
    phase = np.where(rot & (lane >= LANES // 2), np.pi / 2, 0.0).astype(np.float32)[None, :]
    row = pl.BlockSpec((1, LANES), lambda i: (0, 0))
    tab = pl.BlockSpec((tm, LANES), lambda i: (i, 0))
    sds = jax.ShapeDtypeStruct((T, LANES), jnp.float32)
    return pl.pallas_call(
        _rope_tab_kernel,
        grid=(T // tm,),
        in_specs=[pl.BlockSpec((tm, 1), lambda i: (i, 0)), row, row],
        out_specs=[tab, tab],
        out_shape=[sds, sds],
        compiler_params=_params(("parallel",)),
        name="rope_tables",
    )(positions.reshape(T, 1), jnp.asarray(invf), jnp.asarray(phase))


def _permute_qk_columns(w):
    D, n = w.shape
    w = w.reshape(D, n // V_DIM, 2, 4, 2, ROT_HALF)
    return w.transpose(0, 1, 4, 3, 2, 5).reshape(D, n)


def _in_proj_kernel(x_ref, g_ref, sc_ref, sh_ref, w_ref, c_ref, s_ref, z_ref, h_ref):
    j = pl.program_id(1)
    tm = x_ref.shape[0]
    rows = min(tm, IN_ROW_CHUNK)

    @pl.when(j == 0)
    def _():
        h = _rms(x_ref[...], NORM_EPS) * g_ref[...] * (1.0 + sc_ref[0]) + sh_ref[0]
        h_ref[...] = h.astype(h_ref.dtype)

    def chunks(epilogue):
        for r in range(tm // rows):
            sl = slice(r * rows, (r + 1) * rows)
            z = jnp.dot(h_ref[sl, :], w_ref[...], preferred_element_type=jnp.float32)
            z_ref[sl, :] = epilogue(z, sl).astype(z_ref.dtype)

    @pl.when(j < COL_V)
    def _():
        scale = jnp.where(j < COL_K, HEAD_DIM ** -0.5 * math.log2(math.e), 1.0)

        def rope(z, sl):
            c = c_ref[sl, :] * scale
            s = s_ref[sl, :] * scale
            parts = []
            for g in range(COL_TILE // LANES):
                zg = z[:, g * LANES:(g + 1) * LANES]
                parts.append(zg * c + pltpu.roll(zg, LANES // 2, 1) * s)
            return jnp.concatenate(parts, axis=1)

        chunks(rope)

    @pl.when((j >= COL_V) & (j < COL_G))
    def _():
        chunks(lambda z, sl: z)

    @pl.when(j >= COL_G)
    def _():
        chunks(lambda z, sl: 0.5 * jnp.tanh(0.5 * z) + 0.5)


def _in_proj(x2, g_pre, mod3, w_in_bf, tabs, seq):
    T, D = x2.shape
    tm = min(seq, 1024)
    per_b = seq // tm
    mod_spec = lambda col: pl.BlockSpec((1, 1, D), lambda i, j: (i // per_b, 0, col))
    tab = pl.BlockSpec((tm, LANES), lambda i, j: (i, 0))
    return pl.pallas_call(
        _in_proj_kernel,
        grid=(T // tm, IN_COLS // COL_TILE),
        in_specs=[pl.BlockSpec((tm, D), lambda i, j: (i, 0)),
                  pl.BlockSpec((1, D), lambda i, j: (0, 0)),
                  mod_spec(1), mod_spec(0),
                  pl.BlockSpec((D, COL_TILE), lambda i, j: (0, j)),
                  tab, tab],
        out_specs=pl.BlockSpec((tm, COL_TILE), lambda i, j: (i, j)),
        out_shape=jax.ShapeDtypeStruct((T, IN_COLS), jnp.bfloat16),
        scratch_shapes=[pltpu.VMEM((tm, D), jnp.bfloat16)],
        compiler_params=_params(("parallel", "arbitrary")),
        name="in_proj",
    )(x2, g_pre, mod3, mod3, w_in_bf, *tabs)


def _attn_kernel(lq1_ref, lk1_ref, lq2_ref, lk2_ref, gs_ref, q_ref, k_ref, v_ref, o_ref, vt_ref, *acc_refs):
    i = pl.program_id(2)
    blk = q_ref.shape[0]
    nblk = v_ref.shape[0] // blk
    chains = [(h, comp) for h in range(ATT_HEADS) for comp in range(2)]

    @pl.when(i == 0)
    def _():
        for c in range(nblk):
            for h in range(ATT_HEADS):
                v = v_ref[c * blk:(c + 1) * blk, h * V_DIM:(h + 1) * V_DIM]
                vt_ref[c, h] = v.astype(jnp.float32).T.astype(vt_ref.dtype)

    lam = (jnp.exp(jnp.sum(lq1_ref[...] * lk1_ref[...], axis=-1, keepdims=True))
           - jnp.exp(jnp.sum(lq2_ref[...] * lk2_ref[...], axis=-1, keepdims=True))
           + LAMBDA_INIT)
    lane_comp = lax.broadcasted_iota(jnp.int32, (blk, V_DIM), 1) // ROT_HALF % 2
    qs = []
    for h, comp in chains:
        q = q_ref[:, h * V_DIM:(h + 1) * V_DIM]
        qs.append(jnp.where(lane_comp == comp, q, jnp.zeros_like(q)))
    nt = (((1,), (1,)), ((), ()))
    for acc_ref in acc_refs:
        acc_ref[...] = jnp.zeros_like(acc_ref)

    def step(c, carry, mask):
        off = pl.multiple_of(c * blk, blk)
        out = []
        for n, (h, comp) in enumerate(chains):
            m, l = carry[n]
            k = k_ref[pl.ds(off, blk), h * V_DIM:(h + 1) * V_DIM]
            vt = vt_ref[c, h]
            s = lax.dot_general(k, qs[n], nt, preferred_element_type=jnp.float32)
            if mask is not None:
                s = jnp.where(mask, s, NEG_BIG)
            m_new = jnp.maximum(m, jnp.max(s, axis=0, keepdims=True))
            alpha = jnp.exp2(m - m_new)
            p = jnp.exp2(s - m_new)
            l = alpha * l + jnp.sum(p, axis=0, keepdims=True)
            acc_refs[n][...] = alpha * acc_refs[n][...] + jnp.dot(vt, p.astype(vt.dtype),
                                                                  preferred_element_type=jnp.float32)
            out.append((m_new, l))
        return tuple(out)

    init = tuple((jnp.full((1, blk), NEG_BIG, jnp.float32), jnp.zeros((1, blk), jnp.float32))
                 for _ in chains)
    carry = lax.fori_loop(0, i, lambda c, carry: step(c, carry, None), init)
    key = lax.broadcasted_iota(jnp.int32, (blk, blk), 0)
    qry = lax.broadcasted_iota(jnp.int32, (blk, blk), 1)
    carry = step(i, carry, key <= qry)
    for h in range(ATT_HEADS):
        (_, l1), (_, l2) = carry[2 * h], carry[2 * h + 1]
        ot = acc_refs[2 * h][...] / l1 - lam * (acc_refs[2 * h + 1][...] / l2)
        o = _rms(ot.T, SUBLN_EPS) * gs_ref[...] * (1.0 - LAMBDA_INIT)
        o_ref[:, h * V_DIM:(h + 1) * V_DIM] = o.astype(o_ref.dtype)


def _diff_attention(z, lams, g_subln, batch, seq):
    T = z.shape[0]
    blk = min(ATT_BLOCK, seq)
    nq = seq // blk
    width = ATT_HEADS * V_DIM
    vec = pl.BlockSpec((1, HEAD_DIM), lambda b, h, i: (0, 0))
    kcol = DIFF_WIDTH // width
    return pl.pallas_call(
        _attn_kernel,
        grid=(batch, N_HEADS // ATT_HEADS, nq),
        in_specs=[vec, vec, vec, vec,
                  pl.BlockSpec((1, V_DIM), lambda b, h, i: (0, 0)),
                  pl.BlockSpec((blk, width), lambda b, h, i: (b * nq + i, h)),
                  pl.BlockSpec((seq, width), lambda b, h, i: (b, kcol + h)),
                  pl.BlockSpec((seq, width), lambda b, h, i: (b, 2 * kcol + h))],
        out_specs=pl.BlockSpec((blk, width), lambda b, h, i: (b * nq + i, h)),
        out_shape=jax.ShapeDtypeStruct((T, DIFF_WIDTH), jnp.bfloat16),
        scratch_shapes=[pltpu.VMEM((nq, ATT_HEADS, V_DIM, blk), jnp.bfloat16)]
        + [pltpu.VMEM((V_DIM, blk), jnp.float32)] * (2 * ATT_HEADS),
        compiler_params=_params(("parallel", "parallel", "arbitrary")),
        name="diff_attn",
    )(*lams, g_subln, z, z, z)


def _pool_kernel(u_ref, w_ref, ps_ref, o_ref):
    g = pl.program_id(1)
    u = u_ref[...].astype(jnp.float32)
    t = lax.broadcasted_iota(jnp.int32, u.shape, 0)

    def shifted(x, k):
        return jnp.where(t >= k, pltpu.roll(x, k, 0), 0.0)

    s = u
    d = jnp.zeros_like(u)
    for gi, w in enumerate(POOL_WINDOWS):
        s = s + shifted(s, w // 2)
        cnt = jnp.minimum(t + 1, w).astype(jnp.float32)
        d = jnp.where(g == gi, s / cnt - u, d)
    y = jnp.dot(d.astype(jnp.bfloat16), w_ref[0].astype(jnp.bfloat16), preferred_element_type=jnp.float32)
    o_ref[...] = (y * ps_ref[...]).astype(o_ref.dtype)


def _pool(z, w_pool, pool_scale, batch, seq):
    T = z.shape[0]
    G = len(POOL_WINDOWS)
    ucol = 3 * DIFF_WIDTH // POOL_GROUP_DIM
    return pl.pallas_call(
        _pool_kernel,
        grid=(batch, G),
        in_specs=[pl.BlockSpec((seq, POOL_GROUP_DIM), lambda b, g: (b, ucol + g)),
                  pl.BlockSpec((1, POOL_GROUP_DIM, POOL_GROUP_DIM), lambda b, g: (g, 0, 0)),
                  pl.BlockSpec((1, POOL_GROUP_DIM), lambda b, g: (0, g))],
        out_specs=pl.BlockSpec((seq, POOL_GROUP_DIM), lambda b, g: (b, g)),
        out_shape=jax.ShapeDtypeStruct((T, POOL_WIDTH), jnp.bfloat16),
        compiler_params=_params(("parallel", "parallel")),
        name="pool",
    )(z, w_pool, pool_scale)


def _mix_kernel(x_ref, o_ref, p_ref, ga0, ga1, gb0, gb1, wa_ref, wb_ref, wo_ref, gpm_ref, gt1_ref,
                gpf_ref, sc2_ref, sh2_ref, wr_ref, br_ref,
                x1_ref, xl_ref, ls_ref, tw_ref, cnt_ref):
    for t in range(x_ref.shape[0] // TOKEN_TILE):
        _mix_tile(t, x_ref, o_ref, p_ref, ga0, ga1, gb0, gb1, wa_ref, wb_ref, wo_ref, gpm_ref, gt1_ref,
                  gpf_ref, sc2_ref, sh2_ref, wr_ref, br_ref, x1_ref, xl_ref, ls_ref, tw_ref, cnt_ref)


def _mix_tile(t, x_ref, o_ref, p_ref, ga0, ga1, gb0, gb1, wa_ref, wb_ref, wo_ref, gpm_ref, gt1_ref,
              gpf_ref, sc2_ref, sh2_ref, wr_ref, br_ref, x1_ref, xl_ref, ls_ref, tw_ref, cnt_ref):
    tm = TOKEN_TILE
    rows = slice(t * tm, (t + 1) * tm)
    ya = jnp.dot(o_ref[rows, :], wa_ref[...], preferred_element_type=jnp.float32)
    yb = jnp.dot(p_ref[rows, :], wb_ref[...], preferred_element_type=jnp.float32)
    ga = jnp.concatenate([ga0[rows, :], ga1[rows, :]], axis=1).astype(jnp.float32)
    gb = jnp.concatenate([gb0[rows, :], gb1[rows, :]], axis=1).astype(jnp.float32)
    merged = (ga * ya + gb * yb).astype(jnp.bfloat16)
    mixed = jnp.dot(merged, wo_ref[...], preferred_element_type=jnp.float32)
    x1 = x_ref[rows, :] + gt1_ref[0] * (_rms(mixed, NORM_EPS) * gpm_ref[...])
    x1_ref[rows, :] = x1
    h2 = _rms(x1, NORM_EPS) * gpf_ref[...] * (1.0 + sc2_ref[0]) + sh2_ref[0]

    logits = lax.dot_general(wr_ref[...], h2, (((1,), (1,)), ((), ())), precision=_HI,
                             preferred_element_type=jnp.float32) + br_ref[...]
    eid = lax.broadcasted_iota(jnp.int32, logits.shape, 0)
    work = logits
    sels, vals = [], []
    for _ in range(TOP_K):
        mx = jnp.max(work, axis=0, keepdims=True)
        idx = jnp.min(jnp.where(work == mx, eid, N_EXPERTS), axis=0, keepdims=True)
        sel = eid == idx
        work = jnp.where(sel, -jnp.inf, work)
        sels.append(sel)
        vals.append(mx)
    ex = [jnp.exp(v - vals[0]) for v in vals]
    den = ex[0] + ex[1] + ex[2] + ex[3]
    onehot = jnp.zeros(logits.shape, jnp.float32)
    for sel in sels:
        onehot = jnp.where(sel, 1.0, onehot)
    r = lax.broadcasted_iota(jnp.int32, (tm, tm), 0)
    c = lax.broadcasted_iota(jnp.int32, (tm, tm), 1)
    tri = jnp.where(r < c, 1.0, 0.0).astype(jnp.bfloat16)
    rank = jnp.dot(onehot.astype(jnp.bfloat16), tri, preferred_element_type=jnp.float32)
    counts = jnp.broadcast_to(jnp.sum(onehot, axis=1, keepdims=True), (N_EXPERTS, LANES))
    rounded = jnp.floor((counts + (ROW_ALIGN - 1)) * (1.0 / ROW_ALIGN)) * ROW_ALIGN
    er = lax.broadcasted_iota(jnp.int32, (N_EXPERTS, N_EXPERTS), 0)
    ec = lax.broadcasted_iota(jnp.int32, (N_EXPERTS, N_EXPERTS), 1)
    below = jnp.where(ec < er, 1.0, 0.0).astype(jnp.bfloat16)
    offset = jnp.dot(below, rounded.astype(jnp.bfloat16), preferred_element_type=jnp.float32)
    slot_of = rank + offset[:, 0:1]
    cnt_ref[t] = rounded
    slot_id = lax.broadcasted_iota(jnp.int32, (LOCAL_ROWS, tm), 0)
    place = jnp.zeros((LOCAL_ROWS, tm), jnp.float32)
    for kk in range(TOP_K):
        ls = jnp.sum(jnp.where(sels[kk], slot_of, 0.0), axis=0, keepdims=True).astype(jnp.int32)
        ls_ref[kk:kk + 1, rows] = ls
        tw_ref[kk:kk + 1, rows] = ex[kk] / den
        place = jnp.where(slot_id == ls, 1.0, place)
    xl = jnp.dot(place.astype(jnp.bfloat16), h2.astype(jnp.bfloat16), preferred_element_type=jnp.float32)
    xl_ref[t * LOCAL_ROWS:(t + 1) * LOCAL_ROWS, :] = xl.astype(xl_ref.dtype)


def _mix(x2, o, p, z, wa, wb, wo, g_post_mix, mod3, g_pre_ffn, w_router_t, b_router, seq):
    T, D = x2.shape
    sub = min(MIX_TILES, seq // TOKEN_TILE)
    tm = sub * TOKEN_TILE
    per_b = seq // tm
    E = N_EXPERTS
    row = lambda n: pl.BlockSpec((1, n), lambda i: (0, 0))
    mod_spec = lambda col: pl.BlockSpec((1, 1, D), lambda i: (i // per_b, 0, col))
    gate = lambda cb: pl.BlockSpec((tm, COL_TILE), lambda i: (i, cb))
    full = lambda a: pl.BlockSpec(a.shape, lambda i: (0,) * a.ndim)
    tile = pl.BlockSpec((tm, D), lambda i: (i, 0))
    small = pl.BlockSpec((TOP_K, tm), lambda i: (0, i))
    return pl.pallas_call(
        _mix_kernel,
        grid=(T // tm,),
        in_specs=[tile, tile,
                  pl.BlockSpec((tm, POOL_WIDTH), lambda i: (i, 0)),
                  gate(COL_G), gate(COL_G + 1), gate(COL_G + 2), gate(COL_G + 3),
                  full(wa), full(wb), full(wo), row(D), mod_spec(2),
                  row(D), mod_spec(4), mod_spec(3), full(w_router_t),
                  pl.BlockSpec((E, 1), lambda i: (0, 0))],
        out_specs=[tile, pl.BlockSpec((sub * LOCAL_ROWS, D), lambda i: (i, 0)), small, small,
                   pl.BlockSpec((sub, E, LANES), lambda i: (i, 0, 0))],
        out_shape=[jax.ShapeDtypeStruct((T, D), jnp.float32),
                   jax.ShapeDtypeStruct((T // TOKEN_TILE * LOCAL_ROWS, D), jnp.bfloat16),
                   jax.ShapeDtypeStruct((TOP_K, T), jnp.int32),
                   jax.ShapeDtypeStruct((TOP_K, T), jnp.float32),
                   jax.ShapeDtypeStruct((T // TOKEN_TILE, E, LANES), jnp.float32)],
        compiler_params=_params(("parallel",)),
        name="mix_tail",
    )(x2, o, p, z, z, z, z, wa, wb, wo, g_post_mix, mod3, g_pre_ffn, mod3, mod3, w_router_t,
      b_router.reshape(E, 1))


def _split_copy(src_ref, src_row, dst_ref, dst_row, n, sem, act):
    off = 0
    for p in PIECE_SIZES:
        take = (n & p) != 0

        @pl.when(take)
        def _(off=off, p=p):
            src_at = 0 if src_row is None else pl.multiple_of(src_row + off, ROW_ALIGN)
            dst_at = pl.multiple_of(dst_row + off, ROW_ALIGN)
            act(pltpu.make_async_copy(src_ref.at[pl.ds(src_at, p), :], dst_ref.at[pl.ds(dst_at, p), :], sem))

        off = off + jnp.where(take, p, 0)


def _dispatch_kernel(local_off_ref, slot_off_ref, len_ref, zero_off_ref, zero_len_ref, ntile_ref,
                     xl_ref, xs_ref, zero_ref, sem, zsem, *, zero_per_step):
    i = pl.program_id(0)

    @pl.when(i == 0)
    def _():
        zero_ref[...] = jnp.zeros_like(zero_ref)

    def pieces(act):
        def body(e, _):
            idx = i * N_EXPERTS + e
            _split_copy(xl_ref, local_off_ref[idx], xs_ref, slot_off_ref[idx], len_ref[idx], sem, act)
            return 0
        lax.fori_loop(0, N_EXPERTS, body, 0)
        for r in range(zero_per_step):
            idx = i * zero_per_step + r
            _split_copy(zero_ref, None, xs_ref, zero_off_ref[idx], zero_len_ref[idx], zsem, act)

    def tails(act):
        def body(t, _):
            off = pl.multiple_of(t * MOE_TILE, MOE_TILE)
            act(pltpu.make_async_copy(zero_ref.at[pl.ds(0, MOE_TILE), :],
                                      xs_ref.at[pl.ds(off, MOE_TILE), :], zsem))
            return 0
        lax.fori_loop(ntile_ref[0], xs_ref.shape[0] // MOE_TILE, body, 0)

    for act in (lambda cp: cp.start(), lambda cp: cp.wait()):
        pieces(act)
        pl.when(i == 0)(functools.partial(tails, act))


def _dispatch(local_off, slot_off, lens, zero_off, zero_len, n_tiles, xs_local, n_slots):
    W = xs_local.shape[1]
    steps = lens.shape[0] // N_EXPERTS
    return pl.pallas_call(
        functools.partial(_dispatch_kernel, zero_per_step=zero_len.shape[0] // steps),
        grid_spec=pltpu.PrefetchScalarGridSpec(
            num_scalar_prefetch=6,
            grid=(steps,),
            in_specs=[pl.BlockSpec((LOCAL_ROWS, W), lambda i, *_: (i, 0))],
            out_specs=pl.BlockSpec(memory_space=pl.ANY),
            scratch_shapes=[pltpu.VMEM((PIECE_SIZES[0], W), xs_local.dtype),
                            pltpu.SemaphoreType.DMA, pltpu.SemaphoreType.DMA]),
        out_shape=jax.ShapeDtypeStruct((n_slots, W), xs_local.dtype),
        compiler_params=_params(("arbitrary",)),
        name="dispatch",
    )(local_off, slot_off, lens, zero_off, zero_len, n_tiles, xs_local)


def _moe_kernel(te_ref, nt_ref, x_ref, w1_ref, b1_ref, w2_ref, b2_ref, perm_ref, y_ref, w1s_ref, w2s_ref):
    j = pl.program_id(0)
    prev = te_ref[jnp.maximum(j - 1, 0)]
    fresh = (j == 0) | (te_ref[j] != prev)
    nblk = w1_ref.shape[2] // MXU_DIM

    @pl.when(fresh & (j < nt_ref[0]))
    def _():
        for b in range(nblk):
            cols = slice(b * MXU_DIM, (b + 1) * MXU_DIM)
            blk = w1_ref[0, :, cols].astype(jnp.bfloat16)
            w1s_ref[:, cols] = jnp.dot(blk, perm_ref[...],
                                       preferred_element_type=jnp.float32).astype(jnp.bfloat16)
        w2s_ref[...] = w2_ref[0].astype(jnp.bfloat16)

    @pl.when(j < nt_ref[0])
    def _():
        z = jnp.dot(x_ref[...], w1s_ref[...], preferred_element_type=jnp.float32) + b1_ref[0]
        acts = []
        for b in range(nblk):
            gate = jnp.minimum(z[:, b * MXU_DIM:b * MXU_DIM + LANES], SWIGLU_LIMIT)
            up = jnp.clip(z[:, b * MXU_DIM + LANES:(b + 1) * MXU_DIM], -SWIGLU_LIMIT, SWIGLU_LIMIT)
            acts.append(gate * jax.nn.sigmoid(SWIGLU_ALPHA * gate) * (up + 1.0))
        a = jnp.concatenate(acts, axis=1).astype(jnp.bfloat16)
        y = jnp.dot(a, w2s_ref[...], preferred_element_type=jnp.float32) + b2_ref[0]
        y_ref[...] = y.astype(y_ref.dtype)

    @pl.when(j >= nt_ref[0])
    def _():
        y_ref[...] = jnp.zeros_like(y_ref)


def _regroup_perm():
    src = np.arange(MXU_DIM)
    dst = np.where(src % 2 == 0, src // 2, LANES + src // 2)
    perm = np.zeros((MXU_DIM, MXU_DIM), np.float32)
    perm[src, dst] = 1.0
    return jnp.asarray(perm, jnp.bfloat16)


def _moe(tile_expert, n_tiles, xs, w1, b1g, w2, b2):
    n_slots, W = xs.shape
    tm = MOE_TILE
    E, D, F2 = w1.shape
    F = w2.shape[1]
    xmap = lambda j, te, nt: (jnp.minimum(j, nt[0] - 1), 0)
    emap = lambda j, te, nt: (te[j], 0, 0)
    return pl.pallas_call(
        _moe_kernel,
        grid_spec=pltpu.PrefetchScalarGridSpec(
            num_scalar_prefetch=2,
            grid=(n_slots // tm,),
            in_specs=[pl.BlockSpec((tm, W), xmap),
                      pl.BlockSpec((1, D, F2), emap),
                      pl.BlockSpec((1, 1, F2), emap),
                      pl.BlockSpec((1, F, D), emap),
                      pl.BlockSpec((1, 1, D), emap),
                      pl.BlockSpec((MXU_DIM, MXU_DIM), lambda j, te, nt: (0, 0))],
            out_specs=pl.BlockSpec((tm, W), lambda j, te, nt: (j, 0)),
            scratch_shapes=[pltpu.VMEM((D, F2), jnp.bfloat16), pltpu.VMEM((F, D), jnp.bfloat16)]),
        out_shape=jax.ShapeDtypeStruct((n_slots, W), xs.dtype),
        compiler_params=_params(("arbitrary",)),
        name="moe_experts",
    )(tile_expert, n_tiles, xs, w1, b1g, w2, b2.reshape(E, 1, D), _regroup_perm())


def _combine_kernel(local_off_ref, slot_off_ref, len_ref, ys_ref, ls_ref, w_ref, x1_ref, gt2_ref, g_ref,
                    o_ref, yl_ref, sem):
    i = pl.program_id(0)
    tc = x1_ref.shape[0]
    yl_ref[TOP_K * tc:, :] = jnp.zeros((LOCAL_ROWS - TOP_K * tc, yl_ref.shape[1]), yl_ref.dtype)

    def pieces(act):
        def body(e, _):
            idx = i * N_EXPERTS + e
            _split_copy(ys_ref, slot_off_ref[idx], yl_ref, local_off_ref[idx], len_ref[idx], sem, act)
            return 0
        lax.fori_loop(0, N_EXPERTS, body, 0)

    pieces(lambda cp: cp.start())
    slot_id = lax.broadcasted_iota(jnp.int32, (tc, LOCAL_ROWS), 1)
    ls = ls_ref[...]
    w = w_ref[...]
    mix = jnp.zeros((tc, LOCAL_ROWS), jnp.float32)
    for k in range(TOP_K):
        mix = jnp.where(slot_id == ls[:, k:k + 1], w[:, k:k + 1], mix)
    hi = mix.astype(jnp.bfloat16)
    lo = (mix - hi.astype(jnp.float32)).astype(jnp.bfloat16)
    pieces(lambda cp: cp.wait())
    y = yl_ref[...]
    f = (jnp.dot(hi, y, preferred_element_type=jnp.float32)
         + jnp.dot(lo, y, preferred_element_type=jnp.float32))
    o_ref[...] = x1_ref[...] + gt2_ref[0] * (_rms(f, NORM_EPS) * g_ref[...])


def _combine(local_off, slot_off, lens, ys, ls_t, topw_t, x1, mod3, g_post_ffn, seq):
    T, D = x1.shape
    tc = TOKEN_TILE
    per_b = seq // tc
    tile = pl.BlockSpec((tc, D), lambda i, *_: (i, 0))
    small = pl.BlockSpec((tc, TOP_K), lambda i, *_: (i, 0))
    return pl.pallas_call(
        _combine_kernel,
        grid_spec=pltpu.PrefetchScalarGridSpec(
            num_scalar_prefetch=3,
            grid=(T // tc,),
            in_specs=[pl.BlockSpec(memory_space=pl.ANY), small, small, tile,
                      pl.BlockSpec((1, 1, D), lambda i, *_: (i // per_b, 0, 5)),
                      pl.BlockSpec((1, D), lambda i, *_: (0, 0))],
            out_specs=tile,
            scratch_shapes=[pltpu.VMEM((LOCAL_ROWS, ys.shape[1]), ys.dtype), pltpu.SemaphoreType.DMA]),
        out_shape=jax.ShapeDtypeStruct((T, D), jnp.float32),
        compiler_params=_params(("arbitrary",)),
        name="combine",
    )(local_off, slot_off, lens, ys, ls_t, topw_t, x1, mod3, g_post_ffn)


def kernel(x, c, positions, w_ada, b_ada, g_pre_mix, w_in, lambda_q1, lambda_k1, lambda_q2, lambda_k2,
           g_subln, w_pool, pool_scale, w_proj_a, w_proj_b, w_out, g_post_mix, g_pre_ffn,
           w_router, b_router, w_exp1, b_exp1, w_exp2, b_exp2, g_post_ffn):
    B, S, D = x.shape
    T = B * S
    bf = jnp.bfloat16
    tabs = _rope_tables(positions)
    for l in range(w_ada.shape[0]):
        x2 = x.reshape(T, D)
        mod3 = _ada(c, w_ada[l], b_ada[l]).reshape(B, 1, N_MOD * D)
        n_qk = 2 * DIFF_WIDTH
        w_in_bf = jnp.concatenate([_permute_qk_columns(w_in[l][:, :n_qk]), w_in[l][:, n_qk:]], axis=1).astype(bf)
        z = _in_proj(x2, g_pre_mix[l][None], mod3, w_in_bf, tabs, S)
        lams = [v[l][None] for v in (lambda_q1, lambda_k1, lambda_q2, lambda_k2)]
        o = _diff_attention(z, lams, g_subln[l][None], B, S)
        p = _pool(z, w_pool[l], pool_scale[l][None], B, S)
        x1, xs_local, ls, topw, cnt = _mix(
            x2, o, p, z, w_proj_a[l].astype(bf), w_proj_b[l].astype(bf), w_out[l].astype(bf),
            g_post_mix[l][None], mod3, g_pre_ffn[l][None], w_router[l].T, b_router[l], S)

        i32 = jnp.int32
        n = cnt[:, :, 0].astype(i32)
        n_tok_tiles = n.shape[0]
        counts = jnp.sum(n, axis=0)
        padded = (counts + MOE_TILE - 1) // MOE_TILE * MOE_TILE
        gend = jnp.cumsum(padded).astype(i32)
        gstart = gend - padded
        local_off = (jnp.cumsum(n, axis=1) - n).reshape(-1).astype(i32)
        slot_off = (gstart[None, :] + jnp.cumsum(n, axis=0) - n).reshape(-1).astype(i32)
        lens = n.reshape(-1)
        steps_pad = -N_EXPERTS % n_tok_tiles
        group_pad_off = jnp.pad(gstart + counts, (0, steps_pad)).astype(i32)
        group_pad_len = jnp.pad(padded - counts, (0, steps_pad)).astype(i32)
        n_slots = -(-(T * TOP_K + n_tok_tiles * N_EXPERTS * ROW_ALIGN) // MOE_TILE) * MOE_TILE + N_EXPERTS * MOE_TILE
        tile_start = jnp.arange(n_slots // MOE_TILE, dtype=i32) * MOE_TILE
        tile_expert = jnp.minimum(jnp.sum(tile_start[:, None] >= gend[None, :], axis=1), N_EXPERTS - 1)
        n_tiles = gend[-1:] // MOE_TILE

        xs = _dispatch(local_off, slot_off, lens, group_pad_off, group_pad_len, n_tiles, xs_local, n_slots)
        half = D_FF
        b1g = b_exp1[l].reshape(N_EXPERTS, half // LANES, LANES, 2).transpose(0, 1, 3, 2)
        b1g = b1g.reshape(N_EXPERTS, 1, 2 * half)
        ys = _moe(tile_expert.astype(i32), n_tiles, xs, w_exp1[l], b1g, w_exp2[l], b_exp2[l])
        x = _combine(local_off, slot_off, lens, ys, ls.T, topw.T, x1, mod3, g_post_ffn[l][None], S)
        x = x.reshape(B, S, D)
    return x
```

```python
import functools
import math

import numpy as np
import jax
import jax.numpy as jnp
from jax import lax
from jax.experimental import pallas as pl
from jax.experimental.pallas import tpu as pltpu

D_MODEL = 1024
N_HEADS = 8
HEAD_DIM = 64
V_DIM = 2 * HEAD_DIM
DIFF_WIDTH = N_HEADS * V_DIM
POOL_WINDOWS = (2, 4, 8, 16)
POOL_GROUP_DIM = 128
POOL_WIDTH = len(POOL_WINDOWS) * POOL_GROUP_DIM
IN_COLS = 3 * DIFF_WIDTH + POOL_WIDTH + 2 * D_MODEL
ROPE_THETA = 500000.0
ROT_DIM = HEAD_DIM // 4
ROT_HALF = ROT_DIM // 2
N_EXPERTS = 32
TOP_K = 4
D_FF = D_MODEL
SWIGLU_ALPHA = 1.702
SWIGLU_LIMIT = 7.0
NORM_EPS = 1e-6
SUBLN_EPS = 1e-5
N_MOD = 6
NEG_BIG = -1e30
LAMBDA_INIT = 0.8 - 0.6 * math.exp(-0.3 * 0)

LANES = 128
MXU_DIM = 256
VMEM_LIMIT = 56 * 1024 * 1024

COL_TILE = 512
IN_ROW_TILE = 2048
IN_ROW_CHUNK = 256
COL_K = DIFF_WIDTH // COL_TILE
COL_V = 2 * DIFF_WIDTH // COL_TILE
COL_G = (3 * DIFF_WIDTH + POOL_WIDTH) // COL_TILE
ATT_BLOCK = 512
ATT_HEADS = 2
MOE_TILE = 256
TOKEN_TILE = 256
MIX_TILES = 2
ROW_ALIGN = 16
LOCAL_ROWS = TOP_K * TOKEN_TILE + N_EXPERTS * ROW_ALIGN
PIECE_SIZES = tuple(1 << b for b in range(max(TOKEN_TILE, MOE_TILE).bit_length() - 1,
                                          ROW_ALIGN.bit_length() - 2, -1))

_HI = lax.Precision.HIGHEST


def _params(sem, vmem=VMEM_LIMIT):
    return pltpu.CompilerParams(dimension_semantics=sem, vmem_limit_bytes=vmem)


def _rms(x, eps):
    return x * lax.rsqrt(jnp.mean(x * x, axis=-1, keepdims=True) + eps)


def _ada_kernel(c_ref, w_ref, b_ref, o_ref):
    c = c_ref[...]
    s = c * jax.nn.sigmoid(c)
    o_ref[...] = jnp.dot(s, w_ref[...], precision=_HI, preferred_element_type=jnp.float32) + b_ref[...]


def _ada(c, w_ada, b_ada):
    B, D = c.shape
    N = w_ada.shape[1]
    tn = 1536
    return pl.pallas_call(
        _ada_kernel,
        grid=(N // tn,),
        in_specs=[pl.BlockSpec((B, D), lambda j: (0, 0)),
                  pl.BlockSpec((D, tn), lambda j: (0, j)),
                  pl.BlockSpec((1, tn), lambda j: (0, j))],
        out_specs=pl.BlockSpec((B, tn), lambda j: (0, j)),
        out_shape=jax.ShapeDtypeStruct((B, N), jnp.float32),
        compiler_params=_params(("parallel",)),
        name="ada",
    )(c, w_ada, b_ada.reshape(1, N))


def _rope_tab_kernel(pos_ref, invf_ref, phase_ref, c_ref, s_ref):
    pos = pos_ref[...].astype(jnp.float32)
    cs = jnp.cos(pos * invf_ref[...] - phase_ref[...])
    rolled = pltpu.roll(cs, LANES // 2, 1)
    lane = lax.broadcasted_iota(jnp.int32, cs.shape, 1)
    lower = lane < LANES // 2
    rot = lane % (LANES // 2) < ROT_DIM
    c_ref[...] = jnp.where(lower, cs, rolled)
    s_ref[...] = jnp.where(rot, jnp.where(lower, -rolled, cs), 0.0)


def _rope_tables(positions):
    T = positions.size
    tm = min(T, 2048)
    lane = np.arange(LANES)
    rot = lane % (LANES // 2) < ROT_DIM
    inv = ROPE_THETA ** (-(np.arange(ROT_HALF, dtype=np.float64) * 2.0 / ROT_DIM))
    invf = np.where(rot, inv[lane % ROT_HALF], 0.0).astype(np.float32)[None, :]
    phase = np.where(rot & (lane >= LANES // 2), np.pi / 2, 0.0).astype(np.float32)[None, :]
    row = pl.BlockSpec((1, LANES), lambda i: (0, 0))
    tab = pl.BlockSpec((tm, LANES), lambda i: (i, 0))
    sds = jax.ShapeDtypeStruct((T, LANES), jnp.float32)
    return pl.pallas_call(
        _rope_tab_kernel,
        grid=(T // tm,),
        in_specs=[pl.BlockSpec((tm, 1), lambda i: (i, 0)), row, row],
        out_specs=[tab, tab],
        out_shape=[sds, sds],
        compiler_params=_params(("parallel",)),
        name="rope_tables",
    )(positions.reshape(T, 1), jnp.asarray(invf), jnp.asarray(phase))


def _permute_qk_columns(w):
    D, n = w.shape
    w = w.reshape(D, n // V_DIM, 2, 4, 2, ROT_HALF)
    return w.transpose(0, 1, 4, 3, 2, 5).reshape(D, n)


def _in_proj_kernel(x_ref, g_ref, sc_ref, sh_ref, w_ref, c_ref, s_ref, z_ref, h_ref):
    j = pl.program_id(1)
    tm = x_ref.shape[0]
    rows = min(tm, IN_ROW_CHUNK)

    @pl.when(j == 0)
    def _():
        h = _rms(x_ref[...], NORM_EPS) * g_ref[...] * (1.0 + sc_ref[0]) + sh_ref[0]
        h_ref[...] = h.astype(h_ref.dtype)

    def chunks(epilogue):
        for r in range(tm // rows):
            sl = slice(r * rows, (r + 1) * rows)
            z = jnp.dot(h_ref[sl, :], w_ref[...], preferred_element_type=jnp.float32)
            z_ref[sl, :] = epilogue(z, sl).astype(z_ref.dtype)

    @pl.when(j < COL_V)
    def _():
        scale = jnp.where(j < COL_K, HEAD_DIM ** -0.5 * math.log2(math.e), 1.0)

        def rope(z, sl):
            c = c_ref[sl, :] * scale
            s = s_ref[sl, :] * scale
            parts = []
            for g in range(COL_TILE // LANES):
                zg = z[:, g * LANES:(g + 1) * LANES]
                parts.append(zg * c + pltpu.roll(zg, LANES // 2, 1) * s)
            return jnp.concatenate(parts, axis=1)

        chunks(rope)

    @pl.when((j >= COL_V) & (j < COL_G))
    def _():
        chunks(lambda z, sl: z)

    @pl.when(j >= COL_G)
    def _():
        chunks(lambda z, sl: 0.5 * jnp.tanh(0.5 * z) + 0.5)


def _in_proj(x2, g_pre, mod3, w_in_bf, tabs, seq):
    T, D = x2.shape
    tm = min(seq, IN_ROW_TILE)
    per_b = seq // tm
    mod_spec = lambda col: pl.BlockSpec((1, 1, D), lambda i, j: (i // per_b, 0, col))
    tab = pl.BlockSpec((tm, LANES), lambda i, j: (i, 0))
    return pl.pallas_call(
        _in_proj_kernel,
        grid=(T // tm, IN_COLS // COL_TILE),
        in_specs=[pl.BlockSpec((tm, D), lambda i, j: (i, 0)),
                  pl.BlockSpec((1, D), lambda i, j: (0, 0)),
                  mod_spec(1), mod_spec(0),
                  pl.BlockSpec((D, COL_TILE), lambda i, j: (0, j)),
                  tab, tab],
        out_specs=pl.BlockSpec((tm, COL_TILE), lambda i, j: (i, j)),
        out_shape=jax.ShapeDtypeStruct((T, IN_COLS), jnp.bfloat16),
        scratch_shapes=[pltpu.VMEM((tm, D), jnp.bfloat16)],
        compiler_params=_params(("parallel", "arbitrary")),
        name="in_proj",
    )(x2, g_pre, mod3, mod3, w_in_bf, *tabs)


def _attn_kernel(lq1_ref, lk1_ref, lq2_ref, lk2_ref, gs_ref, q_ref, k_ref, v_ref, o_ref, vt_ref, *acc_refs):
    i = pl.program_id(2)
    blk = q_ref.shape[0]
    nblk = v_ref.shape[0] // blk
    chains = [(h, comp) for h in range(ATT_HEADS) for comp in range(2)]

    @pl.when(i == 0)
    def _():
        for c in range(nblk):
            for h in range(ATT_HEADS):
                v = v_ref[c * blk:(c + 1) * blk, h * V_DIM:(h + 1) * V_DIM]
                vt_ref[c, h] = v.astype(jnp.float32).T.astype(vt_ref.dtype)

    lam = (jnp.exp(jnp.sum(lq1_ref[...] * lk1_ref[...], axis=-1, keepdims=True))
           - jnp.exp(jnp.sum(lq2_ref[...] * lk2_ref[...], axis=-1, keepdims=True))
           + LAMBDA_INIT)
    lane_comp = lax.broadcasted_iota(jnp.int32, (blk, V_DIM), 1) // ROT_HALF % 2
    qs = []
    for h, comp in chains:
        q = q_ref[:, h * V_DIM:(h + 1) * V_DIM]
        qs.append(jnp.where(lane_comp == comp, q, jnp.zeros_like(q)))
    nt = (((1,), (1,)), ((), ()))
    for acc_ref in acc_refs:
        acc_ref[...] = jnp.zeros_like(acc_ref)

    def step(c, carry, mask):
        off = pl.multiple_of(c * blk, blk)
        scores = []
        for n, (h, comp) in enumerate(chains):
            k = k_ref[pl.ds(off, blk), h * V_DIM:(h + 1) * V_DIM]
            scores.append(lax.dot_general(k, qs[n], nt, preferred_element_type=jnp.float32))
        out, probs, alphas = [], [], []
        for n, s in enumerate(scores):
            m, l = carry[n]
            if mask is not None:
                s = jnp.where(mask, s, NEG_BIG)
            m_new = jnp.maximum(m, jnp.max(s, axis=0, keepdims=True))
            alpha = jnp.exp2(m - m_new)
            p = jnp.exp2(s - m_new)
            out.append((m_new, alpha * l + jnp.sum(p, axis=0, keepdims=True)))
            probs.append(p.astype(vt_ref.dtype))
            alphas.append(alpha)
        for n, (h, comp) in enumerate(chains):
            pv = jnp.dot(vt_ref[c, h], probs[n], preferred_element_type=jnp.float32)
            acc_refs[n][...] = alphas[n] * acc_refs[n][...] + pv
        return tuple(out)

    init = tuple((jnp.full((1, blk), NEG_BIG, jnp.float32), jnp.zeros((1, blk), jnp.float32))
                 for _ in chains)
    carry = lax.fori_loop(0, i, lambda c, carry: step(c, carry, None), init)
    key = lax.broadcasted_iota(jnp.int32, (blk, blk), 0)
    qry = lax.broadcasted_iota(jnp.int32, (blk, blk), 1)
    carry = step(i, carry, key <= qry)
    for h in range(ATT_HEADS):
        (_, l1), (_, l2) = carry[2 * h], carry[2 * h + 1]
        ot = acc_refs[2 * h][...] / l1 - lam * (acc_refs[2 * h + 1][...] / l2)
        o = _rms(ot.T, SUBLN_EPS) * gs_ref[...] * (1.0 - LAMBDA_INIT)
        o_ref[:, h * V_DIM:(h + 1) * V_DIM] = o.astype(o_ref.dtype)


def _diff_attention(z, lams, g_subln, batch, seq):
    T = z.shape[0]
    blk = min(ATT_BLOCK, seq)
    nq = seq // blk
    width = ATT_HEADS * V_DIM
    vec = pl.BlockSpec((1, HEAD_DIM), lambda b, h, i: (0, 0))
    kcol = DIFF_WIDTH // width
    return pl.pallas_call(
        _attn_kernel,
        grid=(batch, N_HEADS // ATT_HEADS, nq),
        in_specs=[vec, vec, vec, vec,
                  pl.BlockSpec((1, V_DIM), lambda b, h, i: (0, 0)),
                  pl.BlockSpec((blk, width), lambda b, h, i: (b * nq + i, h)),
                  pl.BlockSpec((seq, width), lambda b, h, i: (b, kcol + h)),
                  pl.BlockSpec((seq, width), lambda b, h, i: (b, 2 * kcol + h))],
        out_specs=pl.BlockSpec((blk, width), lambda b, h, i: (b * nq + i, h)),
        out_shape=jax.ShapeDtypeStruct((T, DIFF_WIDTH), jnp.bfloat16),
        scratch_shapes=[pltpu.VMEM((nq, ATT_HEADS, V_DIM, blk), jnp.bfloat16)]
        + [pltpu.VMEM((V_DIM, blk), jnp.float32)] * (2 * ATT_HEADS),
        compiler_params=_params(("parallel", "parallel", "arbitrary")),
        name="diff_attn",
    )(*lams, g_subln, z, z, z)


def _pool_kernel(u_ref, w_ref, ps_ref, o_ref):
    g = pl.program_id(1)
    u = u_ref[...].astype(jnp.float32)
    t = lax.broadcasted_iota(jnp.int32, u.shape, 0)

    def shifted(x, k):
        return jnp.where(t >= k, pltpu.roll(x, k, 0), 0.0)

    s = u
    d = jnp.zeros_like(u)
    for gi, w in enumerate(POOL_WINDOWS):
        s = s + shifted(s, w // 2)
        cnt = jnp.minimum(t + 1, w).astype(jnp.float32)
        d = jnp.where(g == gi, s / cnt - u, d)
    y = jnp.dot(d.astype(jnp.bfloat16), w_ref[0].astype(jnp.bfloat16), preferred_element_type=jnp.float32)
    o_ref[...] = (y * ps_ref[...]).astype(o_ref.dtype)


def _pool(z, w_pool, pool_scale, batch, seq):
    T = z.shape[0]
    G = len(POOL_WINDOWS)
    ucol = 3 * DIFF_WIDTH // POOL_GROUP_DIM
    return pl.pallas_call(
        _pool_kernel,
        grid=(batch, G),
        in_specs=[pl.BlockSpec((seq, POOL_GROUP_DIM), lambda b, g: (b, ucol + g)),
                  pl.BlockSpec((1, POOL_GROUP_DIM, POOL_GROUP_DIM), lambda b, g: (g, 0, 0)),
                  pl.BlockSpec((1, POOL_GROUP_DIM), lambda b, g: (0, g))],
        out_specs=pl.BlockSpec((seq, POOL_GROUP_DIM), lambda b, g: (b, g)),
        out_shape=jax.ShapeDtypeStruct((T, POOL_WIDTH), jnp.bfloat16),
        compiler_params=_params(("parallel", "parallel")),
        name="pool",
    )(z, w_pool, pool_scale)


def _mix_kernel(x_ref, o_ref, p_ref, ga0, ga1, gb0, gb1, wa_ref, wb_ref, wo_ref, gpm_ref, gt1_ref,
                gpf_ref, sc2_ref, sh2_ref, wr_ref, br_ref,
                x1_ref, xl_ref, ls_ref, tw_ref, cnt_ref):
    tiles = range(x_ref.shape[0] // TOKEN_TILE)
    h2s = [_mix_front(t, x_ref, o_ref, p_ref, ga0, ga1, gb0, gb1, wa_ref, wb_ref, wo_ref, gpm_ref, gt1_ref,
                      gpf_ref, sc2_ref, sh2_ref, x1_ref) for t in tiles]
    places = [_mix_route(t, h2s[t], wr_ref, br_ref, ls_ref, tw_ref, cnt_ref) for t in tiles]
    for t in tiles:
        xl = jnp.dot(places[t], h2s[t].astype(jnp.bfloat16), preferred_element_type=jnp.float32)
        xl_ref[t * LOCAL_ROWS:(t + 1) * LOCAL_ROWS, :] = xl.astype(xl_ref.dtype)


def _mix_front(t, x_ref, o_ref, p_ref, ga0, ga1, gb0, gb1, wa_ref, wb_ref, wo_ref, gpm_ref, gt1_ref,
               gpf_ref, sc2_ref, sh2_ref, x1_ref):
    rows = slice(t * TOKEN_TILE, (t + 1) * TOKEN_TILE)
    ya = jnp.dot(o_ref[rows, :], wa_ref[...], preferred_element_type=jnp.float32)
    yb = jnp.dot(p_ref[rows, :], wb_ref[...], preferred_element_type=jnp.float32)
    ga = jnp.concatenate([ga0[rows, :], ga1[rows, :]], axis=1).astype(jnp.float32)
    gb = jnp.concatenate([gb0[rows, :], gb1[rows, :]], axis=1).astype(jnp.float32)
    merged = (ga * ya + gb * yb).astype(jnp.bfloat16)
    mixed = jnp.dot(merged, wo_ref[...], preferred_element_type=jnp.float32)
    x1 = x_ref[rows, :] + gt1_ref[0] * (_rms(mixed, NORM_EPS) * gpm_ref[...])
    x1_ref[rows, :] = x1
    return _rms(x1, NORM_EPS) * gpf_ref[...] * (1.0 + sc2_ref[0]) + sh2_ref[0]


def _mix_route(t, h2, wr_ref, br_ref, ls_ref, tw_ref, cnt_ref):
    tm = TOKEN_TILE
    rows = slice(t * tm, (t + 1) * tm)
    logits =lax.dot_general(wr_ref[...], h2, (((1,), (1,)), ((), ())), precision=_HI,
                             preferred_element_type=jnp.float32) + br_ref[...]
    eid = lax.broadcasted_iota(jnp.int32, logits.shape, 0)
    work = logits
    sels, vals = [], []
    for _ in range(TOP_K):
        mx = jnp.max(work, axis=0, keepdims=True)
        idx = jnp.min(jnp.where(work == mx, eid, N_EXPERTS), axis=0, keepdims=True)
        sel = eid == idx
        work = jnp.where(sel, -jnp.inf, work)
        sels.append(sel)
        vals.append(mx)
    ex = [jnp.exp(v - vals[0]) for v in vals]
    den = ex[0] + ex[1] + ex[2] + ex[3]
    onehot = jnp.zeros(logits.shape, jnp.float32)
    for sel in sels:
        onehot = jnp.where(sel, 1.0, onehot)
    r = lax.broadcasted_iota(jnp.int32, (tm, tm), 0)
    c = lax.broadcasted_iota(jnp.int32, (tm, tm), 1)
    tri = jnp.where(r < c, 1.0, 0.0).astype(jnp.bfloat16)
    rank = jnp.dot(onehot.astype(jnp.bfloat16), tri, preferred_element_type=jnp.float32)
    counts = jnp.broadcast_to(jnp.sum(onehot, axis=1, keepdims=True), (N_EXPERTS, LANES))
    rounded = jnp.floor((counts + (ROW_ALIGN - 1)) * (1.0 / ROW_ALIGN)) * ROW_ALIGN
    er = lax.broadcasted_iota(jnp.int32, (N_EXPERTS, N_EXPERTS), 0)
    ec = lax.broadcasted_iota(jnp.int32, (N_EXPERTS, N_EXPERTS), 1)
    below = jnp.where(ec < er, 1.0, 0.0).astype(jnp.bfloat16)
    offset = jnp.dot(below, rounded.astype(jnp.bfloat16), preferred_element_type=jnp.float32)
    slot_of = rank + offset[:, 0:1]
    cnt_ref[t] = rounded
    slot_id = lax.broadcasted_iota(jnp.int32, (LOCAL_ROWS, tm), 0)
    place = jnp.zeros((LOCAL_ROWS, tm), jnp.float32)
    for kk in range(TOP_K):
        ls = jnp.sum(jnp.where(sels[kk], slot_of, 0.0), axis=0, keepdims=True).astype(jnp.int32)
        ls_ref[kk:kk + 1, rows] = ls
        tw_ref[kk:kk + 1, rows] = ex[kk] / den
        place = jnp.where(slot_id == ls, 1.0, place)
    return place.astype(jnp.bfloat16)


def _mix(x2, o, p, z, wa, wb, wo, g_post_mix, mod3, g_pre_ffn, w_router_t, b_router, seq):
    T, D = x2.shape
    sub = min(MIX_TILES, seq // TOKEN_TILE)
    tm = sub * TOKEN_TILE
    per_b = seq // tm
    E = N_EXPERTS
    row = lambda n: pl.BlockSpec((1, n), lambda i: (0, 0))
    mod_spec = lambda col: pl.BlockSpec((1, 1, D), lambda i: (i // per_b, 0, col))
    gate = lambda cb: pl.BlockSpec((tm, COL_TILE), lambda i: (i, cb))
    full = lambda a: pl.BlockSpec(a.shape, lambda i: (0,) * a.ndim)
    tile = pl.BlockSpec((tm, D), lambda i: (i, 0))
    small = pl.BlockSpec((TOP_K, tm), lambda i: (0, i))
    return pl.pallas_call(
        _mix_kernel,
        grid=(T // tm,),
        in_specs=[tile, tile,
                  pl.BlockSpec((tm, POOL_WIDTH), lambda i: (i, 0)),
                  gate(COL_G), gate(COL_G + 1), gate(COL_G + 2), gate(COL_G + 3),
                  full(wa), full(wb), full(wo), row(D), mod_spec(2),
                  row(D), mod_spec(4), mod_spec(3), full(w_router_t),
                  pl.BlockSpec((E, 1), lambda i: (0, 0))],
        out_specs=[tile, pl.BlockSpec((sub * LOCAL_ROWS, D), lambda i: (i, 0)), small, small,
                   pl.BlockSpec((sub, E, LANES), lambda i: (i, 0, 0))],
        out_shape=[jax.ShapeDtypeStruct((T, D), jnp.float32),
                   jax.ShapeDtypeStruct((T // TOKEN_TILE * LOCAL_ROWS, D), jnp.bfloat16),
                   jax.ShapeDtypeStruct((TOP_K, T), jnp.int32),
                   jax.ShapeDtypeStruct((TOP_K, T), jnp.float32),
                   jax.ShapeDtypeStruct((T // TOKEN_TILE, E, LANES), jnp.float32)],
        compiler_params=_params(("parallel",)),
        name="mix_tail",
    )(x2, o, p, z, z, z, z, wa, wb, wo, g_post_mix, mod3, g_pre_ffn, mod3, mod3, w_router_t,
      b_router.reshape(E, 1))


def _split_copy(src_ref, src_row, dst_ref, dst_row, n, sem, act):
    off = 0
    for p in PIECE_SIZES:
        take = (n & p) != 0

        @pl.when(take)
        def _(off=off, p=p):
            src_at = 0 if src_row is None else pl.multiple_of(src_row + off, ROW_ALIGN)
            dst_at = pl.multiple_of(dst_row + off, ROW_ALIGN)
            act(pltpu.make_async_copy(src_ref.at[pl.ds(src_at, p), :], dst_ref.at[pl.ds(dst_at, p), :], sem))

        off = off + jnp.where(take, p, 0)


def _dispatch_kernel(local_off_ref, slot_off_ref, len_ref, zero_off_ref, zero_len_ref, ntile_ref,
                     xl_ref, xs_ref, zero_ref, sem, zsem, *, zero_per_step):
    i = pl.program_id(0)

    @pl.when(i == 0)
    def _():
        zero_ref[...] = jnp.zeros_like(zero_ref)

    def pieces(act):
        def body(e, _):
            idx = i * N_EXPERTS + e
            _split_copy(xl_ref, local_off_ref[idx], xs_ref, slot_off_ref[idx], len_ref[idx], sem, act)
            return 0
        lax.fori_loop(0, N_EXPERTS, body, 0)
        for r in range(zero_per_step):
            idx = i * zero_per_step + r
            _split_copy(zero_ref, None, xs_ref, zero_off_ref[idx], zero_len_ref[idx], zsem, act)

    def tails(act):
        def body(t, _):
            off = pl.multiple_of(t * MOE_TILE, MOE_TILE)
            act(pltpu.make_async_copy(zero_ref.at[pl.ds(0, MOE_TILE), :],
                                      xs_ref.at[pl.ds(off, MOE_TILE), :], zsem))
            return 0
        lax.fori_loop(ntile_ref[0], xs_ref.shape[0] // MOE_TILE, body, 0)

    for act in (lambda cp: cp.start(), lambda cp: cp.wait()):
        pieces(act)
        pl.when(i == 0)(functools.partial(tails, act))


def _dispatch(local_off, slot_off, lens, zero_off, zero_len, n_tiles, xs_local, n_slots):
    W = xs_local.shape[1]
    steps = lens.shape[0] // N_EXPERTS
    return pl.pallas_call(
        functools.partial(_dispatch_kernel, zero_per_step=zero_len.shape[0] // steps),
        grid_spec=pltpu.PrefetchScalarGridSpec(
            num_scalar_prefetch=6,
            grid=(steps,),
            in_specs=[pl.BlockSpec((LOCAL_ROWS, W), lambda i, *_: (i, 0))],
            out_specs=pl.BlockSpec(memory_space=pl.ANY),
            scratch_shapes=[pltpu.VMEM((PIECE_SIZES[0], W), xs_local.dtype),
                            pltpu.SemaphoreType.DMA, pltpu.SemaphoreType.DMA]),
        out_shape=jax.ShapeDtypeStruct((n_slots, W), xs_local.dtype),
        compiler_params=_params(("arbitrary",)),
        name="dispatch",
    )(local_off, slot_off, lens, zero_off, zero_len, n_tiles, xs_local)


def _moe_kernel(te_ref, nt_ref, x_ref, w1_ref, b1_ref, w2_ref, b2_ref, perm_ref, y_ref, w1s_ref, w2s_ref):
    j = pl.program_id(0)
    prev = te_ref[jnp.maximum(j - 1, 0)]
    fresh = (j == 0) | (te_ref[j] != prev)
    nblk = w1_ref.shape[2] // MXU_DIM

    @pl.when(fresh & (j < nt_ref[0]))
    def _():
        for b in range(nblk):
            cols = slice(b * MXU_DIM, (b + 1) * MXU_DIM)
            blk = w1_ref[0, :, cols].astype(jnp.bfloat16)
            w1s_ref[:, cols] = jnp.dot(blk, perm_ref[...],
                                       preferred_element_type=jnp.float32).astype(jnp.bfloat16)
        w2s_ref[...] = w2_ref[0].astype(jnp.bfloat16)

    @pl.when(j < nt_ref[0])
    def _():
        z = jnp.dot(x_ref[...], w1s_ref[...], preferred_element_type=jnp.float32) + b1_ref[0]
        acts = []
        for b in range(nblk):
            gate = jnp.minimum(z[:, b * MXU_DIM:b * MXU_DIM + LANES], SWIGLU_LIMIT)
            up = jnp.clip(z[:, b * MXU_DIM + LANES:(b + 1) * MXU_DIM], -SWIGLU_LIMIT, SWIGLU_LIMIT)
            acts.append(gate * jax.nn.sigmoid(SWIGLU_ALPHA * gate) * (up + 1.0))
        a = jnp.concatenate(acts, axis=1).astype(jnp.bfloat16)
        y = jnp.dot(a, w2s_ref[...], preferred_element_type=jnp.float32) + b2_ref[0]
        y_ref[...] = y.astype(y_ref.dtype)

    @pl.when(j >= nt_ref[0])
    def _():
        y_ref[...] = jnp.zeros_like(y_ref)


def _regroup_perm():
    src = np.arange(MXU_DIM)
    dst = np.where(src % 2 == 0, src // 2, LANES + src // 2)
    perm = np.zeros((MXU_DIM, MXU_DIM), np.float32)
    perm[src, dst] = 1.0
    return jnp.asarray(perm, jnp.bfloat16)


def _moe(tile_expert, n_tiles, xs, w1, b1g, w2, b2):
    n_slots, W = xs.shape
    tm = MOE_TILE
    E, D, F2 = w1.shape
    F = w2.shape[1]
    xmap = lambda j, te, nt: (jnp.minimum(j, nt[0] - 1), 0)
    emap = lambda j, te, nt: (te[j], 0, 0)
    return pl.pallas_call(
        _moe_kernel,
        grid_spec=pltpu.PrefetchScalarGridSpec(
            num_scalar_prefetch=2,
            grid=(n_slots // tm,),
            in_specs=[pl.BlockSpec((tm, W), xmap),
                      pl.BlockSpec((1, D, F2), emap),
                      pl.BlockSpec((1, 1, F2), emap),
                      pl.BlockSpec((1, F, D), emap),
                      pl.BlockSpec((1, 1, D), emap),
                      pl.BlockSpec((MXU_DIM, MXU_DIM), lambda j, te, nt: (0, 0))],
            out_specs=pl.BlockSpec((tm, W), lambda j, te, nt: (j, 0)),
            scratch_shapes=[pltpu.VMEM((D, F2), jnp.bfloat16), pltpu.VMEM((F, D), jnp.bfloat16)]),
        out_shape=jax.ShapeDtypeStruct((n_slots, W), xs.dtype),
        compiler_params=_params(("arbitrary",)),
        name="moe_experts",
    )(tile_expert, n_tiles, xs, w1, b1g, w2, b2.reshape(E, 1, D), _regroup_perm())


def _combine_kernel(local_off_ref, slot_off_ref, len_ref, ys_ref, ls_ref, w_ref, x1_ref, gt2_ref, g_ref,
                    o_ref, yl_ref, sem):
    i = pl.program_id(0)
    tc = x1_ref.shape[0]

    def pieces(tile, act):
        slot = tile % 2

        def body(e, _):
            idx = tile * N_EXPERTS + e
            _split_copy(ys_ref, slot_off_ref[idx], yl_ref.at[slot], local_off_ref[idx], len_ref[idx],
                        sem.at[slot], act)
            return 0
        lax.fori_loop(0, N_EXPERTS, body, 0)

    def fetch(tile):
        yl_ref[tile % 2, TOP_K * tc:, :] = jnp.zeros((LOCAL_ROWS - TOP_K * tc, yl_ref.shape[2]), yl_ref.dtype)
        pieces(tile, lambda cp: cp.start())

    pl.when(i == 0)(lambda: fetch(i))
    pl.when(i + 1 < pl.num_programs(0))(lambda: fetch(i + 1))
    slot_id = lax.broadcasted_iota(jnp.int32, (tc, LOCAL_ROWS), 1)
    ls = ls_ref[...]
    w = w_ref[...]
    mix = jnp.zeros((tc, LOCAL_ROWS), jnp.float32)
    for k in range(TOP_K):
        mix = jnp.where(slot_id == ls[:, k:k + 1], w[:, k:k + 1], mix)
    hi = mix.astype(jnp.bfloat16)
    lo = (mix - hi.astype(jnp.float32)).astype(jnp.bfloat16)
    pieces(i, lambda cp: cp.wait())
    y = yl_ref[i % 2]
    f = (jnp.dot(hi, y, preferred_element_type=jnp.float32)
         + jnp.dot(lo, y, preferred_element_type=jnp.float32))
    o_ref[...] = x1_ref[...] + gt2_ref[0] * (_rms(f, NORM_EPS) * g_ref[...])


def _combine(local_off, slot_off, lens, ys, ls_t, topw_t, x1, mod3, g_post_ffn, seq):
    T, D = x1.shape
    tc = TOKEN_TILE
    per_b = seq // tc
    tile = pl.BlockSpec((tc, D), lambda i, *_: (i, 0))
    small = pl.BlockSpec((tc, TOP_K), lambda i, *_: (i, 0))
    return pl.pallas_call(
        _combine_kernel,
        grid_spec=pltpu.PrefetchScalarGridSpec(
            num_scalar_prefetch=3,
            grid=(T // tc,),
            in_specs=[pl.BlockSpec(memory_space=pl.ANY), small, small, tile,
                      pl.BlockSpec((1, 1, D), lambda i, *_: (i // per_b, 0, 5)),
                      pl.BlockSpec((1, D), lambda i, *_: (0, 0))],
            out_specs=tile,
            scratch_shapes=[pltpu.VMEM((2, LOCAL_ROWS, ys.shape[1]), ys.dtype),
                            pltpu.SemaphoreType.DMA((2,))]),
        out_shape=jax.ShapeDtypeStruct((T, D), jnp.float32),
        compiler_params=_params(("arbitrary",)),
        name="combine",
    )(local_off, slot_off, lens, ys, ls_t, topw_t, x1, mod3, g_post_ffn)


def kernel(x, c, positions, w_ada, b_ada, g_pre_mix, w_in, lambda_q1, lambda_k1, lambda_q2, lambda_k2,
           g_subln, w_pool, pool_scale, w_proj_a, w_proj_b, w_out, g_post_mix, g_pre_ffn,
           w_router, b_router, w_exp1, b_exp1, w_exp2, b_exp2, g_post_ffn):
    B, S, D = x.shape
    T = B * S
    bf = jnp.bfloat16
    tabs = _rope_tables(positions)
    for l in range(w_ada.shape[0]):
        x2 = x.reshape(T, D)
        mod3 = _ada(c, w_ada[l], b_ada[l]).reshape(B, 1, N_MOD * D)
        n_qk = 2 * DIFF_WIDTH
        w_in_bf = jnp.concatenate([_permute_qk_columns(w_in[l][:, :n_qk]), w_in[l][:, n_qk:]], axis=1).astype(bf)
        z = _in_proj(x2, g_pre_mix[l][None], mod3, w_in_bf, tabs, S)
        lams = [v[l][None] for v in (lambda_q1, lambda_k1, lambda_q2, lambda_k2)]
        o = _diff_attention(z, lams, g_subln[l][None], B, S)
        p = _pool(z, w_pool[l], pool_scale[l][None], B, S)
        x1, xs_local, ls, topw, cnt = _mix(
            x2, o, p, z, w_proj_a[l].astype(bf), w_proj_b[l].astype(bf), w_out[l].astype(bf),
            g_post_mix[l][None], mod3, g_pre_ffn[l][None], w_router[l].T, b_router[l], S)

        i32 = jnp.int32
        n = cnt[:, :, 0].astype(i32)
        n_tok_tiles = n.shape[0]
        counts = jnp.sum(n, axis=0)
        padded = (counts + MOE_TILE - 1) // MOE_TILE * MOE_TILE
        gend = jnp.cumsum(padded).astype(i32)
        gstart = gend - padded
        local_off = (jnp.cumsum(n, axis=1) - n).reshape(-1).astype(i32)
        slot_off = (gstart[None, :] + jnp.cumsum(n, axis=0) - n).reshape(-1).astype(i32)
        lens = n.reshape(-1)
        steps_pad = -N_EXPERTS % n_tok_tiles
        group_pad_off = jnp.pad(gstart + counts, (0, steps_pad)).astype(i32)
        group_pad_len = jnp.pad(padded - counts, (0, steps_pad)).astype(i32)
        n_slots = -(-(T * TOP_K + n_tok_tiles * N_EXPERTS * ROW_ALIGN) // MOE_TILE) * MOE_TILE + N_EXPERTS * MOE_TILE
        tile_start = jnp.arange(n_slots // MOE_TILE, dtype=i32) * MOE_TILE
        tile_expert = jnp.minimum(jnp.sum(tile_start[:, None] >= gend[None, :], axis=1), N_EXPERTS - 1)
        n_tiles = gend[-1:] // MOE_TILE

        xs = _dispatch(local_off, slot_off, lens, group_pad_off, group_pad_len, n_tiles, xs_local, n_slots)
        half = D_FF
        b1g = b_exp1[l].reshape(N_EXPERTS, half // LANES, LANES, 2).transpose(0, 1, 3, 2)
        b1g = b1g.reshape(N_EXPERTS, 1, 2 * half)
        ys = _moe(tile_expert.astype(i32), n_tiles, xs, w_exp1[l], b1g, w_exp2[l], b_exp2[l])
        x = _combine(local_off, slot_off, lens, ys, ls.T, topw.T, x1, mod3, g_post_ffn[l][None], S)
        x = x.reshape(B, S, D)
    return x
```

```python
import functools
import math

import numpy as np
import jax
import jax.numpy as jnp
from jax import lax
from jax.experimental import pallas as pl
from jax.experimental.pallas import tpu as pltpu

D_MODEL = 1024
N_HEADS = 8
HEAD_DIM = 64
V_DIM = 2 * HEAD_DIM
DIFF_WIDTH = N_HEADS * V_DIM
POOL_WINDOWS = (2, 4, 8, 16)
POOL_GROUP_DIM = 128
POOL_WIDTH = len(POOL_WINDOWS) * POOL_GROUP_DIM
IN_COLS = 3 * DIFF_WIDTH + POOL_WIDTH + 2 * D_MODEL
ROPE_THETA = 500000.0
ROT_DIM = HEAD_DIM // 4
ROT_HALF = ROT_DIM // 2
N_EXPERTS = 32
TOP_K = 4
D_FF = D_MODEL
SWIGLU_ALPHA = 1.702
SWIGLU_LIMIT = 7.0
NORM_EPS = 1e-6
SUBLN_EPS = 1e-5
N_MOD = 6
NEG_BIG = -1e30
LAMBDA_INIT = 0.8 - 0.6 * math.exp(-0.3 * 0)

LANES = 128
MXU_DIM = 256
VMEM_LIMIT = 56 * 1024 * 1024

COL_TILE = 512
IN_ROW_TILE = 2048
IN_ROW_CHUNK = 256
COL_K = DIFF_WIDTH // COL_TILE
COL_V = 2 * DIFF_WIDTH // COL_TILE
COL_G = (3 * DIFF_WIDTH + POOL_WIDTH) // COL_TILE
ATT_BLOCK = 512
ATT_HEADS = 2
MOE_TILE = 256
TOKEN_TILE = 256
MIX_TILES = 2
ROW_ALIGN = 16
LOCAL_ROWS = TOP_K * TOKEN_TILE + N_EXPERTS * ROW_ALIGN
PIECE_SIZES = tuple(1 << b for b in range(max(TOKEN_TILE, MOE_TILE).bit_length() - 1,
                                          ROW_ALIGN.bit_length() - 2, -1))

_HI = lax.Precision.HIGHEST


def _params(sem, vmem=VMEM_LIMIT):
    return pltpu.CompilerParams(dimension_semantics=sem, vmem_limit_bytes=vmem)


def _rms(x, eps):
    return x * lax.rsqrt(jnp.mean(x * x, axis=-1, keepdims=True) + eps)


def _ada_kernel(c_ref, w_ref, b_ref, o_ref):
    c = c_ref[...]
    s = c * jax.nn.sigmoid(c)
    o_ref[...] = jnp.dot(s, w_ref[...], precision=_HI, preferred_element_type=jnp.float32) + b_ref[...]


def _ada(c, w_ada, b_ada):
    B, D = c.shape
    N = w_ada.shape[1]
    tn = 1536
    return pl.pallas_call(
        _ada_kernel,
        grid=(N // tn,),
        in_specs=[pl.BlockSpec((B, D), lambda j: (0, 0)),
                  pl.BlockSpec((D, tn), lambda j: (0, j)),
                  pl.BlockSpec((1, tn), lambda j: (0, j))],
        out_specs=pl.BlockSpec((B, tn), lambda j: (0, j)),
        out_shape=jax.ShapeDtypeStruct((B, N), jnp.float32),
        compiler_params=_params(("parallel",)),
        name="ada",
    )(c, w_ada, b_ada.reshape(1, N))


def _rope_tab_kernel(pos_ref, invf_ref, phase_ref, c_ref, s_ref):
    pos = pos_ref[...].astype(jnp.float32)
    cs = jnp.cos(pos * invf_ref[...] - phase_ref[...])
    rolled = pltpu.roll(cs, LANES // 2, 1)
    lane = lax.broadcasted_iota(jnp.int32, cs.shape, 1)
    lower = lane < LANES // 2
    rot = lane % (LANES // 2) < ROT_DIM
    c_ref[...] = jnp.where(lower, cs, rolled)
    s_ref[...] = jnp.where(rot, jnp.where(lower, -rolled, cs), 0.0)


def _rope_tables(positions):
    T = positions.size
    tm = min(T, 2048)
    lane = np.arange(LANES)
    rot = lane % (LANES // 2) < ROT_DIM
    inv = ROPE_THETA ** (-(np.arange(ROT_HALF, dtype=np.float64) * 2.0 / ROT_DIM))
    invf = np.where(rot, inv[lane % ROT_HALF], 0.0).astype(np.float32)[None, :]
    phase = np.where(rot & (lane >= LANES // 2), np.pi / 2, 0.0).astype(np.float32)[None, :]
    row = pl.BlockSpec((1, LANES), lambda i: (0, 0))
    tab = pl.BlockSpec((tm, LANES), lambda i: (i, 0))
    sds = jax.ShapeDtypeStruct((T, LANES), jnp.float32)
    return pl.pallas_call(
        _rope_tab_kernel,
        grid=(T // tm,),
        in_specs=[pl.BlockSpec((tm, 1), lambda i: (i, 0)), row, row],
        out_specs=[tab, tab],
        out_shape=[sds, sds],
        compiler_params=_params(("parallel",)),
        name="rope_tables",
    )(positions.reshape(T, 1), jnp.asarray(invf), jnp.asarray(phase))


def _permute_qk_columns(w):
    D, n = w.shape
    w = w.reshape(D, n // V_DIM, 2, 4, 2, ROT_HALF)
    return w.transpose(0, 1, 4, 3, 2, 5).reshape(D, n)


def _in_proj_kernel(x_ref, g_ref, sc_ref, sh_ref, w_ref, c_ref, s_ref, z_ref, h_ref):
    j = pl.program_id(1)
    tm = x_ref.shape[0]
    rows = min(tm, IN_ROW_CHUNK)

    @pl.when(j == 0)
    def _():
        h = _rms(x_ref[...], NORM_EPS) * g_ref[...] * (1.0 + sc_ref[0]) + sh_ref[0]
        h_ref[...] = h.astype(h_ref.dtype)

    def chunks(epilogue):
        for r in range(tm // rows):
            sl = slice(r * rows, (r + 1) * rows)
            z = jnp.dot(h_ref[sl, :], w_ref[...], preferred_element_type=jnp.float32)
            z_ref[sl, :] = epilogue(z, sl).astype(z_ref.dtype)

    @pl.when(j < COL_V)
    def _():
        scale = jnp.where(j < COL_K, HEAD_DIM ** -0.5 * math.log2(math.e), 1.0)

        def rope(z, sl):
            c = c_ref[sl, :] * scale
            s = s_ref[sl, :] * scale
            parts = []
            for g in range(COL_TILE // LANES):
                zg = z[:, g * LANES:(g + 1) * LANES]
                parts.append(zg * c + pltpu.roll(zg, LANES // 2, 1) * s)
            return jnp.concatenate(parts, axis=1)

        chunks(rope)

    @pl.when((j >= COL_V) & (j < COL_G))
    def _():
        chunks(lambda z, sl: z)

    @pl.when(j >= COL_G)
    def _():
        chunks(lambda z, sl: 0.5 * jnp.tanh(0.5 * z) + 0.5)


def _in_proj(x2, g_pre, mod3, w_in_bf, tabs, seq):
    T, D = x2.shape
    tm = min(seq, IN_ROW_TILE)
    per_b = seq // tm
    mod_spec = lambda col: pl.BlockSpec((1, 1, D), lambda i, j: (i // per_b, 0, col))
    tab = pl.BlockSpec((tm, LANES), lambda i, j: (i, 0))
    return pl.pallas_call(
        _in_proj_kernel,
        grid=(T // tm, IN_COLS // COL_TILE),
        in_specs=[pl.BlockSpec((tm, D), lambda i, j: (i, 0)),
                  pl.BlockSpec((1, D), lambda i, j: (0, 0)),
                  mod_spec(1), mod_spec(0),
                  pl.BlockSpec((D, COL_TILE), lambda i, j: (0, j)),
                  tab, tab],
        out_specs=pl.BlockSpec((tm, COL_TILE), lambda i, j: (i, j)),
        out_shape=jax.ShapeDtypeStruct((T, IN_COLS), jnp.bfloat16),
        scratch_shapes=[pltpu.VMEM((tm, D), jnp.bfloat16)],
        compiler_params=_params(("parallel", "arbitrary")),
        name="in_proj",
    )(x2, g_pre, mod3, mod3, w_in_bf, *tabs)


def _attn_kernel(lq1_ref, lk1_ref, lq2_ref, lk2_ref, gs_ref, q_ref, k_ref, v_ref, o_ref, vt_ref, *acc_refs):
    i = pl.program_id(2)
    blk = q_ref.shape[0]
    nblk = v_ref.shape[0] // blk
    chains = [(h, comp) for h in range(ATT_HEADS) for comp in range(2)]

    @pl.when(i == 0)
    def _():
        for c in range(nblk):
            for h in range(ATT_HEADS):
                v = v_ref[c * blk:(c + 1) * blk, h * V_DIM:(h + 1) * V_DIM]
                vt_ref[c, h] = v.astype(jnp.float32).T.astype(vt_ref.dtype)

    lam = (jnp.exp(jnp.sum(lq1_ref[...] * lk1_ref[...], axis=-1, keepdims=True))
           - jnp.exp(jnp.sum(lq2_ref[...] * lk2_ref[...], axis=-1, keepdims=True))
           + LAMBDA_INIT)
    lane_comp = lax.broadcasted_iota(jnp.int32, (blk, V_DIM), 1) // ROT_HALF % 2
    qs = []
    for h, comp in chains:
        q = q_ref[:, h * V_DIM:(h + 1) * V_DIM]
        qs.append(jnp.where(lane_comp == comp, q, jnp.zeros_like(q)))
    nt = (((1,), (1,)), ((), ()))
    for acc_ref in acc_refs:
        acc_ref[...] = jnp.zeros_like(acc_ref)

    def step(c, carry, mask):
        off = pl.multiple_of(c * blk, blk)
        scores = []
        for n, (h, comp) in enumerate(chains):
            k = k_ref[pl.ds(off, blk), h * V_DIM:(h + 1) * V_DIM]
            scores.append(lax.dot_general(k, qs[n], nt, preferred_element_type=jnp.float32))
        out, probs, alphas = [], [], []
        for n, s in enumerate(scores):
            m, l = carry[n]
            if mask is not None:
                s = jnp.where(mask, s, NEG_BIG)
            m_new = jnp.maximum(m, jnp.max(s, axis=0, keepdims=True))
            alpha = jnp.exp2(m - m_new)
            p = jnp.exp2(s - m_new)
            out.append((m_new, alpha * l + jnp.sum(p, axis=0, keepdims=True)))
            probs.append(p.astype(vt_ref.dtype))
            alphas.append(alpha)
        for n, (h, comp) in enumerate(chains):
            pv = jnp.dot(vt_ref[c, h], probs[n], preferred_element_type=jnp.float32)
            acc_refs[n][...] = alphas[n] * acc_refs[n][...] + pv
        return tuple(out)

    init = tuple((jnp.full((1, blk), NEG_BIG, jnp.float32), jnp.zeros((1, blk), jnp.float32))
                 for _ in chains)
    carry = lax.fori_loop(0, i, lambda c, carry: step(c, carry, None), init)
    key = lax.broadcasted_iota(jnp.int32, (blk, blk), 0)
    qry = lax.broadcasted_iota(jnp.int32, (blk, blk), 1)
    carry = step(i, carry, key <= qry)
    for h in range(ATT_HEADS):
        (_, l1), (_, l2) = carry[2 * h], carry[2 * h + 1]
        ot = acc_refs[2 * h][...] / l1 - lam * (acc_refs[2 * h + 1][...] / l2)
        o = _rms(ot.T, SUBLN_EPS) * gs_ref[...] * (1.0 - LAMBDA_INIT)
        o_ref[:, h * V_DIM:(h + 1) * V_DIM] = o.astype(o_ref.dtype)


def _diff_attention(z, lams, g_subln, batch, seq):
    T = z.shape[0]
    blk = min(ATT_BLOCK, seq)
    nq = seq // blk
    width = ATT_HEADS * V_DIM
    vec = pl.BlockSpec((1, HEAD_DIM), lambda b, h, i: (0, 0))
    kcol = DIFF_WIDTH // width
    return pl.pallas_call(
        _attn_kernel,
        grid=(batch, N_HEADS // ATT_HEADS, nq),
        in_specs=[vec, vec, vec, vec,
                  pl.BlockSpec((1, V_DIM), lambda b, h, i: (0, 0)),
                  pl.BlockSpec((blk, width), lambda b, h, i: (b * nq + i, h)),
                  pl.BlockSpec((seq, width), lambda b, h, i: (b, kcol + h)),
                  pl.BlockSpec((seq, width), lambda b, h, i: (b, 2 * kcol + h))],
        out_specs=pl.BlockSpec((blk, width), lambda b, h, i: (b * nq + i, h)),
        out_shape=jax.ShapeDtypeStruct((T, DIFF_WIDTH), jnp.bfloat16),
        scratch_shapes=[pltpu.VMEM((nq, ATT_HEADS, V_DIM, blk), jnp.bfloat16)]
        + [pltpu.VMEM((V_DIM, blk), jnp.float32)] * (2 * ATT_HEADS),
        compiler_params=_params(("parallel", "parallel", "arbitrary")),
        name="diff_attn",
    )(*lams, g_subln, z, z, z)


def _pool_kernel(u_ref, w_ref, ps_ref, o_ref):
    g = pl.program_id(1)
    u = u_ref[...].astype(jnp.float32)
    t = lax.broadcasted_iota(jnp.int32, u.shape, 0)

    def shifted(x, k):
        return jnp.where(t >= k, pltpu.roll(x, k, 0), 0.0)

    s = u
    d = jnp.zeros_like(u)
    for gi, w in enumerate(POOL_WINDOWS):
        s = s + shifted(s, w // 2)
        cnt = jnp.minimum(t + 1, w).astype(jnp.float32)
        d = jnp.where(g == gi, s / cnt - u, d)
    y = jnp.dot(d.astype(jnp.bfloat16), w_ref[0].astype(jnp.bfloat16), preferred_element_type=jnp.float32)
    o_ref[...] = (y * ps_ref[...]).astype(o_ref.dtype)


def _pool(z, w_pool, pool_scale, batch, seq):
    T = z.shape[0]
    G = len(POOL_WINDOWS)
    ucol = 3 * DIFF_WIDTH // POOL_GROUP_DIM
    return pl.pallas_call(
        _pool_kernel,
        grid=(batch, G),
        in_specs=[pl.BlockSpec((seq, POOL_GROUP_DIM), lambda b, g: (b, ucol + g)),
                  pl.BlockSpec((1, POOL_GROUP_DIM, POOL_GROUP_DIM), lambda b, g: (g, 0, 0)),
                  pl.BlockSpec((1, POOL_GROUP_DIM), lambda b, g: (0, g))],
        out_specs=pl.BlockSpec((seq, POOL_GROUP_DIM), lambda b, g: (b, g)),
        out_shape=jax.ShapeDtypeStruct((T, POOL_WIDTH), jnp.bfloat16),
        compiler_params=_params(("parallel", "parallel")),
        name="pool",
    )(z, w_pool, pool_scale)


def _mix_kernel(x_ref, o_ref, p_ref, ga0, ga1, gb0, gb1, wa_ref, wb_ref, wo_ref, gpm_ref, gt1_ref,
                gpf_ref, sc2_ref, sh2_ref, wr_ref, br_ref,
                x1_ref, xl_ref, ls_ref, tw_ref, cnt_ref):
    tiles = range(x_ref.shape[0] // TOKEN_TILE)
    h2s = [_mix_front(t, x_ref, o_ref, p_ref, ga0, ga1, gb0, gb1, wa_ref, wb_ref, wo_ref, gpm_ref, gt1_ref,
                      gpf_ref, sc2_ref, sh2_ref, x1_ref) for t in tiles]
    places = [_mix_route(t, h2s[t], wr_ref, br_ref, ls_ref, tw_ref, cnt_ref) for t in tiles]
    for t in tiles:
        xl = jnp.dot(places[t], h2s[t].astype(jnp.bfloat16), preferred_element_type=jnp.float32)
        xl_ref[t * LOCAL_ROWS:(t + 1) * LOCAL_ROWS, :] = xl.astype(xl_ref.dtype)


def _mix_front(t, x_ref, o_ref, p_ref, ga0, ga1, gb0, gb1, wa_ref, wb_ref, wo_ref, gpm_ref, gt1_ref,
               gpf_ref, sc2_ref, sh2_ref, x1_ref):
    rows = slice(t * TOKEN_TILE, (t + 1) * TOKEN_TILE)
    ya = jnp.dot(o_ref[rows, :], wa_ref[...], preferred_element_type=jnp.float32)
    yb = jnp.dot(p_ref[rows, :], wb_ref[...], preferred_element_type=jnp.float32)
    ga = jnp.concatenate([ga0[rows, :], ga1[rows, :]], axis=1).astype(jnp.float32)
    gb = jnp.concatenate([gb0[rows, :], gb1[rows, :]], axis=1).astype(jnp.float32)
    merged = (ga * ya + gb * yb).astype(jnp.bfloat16)
    mixed = jnp.dot(merged, wo_ref[...], preferred_element_type=jnp.float32)
    x1 = x_ref[rows, :] + gt1_ref[0] * (_rms(mixed, NORM_EPS) * gpm_ref[...])
    x1_ref[rows, :] = x1
    return _rms(x1, NORM_EPS) * gpf_ref[...] * (1.0 + sc2_ref[0]) + sh2_ref[0]


def _mix_route(t, h2, wr_ref, br_ref, ls_ref, tw_ref, cnt_ref):
    tm = TOKEN_TILE
    rows = slice(t * tm, (t + 1) * tm)
    logits =lax.dot_general(wr_ref[...], h2, (((1,), (1,)), ((), ())), precision=_HI,
                             preferred_element_type=jnp.float32) + br_ref[...]
    eid = lax.broadcasted_iota(jnp.int32, logits.shape, 0)
    work = logits
    sels, vals = [], []
    for _ in range(TOP_K):
        mx = jnp.max(work, axis=0, keepdims=True)
        idx = jnp.min(jnp.where(work == mx, eid, N_EXPERTS), axis=0, keepdims=True)
        sel = eid == idx
        work = jnp.where(sel, -jnp.inf, work)
        sels.append(sel)
        vals.append(mx)
    ex = [jnp.exp(v - vals[0]) for v in vals]
    den = ex[0] + ex[1] + ex[2] + ex[3]
    onehot = jnp.zeros(logits.shape, jnp.float32)
    for sel in sels:
        onehot = jnp.where(sel, 1.0, onehot)
    r = lax.broadcasted_iota(jnp.int32, (tm, tm), 0)
    c = lax.broadcasted_iota(jnp.int32, (tm, tm), 1)
    tri = jnp.where(r < c, 1.0, 0.0).astype(jnp.bfloat16)
    rank = jnp.dot(onehot.astype(jnp.bfloat16), tri, preferred_element_type=jnp.float32)
    counts = jnp.broadcast_to(jnp.sum(onehot, axis=1, keepdims=True), (N_EXPERTS, LANES))
    rounded = jnp.floor((counts + (ROW_ALIGN - 1)) * (1.0 / ROW_ALIGN)) * ROW_ALIGN
    er = lax.broadcasted_iota(jnp.int32, (N_EXPERTS, N_EXPERTS), 0)
    ec = lax.broadcasted_iota(jnp.int32, (N_EXPERTS, N_EXPERTS), 1)
    below = jnp.where(ec < er, 1.0, 0.0).astype(jnp.bfloat16)
    offset = jnp.dot(below, rounded.astype(jnp.bfloat16), preferred_element_type=jnp.float32)
    slot_of = rank + offset[:, 0:1]
    cnt_ref[t] = rounded
    slot_id = lax.broadcasted_iota(jnp.int32, (LOCAL_ROWS, tm), 0)
    place = jnp.zeros((LOCAL_ROWS, tm), jnp.float32)
    for kk in range(TOP_K):
        ls = jnp.sum(jnp.where(sels[kk], slot_of, 0.0), axis=0, keepdims=True).astype(jnp.int32)
        ls_ref[kk:kk + 1, rows] = ls
        tw_ref[kk:kk + 1, rows] = ex[kk] / den
        place = jnp.where(slot_id == ls, 1.0, place)
    return place.astype(jnp.bfloat16)


def _mix(x2, o, p, z, wa, wb, wo, g_post_mix, mod3, g_pre_ffn, w_router_t, b_router, seq):
    T, D = x2.shape
    sub = min(MIX_TILES, seq // TOKEN_TILE)
    tm = sub * TOKEN_TILE
    per_b = seq // tm
    E = N_EXPERTS
    row = lambda n: pl.BlockSpec((1, n), lambda i: (0, 0))
    mod_spec = lambda col: pl.BlockSpec((1, 1, D), lambda i: (i // per_b, 0, col))
    gate = lambda cb: pl.BlockSpec((tm, COL_TILE), lambda i: (i, cb))
    full = lambda a: pl.BlockSpec(a.shape, lambda i: (0,) * a.ndim)
    tile = pl.BlockSpec((tm, D), lambda i: (i, 0))
    small = pl.BlockSpec((TOP_K, tm), lambda i: (0, i))
    return pl.pallas_call(
        _mix_kernel,
        grid=(T // tm,),
        in_specs=[tile, tile,
                  pl.BlockSpec((tm, POOL_WIDTH), lambda i: (i, 0)),
                  gate(COL_G), gate(COL_G + 1), gate(COL_G + 2), gate(COL_G + 3),
                  full(wa), full(wb), full(wo), row(D), mod_spec(2),
                  row(D), mod_spec(4), mod_spec(3), full(w_router_t),
                  pl.BlockSpec((E, 1), lambda i: (0, 0))],
        out_specs=[tile, pl.BlockSpec((sub * LOCAL_ROWS, D), lambda i: (i, 0)), small, small,
                   pl.BlockSpec((sub, E, LANES), lambda i: (i, 0, 0))],
        out_shape=[jax.ShapeDtypeStruct((T, D), jnp.float32),
                   jax.ShapeDtypeStruct((T // TOKEN_TILE * LOCAL_ROWS, D), jnp.bfloat16),
                   jax.ShapeDtypeStruct((TOP_K, T), jnp.int32),
                   jax.ShapeDtypeStruct((TOP_K, T), jnp.float32),
                   jax.ShapeDtypeStruct((T // TOKEN_TILE, E, LANES), jnp.float32)],
        compiler_params=_params(("parallel",)),
        name="mix_tail",
    )(x2, o, p, z, z, z, z, wa, wb, wo, g_post_mix, mod3, g_pre_ffn, mod3, mod3, w_router_t,
      b_router.reshape(E, 1))


def _split_copy(src_ref, src_row, dst_ref, dst_row, n, sem, act):
    off = 0
    for p in PIECE_SIZES:
        take = (n & p) != 0

        @pl.when(take)
        def _(off=off, p=p):
            src_at = 0 if src_row is None else pl.multiple_of(src_row + off, ROW_ALIGN)
            dst_at = pl.multiple_of(dst_row + off, ROW_ALIGN)
            act(pltpu.make_async_copy(src_ref.at[pl.ds(src_at, p), :], dst_ref.at[pl.ds(dst_at, p), :], sem))

        off = off + jnp.where(take, p, 0)


def _dispatch_kernel(local_off_ref, slot_off_ref, len_ref, zero_off_ref, zero_len_ref, ntile_ref,
                     xl_ref, xs_ref, zero_ref, sem, zsem, *, zero_per_step):
    i = pl.program_id(0)

    @pl.when(i == 0)
    def _():
        zero_ref[...] = jnp.zeros_like(zero_ref)

    def pieces(act):
        def body(e, _):
            idx = i * N_EXPERTS + e
            _split_copy(xl_ref, local_off_ref[idx], xs_ref, slot_off_ref[idx], len_ref[idx], sem, act)
            return 0
        lax.fori_loop(0, N_EXPERTS, body, 0)
        for r in range(zero_per_step):
            idx = i * zero_per_step + r
            _split_copy(zero_ref, None, xs_ref, zero_off_ref[idx], zero_len_ref[idx], zsem, act)

    def tails(act):
        def body(t, _):
            off = pl.multiple_of(t * MOE_TILE, MOE_TILE)
            act(pltpu.make_async_copy(zero_ref.at[pl.ds(0, MOE_TILE), :],
                                      xs_ref.at[pl.ds(off, MOE_TILE), :], zsem))
            return 0
        lax.fori_loop(ntile_ref[0], xs_ref.shape[0] // MOE_TILE, body, 0)

    for act in (lambda cp: cp.start(), lambda cp: cp.wait()):
        pieces(act)
        pl.when(i == 0)(functools.partial(tails, act))


def _dispatch(local_off, slot_off, lens, zero_off, zero_len, n_tiles, xs_local, n_slots):
    W = xs_local.shape[1]
    steps = lens.shape[0] // N_EXPERTS
    return pl.pallas_call(
        functools.partial(_dispatch_kernel, zero_per_step=zero_len.shape[0] // steps),
        grid_spec=pltpu.PrefetchScalarGridSpec(
            num_scalar_prefetch=6,
            grid=(steps,),
            in_specs=[pl.BlockSpec((LOCAL_ROWS, W), lambda i, *_: (i, 0))],
            out_specs=pl.BlockSpec(memory_space=pl.ANY),
            scratch_shapes=[pltpu.VMEM((PIECE_SIZES[0], W), xs_local.dtype),
                            pltpu.SemaphoreType.DMA, pltpu.SemaphoreType.DMA]),
        out_shape=jax.ShapeDtypeStruct((n_slots, W), xs_local.dtype),
        compiler_params=_params(("arbitrary",)),
        name="dispatch",
    )(local_off, slot_off, lens, zero_off, zero_len, n_tiles, xs_local)


def _moe_kernel(te_ref, nt_ref, buf_ref, nxt_ref, x_ref, w1_hbm, b1_ref, w2_hbm, b2_ref, perm_ref, y_ref,
                w1raw_ref, w2raw_ref, w1s_ref, w2s_ref, sem):
    j = pl.program_id(0)
    prev = te_ref[jnp.maximum(j - 1, 0)]
    fresh = (j == 0) | (te_ref[j] != prev)
    nblk = w1s_ref.shape[1] // MXU_DIM

    def weight_copies(e, b):
        return (pltpu.make_async_copy(w1_hbm.at[e], w1raw_ref.at[b], sem.at[0, b]),
                pltpu.make_async_copy(w2_hbm.at[e], w2raw_ref.at[b], sem.at[1, b]))

    @pl.when(fresh & (j < nt_ref[0]))
    def _():
        e, b, nxt = te_ref[j], buf_ref[j], nxt_ref[j]

        @pl.when(j == 0)
        def _():
            for cp in weight_copies(e, b):
                cp.start()

        for cp in weight_copies(e, b):
            cp.wait()

        @pl.when(nxt >= 0)
        def _():
            for cp in weight_copies(nxt, 1 - b):
                cp.start()

        for c in range(nblk):
            cols = slice(c * MXU_DIM, (c + 1) * MXU_DIM)
            blk = w1raw_ref[b, :, cols].astype(jnp.bfloat16)
            w1s_ref[:, cols] = jnp.dot(blk, perm_ref[...],
                                       preferred_element_type=jnp.float32).astype(jnp.bfloat16)
        w2s_ref[...] = w2raw_ref[b].astype(jnp.bfloat16)

    @pl.when(j < nt_ref[0])
    def _():
        z = jnp.dot(x_ref[...], w1s_ref[...], preferred_element_type=jnp.float32) + b1_ref[0]
        acts = []
        for b in range(nblk):
            gate = jnp.minimum(z[:, b * MXU_DIM:b * MXU_DIM + LANES], SWIGLU_LIMIT)
            up = jnp.clip(z[:, b * MXU_DIM + LANES:(b + 1) * MXU_DIM], -SWIGLU_LIMIT, SWIGLU_LIMIT)
            acts.append(gate * jax.nn.sigmoid(SWIGLU_ALPHA * gate) * (up + 1.0))
        a = jnp.concatenate(acts, axis=1).astype(jnp.bfloat16)
        y = jnp.dot(a, w2s_ref[...], preferred_element_type=jnp.float32) + b2_ref[0]
        y_ref[...] = y.astype(y_ref.dtype)

    @pl.when(j >= nt_ref[0])
    def _():
        y_ref[...] = jnp.zeros_like(y_ref)


def _regroup_perm():
    src = np.arange(MXU_DIM)
    dst = np.where(src % 2 == 0, src // 2, LANES + src // 2)
    perm = np.zeros((MXU_DIM, MXU_DIM), np.float32)
    perm[src, dst] = 1.0
    return jnp.asarray(perm, jnp.bfloat16)


def _moe(tile_expert, n_tiles, tile_buf, tile_next, xs, w1, b1g, w2, b2):
    n_slots, W = xs.shape
    tm = MOE_TILE
    E, D, F2 = w1.shape
    F = w2.shape[1]
    xmap = lambda j, te, nt, *_: (jnp.minimum(j, nt[0] - 1), 0)
    emap = lambda j, te, *_: (te[j], 0, 0)
    return pl.pallas_call(
        _moe_kernel,
        grid_spec=pltpu.PrefetchScalarGridSpec(
            num_scalar_prefetch=4,
            grid=(n_slots // tm,),
            in_specs=[pl.BlockSpec((tm, W), xmap),
                      pl.BlockSpec(memory_space=pl.ANY),
                      pl.BlockSpec((1, 1, F2), emap),
                      pl.BlockSpec(memory_space=pl.ANY),
                      pl.BlockSpec((1, 1, D), emap),
                      pl.BlockSpec((MXU_DIM, MXU_DIM), lambda j, *_: (0, 0))],
            out_specs=pl.BlockSpec((tm, W), lambda j, *_: (j, 0)),
            scratch_shapes=[pltpu.VMEM((2, D, F2), w1.dtype), pltpu.VMEM((2, F, D), w2.dtype),
                            pltpu.VMEM((D, F2), jnp.bfloat16), pltpu.VMEM((F, D), jnp.bfloat16),
                            pltpu.SemaphoreType.DMA((2, 2))]),
        out_shape=jax.ShapeDtypeStruct((n_slots, W), xs.dtype),
        compiler_params=_params(("arbitrary",)),
        name="moe_experts",
    )(tile_expert, n_tiles, tile_buf, tile_next, xs, w1, b1g, w2, b2.reshape(E, 1, D), _regroup_perm())


def _combine_kernel(local_off_ref, slot_off_ref, len_ref, ys_ref, ls_ref, w_ref, x1_ref, gt2_ref, g_ref,
                    o_ref, yl_ref, sem):
    i = pl.program_id(0)
    tc = x1_ref.shape[0]

    def pieces(tile, act):
        slot = tile % 2

        def body(e, _):
            idx = tile * N_EXPERTS + e
            _split_copy(ys_ref, slot_off_ref[idx], yl_ref.at[slot], local_off_ref[idx], len_ref[idx],
                        sem.at[slot], act)
            return 0
        lax.fori_loop(0, N_EXPERTS, body, 0)

    def fetch(tile):
        yl_ref[tile % 2, TOP_K * tc:, :] = jnp.zeros((LOCAL_ROWS - TOP_K * tc, yl_ref.shape[2]), yl_ref.dtype)
        pieces(tile, lambda cp: cp.start())

    pl.when(i == 0)(lambda: fetch(i))
    pl.when(i + 1 < pl.num_programs(0))(lambda: fetch(i + 1))
    slot_id = lax.broadcasted_iota(jnp.int32, (tc, LOCAL_ROWS), 1)
    ls = ls_ref[...]
    w = w_ref[...]
    mix = jnp.zeros((tc, LOCAL_ROWS), jnp.float32)
    for k in range(TOP_K):
        mix = jnp.where(slot_id == ls[:, k:k + 1], w[:, k:k + 1], mix)
    hi = mix.astype(jnp.bfloat16)
    lo = (mix - hi.astype(jnp.float32)).astype(jnp.bfloat16)
    pieces(i, lambda cp: cp.wait())
    y = yl_ref[i % 2]
    f = (jnp.dot(hi, y, preferred_element_type=jnp.float32)
         + jnp.dot(lo, y, preferred_element_type=jnp.float32))
    o_ref[...] = x1_ref[...] + gt2_ref[0] * (_rms(f, NORM_EPS) * g_ref[...])


def _combine(local_off, slot_off, lens, ys, ls_t, topw_t, x1, mod3, g_post_ffn, seq):
    T, D = x1.shape
    tc = TOKEN_TILE
    per_b = seq // tc
    tile = pl.BlockSpec((tc, D), lambda i, *_: (i, 0))
    small = pl.BlockSpec((tc, TOP_K), lambda i, *_: (i, 0))
    return pl.pallas_call(
        _combine_kernel,
        grid_spec=pltpu.PrefetchScalarGridSpec(
            num_scalar_prefetch=3,
            grid=(T // tc,),
            in_specs=[pl.BlockSpec(memory_space=pl.ANY), small, small, tile,
                      pl.BlockSpec((1, 1, D), lambda i, *_: (i // per_b, 0, 5)),
                      pl.BlockSpec((1, D), lambda i, *_: (0, 0))],
            out_specs=tile,
            scratch_shapes=[pltpu.VMEM((2, LOCAL_ROWS, ys.shape[1]), ys.dtype),
                            pltpu.SemaphoreType.DMA((2,))]),
        out_shape=jax.ShapeDtypeStruct((T, D), jnp.float32),
        compiler_params=_params(("arbitrary",)),
        name="combine",
    )(local_off, slot_off, lens, ys, ls_t, topw_t, x1, mod3, g_post_ffn)


def kernel(x, c, positions, w_ada, b_ada, g_pre_mix, w_in, lambda_q1, lambda_k1, lambda_q2, lambda_k2,
           g_subln, w_pool, pool_scale, w_proj_a, w_proj_b, w_out, g_post_mix, g_pre_ffn,
           w_router, b_router, w_exp1, b_exp1, w_exp2, b_exp2, g_post_ffn):
    B, S, D = x.shape
    T = B * S
    bf = jnp.bfloat16
    tabs = _rope_tables(positions)
    for l in range(w_ada.shape[0]):
        x2 = x.reshape(T, D)
        mod3 = _ada(c, w_ada[l], b_ada[l]).reshape(B, 1, N_MOD * D)
        n_qk = 2 * DIFF_WIDTH
        w_in_bf = jnp.concatenate([_permute_qk_columns(w_in[l][:, :n_qk]), w_in[l][:, n_qk:]], axis=1).astype(bf)
        z = _in_proj(x2, g_pre_mix[l][None], mod3, w_in_bf, tabs, S)
        lams = [v[l][None] for v in (lambda_q1, lambda_k1, lambda_q2, lambda_k2)]
        o = _diff_attention(z, lams, g_subln[l][None], B, S)
        p = _pool(z, w_pool[l], pool_scale[l][None], B, S)
        x1, xs_local, ls, topw, cnt = _mix(
            x2, o, p, z, w_proj_a[l].astype(bf), w_proj_b[l].astype(bf), w_out[l].astype(bf),
            g_post_mix[l][None], mod3, g_pre_ffn[l][None], w_router[l].T, b_router[l], S)

        i32 = jnp.int32
        n = cnt[:, :, 0].astype(i32)
        n_tok_tiles = n.shape[0]
        counts = jnp.sum(n, axis=0)
        padded = (counts + MOE_TILE - 1) // MOE_TILE * MOE_TILE
        gend = jnp.cumsum(padded).astype(i32)
        gstart = gend - padded
        local_off = (jnp.cumsum(n, axis=1) - n).reshape(-1).astype(i32)
        slot_off = (gstart[None, :] + jnp.cumsum(n, axis=0) - n).reshape(-1).astype(i32)
        lens = n.reshape(-1)
        steps_pad = -N_EXPERTS % n_tok_tiles
        group_pad_off = jnp.pad(gstart + counts, (0, steps_pad)).astype(i32)
        group_pad_len = jnp.pad(padded - counts, (0, steps_pad)).astype(i32)
        n_slots = -(-(T * TOP_K + n_tok_tiles * N_EXPERTS * ROW_ALIGN) // MOE_TILE) * MOE_TILE + N_EXPERTS * MOE_TILE
        tile_start = jnp.arange(n_slots // MOE_TILE, dtype=i32) * MOE_TILE
        tile_expert = jnp.minimum(jnp.sum(tile_start[:, None] >= gend[None, :], axis=1), N_EXPERTS - 1)
        n_tiles = gend[-1:] // MOE_TILE
        has_tiles = padded > 0
        expert_ids = jnp.arange(N_EXPERTS, dtype=i32)
        later = lax.cummin(jnp.where(has_tiles, expert_ids, N_EXPERTS), reverse=True)
        next_expert = jnp.concatenate([later[1:], jnp.full((1,), N_EXPERTS, i32)])
        next_expert = jnp.where(next_expert < N_EXPERTS, next_expert, -1)
        tile_buf = ((jnp.cumsum(has_tiles.astype(i32)) - 1) % 2)[tile_expert].astype(i32)
        tile_next = next_expert[tile_expert].astype(i32)

        xs = _dispatch(local_off, slot_off, lens, group_pad_off, group_pad_len, n_tiles, xs_local, n_slots)
        half = D_FF
        b1g = b_exp1[l].reshape(N_EXPERTS, half // LANES, LANES, 2).transpose(0, 1, 3, 2)
        b1g = b1g.reshape(N_EXPERTS, 1, 2 * half)
        ys = _moe(tile_expert.astype(i32), n_tiles, tile_buf, tile_next, xs, w_exp1[l], b1g, w_exp2[l], b_exp2[l])
        x = _combine(local_off, slot_off, lens, ys, ls.T, topw.T, x1, mod3, g_post_ffn[l][None], S)
        x = x.reshape(B, S, D)
    return x
```

```python
import functools
import math

import numpy as np
import jax
import jax.numpy as jnp
from jax import lax
from jax.experimental import pallas as pl
from jax.experimental.pallas import tpu as pltpu

D_MODEL = 1024
N_HEADS = 8
HEAD_DIM = 64
V_DIM = 2 * HEAD_DIM
DIFF_WIDTH = N_HEADS * V_DIM
POOL_WINDOWS = (2, 4, 8, 16)
POOL_GROUP_DIM = 128
POOL_WIDTH = len(POOL_WINDOWS) * POOL_GROUP_DIM
IN_COLS = 3 * DIFF_WIDTH + POOL_WIDTH + 2 * D_MODEL
ROPE_THETA = 500000.0
ROT_DIM = HEAD_DIM // 4
ROT_HALF = ROT_DIM // 2
N_EXPERTS = 32
TOP_K = 4
D_FF = D_MODEL
SWIGLU_ALPHA = 1.702
SWIGLU_LIMIT = 7.0
NORM_EPS = 1e-6
SUBLN_EPS = 1e-5
N_MOD = 6
NEG_BIG = -1e30
LAMBDA_INIT = 0.8 - 0.6 * math.exp(-0.3 * 0)

LANES = 128
MXU_DIM = 256
VMEM_LIMIT = 56 * 1024 * 1024

COL_TILE = 512
IN_ROW_TILE = 2048
IN_ROW_CHUNK = 256
COL_K = DIFF_WIDTH // COL_TILE
COL_V = 2 * DIFF_WIDTH // COL_TILE
COL_G = (3 * DIFF_WIDTH + POOL_WIDTH) // COL_TILE
ATT_BLOCK = 512
ATT_HEADS = 4
MOE_TILE = 512
TOKEN_TILE = 256
MIX_TILES = 2
ROW_ALIGN = 16
LOCAL_ROWS = TOP_K * TOKEN_TILE + N_EXPERTS * ROW_ALIGN
PIECE_SIZES = tuple(1 << b for b in range(max(TOKEN_TILE, MOE_TILE).bit_length() - 1,
                                          ROW_ALIGN.bit_length() - 2, -1))

_HI = lax.Precision.HIGHEST


def _params(sem, vmem=VMEM_LIMIT):
    return pltpu.CompilerParams(dimension_semantics=sem, vmem_limit_bytes=vmem)


def _rms(x, eps):
    return x * lax.rsqrt(jnp.mean(x * x, axis=-1, keepdims=True) + eps)


def _ada_kernel(c_ref, w_ref, b_ref, o_ref):
    c = c_ref[...]
    s = c * jax.nn.sigmoid(c)
    o_ref[...] = jnp.dot(s, w_ref[...], precision=_HI, preferred_element_type=jnp.float32) + b_ref[...]


def _ada(c, w_ada, b_ada):
    B, D = c.shape
    N = w_ada.shape[1]
    tn = 1536
    return pl.pallas_call(
        _ada_kernel,
        grid=(N // tn,),
        in_specs=[pl.BlockSpec((B, D), lambda j: (0, 0)),
                  pl.BlockSpec((D, tn), lambda j: (0, j)),
                  pl.BlockSpec((1, tn), lambda j: (0, j))],
        out_specs=pl.BlockSpec((B, tn), lambda j: (0, j)),
        out_shape=jax.ShapeDtypeStruct((B, N), jnp.float32),
        compiler_params=_params(("parallel",)),
        name="ada",
    )(c, w_ada, b_ada.reshape(1, N))


def _rope_tab_kernel(pos_ref, invf_ref, phase_ref, c_ref, s_ref):
    pos = pos_ref[...].astype(jnp.float32)
    cs = jnp.cos(pos * invf_ref[...] - phase_ref[...])
    rolled = pltpu.roll(cs, LANES // 2, 1)
    lane = lax.broadcasted_iota(jnp.int32, cs.shape, 1)
    lower = lane < LANES // 2
    rot = lane % (LANES // 2) < ROT_DIM
    c_ref[...] = jnp.where(lower, cs, rolled)
    s_ref[...] = jnp.where(rot, jnp.where(lower, -rolled, cs), 0.0)


def _rope_tables(positions):
    T = positions.size
    tm = min(T, 2048)
    lane = np.arange(LANES)
    rot = lane % (LANES // 2) < ROT_DIM
    inv = ROPE_THETA ** (-(np.arange(ROT_HALF, dtype=np.float64) * 2.0 / ROT_DIM))
    invf = np.where(rot, inv[lane % ROT_HALF], 0.0).astype(np.float32)[None, :]
    phase = np.where(rot & (lane >= LANES // 2), np.pi / 2, 0.0).astype(np.float32)[None, :]
    row = pl.BlockSpec((1, LANES), lambda i: (0, 0))
    tab = pl.BlockSpec((tm, LANES), lambda i: (i, 0))
    sds = jax.ShapeDtypeStruct((T, LANES), jnp.float32)
    return pl.pallas_call(
        _rope_tab_kernel,
        grid=(T // tm,),
        in_specs=[pl.BlockSpec((tm, 1), lambda i: (i, 0)), row, row],
        out_specs=[tab, tab],
        out_shape=[sds, sds],
        compiler_params=_params(("parallel",)),
        name="rope_tables",
    )(positions.reshape(T, 1), jnp.asarray(invf), jnp.asarray(phase))


def _permute_qk_columns(w):
    D, n = w.shape
    w = w.reshape(D, n // V_DIM, 2, 4, 2, ROT_HALF)
    return w.transpose(0, 1, 4, 3, 2, 5).reshape(D, n)


def _in_proj_kernel(x_ref, g_ref, sc_ref, sh_ref, w_ref, c_ref, s_ref, z_ref, h_ref):
    j = pl.program_id(1)
    tm = x_ref.shape[0]
    rows = min(tm, IN_ROW_CHUNK)

    @pl.when(j == 0)
    def _():
        h = _rms(x_ref[...], NORM_EPS) * g_ref[...] * (1.0 + sc_ref[0]) + sh_ref[0]
        h_ref[...] = h.astype(h_ref.dtype)

    def chunks(epilogue):
        for r in range(tm // rows):
            sl = slice(r * rows, (r + 1) * rows)
            z = jnp.dot(h_ref[sl, :], w_ref[...], preferred_element_type=jnp.float32)
            z_ref[sl, :] = epilogue(z, sl).astype(z_ref.dtype)

    @pl.when(j < COL_V)
    def _():
        scale = jnp.where(j < COL_K, HEAD_DIM ** -0.5 * math.log2(math.e), 1.0)

        def rope(z, sl):
            c = c_ref[sl, :] * scale
            s = s_ref[sl, :] * scale
            parts = []
            for g in range(COL_TILE // LANES):
                zg = z[:, g * LANES:(g + 1) * LANES]
                parts.append(zg * c + pltpu.roll(zg, LANES // 2, 1) * s)
            return jnp.concatenate(parts, axis=1)

        chunks(rope)

    @pl.when((j >= COL_V) & (j < COL_G))
    def _():
        chunks(lambda z, sl: z)

    @pl.when(j >= COL_G)
    def _():
        chunks(lambda z, sl: 0.5 * jnp.tanh(0.5 * z) + 0.5)


def _in_proj(x2, g_pre, mod3, w_in_bf, tabs, seq):
    T, D = x2.shape
    tm = min(seq, IN_ROW_TILE)
    per_b = seq // tm
    mod_spec = lambda col: pl.BlockSpec((1, 1, D), lambda i, j: (i // per_b, 0, col))
    tab = pl.BlockSpec((tm, LANES), lambda i, j: (i, 0))
    return pl.pallas_call(
        _in_proj_kernel,
        grid=(T // tm, IN_COLS // COL_TILE),
        in_specs=[pl.BlockSpec((tm, D), lambda i, j: (i, 0)),
                  pl.BlockSpec((1, D), lambda i, j: (0, 0)),
                  mod_spec(1), mod_spec(0),
                  pl.BlockSpec((D, COL_TILE), lambda i, j: (0, j)),
                  tab, tab],
        out_specs=pl.BlockSpec((tm, COL_TILE), lambda i, j: (i, j)),
        out_shape=jax.ShapeDtypeStruct((T, IN_COLS), jnp.bfloat16),
        scratch_shapes=[pltpu.VMEM((tm, D), jnp.bfloat16)],
        compiler_params=_params(("parallel", "arbitrary")),
        name="in_proj",
    )(x2, g_pre, mod3, mod3, w_in_bf, *tabs)


def _attn_kernel(lq1_ref, lk1_ref, lq2_ref, lk2_ref, gs_ref, q_ref, k_ref, v_ref, o_ref, vt_ref, *acc_refs):
    i = pl.program_id(2)
    blk = q_ref.shape[0]
    nblk = v_ref.shape[0] // blk
    chains = [(h, comp) for h in range(ATT_HEADS) for comp in range(2)]

    @pl.when(i == 0)
    def _():
        for c in range(nblk):
            for h in range(ATT_HEADS):
                v = v_ref[c * blk:(c + 1) * blk, h * V_DIM:(h + 1) * V_DIM]
                vt_ref[c, h] = v.astype(jnp.float32).T.astype(vt_ref.dtype)

    lam = (jnp.exp(jnp.sum(lq1_ref[...] * lk1_ref[...], axis=-1, keepdims=True))
           - jnp.exp(jnp.sum(lq2_ref[...] * lk2_ref[...], axis=-1, keepdims=True))
           + LAMBDA_INIT)
    lane_comp = lax.broadcasted_iota(jnp.int32, (blk, V_DIM), 1) // ROT_HALF % 2
    qs = []
    for h, comp in chains:
        q = q_ref[:, h * V_DIM:(h + 1) * V_DIM]
        qs.append(jnp.where(lane_comp == comp, q, jnp.zeros_like(q)))
    nt = (((1,), (1,)), ((), ()))
    for acc_ref in acc_refs:
        acc_ref[...] = jnp.zeros_like(acc_ref)

    def step(c, carry, mask):
        off = pl.multiple_of(c * blk, blk)
        scores = []
        for n, (h, comp) in enumerate(chains):
            k = k_ref[pl.ds(off, blk), h * V_DIM:(h + 1) * V_DIM]
            scores.append(lax.dot_general(k, qs[n], nt, preferred_element_type=jnp.float32))
        out, probs, alphas = [], [], []
        for n, s in enumerate(scores):
            m, l = carry[n]
            if mask is not None:
                s = jnp.where(mask, s, NEG_BIG)
            m_new = jnp.maximum(m, jnp.max(s, axis=0, keepdims=True))
            alpha = jnp.exp2(m - m_new)
            p = jnp.exp2(s - m_new)
            out.append((m_new, alpha * l + jnp.sum(p, axis=0, keepdims=True)))
            probs.append(p.astype(vt_ref.dtype))
            alphas.append(alpha)
        for n, (h, comp) in enumerate(chains):
            pv = jnp.dot(vt_ref[c, h], probs[n], preferred_element_type=jnp.float32)
            acc_refs[n][...] = alphas[n] * acc_refs[n][...] + pv
        return tuple(out)

    init = tuple((jnp.full((1, blk), NEG_BIG, jnp.float32), jnp.zeros((1, blk), jnp.float32))
                 for _ in chains)
    carry = lax.fori_loop(0, i, lambda c, carry: step(c, carry, None), init)
    key = lax.broadcasted_iota(jnp.int32, (blk, blk), 0)
    qry = lax.broadcasted_iota(jnp.int32, (blk, blk), 1)
    carry = step(i, carry, key <= qry)
    for h in range(ATT_HEADS):
        (_, l1), (_, l2) = carry[2 * h], carry[2 * h + 1]
        ot = acc_refs[2 * h][...] / l1 - lam * (acc_refs[2 * h + 1][...] / l2)
        o = _rms(ot.T, SUBLN_EPS) * gs_ref[...] * (1.0 - LAMBDA_INIT)
        o_ref[:, h * V_DIM:(h + 1) * V_DIM] = o.astype(o_ref.dtype)


def _diff_attention(z, lams, g_subln, batch, seq):
    T = z.shape[0]
    blk = min(ATT_BLOCK, seq)
    nq = seq // blk
    width = ATT_HEADS * V_DIM
    vec = pl.BlockSpec((1, HEAD_DIM), lambda b, h, i: (0, 0))
    kcol = DIFF_WIDTH // width
    return pl.pallas_call(
        _attn_kernel,
        grid=(batch, N_HEADS // ATT_HEADS, nq),
        in_specs=[vec, vec, vec, vec,
                  pl.BlockSpec((1, V_DIM), lambda b, h, i: (0, 0)),
                  pl.BlockSpec((blk, width), lambda b, h, i: (b * nq + i, h)),
                  pl.BlockSpec((seq, width), lambda b, h, i: (b, kcol + h)),
                  pl.BlockSpec((seq, width), lambda b, h, i: (b, 2 * kcol + h))],
        out_specs=pl.BlockSpec((blk, width), lambda b, h, i: (b * nq + i, h)),
        out_shape=jax.ShapeDtypeStruct((T, DIFF_WIDTH), jnp.bfloat16),
        scratch_shapes=[pltpu.VMEM((nq, ATT_HEADS, V_DIM, blk), jnp.bfloat16)]
        + [pltpu.VMEM((V_DIM, blk), jnp.float32)] * (2 * ATT_HEADS),
        compiler_params=_params(("parallel", "parallel", "arbitrary")),
        name="diff_attn",
    )(*lams, g_subln, z, z, z)


def _pool_kernel(u_ref, w_ref, ps_ref, o_ref):
    g = pl.program_id(1)
    u = u_ref[...].astype(jnp.float32)
    t = lax.broadcasted_iota(jnp.int32, u.shape, 0)

    def shifted(x, k):
        return jnp.where(t >= k, pltpu.roll(x, k, 0), 0.0)

    s = u
    d = jnp.zeros_like(u)
    for gi, w in enumerate(POOL_WINDOWS):
        s = s + shifted(s, w // 2)
        cnt = jnp.minimum(t + 1, w).astype(jnp.float32)
        d = jnp.where(g == gi, s / cnt - u, d)
    y = jnp.dot(d.astype(jnp.bfloat16), w_ref[0].astype(jnp.bfloat16), preferred_element_type=jnp.float32)
    o_ref[...] = (y * ps_ref[...]).astype(o_ref.dtype)


def _pool(z, w_pool, pool_scale, batch, seq):
    T = z.shape[0]
    G = len(POOL_WINDOWS)
    ucol = 3 * DIFF_WIDTH // POOL_GROUP_DIM
    return pl.pallas_call(
        _pool_kernel,
        grid=(batch, G),
        in_specs=[pl.BlockSpec((seq, POOL_GROUP_DIM), lambda b, g: (b, ucol + g)),
                  pl.BlockSpec((1, POOL_GROUP_DIM, POOL_GROUP_DIM), lambda b, g: (g, 0, 0)),
                  pl.BlockSpec((1, POOL_GROUP_DIM), lambda b, g: (0, g))],
        out_specs=pl.BlockSpec((seq, POOL_GROUP_DIM), lambda b, g: (b, g)),
        out_shape=jax.ShapeDtypeStruct((T, POOL_WIDTH), jnp.bfloat16),
        compiler_params=_params(("parallel", "parallel")),
        name="pool",
    )(z, w_pool, pool_scale)


def _mix_kernel(x_ref, o_ref, p_ref, ga0, ga1, gb0, gb1, wa_ref, wb_ref, wo_ref, gpm_ref, gt1_ref,
                gpf_ref, sc2_ref, sh2_ref, wr_ref, br_ref,
                x1_ref, xl_ref, ls_ref, tw_ref, cnt_ref):
    tiles = range(x_ref.shape[0] // TOKEN_TILE)
    h2s = [_mix_front(t, x_ref, o_ref, p_ref, ga0, ga1, gb0, gb1, wa_ref, wb_ref, wo_ref, gpm_ref, gt1_ref,
                      gpf_ref, sc2_ref, sh2_ref, x1_ref) for t in tiles]
    places = [_mix_route(t, h2s[t], wr_ref, br_ref, ls_ref, tw_ref, cnt_ref) for t in tiles]
    for t in tiles:
        xl = jnp.dot(places[t], h2s[t].astype(jnp.bfloat16), preferred_element_type=jnp.float32)
        xl_ref[t * LOCAL_ROWS:(t + 1) * LOCAL_ROWS, :] = xl.astype(xl_ref.dtype)


def _mix_front(t, x_ref, o_ref, p_ref, ga0, ga1, gb0, gb1, wa_ref, wb_ref, wo_ref, gpm_ref, gt1_ref,
               gpf_ref, sc2_ref, sh2_ref, x1_ref):
    rows = slice(t * TOKEN_TILE, (t + 1) * TOKEN_TILE)
    ya = jnp.dot(o_ref[rows, :], wa_ref[...], preferred_element_type=jnp.float32)
    yb = jnp.dot(p_ref[rows, :], wb_ref[...], preferred_element_type=jnp.float32)
    ga = jnp.concatenate([ga0[rows, :], ga1[rows, :]], axis=1).astype(jnp.float32)
    gb = jnp.concatenate([gb0[rows, :], gb1[rows, :]], axis=1).astype(jnp.float32)
    merged = (ga * ya + gb * yb).astype(jnp.bfloat16)
    mixed = jnp.dot(merged, wo_ref[...], preferred_element_type=jnp.float32)
    x1 = x_ref[rows, :] + gt1_ref[0] * (_rms(mixed, NORM_EPS) * gpm_ref[...])
    x1_ref[rows, :] = x1
    return _rms(x1, NORM_EPS) * gpf_ref[...] * (1.0 + sc2_ref[0]) + sh2_ref[0]


def _mix_route(t, h2, wr_ref, br_ref, ls_ref, tw_ref, cnt_ref):
    tm = TOKEN_TILE
    rows = slice(t * tm, (t + 1) * tm)
    logits =lax.dot_general(wr_ref[...], h2, (((1,), (1,)), ((), ())), precision=_HI,
                             preferred_element_type=jnp.float32) + br_ref[...]
    eid = lax.broadcasted_iota(jnp.int32, logits.shape, 0)
    work = logits
    sels, vals = [], []
    for _ in range(TOP_K):
        mx = jnp.max(work, axis=0, keepdims=True)
        idx = jnp.min(jnp.where(work == mx, eid, N_EXPERTS), axis=0, keepdims=True)
        sel = eid == idx
        work = jnp.where(sel, -jnp.inf, work)
        sels.append(sel)
        vals.append(mx)
    ex = [jnp.exp(v - vals[0]) for v in vals]
    den = ex[0] + ex[1] + ex[2] + ex[3]
    onehot = jnp.zeros(logits.shape, jnp.float32)
    for sel in sels:
        onehot = jnp.where(sel, 1.0, onehot)
    r = lax.broadcasted_iota(jnp.int32, (tm, tm), 0)
    c = lax.broadcasted_iota(jnp.int32, (tm, tm), 1)
    tri = jnp.where(r < c, 1.0, 0.0).astype(jnp.bfloat16)
    rank = jnp.dot(onehot.astype(jnp.bfloat16), tri, preferred_element_type=jnp.float32)
    counts = jnp.broadcast_to(jnp.sum(onehot, axis=1, keepdims=True), (N_EXPERTS, LANES))
    rounded = jnp.floor((counts + (ROW_ALIGN - 1)) * (1.0 / ROW_ALIGN)) * ROW_ALIGN
    er = lax.broadcasted_iota(jnp.int32, (N_EXPERTS, N_EXPERTS), 0)
    ec = lax.broadcasted_iota(jnp.int32, (N_EXPERTS, N_EXPERTS), 1)
    below = jnp.where(ec < er, 1.0, 0.0).astype(jnp.bfloat16)
    offset = jnp.dot(below, rounded.astype(jnp.bfloat16), preferred_element_type=jnp.float32)
    slot_of = rank + offset[:, 0:1]
    cnt_ref[t] = rounded
    slot_id = lax.broadcasted_iota(jnp.int32, (LOCAL_ROWS, tm), 0)
    place = jnp.zeros((LOCAL_ROWS, tm), jnp.float32)
    for kk in range(TOP_K):
        ls = jnp.sum(jnp.where(sels[kk], slot_of, 0.0), axis=0, keepdims=True).astype(jnp.int32)
        ls_ref[kk:kk + 1, rows] = ls
        tw_ref[kk:kk + 1, rows] = ex[kk] / den
        place = jnp.where(slot_id == ls, 1.0, place)
    return place.astype(jnp.bfloat16)


def _mix(x2, o, p, z, wa, wb, wo, g_post_mix, mod3, g_pre_ffn, w_router_t, b_router, seq):
    T, D = x2.shape
    sub = min(MIX_TILES, seq // TOKEN_TILE)
    tm = sub * TOKEN_TILE
    per_b = seq // tm
    E = N_EXPERTS
    row = lambda n: pl.BlockSpec((1, n), lambda i: (0, 0))
    mod_spec = lambda col: pl.BlockSpec((1, 1, D), lambda i: (i // per_b, 0, col))
    gate = lambda cb: pl.BlockSpec((tm, COL_TILE), lambda i: (i, cb))
    full = lambda a: pl.BlockSpec(a.shape, lambda i: (0,) * a.ndim)
    tile = pl.BlockSpec((tm, D), lambda i: (i, 0))
    small = pl.BlockSpec((TOP_K, tm), lambda i: (0, i))
    return pl.pallas_call(
        _mix_kernel,
        grid=(T // tm,),
        in_specs=[tile, tile,
                  pl.BlockSpec((tm, POOL_WIDTH), lambda i: (i, 0)),
                  gate(COL_G), gate(COL_G + 1), gate(COL_G + 2), gate(COL_G + 3),
                  full(wa), full(wb), full(wo), row(D), mod_spec(2),
                  row(D), mod_spec(4), mod_spec(3), full(w_router_t),
                  pl.BlockSpec((E, 1), lambda i: (0, 0))],
        out_specs=[tile, pl.BlockSpec((sub * LOCAL_ROWS, D), lambda i: (i, 0)), small, small,
                   pl.BlockSpec((sub, E, LANES), lambda i: (i, 0, 0))],
        out_shape=[jax.ShapeDtypeStruct((T, D), jnp.float32),
                   jax.ShapeDtypeStruct((T // TOKEN_TILE * LOCAL_ROWS, D), jnp.bfloat16),
                   jax.ShapeDtypeStruct((TOP_K, T), jnp.int32),
                   jax.ShapeDtypeStruct((TOP_K, T), jnp.float32),
                   jax.ShapeDtypeStruct((T // TOKEN_TILE, E, LANES), jnp.float32)],
        compiler_params=_params(("parallel",)),
        name="mix_tail",
    )(x2, o, p, z, z, z, z, wa, wb, wo, g_post_mix, mod3, g_pre_ffn, mod3, mod3, w_router_t,
      b_router.reshape(E, 1))


def _split_copy(src_ref, src_row, dst_ref, dst_row, n, sem, act):
    off = 0
    for p in PIECE_SIZES:
        take = (n & p) != 0

        @pl.when(take)
        def _(off=off, p=p):
            src_at = 0 if src_row is None else pl.multiple_of(src_row + off, ROW_ALIGN)
            dst_at = pl.multiple_of(dst_row + off, ROW_ALIGN)
            act(pltpu.make_async_copy(src_ref.at[pl.ds(src_at, p), :], dst_ref.at[pl.ds(dst_at, p), :], sem))

        off = off + jnp.where(take, p, 0)


def _dispatch_kernel(local_off_ref, slot_off_ref, len_ref, zero_off_ref, zero_len_ref, ntile_ref,
                     xl_ref, xs_ref, zero_ref, sem, zsem, *, zero_per_step):
    i = pl.program_id(0)

    @pl.when(i == 0)
    def _():
        zero_ref[...] = jnp.zeros_like(zero_ref)

    def pieces(act):
        def body(e, _):
            idx = i * N_EXPERTS + e
            _split_copy(xl_ref, local_off_ref[idx], xs_ref, slot_off_ref[idx], len_ref[idx], sem, act)
            return 0
        lax.fori_loop(0, N_EXPERTS, body, 0)
        for r in range(zero_per_step):
            idx = i * zero_per_step + r
            _split_copy(zero_ref, None, xs_ref, zero_off_ref[idx], zero_len_ref[idx], zsem, act)

    def tails(act):
        def body(t, _):
            off = pl.multiple_of(t * MOE_TILE, MOE_TILE)
            act(pltpu.make_async_copy(zero_ref.at[pl.ds(0, MOE_TILE), :],
                                      xs_ref.at[pl.ds(off, MOE_TILE), :], zsem))
            return 0
        lax.fori_loop(ntile_ref[0], xs_ref.shape[0] // MOE_TILE, body, 0)

    for act in (lambda cp: cp.start(), lambda cp: cp.wait()):
        pieces(act)
        pl.when(i == 0)(functools.partial(tails, act))


def _dispatch(local_off, slot_off, lens, zero_off, zero_len, n_tiles, xs_local, n_slots):
    W = xs_local.shape[1]
    steps = lens.shape[0] // N_EXPERTS
    return pl.pallas_call(
        functools.partial(_dispatch_kernel, zero_per_step=zero_len.shape[0] // steps),
        grid_spec=pltpu.PrefetchScalarGridSpec(
            num_scalar_prefetch=6,
            grid=(steps,),
            in_specs=[pl.BlockSpec((LOCAL_ROWS, W), lambda i, *_: (i, 0))],
            out_specs=pl.BlockSpec(memory_space=pl.ANY),
            scratch_shapes=[pltpu.VMEM((PIECE_SIZES[0], W), xs_local.dtype),
                            pltpu.SemaphoreType.DMA, pltpu.SemaphoreType.DMA]),
        out_shape=jax.ShapeDtypeStruct((n_slots, W), xs_local.dtype),
        compiler_params=_params(("arbitrary",)),
        name="dispatch",
    )(local_off, slot_off, lens, zero_off, zero_len, n_tiles, xs_local)


def _moe_kernel(te_ref, nt_ref, buf_ref, nxt_ref, x_ref, w1_hbm, b1_ref, w2_hbm, b2_ref, perm_ref, y_ref,
                w1raw_ref, w2raw_ref, w1s_ref, w2s_ref, sem):
    j = pl.program_id(0)
    prev = te_ref[jnp.maximum(j - 1, 0)]
    fresh = (j == 0) | (te_ref[j] != prev)
    nblk = w1s_ref.shape[1] // MXU_DIM

    def weight_copies(e, b):
        return (pltpu.make_async_copy(w1_hbm.at[e], w1raw_ref.at[b], sem.at[0, b]),
                pltpu.make_async_copy(w2_hbm.at[e], w2raw_ref.at[b], sem.at[1, b]))

    @pl.when(fresh & (j < nt_ref[0]))
    def _():
        e, b, nxt = te_ref[j], buf_ref[j], nxt_ref[j]

        @pl.when(j == 0)
        def _():
            for cp in weight_copies(e, b):
                cp.start()

        for cp in weight_copies(e, b):
            cp.wait()

        @pl.when(nxt >= 0)
        def _():
            for cp in weight_copies(nxt, 1 - b):
                cp.start()

        for c in range(nblk):
            cols = slice(c * MXU_DIM, (c + 1) * MXU_DIM)
            blk = w1raw_ref[b, :, cols].astype(jnp.bfloat16)
            w1s_ref[:, cols] = jnp.dot(blk, perm_ref[...],
                                       preferred_element_type=jnp.float32).astype(jnp.bfloat16)
        w2s_ref[...] = w2raw_ref[b].astype(jnp.bfloat16)

    @pl.when(j < nt_ref[0])
    def _():
        z = jnp.dot(x_ref[...], w1s_ref[...], preferred_element_type=jnp.float32) + b1_ref[0]
        acts = []
        for b in range(nblk):
            gate = jnp.minimum(z[:, b * MXU_DIM:b * MXU_DIM + LANES], SWIGLU_LIMIT)
            up = jnp.clip(z[:, b * MXU_DIM + LANES:(b + 1) * MXU_DIM], -SWIGLU_LIMIT, SWIGLU_LIMIT)
            acts.append(gate * jax.nn.sigmoid(SWIGLU_ALPHA * gate) * (up + 1.0))
        a = jnp.concatenate(acts, axis=1).astype(jnp.bfloat16)
        y = jnp.dot(a, w2s_ref[...], preferred_element_type=jnp.float32) + b2_ref[0]
        y_ref[...] = y.astype(y_ref.dtype)

    @pl.when(j >= nt_ref[0])
    def _():
        y_ref[...] = jnp.zeros_like(y_ref)


def _regroup_perm():
    src = np.arange(MXU_DIM)
    dst = np.where(src % 2 == 0, src // 2, LANES + src // 2)
    perm = np.zeros((MXU_DIM, MXU_DIM), np.float32)
    perm[src, dst] = 1.0
    return jnp.asarray(perm, jnp.bfloat16)


def _moe(tile_expert, n_tiles, tile_buf, tile_next, xs, w1, b1g, w2, b2):
    n_slots, W = xs.shape
    tm = MOE_TILE
    E, D, F2 = w1.shape
    F = w2.shape[1]
    xmap = lambda j, te, nt, *_: (jnp.minimum(j, nt[0] - 1), 0)
    emap = lambda j, te, *_: (te[j], 0, 0)
    return pl.pallas_call(
        _moe_kernel,
        grid_spec=pltpu.PrefetchScalarGridSpec(
            num_scalar_prefetch=4,
            grid=(n_slots // tm,),
            in_specs=[pl.BlockSpec((tm, W), xmap),
                      pl.BlockSpec(memory_space=pl.ANY),
                      pl.BlockSpec((1, 1, F2), emap),
                      pl.BlockSpec(memory_space=pl.ANY),
                      pl.BlockSpec((1, 1, D), emap),
                      pl.BlockSpec((MXU_DIM, MXU_DIM), lambda j, *_: (0, 0))],
            out_specs=pl.BlockSpec((tm, W), lambda j, *_: (j, 0)),
            scratch_shapes=[pltpu.VMEM((2, D, F2), w1.dtype), pltpu.VMEM((2, F, D), w2.dtype),
                            pltpu.VMEM((D, F2), jnp.bfloat16), pltpu.VMEM((F, D), jnp.bfloat16),
                            pltpu.SemaphoreType.DMA((2, 2))]),
        out_shape=jax.ShapeDtypeStruct((n_slots, W), xs.dtype),
        compiler_params=_params(("arbitrary",)),
        name="moe_experts",
    )(tile_expert, n_tiles, tile_buf, tile_next, xs, w1, b1g, w2, b2.reshape(E, 1, D), _regroup_perm())


def _combine_kernel(local_off_ref, slot_off_ref, len_ref, ys_ref, ls_ref, w_ref, x1_ref, gt2_ref, g_ref,
                    o_ref, yl_ref, sem):
    i = pl.program_id(0)
    tc = x1_ref.shape[0]

    def pieces(tile, act):
        slot = tile % 2

        def body(e, _):
            idx = tile * N_EXPERTS + e
            _split_copy(ys_ref, slot_off_ref[idx], yl_ref.at[slot], local_off_ref[idx], len_ref[idx],
                        sem.at[slot], act)
            return 0
        lax.fori_loop(0, N_EXPERTS, body, 0)

    def fetch(tile):
        yl_ref[tile % 2, TOP_K * tc:, :] = jnp.zeros((LOCAL_ROWS - TOP_K * tc, yl_ref.shape[2]), yl_ref.dtype)
        pieces(tile, lambda cp: cp.start())

    pl.when(i == 0)(lambda: fetch(i))
    pl.when(i + 1 < pl.num_programs(0))(lambda: fetch(i + 1))
    slot_id = lax.broadcasted_iota(jnp.int32, (tc, LOCAL_ROWS), 1)
    ls = ls_ref[...]
    w = w_ref[...]
    mix = jnp.zeros((tc, LOCAL_ROWS), jnp.float32)
    for k in range(TOP_K):
        mix = jnp.where(slot_id == ls[:, k:k + 1], w[:, k:k + 1], mix)
    hi = mix.astype(jnp.bfloat16)
    lo = (mix - hi.astype(jnp.float32)).astype(jnp.bfloat16)
    pieces(i, lambda cp: cp.wait())
    y = yl_ref[i % 2]
    f = (jnp.dot(hi, y, preferred_element_type=jnp.float32)
         + jnp.dot(lo, y, preferred_element_type=jnp.float32))
    o_ref[...] = x1_ref[...] + gt2_ref[0] * (_rms(f, NORM_EPS) * g_ref[...])


def _combine(local_off, slot_off, lens, ys, ls_t, topw_t, x1, mod3, g_post_ffn, seq):
    T, D = x1.shape
    tc = TOKEN_TILE
    per_b = seq // tc
    tile = pl.BlockSpec((tc, D), lambda i, *_: (i, 0))
    small = pl.BlockSpec((tc, TOP_K), lambda i, *_: (i, 0))
    return pl.pallas_call(
        _combine_kernel,
        grid_spec=pltpu.PrefetchScalarGridSpec(
            num_scalar_prefetch=3,
            grid=(T // tc,),
            in_specs=[pl.BlockSpec(memory_space=pl.ANY), small, small, tile,
                      pl.BlockSpec((1, 1, D), lambda i, *_: (i // per_b, 0, 5)),
                      pl.BlockSpec((1, D), lambda i, *_: (0, 0))],
            out_specs=tile,
            scratch_shapes=[pltpu.VMEM((2, LOCAL_ROWS, ys.shape[1]), ys.dtype),
                            pltpu.SemaphoreType.DMA((2,))]),
        out_shape=jax.ShapeDtypeStruct((T, D), jnp.float32),
        compiler_params=_params(("arbitrary",)),
        name="combine",
    )(local_off, slot_off, lens, ys, ls_t, topw_t, x1, mod3, g_post_ffn)


def kernel(x, c, positions, w_ada, b_ada, g_pre_mix, w_in, lambda_q1, lambda_k1, lambda_q2, lambda_k2,
           g_subln, w_pool, pool_scale, w_proj_a, w_proj_b, w_out, g_post_mix, g_pre_ffn,
           w_router, b_router, w_exp1, b_exp1, w_exp2, b_exp2, g_post_ffn):
    B, S, D = x.shape
    T = B * S
    bf = jnp.bfloat16
    tabs = _rope_tables(positions)
    for l in range(w_ada.shape[0]):
        x2 = x.reshape(T, D)
        mod3 = _ada(c, w_ada[l], b_ada[l]).reshape(B, 1, N_MOD * D)
        n_qk = 2 * DIFF_WIDTH
        w_in_bf = jnp.concatenate([_permute_qk_columns(w_in[l][:, :n_qk]), w_in[l][:, n_qk:]], axis=1).astype(bf)
        z = _in_proj(x2, g_pre_mix[l][None], mod3, w_in_bf, tabs, S)
        lams = [v[l][None] for v in (lambda_q1, lambda_k1, lambda_q2, lambda_k2)]
        o = _diff_attention(z, lams, g_subln[l][None], B, S)
        p = _pool(z, w_pool[l], pool_scale[l][None], B, S)
        x1, xs_local, ls, topw, cnt = _mix(
            x2, o, p, z, w_proj_a[l].astype(bf), w_proj_b[l].astype(bf), w_out[l].astype(bf),
            g_post_mix[l][None], mod3, g_pre_ffn[l][None], w_router[l].T, b_router[l], S)

        i32 = jnp.int32
        n = cnt[:, :, 0].astype(i32)
        n_tok_tiles = n.shape[0]
        counts = jnp.sum(n, axis=0)
        padded = (counts + MOE_TILE - 1) // MOE_TILE * MOE_TILE
        gend = jnp.cumsum(padded).astype(i32)
        gstart = gend - padded
        local_off = (jnp.cumsum(n, axis=1) - n).reshape(-1).astype(i32)
        slot_off = (gstart[None, :] + jnp.cumsum(n, axis=0) - n).reshape(-1).astype(i32)
        lens = n.reshape(-1)
        steps_pad = -N_EXPERTS % n_tok_tiles
        group_pad_off = jnp.pad(gstart + counts, (0, steps_pad)).astype(i32)
        group_pad_len = jnp.pad(padded - counts, (0, steps_pad)).astype(i32)
        n_slots = -(-(T * TOP_K + n_tok_tiles * N_EXPERTS * ROW_ALIGN) // MOE_TILE) * MOE_TILE + N_EXPERTS * MOE_TILE
        tile_start = jnp.arange(n_slots // MOE_TILE, dtype=i32) * MOE_TILE
        tile_expert = jnp.minimum(jnp.sum(tile_start[:, None] >= gend[None, :], axis=1), N_EXPERTS - 1)
        n_tiles = gend[-1:] // MOE_TILE
        has_tiles = padded > 0
        expert_ids = jnp.arange(N_EXPERTS, dtype=i32)
        later = lax.cummin(jnp.where(has_tiles, expert_ids, N_EXPERTS), reverse=True)
        next_expert = jnp.concatenate([later[1:], jnp.full((1,), N_EXPERTS, i32)])
        next_expert = jnp.where(next_expert < N_EXPERTS, next_expert, -1)
        tile_buf = ((jnp.cumsum(has_tiles.astype(i32)) - 1) % 2)[tile_expert].astype(i32)
        tile_next = next_expert[tile_expert].astype(i32)

        xs = _dispatch(local_off, slot_off, lens, group_pad_off, group_pad_len, n_tiles, xs_local, n_slots)
        half = D_FF
        b1g = b_exp1[l].reshape(N_EXPERTS, half // LANES, LANES, 2).transpose(0, 1, 3, 2)
        b1g = b1g.reshape(N_EXPERTS, 1, 2 * half)
        ys = _moe(tile_expert.astype(i32), n_tiles, tile_buf, tile_next, xs, w_exp1[l], b1g, w_exp2[l], b_exp2[l])
        x = _combine(local_off, slot_off, lens, ys, ls.T, topw.T, x1, mod3, g_post_ffn[l][None], S)
        x = x.reshape(B, S, D)
    return x
```

```python
import functools
import math

import numpy as np
import jax
import jax.numpy as jnp
from jax import lax
from jax.experimental import pallas as pl
from jax.experimental.pallas import tpu as pltpu

D_MODEL = 1024
N_HEADS = 8
HEAD_DIM = 64
V_DIM = 2 * HEAD_DIM
DIFF_WIDTH = N_HEADS * V_DIM
POOL_WINDOWS = (2, 4, 8, 16)
POOL_GROUP_DIM = 128
POOL_WIDTH = len(POOL_WINDOWS) * POOL_GROUP_DIM
IN_COLS = 3 * DIFF_WIDTH + POOL_WIDTH + 2 * D_MODEL
ROPE_THETA = 500000.0
ROT_DIM = HEAD_DIM // 4
ROT_HALF = ROT_DIM // 2
N_EXPERTS = 32
TOP_K = 4
D_FF = D_MODEL
SWIGLU_ALPHA = 1.702
SWIGLU_LIMIT = 7.0
NORM_EPS = 1e-6
SUBLN_EPS = 1e-5
N_MOD = 6
NEG_BIG = -1e30
LAMBDA_INIT = 0.8 - 0.6 * math.exp(-0.3 * 0)

LANES = 128
MXU_DIM = 256
VMEM_LIMIT = 56 * 1024 * 1024

COL_TILE = 512
IN_ROW_TILE = 2048
IN_ROW_CHUNK = 256
COL_K = DIFF_WIDTH // COL_TILE
COL_V = 2 * DIFF_WIDTH // COL_TILE
COL_G = (3 * DIFF_WIDTH + POOL_WIDTH) // COL_TILE
ATT_BLOCK = 512
ATT_HEADS = 4
MOE_TILE = 512
TOKEN_TILE = 256
MIX_TILES = 2
ROW_ALIGN = 16
LOCAL_ROWS = TOP_K * TOKEN_TILE + N_EXPERTS * ROW_ALIGN
PIECE_SIZES = tuple(1 << b for b in range(max(TOKEN_TILE, MOE_TILE).bit_length() - 1,
                                          ROW_ALIGN.bit_length() - 2, -1))

_HI = lax.Precision.HIGHEST


def _params(sem, vmem=VMEM_LIMIT):
    return pltpu.CompilerParams(dimension_semantics=sem, vmem_limit_bytes=vmem)


def _rms(x, eps):
    return x * lax.rsqrt(jnp.mean(x * x, axis=-1, keepdims=True) + eps)


def _ada_kernel(c_ref, w_ref, b_ref, o_ref):
    c = c_ref[...]
    s = c * jax.nn.sigmoid(c)
    o_ref[...] = jnp.dot(s, w_ref[...], precision=_HI, preferred_element_type=jnp.float32) + b_ref[...]


def _ada(c, w_ada, b_ada):
    B, D = c.shape
    N = w_ada.shape[1]
    tn = 1536
    return pl.pallas_call(
        _ada_kernel,
        grid=(N // tn,),
        in_specs=[pl.BlockSpec((B, D), lambda j: (0, 0)),
                  pl.BlockSpec((D, tn), lambda j: (0, j)),
                  pl.BlockSpec((1, tn), lambda j: (0, j))],
        out_specs=pl.BlockSpec((B, tn), lambda j: (0, j)),
        out_shape=jax.ShapeDtypeStruct((B, N), jnp.float32),
        compiler_params=_params(("parallel",)),
        name="ada",
    )(c, w_ada, b_ada.reshape(1, N))


def _rope_tab_kernel(pos_ref, invf_ref, phase_ref, c_ref, s_ref):
    pos = pos_ref[...].astype(jnp.float32)
    cs = jnp.cos(pos * invf_ref[...] - phase_ref[...])
    rolled = pltpu.roll(cs, LANES // 2, 1)
    lane = lax.broadcasted_iota(jnp.int32, cs.shape, 1)
    lower = lane < LANES // 2
    rot = lane % (LANES // 2) < ROT_DIM
    c_ref[...] = jnp.where(lower, cs, rolled)
    s_ref[...] = jnp.where(rot, jnp.where(lower, -rolled, cs), 0.0)


def _rope_tables(positions):
    T = positions.size
    tm = min(T, 2048)
    lane = np.arange(LANES)
    rot = lane % (LANES // 2) < ROT_DIM
    inv = ROPE_THETA ** (-(np.arange(ROT_HALF, dtype=np.float64) * 2.0 / ROT_DIM))
    invf = np.where(rot, inv[lane % ROT_HALF], 0.0).astype(np.float32)[None, :]
    phase = np.where(rot & (lane >= LANES // 2), np.pi / 2, 0.0).astype(np.float32)[None, :]
    row = pl.BlockSpec((1, LANES), lambda i: (0, 0))
    tab = pl.BlockSpec((tm, LANES), lambda i: (i, 0))
    sds = jax.ShapeDtypeStruct((T, LANES), jnp.float32)
    return pl.pallas_call(
        _rope_tab_kernel,
        grid=(T // tm,),
        in_specs=[pl.BlockSpec((tm, 1), lambda i: (i, 0)), row, row],
        out_specs=[tab, tab],
        out_shape=[sds, sds],
        compiler_params=_params(("parallel",)),
        name="rope_tables",
    )(positions.reshape(T, 1), jnp.asarray(invf), jnp.asarray(phase))


def _permute_qk_columns(w):
    D, n = w.shape
    w = w.reshape(D, n // V_DIM, 2, 4, 2, ROT_HALF)
    return w.transpose(0, 1, 4, 3, 2, 5).reshape(D, n)


def _in_proj_kernel(x_ref, g_ref, sc_ref, sh_ref, w_ref, c_ref, s_ref, z_ref, h_ref):
    j = pl.program_id(1)
    tm = x_ref.shape[0]
    rows = min(tm, IN_ROW_CHUNK)

    @pl.when(j == 0)
    def _():
        h = _rms(x_ref[...], NORM_EPS) * g_ref[...] * (1.0 + sc_ref[0]) + sh_ref[0]
        h_ref[...] = h.astype(h_ref.dtype)

    def chunks(epilogue):
        for r in range(tm // rows):
            sl = slice(r * rows, (r + 1) * rows)
            z = jnp.dot(h_ref[sl, :], w_ref[...], preferred_element_type=jnp.float32)
            z_ref[sl, :] = epilogue(z, sl).astype(z_ref.dtype)

    @pl.when(j < COL_V)
    def _():
        scale = jnp.where(j < COL_K, HEAD_DIM ** -0.5 * math.log2(math.e), 1.0)

        def rope(z, sl):
            c = c_ref[sl, :] * scale
            s = s_ref[sl, :] * scale
            parts = []
            for g in range(COL_TILE // LANES):
                zg = z[:, g * LANES:(g + 1) * LANES]
                parts.append(zg * c + pltpu.roll(zg, LANES // 2, 1) * s)
            return jnp.concatenate(parts, axis=1)

        chunks(rope)

    @pl.when((j >= COL_V) & (j < COL_G))
    def _():
        chunks(lambda z, sl: z)

    @pl.when(j >= COL_G)
    def _():
        chunks(lambda z, sl: 0.5 * jnp.tanh(0.5 * z) + 0.5)


def _in_proj(x2, g_pre, mod3, w_in_bf, tabs, seq):
    T, D = x2.shape
    tm = min(seq, IN_ROW_TILE)
    per_b = seq // tm
    mod_spec = lambda col: pl.BlockSpec((1, 1, D), lambda i, j: (i // per_b, 0, col))
    tab = pl.BlockSpec((tm, LANES), lambda i, j: (i, 0))
    return pl.pallas_call(
        _in_proj_kernel,
        grid=(T // tm, IN_COLS // COL_TILE),
        in_specs=[pl.BlockSpec((tm, D), lambda i, j: (i, 0)),
                  pl.BlockSpec((1, D), lambda i, j: (0, 0)),
                  mod_spec(1), mod_spec(0),
                  pl.BlockSpec((D, COL_TILE), lambda i, j: (0, j)),
                  tab, tab],
        out_specs=pl.BlockSpec((tm, COL_TILE), lambda i, j: (i, j)),
        out_shape=jax.ShapeDtypeStruct((T, IN_COLS), jnp.bfloat16),
        scratch_shapes=[pltpu.VMEM((tm, D), jnp.bfloat16)],
        compiler_params=_params(("parallel", "arbitrary")),
        name="in_proj",
    )(x2, g_pre, mod3, mod3, w_in_bf, *tabs)


def _attn_kernel(lq1_ref, lk1_ref, lq2_ref, lk2_ref, gs_ref, q_ref, k_ref, v_ref, o_ref, vt_ref, *acc_refs):
    i = pl.program_id(2)
    blk = q_ref.shape[0]
    nblk = v_ref.shape[0] // blk
    chains = [(h, comp) for h in range(ATT_HEADS) for comp in range(2)]

    @pl.when(i == 0)
    def _():
        for c in range(nblk):
            for h in range(ATT_HEADS):
                v = v_ref[c * blk:(c + 1) * blk, h * V_DIM:(h + 1) * V_DIM]
                vt_ref[c, h] = v.astype(jnp.float32).T.astype(vt_ref.dtype)

    lam = (jnp.exp(jnp.sum(lq1_ref[...] * lk1_ref[...], axis=-1, keepdims=True))
           - jnp.exp(jnp.sum(lq2_ref[...] * lk2_ref[...], axis=-1, keepdims=True))
           + LAMBDA_INIT)
    lane_comp = lax.broadcasted_iota(jnp.int32, (blk, V_DIM), 1) // ROT_HALF % 2
    qs = []
    for h, comp in chains:
        q = q_ref[:, h * V_DIM:(h + 1) * V_DIM]
        qs.append(jnp.where(lane_comp == comp, q, jnp.zeros_like(q)))
    nt = (((1,), (1,)), ((), ()))
    for acc_ref in acc_refs:
        acc_ref[...] = jnp.zeros_like(acc_ref)

    def step(c, carry, mask=None):
        off = pl.multiple_of(c * blk, blk)
        scores = []
        for n, (h, comp) in enumerate(chains):
            k = k_ref[pl.ds(off, blk), h * V_DIM:(h + 1) * V_DIM]
            scores.append(lax.dot_general(k, qs[n], nt, preferred_element_type=jnp.float32))
        out, probs, alphas = [], [], []
        for n, s in enumerate(scores):
            m, l = carry[n]
            if mask is not None:
                s = jnp.where(mask, s, NEG_BIG)
            m_new = jnp.maximum(m, jnp.max(s, axis=0, keepdims=True))
            alpha = jnp.exp2(m - m_new)
            p = jnp.exp2(s - m_new)
            out.append((m_new, alpha * l + jnp.sum(p, axis=0, keepdims=True)))
            probs.append(p.astype(vt_ref.dtype))
            alphas.append(alpha)
        for n, (h, comp) in enumerate(chains):
            pv = jnp.dot(vt_ref[c, h], probs[n], preferred_element_type=jnp.float32)
            acc_refs[n][...] = alphas[n] * acc_refs[n][...] + pv
        return tuple(out)

    init = tuple((jnp.full((1, blk), NEG_BIG, jnp.float32), jnp.zeros((1, blk), jnp.float32))
                 for _ in chains)
    carry = lax.fori_loop(0, i, step, init)

    key = lax.broadcasted_iota(jnp.int32, (blk, blk), 0)
    qry = lax.broadcasted_iota(jnp.int32, (blk, blk), 1)
    carry = step(i, carry, key <= qry)
    for h in range(ATT_HEADS):
        (_, l1), (_, l2) = carry[2 * h], carry[2 * h + 1]
        ot = acc_refs[2 * h][...] / l1 - lam * (acc_refs[2 * h + 1][...] / l2)
        o = _rms(ot.T, SUBLN_EPS) * gs_ref[...] * (1.0 - LAMBDA_INIT)
        o_ref[:, h * V_DIM:(h + 1) * V_DIM] = o.astype(o_ref.dtype)


def _diff_attention(z, lams, g_subln, batch, seq):
    T = z.shape[0]
    blk = min(ATT_BLOCK, seq)
    nq = seq // blk
    width = ATT_HEADS * V_DIM
    vec = pl.BlockSpec((1, HEAD_DIM), lambda b, h, i: (0, 0))
    kcol = DIFF_WIDTH // width
    return pl.pallas_call(
        _attn_kernel,
        grid=(batch, N_HEADS // ATT_HEADS, nq),
        in_specs=[vec, vec, vec, vec,
                  pl.BlockSpec((1, V_DIM), lambda b, h, i: (0, 0)),
                  pl.BlockSpec((blk, width), lambda b, h, i: (b * nq + i, h)),
                  pl.BlockSpec((seq, width), lambda b, h, i: (b, kcol + h)),
                  pl.BlockSpec((seq, width), lambda b, h, i: (b, 2 * kcol + h))],
        out_specs=pl.BlockSpec((blk, width), lambda b, h, i: (b * nq + i, h)),
        out_shape=jax.ShapeDtypeStruct((T, DIFF_WIDTH), jnp.bfloat16),
        scratch_shapes=[pltpu.VMEM((nq, ATT_HEADS, V_DIM, blk), jnp.bfloat16)]
        + [pltpu.VMEM((V_DIM, blk), jnp.float32)] * (2 * ATT_HEADS),
        compiler_params=_params(("parallel", "parallel", "arbitrary")),
        name="diff_attn",
    )(*lams, g_subln, z, z, z)


def _pool_kernel(u_ref, w_ref, ps_ref, o_ref):
    g = pl.program_id(1)
    u = u_ref[...].astype(jnp.float32)
    t = lax.broadcasted_iota(jnp.int32, u.shape, 0)

    def shifted(x, k):
        return jnp.where(t >= k, pltpu.roll(x, k, 0), 0.0)

    s = u
    d = jnp.zeros_like(u)
    for gi, w in enumerate(POOL_WINDOWS):
        s = s + shifted(s, w // 2)
        cnt = jnp.minimum(t + 1, w).astype(jnp.float32)
        d = jnp.where(g == gi, s / cnt - u, d)
    y = jnp.dot(d.astype(jnp.bfloat16), w_ref[0].astype(jnp.bfloat16), preferred_element_type=jnp.float32)
    o_ref[...] = (y * ps_ref[...]).astype(o_ref.dtype)


def _pool(z, w_pool, pool_scale, batch, seq):
    T = z.shape[0]
    G = len(POOL_WINDOWS)
    ucol = 3 * DIFF_WIDTH // POOL_GROUP_DIM
    return pl.pallas_call(
        _pool_kernel,
        grid=(batch, G),
        in_specs=[pl.BlockSpec((seq, POOL_GROUP_DIM), lambda b, g: (b, ucol + g)),
                  pl.BlockSpec((1, POOL_GROUP_DIM, POOL_GROUP_DIM), lambda b, g: (g, 0, 0)),
                  pl.BlockSpec((1, POOL_GROUP_DIM), lambda b, g: (0, g))],
        out_specs=pl.BlockSpec((seq, POOL_GROUP_DIM), lambda b, g: (b, g)),
        out_shape=jax.ShapeDtypeStruct((T, POOL_WIDTH), jnp.bfloat16),
        compiler_params=_params(("parallel", "parallel")),
        name="pool",
    )(z, w_pool, pool_scale)


def _mix_kernel(x_ref, o_ref, p_ref, ga0, ga1, gb0, gb1, wa_ref, wb_ref, wo_ref, gpm_ref, gt1_ref,
                gpf_ref, sc2_ref, sh2_ref, wr_ref, br_ref,
                x1_ref, xl_ref, ls_ref, tw_ref, cnt_ref):
    tiles = range(x_ref.shape[0] // TOKEN_TILE)
    h2s = [_mix_front(t, x_ref, o_ref, p_ref, ga0, ga1, gb0, gb1, wa_ref, wb_ref, wo_ref, gpm_ref, gt1_ref,
                      gpf_ref, sc2_ref, sh2_ref, x1_ref) for t in tiles]
    places = [_mix_route(t, h2s[t], wr_ref, br_ref, ls_ref, tw_ref, cnt_ref) for t in tiles]
    for t in tiles:
        xl = jnp.dot(places[t], h2s[t].astype(jnp.bfloat16), preferred_element_type=jnp.float32)
        xl_ref[t * LOCAL_ROWS:(t + 1) * LOCAL_ROWS, :] = xl.astype(xl_ref.dtype)


def _mix_front(t, x_ref, o_ref, p_ref, ga0, ga1, gb0, gb1, wa_ref, wb_ref, wo_ref, gpm_ref, gt1_ref,
               gpf_ref, sc2_ref, sh2_ref, x1_ref):
    rows = slice(t * TOKEN_TILE, (t + 1) * TOKEN_TILE)
    ya = jnp.dot(o_ref[rows, :], wa_ref[...], preferred_element_type=jnp.float32)
    yb = jnp.dot(p_ref[rows, :], wb_ref[...], preferred_element_type=jnp.float32)
    ga = jnp.concatenate([ga0[rows, :], ga1[rows, :]], axis=1).astype(jnp.float32)
    gb = jnp.concatenate([gb0[rows, :], gb1[rows, :]], axis=1).astype(jnp.float32)
    merged = (ga * ya + gb * yb).astype(jnp.bfloat16)
    mixed = jnp.dot(merged, wo_ref[...], preferred_element_type=jnp.float32)
    x1 = x_ref[rows, :] + gt1_ref[0] * (_rms(mixed, NORM_EPS) * gpm_ref[...])
    x1_ref[rows, :] = x1
    return _rms(x1, NORM_EPS) * gpf_ref[...] * (1.0 + sc2_ref[0]) + sh2_ref[0]


def _mix_route(t, h2, wr_ref, br_ref, ls_ref, tw_ref, cnt_ref):
    tm = TOKEN_TILE
    rows = slice(t * tm, (t + 1) * tm)
    logits =lax.dot_general(wr_ref[...], h2, (((1,), (1,)), ((), ())), precision=_HI,
                             preferred_element_type=jnp.float32) + br_ref[...]
    eid = lax.broadcasted_iota(jnp.int32, logits.shape, 0)
    work = logits
    sels, vals = [], []
    for _ in range(TOP_K):
        mx = jnp.max(work, axis=0, keepdims=True)
        idx = jnp.min(jnp.where(work == mx, eid, N_EXPERTS), axis=0, keepdims=True)
        sel = eid == idx
        work = jnp.where(sel, -jnp.inf, work)
        sels.append(sel)
        vals.append(mx)
    ex = [jnp.exp(v - vals[0]) for v in vals]
    den = ex[0] + ex[1] + ex[2] + ex[3]
    onehot = jnp.zeros(logits.shape, jnp.float32)
    for sel in sels:
        onehot = jnp.where(sel, 1.0, onehot)
    r = lax.broadcasted_iota(jnp.int32, (tm, tm), 0)
    c = lax.broadcasted_iota(jnp.int32, (tm, tm), 1)
    tri = jnp.where(r < c, 1.0, 0.0).astype(jnp.bfloat16)
    rank = jnp.dot(onehot.astype(jnp.bfloat16), tri, preferred_element_type=jnp.float32)
    counts = jnp.broadcast_to(jnp.sum(onehot, axis=1, keepdims=True), (N_EXPERTS, LANES))
    rounded = jnp.floor((counts + (ROW_ALIGN - 1)) * (1.0 / ROW_ALIGN)) * ROW_ALIGN
    er = lax.broadcasted_iota(jnp.int32, (N_EXPERTS, N_EXPERTS), 0)
    ec = lax.broadcasted_iota(jnp.int32, (N_EXPERTS, N_EXPERTS), 1)
    below = jnp.where(ec < er, 1.0, 0.0).astype(jnp.bfloat16)
    offset = jnp.dot(below, rounded.astype(jnp.bfloat16), preferred_element_type=jnp.float32)
    slot_of = rank + offset[:, 0:1]
    cnt_ref[t] = rounded
    slot_id = lax.broadcasted_iota(jnp.int32, (LOCAL_ROWS, tm), 0)
    place = jnp.zeros((LOCAL_ROWS, tm), jnp.float32)
    for kk in range(TOP_K):
        ls = jnp.sum(jnp.where(sels[kk], slot_of, 0.0), axis=0, keepdims=True).astype(jnp.int32)
        ls_ref[kk:kk + 1, rows] = ls
        tw_ref[kk:kk + 1, rows] = ex[kk] / den
        place = jnp.where(slot_id == ls, 1.0, place)
    return place.astype(jnp.bfloat16)


def _mix(x2, o, p, z, wa, wb, wo, g_post_mix, mod3, g_pre_ffn, w_router_t, b_router, seq):
    T, D = x2.shape
    sub = min(MIX_TILES, seq // TOKEN_TILE)
    tm = sub * TOKEN_TILE
    per_b = seq // tm
    E = N_EXPERTS
    row = lambda n: pl.BlockSpec((1, n), lambda i: (0, 0))
    mod_spec = lambda col: pl.BlockSpec((1, 1, D), lambda i: (i // per_b, 0, col))
    gate = lambda cb: pl.BlockSpec((tm, COL_TILE), lambda i: (i, cb))
    full = lambda a: pl.BlockSpec(a.shape, lambda i: (0,) * a.ndim)
    tile = pl.BlockSpec((tm, D), lambda i: (i, 0))
    small = pl.BlockSpec((TOP_K, tm), lambda i: (0, i))
    return pl.pallas_call(
        _mix_kernel,
        grid=(T // tm,),
        in_specs=[tile, tile,
                  pl.BlockSpec((tm, POOL_WIDTH), lambda i: (i, 0)),
                  gate(COL_G), gate(COL_G + 1), gate(COL_G + 2), gate(COL_G + 3),
                  full(wa), full(wb), full(wo), row(D), mod_spec(2),
                  row(D), mod_spec(4), mod_spec(3), full(w_router_t),
                  pl.BlockSpec((E, 1), lambda i: (0, 0))],
        out_specs=[tile, pl.BlockSpec((sub * LOCAL_ROWS, D), lambda i: (i, 0)), small, small,
                   pl.BlockSpec((sub, E, LANES), lambda i: (i, 0, 0))],
        out_shape=[jax.ShapeDtypeStruct((T, D), jnp.float32),
                   jax.ShapeDtypeStruct((T // TOKEN_TILE * LOCAL_ROWS, D), jnp.bfloat16),
                   jax.ShapeDtypeStruct((TOP_K, T), jnp.int32),
                   jax.ShapeDtypeStruct((TOP_K, T), jnp.float32),
                   jax.ShapeDtypeStruct((T // TOKEN_TILE, E, LANES), jnp.float32)],
        compiler_params=_params(("parallel",)),
        name="mix_tail",
    )(x2, o, p, z, z, z, z, wa, wb, wo, g_post_mix, mod3, g_pre_ffn, mod3, mod3, w_router_t,
      b_router.reshape(E, 1))


def _split_copy(src_ref, src_row, dst_ref, dst_row, n, sem, act):
    off = 0
    for p in PIECE_SIZES:
        take = (n & p) != 0

        @pl.when(take)
        def _(off=off, p=p):
            src_at = 0 if src_row is None else pl.multiple_of(src_row + off, ROW_ALIGN)
            dst_at = pl.multiple_of(dst_row + off, ROW_ALIGN)
            act(pltpu.make_async_copy(src_ref.at[pl.ds(src_at, p), :], dst_ref.at[pl.ds(dst_at, p), :], sem))

        off = off + jnp.where(take, p, 0)


def _dispatch_kernel(local_off_ref, slot_off_ref, len_ref, zero_off_ref, zero_len_ref, ntile_ref,
                     xl_ref, xs_ref, zero_ref, sem, zsem, *, zero_per_step):
    i = pl.program_id(0)

    @pl.when(i == 0)
    def _():
        zero_ref[...] = jnp.zeros_like(zero_ref)

    def pieces(act):
        def body(e, _):
            idx = i * N_EXPERTS + e
            _split_copy(xl_ref, local_off_ref[idx], xs_ref, slot_off_ref[idx], len_ref[idx], sem, act)
            return 0
        lax.fori_loop(0, N_EXPERTS, body, 0)
        for r in range(zero_per_step):
            idx = i * zero_per_step + r
            _split_copy(zero_ref, None, xs_ref, zero_off_ref[idx], zero_len_ref[idx], zsem, act)

    def tails(act):
        def body(t, _):
            off = pl.multiple_of(t * MOE_TILE, MOE_TILE)
            act(pltpu.make_async_copy(zero_ref.at[pl.ds(0, MOE_TILE), :],
                                      xs_ref.at[pl.ds(off, MOE_TILE), :], zsem))
            return 0
        lax.fori_loop(ntile_ref[0], xs_ref.shape[0] // MOE_TILE, body, 0)

    for act in (lambda cp: cp.start(), lambda cp: cp.wait()):
        pieces(act)
        pl.when(i == 0)(functools.partial(tails, act))


def _dispatch(local_off, slot_off, lens, zero_off, zero_len, n_tiles, xs_local, n_slots):
    W = xs_local.shape[1]
    steps = lens.shape[0] // N_EXPERTS
    return pl.pallas_call(
        functools.partial(_dispatch_kernel, zero_per_step=zero_len.shape[0] // steps),
        grid_spec=pltpu.PrefetchScalarGridSpec(
            num_scalar_prefetch=6,
            grid=(steps,),
            in_specs=[pl.BlockSpec((LOCAL_ROWS, W), lambda i, *_: (i, 0))],
            out_specs=pl.BlockSpec(memory_space=pl.ANY),
            scratch_shapes=[pltpu.VMEM((PIECE_SIZES[0], W), xs_local.dtype),
                            pltpu.SemaphoreType.DMA, pltpu.SemaphoreType.DMA]),
        out_shape=jax.ShapeDtypeStruct((n_slots, W), xs_local.dtype),
        compiler_params=_params(("arbitrary",)),
        name="dispatch",
    )(local_off, slot_off, lens, zero_off, zero_len, n_tiles, xs_local)


def _moe_kernel(te_ref, nt_ref, buf_ref, nxt_ref, rows_ref, x_ref, w1_hbm, b1_ref, w2_hbm, b2_ref, perm_ref, y_ref,
                w1raw_ref, w2raw_ref, w1s_ref, w2s_ref, sem):
    j = pl.program_id(0)
    prev = te_ref[jnp.maximum(j - 1, 0)]
    fresh = (j == 0) | (te_ref[j] != prev)
    nblk = w1s_ref.shape[1] // MXU_DIM

    def weight_copies(e, b):
        return (pltpu.make_async_copy(w1_hbm.at[e], w1raw_ref.at[b], sem.at[0, b]),
                pltpu.make_async_copy(w2_hbm.at[e], w2raw_ref.at[b], sem.at[1, b]))

    @pl.when(fresh & (j < nt_ref[0]))
    def _():
        e, b, nxt = te_ref[j], buf_ref[j], nxt_ref[j]

        @pl.when(j == 0)
        def _():
            for cp in weight_copies(e, b):
                cp.start()

        for cp in weight_copies(e, b):
            cp.wait()

        @pl.when(nxt >= 0)
        def _():
            for cp in weight_copies(nxt, 1 - b):
                cp.start()

        for c in range(nblk):
            cols = slice(c * MXU_DIM, (c + 1) * MXU_DIM)
            blk = w1raw_ref[b, :, cols].astype(jnp.bfloat16)
            w1s_ref[:, cols] = jnp.dot(blk, perm_ref[...],
                                       preferred_element_type=jnp.float32).astype(jnp.bfloat16)
        w2s_ref[...] = w2raw_ref[b].astype(jnp.bfloat16)

    def experts_mlp(rows):
        z = jnp.dot(x_ref[:rows, :], w1s_ref[...], preferred_element_type=jnp.float32) + b1_ref[0]
        acts = []
        for c in range(nblk):
            gate = jnp.minimum(z[:, c * MXU_DIM:c * MXU_DIM + LANES], SWIGLU_LIMIT)
            up = jnp.clip(z[:, c * MXU_DIM + LANES:(c + 1) * MXU_DIM], -SWIGLU_LIMIT, SWIGLU_LIMIT)
            acts.append(gate * jax.nn.sigmoid(SWIGLU_ALPHA * gate) * (up + 1.0))
        a = jnp.concatenate(acts, axis=1).astype(jnp.bfloat16)
        y = jnp.dot(a, w2s_ref[...], preferred_element_type=jnp.float32) + b2_ref[0]
        y_ref[:rows, :] = y.astype(y_ref.dtype)
        if rows < y_ref.shape[0]:
            y_ref[rows:, :] = jnp.zeros((y_ref.shape[0] - rows, y_ref.shape[1]), y_ref.dtype)

    half = y_ref.shape[0] // 2
    active = j < nt_ref[0]
    pl.when(active & (rows_ref[j] > half))(lambda: experts_mlp(2 * half))
    pl.when(active & (rows_ref[j] <= half))(lambda: experts_mlp(half))

    @pl.when(j >= nt_ref[0])
    def _():
        y_ref[...] = jnp.zeros_like(y_ref)


def _regroup_perm():
    src = np.arange(MXU_DIM)
    dst = np.where(src % 2 == 0, src // 2, LANES + src // 2)
    perm = np.zeros((MXU_DIM, MXU_DIM), np.float32)
    perm[src, dst] = 1.0
    return jnp.asarray(perm, jnp.bfloat16)


def _moe(tile_expert, n_tiles, tile_buf, tile_next, tile_rows, xs, w1, b1g, w2, b2):
    n_slots, W = xs.shape
    tm = MOE_TILE
    E, D, F2 = w1.shape
    F = w2.shape[1]
    xmap = lambda j, te, nt, *_: (jnp.minimum(j, nt[0] - 1), 0)
    emap = lambda j, te, *_: (te[j], 0, 0)
    return pl.pallas_call(
        _moe_kernel,
        grid_spec=pltpu.PrefetchScalarGridSpec(
            num_scalar_prefetch=5,
            grid=(n_slots // tm,),
            in_specs=[pl.BlockSpec((tm, W), xmap),
                      pl.BlockSpec(memory_space=pl.ANY),
                      pl.BlockSpec((1, 1, F2), emap),
                      pl.BlockSpec(memory_space=pl.ANY),
                      pl.BlockSpec((1, 1, D), emap),
                      pl.BlockSpec((MXU_DIM, MXU_DIM), lambda j, *_: (0, 0))],
            out_specs=pl.BlockSpec((tm, W), lambda j, *_: (j, 0)),
            scratch_shapes=[pltpu.VMEM((2, D, F2), w1.dtype), pltpu.VMEM((2, F, D), w2.dtype),
                            pltpu.VMEM((D, F2), jnp.bfloat16), pltpu.VMEM((F, D), jnp.bfloat16),
                            pltpu.SemaphoreType.DMA((2, 2))]),
        out_shape=jax.ShapeDtypeStruct((n_slots, W), xs.dtype),
        compiler_params=_params(("arbitrary",)),
        name="moe_experts",
    )(tile_expert, n_tiles, tile_buf, tile_next, tile_rows, xs, w1, b1g, w2, b2.reshape(E, 1, D),
      _regroup_perm())


def _combine_kernel(local_off_ref, slot_off_ref, len_ref, ys_ref, ls_ref, w_ref, x1_ref, gt2_ref, g_ref,
                    o_ref, yl_ref, sem):
    i = pl.program_id(0)
    tc = x1_ref.shape[0]

    def pieces(tile, act):
        slot = tile % 2

        def body(e, _):
            idx = tile * N_EXPERTS + e
            _split_copy(ys_ref, slot_off_ref[idx], yl_ref.at[slot], local_off_ref[idx], len_ref[idx],
                        sem.at[slot], act)
            return 0
        lax.fori_loop(0, N_EXPERTS, body, 0)

    def fetch(tile):
        yl_ref[tile % 2, TOP_K * tc:, :] = jnp.zeros((LOCAL_ROWS - TOP_K * tc, yl_ref.shape[2]), yl_ref.dtype)
        pieces(tile, lambda cp: cp.start())

    pl.when(i == 0)(lambda: fetch(i))
    pl.when(i + 1 < pl.num_programs(0))(lambda: fetch(i + 1))
    slot_id = lax.broadcasted_iota(jnp.int32, (tc, LOCAL_ROWS), 1)
    ls = ls_ref[...]
    w = w_ref[...]
    mix = jnp.zeros((tc, LOCAL_ROWS), jnp.float32)
    for k in range(TOP_K):
        mix = jnp.where(slot_id == ls[:, k:k + 1], w[:, k:k + 1], mix)
    hi = mix.astype(jnp.bfloat16)
    lo = (mix - hi.astype(jnp.float32)).astype(jnp.bfloat16)
    pieces(i, lambda cp: cp.wait())
    y = yl_ref[i % 2]
    f = (jnp.dot(hi, y, preferred_element_type=jnp.float32)
         + jnp.dot(lo, y, preferred_element_type=jnp.float32))
    o_ref[...] = x1_ref[...] + gt2_ref[0] * (_rms(f, NORM_EPS) * g_ref[...])


def _combine(local_off, slot_off, lens, ys, ls_t, topw_t, x1, mod3, g_post_ffn, seq):
    T, D = x1.shape
    tc = TOKEN_TILE
    per_b = seq // tc
    tile = pl.BlockSpec((tc, D), lambda i, *_: (i, 0))
    small = pl.BlockSpec((tc, TOP_K), lambda i, *_: (i, 0))
    return pl.pallas_call(
        _combine_kernel,
        grid_spec=pltpu.PrefetchScalarGridSpec(
            num_scalar_prefetch=3,
            grid=(T // tc,),
            in_specs=[pl.BlockSpec(memory_space=pl.ANY), small, small, tile,
                      pl.BlockSpec((1, 1, D), lambda i, *_: (i // per_b, 0, 5)),
                      pl.BlockSpec((1, D), lambda i, *_: (0, 0))],
            out_specs=tile,
            scratch_shapes=[pltpu.VMEM((2, LOCAL_ROWS, ys.shape[1]), ys.dtype),
                            pltpu.SemaphoreType.DMA((2,))]),
        out_shape=jax.ShapeDtypeStruct((T, D), jnp.float32),
        compiler_params=_params(("arbitrary",)),
        name="combine",
    )(local_off, slot_off, lens, ys, ls_t, topw_t, x1, mod3, g_post_ffn)


def kernel(x, c, positions, w_ada, b_ada, g_pre_mix, w_in, lambda_q1, lambda_k1, lambda_q2, lambda_k2,
           g_subln, w_pool, pool_scale, w_proj_a, w_proj_b, w_out, g_post_mix, g_pre_ffn,
           w_router, b_router, w_exp1, b_exp1, w_exp2, b_exp2, g_post_ffn):
    B, S, D = x.shape
    T = B * S
    bf = jnp.bfloat16
    tabs = _rope_tables(positions)
    for l in range(w_ada.shape[0]):
        x2 = x.reshape(T, D)
        mod3 = _ada(c, w_ada[l], b_ada[l]).reshape(B, 1, N_MOD * D)
        n_qk = 2 * DIFF_WIDTH
        w_in_bf = jnp.concatenate([_permute_qk_columns(w_in[l][:, :n_qk]), w_in[l][:, n_qk:]], axis=1).astype(bf)
        z = _in_proj(x2, g_pre_mix[l][None], mod3, w_in_bf, tabs, S)
        lams = [v[l][None] for v in (lambda_q1, lambda_k1, lambda_q2, lambda_k2)]
        o = _diff_attention(z, lams, g_subln[l][None], B, S)
        p = _pool(z, w_pool[l], pool_scale[l][None], B, S)
        x1, xs_local, ls, topw, cnt = _mix(
            x2, o, p, z, w_proj_a[l].astype(bf), w_proj_b[l].astype(bf), w_out[l].astype(bf),
            g_post_mix[l][None], mod3, g_pre_ffn[l][None], w_router[l].T, b_router[l], S)

        i32 = jnp.int32
        n = cnt[:, :, 0].astype(i32)
        n_tok_tiles = n.shape[0]
        counts = jnp.sum(n, axis=0)
        padded = (counts + MOE_TILE - 1) // MOE_TILE * MOE_TILE
        gend = jnp.cumsum(padded).astype(i32)
        gstart = gend - padded
        local_off = (jnp.cumsum(n, axis=1) - n).reshape(-1).astype(i32)
        slot_off = (gstart[None, :] + jnp.cumsum(n, axis=0) - n).reshape(-1).astype(i32)
        lens = n.reshape(-1)
        steps_pad = -N_EXPERTS % n_tok_tiles
        group_pad_off = jnp.pad(gstart + counts, (0, steps_pad)).astype(i32)
        group_pad_len = jnp.pad(padded - counts, (0, steps_pad)).astype(i32)
        n_slots = -(-(T * TOP_K + n_tok_tiles * N_EXPERTS * ROW_ALIGN) // MOE_TILE) * MOE_TILE + N_EXPERTS * MOE_TILE
        tile_start = jnp.arange(n_slots // MOE_TILE, dtype=i32) * MOE_TILE
        tile_expert = jnp.minimum(jnp.sum(tile_start[:, None] >= gend[None, :], axis=1), N_EXPERTS - 1)
        n_tiles = gend[-1:] // MOE_TILE
        has_tiles = padded > 0
        expert_ids = jnp.arange(N_EXPERTS, dtype=i32)
        later = lax.cummin(jnp.where(has_tiles, expert_ids, N_EXPERTS), reverse=True)
        next_expert = jnp.concatenate([later[1:], jnp.full((1,), N_EXPERTS, i32)])
        next_expert = jnp.where(next_expert < N_EXPERTS, next_expert, -1)
        of_tile = tile_expert[:, None] == expert_ids[None, :]
        pick = lambda per_expert: jnp.sum(jnp.where(of_tile, per_expert[None, :], 0), axis=1).astype(i32)
        tile_buf = pick((jnp.cumsum(has_tiles.astype(i32)) - 1) % 2)
        tile_next = pick(next_expert)
        tile_rows = jnp.clip(pick(gstart + counts) - tile_start, 0, MOE_TILE).astype(i32)

        xs = _dispatch(local_off, slot_off, lens, group_pad_off, group_pad_len, n_tiles, xs_local, n_slots)
        half = D_FF
        b1g = b_exp1[l].reshape(N_EXPERTS, half // LANES, LANES, 2).transpose(0, 1, 3, 2)
        b1g = b1g.reshape(N_EXPERTS, 1, 2 * half)
        ys = _moe(tile_expert.astype(i32), n_tiles, tile_buf, tile_next, tile_rows, xs,
                  w_exp1[l], b1g, w_exp2[l], b_exp2[l])
        x = _combine(local_off, slot_off, lens, ys, ls.T, topw.T, x1, mod3, g_post_ffn[l][None], S)
        x = x.reshape(B, S, D)
    return x
```

```python
import functools
import math

import numpy as np
import jax
import jax.numpy as jnp
from jax import lax
from jax.experimental import pallas as pl
from jax.experimental.pallas import tpu as pltpu

D_MODEL = 1024
N_HEADS = 8
HEAD_DIM = 64
V_DIM = 2 * HEAD_DIM
DIFF_WIDTH = N_HEADS * V_DIM
POOL_WINDOWS = (2, 4, 8, 16)
POOL_GROUP_DIM = 128
POOL_WIDTH = len(POOL_WINDOWS) * POOL_GROUP_DIM
IN_COLS = 3 * DIFF_WIDTH + POOL_WIDTH + 2 * D_MODEL
ROPE_THETA = 500000.0
ROT_DIM = HEAD_DIM // 4
ROT_HALF = ROT_DIM // 2
N_EXPERTS = 32
TOP_K = 4
D_FF = D_MODEL
SWIGLU_ALPHA = 1.702
SWIGLU_LIMIT = 7.0
NORM_EPS = 1e-6
SUBLN_EPS = 1e-5
N_MOD = 6
NEG_BIG = -1e30
LAMBDA_INIT = 0.8 - 0.6 * math.exp(-0.3 * 0)

LANES = 128
MXU_DIM = 256
VMEM_LIMIT = 56 * 1024 * 1024

COL_TILE = 512
IN_ROW_TILE = 2048
IN_ROW_CHUNK = 256
COL_K = DIFF_WIDTH // COL_TILE
COL_V = 2 * DIFF_WIDTH // COL_TILE
COL_G = (3 * DIFF_WIDTH + POOL_WIDTH) // COL_TILE
ATT_BLOCK = 512
ATT_HEADS = 4
MOE_TILE = 512
TOKEN_TILE = 256
MIX_TILES = 2
ROW_ALIGN = 16
LOCAL_ROWS = TOP_K * TOKEN_TILE + N_EXPERTS * ROW_ALIGN
BLOCK_CHUNKS = LOCAL_ROWS // ROW_ALIGN

_HI = lax.Precision.HIGHEST


def _params(sem, vmem=VMEM_LIMIT):
    return pltpu.CompilerParams(dimension_semantics=sem, vmem_limit_bytes=vmem)


def _rms(x, eps):
    return x * lax.rsqrt(jnp.mean(x * x, axis=-1, keepdims=True) + eps)


def _ada_kernel(c_ref, w_ref, b_ref, o_ref):
    c = c_ref[...]
    s = c * jax.nn.sigmoid(c)
    o_ref[...] = jnp.dot(s, w_ref[...], precision=_HI, preferred_element_type=jnp.float32) + b_ref[...]


def _ada(c, w_ada, b_ada):
    B, D = c.shape
    N = w_ada.shape[1]
    tn = 1536
    return pl.pallas_call(
        _ada_kernel,
        grid=(N // tn,),
        in_specs=[pl.BlockSpec((B, D), lambda j: (0, 0)),
                  pl.BlockSpec((D, tn), lambda j: (0, j)),
                  pl.BlockSpec((1, tn), lambda j: (0, j))],
        out_specs=pl.BlockSpec((B, tn), lambda j: (0, j)),
        out_shape=jax.ShapeDtypeStruct((B, N), jnp.float32),
        compiler_params=_params(("parallel",)),
        name="ada",
    )(c, w_ada, b_ada.reshape(1, N))


def _rope_tab_kernel(pos_ref, invf_ref, phase_ref, c_ref, s_ref):
    pos = pos_ref[...].astype(jnp.float32)
    cs = jnp.cos(pos * invf_ref[...] - phase_ref[...])
    rolled = pltpu.roll(cs, LANES // 2, 1)
    lane = lax.broadcasted_iota(jnp.int32, cs.shape, 1)
    lower = lane < LANES // 2
    rot = lane % (LANES // 2) < ROT_DIM
    c_ref[...] = jnp.where(lower, cs, rolled)
    s_ref[...] = jnp.where(rot, jnp.where(lower, -rolled, cs), 0.0)


def _rope_tables(positions):
    T = positions.size
    tm = min(T, 2048)
    lane = np.arange(LANES)
    rot = lane % (LANES // 2) < ROT_DIM
    inv = ROPE_THETA ** (-(np.arange(ROT_HALF, dtype=np.float64) * 2.0 / ROT_DIM))
    invf = np.where(rot, inv[lane % ROT_HALF], 0.0).astype(np.float32)[None, :]
    phase = np.where(rot & (lane >= LANES // 2), np.pi / 2, 0.0).astype(np.float32)[None, :]
    row = pl.BlockSpec((1, LANES), lambda i: (0, 0))
    tab = pl.BlockSpec((tm, LANES), lambda i: (i, 0))
    sds = jax.ShapeDtypeStruct((T, LANES), jnp.float32)
    return pl.pallas_call(
        _rope_tab_kernel,
        grid=(T // tm,),
        in_specs=[pl.BlockSpec((tm, 1), lambda i: (i, 0)), row, row],
        out_specs=[tab, tab],
        out_shape=[sds, sds],
        compiler_params=_params(("parallel",)),
        name="rope_tables",
    )(positions.reshape(T, 1), jnp.asarray(invf), jnp.asarray(phase))


def _permute_qk_columns(w):
    D, n = w.shape
    w = w.reshape(D, n // V_DIM, 2, 4, 2, ROT_HALF)
    return w.transpose(0, 1, 4, 3, 2, 5).reshape(D, n)


def _in_proj_kernel(x_ref, g_ref, sc_ref, sh_ref, w_ref, c_ref, s_ref, z_ref, h_ref):
    j = pl.program_id(1)
    tm = x_ref.shape[0]
    rows = min(tm, IN_ROW_CHUNK)

    @pl.when(j == 0)
    def _():
        h = _rms(x_ref[...], NORM_EPS) * g_ref[...] * (1.0 + sc_ref[0]) + sh_ref[0]
        h_ref[...] = h.astype(h_ref.dtype)

    def chunks(epilogue):
        for r in range(tm // rows):
            sl = slice(r * rows, (r + 1) * rows)
            z = jnp.dot(h_ref[sl, :], w_ref[...], preferred_element_type=jnp.float32)
            z_ref[sl, :] = epilogue(z, sl).astype(z_ref.dtype)

    @pl.when(j < COL_V)
    def _():
        scale = jnp.where(j < COL_K, HEAD_DIM ** -0.5 * math.log2(math.e), 1.0)

        def rope(z, sl):
            c = c_ref[sl, :] * scale
            s = s_ref[sl, :] * scale
            parts = []
            for g in range(COL_TILE // LANES):
                zg = z[:, g * LANES:(g + 1) * LANES]
                parts.append(zg * c + pltpu.roll(zg, LANES // 2, 1) * s)
            return jnp.concatenate(parts, axis=1)

        chunks(rope)

    @pl.when((j >= COL_V) & (j < COL_G))
    def _():
        chunks(lambda z, sl: z)

    @pl.when(j >= COL_G)
    def _():
        chunks(lambda z, sl: 0.5 * jnp.tanh(0.5 * z) + 0.5)


def _in_proj(x2, g_pre, mod3, w_in_bf, tabs, seq):
    T, D = x2.shape
    tm = min(seq, IN_ROW_TILE)
    per_b = seq // tm
    mod_spec = lambda col: pl.BlockSpec((1, 1, D), lambda i, j: (i // per_b, 0, col))
    tab = pl.BlockSpec((tm, LANES), lambda i, j: (i, 0))
    return pl.pallas_call(
        _in_proj_kernel,
        grid=(T // tm, IN_COLS // COL_TILE),
        in_specs=[pl.BlockSpec((tm, D), lambda i, j: (i, 0)),
                  pl.BlockSpec((1, D), lambda i, j: (0, 0)),
                  mod_spec(1), mod_spec(0),
                  pl.BlockSpec((D, COL_TILE), lambda i, j: (0, j)),
                  tab, tab],
        out_specs=pl.BlockSpec((tm, COL_TILE), lambda i, j: (i, j)),
        out_shape=jax.ShapeDtypeStruct((T, IN_COLS), jnp.bfloat16),
        scratch_shapes=[pltpu.VMEM((tm, D), jnp.bfloat16)],
        compiler_params=_params(("parallel", "arbitrary")),
        name="in_proj",
    )(x2, g_pre, mod3, mod3, w_in_bf, *tabs)


def _attn_kernel(lq1_ref, lk1_ref, lq2_ref, lk2_ref, gs_ref, q_ref, k_ref, v_ref, o_ref, vt_ref, *acc_refs):
    i = pl.program_id(2)
    blk = q_ref.shape[0]
    nblk = v_ref.shape[0] // blk
    chains = [(h, comp) for h in range(ATT_HEADS) for comp in range(2)]

    @pl.when(i == 0)
    def _():
        for c in range(nblk):
            for h in range(ATT_HEADS):
                v = v_ref[c * blk:(c + 1) * blk, h * V_DIM:(h + 1) * V_DIM]
                vt_ref[c, h] = v.astype(jnp.float32).T.astype(vt_ref.dtype)

    lam = (jnp.exp(jnp.sum(lq1_ref[...] * lk1_ref[...], axis=-1, keepdims=True))
           - jnp.exp(jnp.sum(lq2_ref[...] * lk2_ref[...], axis=-1, keepdims=True))
           + LAMBDA_INIT)
    lane_comp = lax.broadcasted_iota(jnp.int32, (blk, V_DIM), 1) // ROT_HALF % 2
    qs = []
    for h, comp in chains:
        q = q_ref[:, h * V_DIM:(h + 1) * V_DIM]
        qs.append(jnp.where(lane_comp == comp, q, jnp.zeros_like(q)))
    nt = (((1,), (1,)), ((), ()))
    for acc_ref in acc_refs:
        acc_ref[...] = jnp.zeros_like(acc_ref)

    def step(c, carry, mask=None):
        off = pl.multiple_of(c * blk, blk)
        scores = []
        for n, (h, comp) in enumerate(chains):
            k = k_ref[pl.ds(off, blk), h * V_DIM:(h + 1) * V_DIM]
            scores.append(lax.dot_general(k, qs[n], nt, preferred_element_type=jnp.float32))
        out, probs, alphas = [], [], []
        for n, s in enumerate(scores):
            m, l = carry[n]
            if mask is not None:
                s = jnp.where(mask, s, NEG_BIG)
            m_new = jnp.maximum(m, jnp.max(s, axis=0, keepdims=True))
            alpha = jnp.exp2(m - m_new)
            p = jnp.exp2(s - m_new)
            out.append((m_new, alpha * l + jnp.sum(p, axis=0, keepdims=True)))
            probs.append(p.astype(vt_ref.dtype))
            alphas.append(alpha)
        for n, (h, comp) in enumerate(chains):
            pv = jnp.dot(vt_ref[c, h], probs[n], preferred_element_type=jnp.float32)
            acc_refs[n][...] = alphas[n] * acc_refs[n][...] + pv
        return tuple(out)

    init = tuple((jnp.full((1, blk), NEG_BIG, jnp.float32), jnp.zeros((1, blk), jnp.float32))
                 for _ in chains)
    carry = lax.fori_loop(0, i, step, init)

    key = lax.broadcasted_iota(jnp.int32, (blk, blk), 0)
    qry = lax.broadcasted_iota(jnp.int32, (blk, blk), 1)
    carry = step(i, carry, key <= qry)
    for h in range(ATT_HEADS):
        (_, l1), (_, l2) = carry[2 * h], carry[2 * h + 1]
        ot = acc_refs[2 * h][...] / l1 - lam * (acc_refs[2 * h + 1][...] / l2)
        o = _rms(ot.T, SUBLN_EPS) * gs_ref[...] * (1.0 - LAMBDA_INIT)
        o_ref[:, h * V_DIM:(h + 1) * V_DIM] = o.astype(o_ref.dtype)


def _diff_attention(z, lams, g_subln, batch, seq):
    T = z.shape[0]
    blk = min(ATT_BLOCK, seq)
    nq = seq // blk
    width = ATT_HEADS * V_DIM
    vec = pl.BlockSpec((1, HEAD_DIM), lambda b, h, i: (0, 0))
    kcol = DIFF_WIDTH // width
    return pl.pallas_call(
        _attn_kernel,
        grid=(batch, N_HEADS // ATT_HEADS, nq),
        in_specs=[vec, vec, vec, vec,
                  pl.BlockSpec((1, V_DIM), lambda b, h, i: (0, 0)),
                  pl.BlockSpec((blk, width), lambda b, h, i: (b * nq + i, h)),
                  pl.BlockSpec((seq, width), lambda b, h, i: (b, kcol + h)),
                  pl.BlockSpec((seq, width), lambda b, h, i: (b, 2 * kcol + h))],
        out_specs=pl.BlockSpec((blk, width), lambda b, h, i: (b * nq + i, h)),
        out_shape=jax.ShapeDtypeStruct((T, DIFF_WIDTH), jnp.bfloat16),
        scratch_shapes=[pltpu.VMEM((nq, ATT_HEADS, V_DIM, blk), jnp.bfloat16)]
        + [pltpu.VMEM((V_DIM, blk), jnp.float32)] * (2 * ATT_HEADS),
        compiler_params=_params(("parallel", "parallel", "arbitrary")),
        name="diff_attn",
    )(*lams, g_subln, z, z, z)


def _pool_kernel(u_ref, w_ref, ps_ref, o_ref):
    g = pl.program_id(1)
    u = u_ref[...].astype(jnp.float32)
    t = lax.broadcasted_iota(jnp.int32, u.shape, 0)

    def shifted(x, k):
        return jnp.where(t >= k, pltpu.roll(x, k, 0), 0.0)

    s = u
    d = jnp.zeros_like(u)
    for gi, w in enumerate(POOL_WINDOWS):
        s = s + shifted(s, w // 2)
        cnt = jnp.minimum(t + 1, w).astype(jnp.float32)
        d = jnp.where(g == gi, s / cnt - u, d)
    y = jnp.dot(d.astype(jnp.bfloat16), w_ref[0].astype(jnp.bfloat16), preferred_element_type=jnp.float32)
    o_ref[...] = (y * ps_ref[...]).astype(o_ref.dtype)


def _pool(z, w_pool, pool_scale, batch, seq):
    T = z.shape[0]
    G = len(POOL_WINDOWS)
    ucol = 3 * DIFF_WIDTH // POOL_GROUP_DIM
    return pl.pallas_call(
        _pool_kernel,
        grid=(batch, G),
        in_specs=[pl.BlockSpec((seq, POOL_GROUP_DIM), lambda b, g: (b, ucol + g)),
                  pl.BlockSpec((1, POOL_GROUP_DIM, POOL_GROUP_DIM), lambda b, g: (g, 0, 0)),
                  pl.BlockSpec((1, POOL_GROUP_DIM), lambda b, g: (0, g))],
        out_specs=pl.BlockSpec((seq, POOL_GROUP_DIM), lambda b, g: (b, g)),
        out_shape=jax.ShapeDtypeStruct((T, POOL_WIDTH), jnp.bfloat16),
        compiler_params=_params(("parallel", "parallel")),
        name="pool",
    )(z, w_pool, pool_scale)


def _mix_kernel(x_ref, o_ref, p_ref, ga0, ga1, gb0, gb1, wa_ref, wb_ref, wo_ref, gpm_ref, gt1_ref,
                gpf_ref, sc2_ref, sh2_ref, wr_ref, br_ref,
                x1_ref, xl_ref, ls_ref, tw_ref, cnt_ref):
    tiles = range(x_ref.shape[0] // TOKEN_TILE)
    h2s = [_mix_front(t, x_ref, o_ref, p_ref, ga0, ga1, gb0, gb1, wa_ref, wb_ref, wo_ref, gpm_ref, gt1_ref,
                      gpf_ref, sc2_ref, sh2_ref, x1_ref) for t in tiles]
    places = [_mix_route(t, h2s[t], wr_ref, br_ref, ls_ref, tw_ref, cnt_ref) for t in tiles]
    for t in tiles:
        xl = jnp.dot(places[t], h2s[t].astype(jnp.bfloat16), preferred_element_type=jnp.float32)
        xl_ref[t * LOCAL_ROWS:(t + 1) * LOCAL_ROWS, :] = xl.astype(xl_ref.dtype)


def _mix_front(t, x_ref, o_ref, p_ref, ga0, ga1, gb0, gb1, wa_ref, wb_ref, wo_ref, gpm_ref, gt1_ref,
               gpf_ref, sc2_ref, sh2_ref, x1_ref):
    rows = slice(t * TOKEN_TILE, (t + 1) * TOKEN_TILE)
    ya = jnp.dot(o_ref[rows, :], wa_ref[...], preferred_element_type=jnp.float32)
    yb = jnp.dot(p_ref[rows, :], wb_ref[...], preferred_element_type=jnp.float32)
    ga = jnp.concatenate([ga0[rows, :], ga1[rows, :]], axis=1).astype(jnp.float32)
    gb = jnp.concatenate([gb0[rows, :], gb1[rows, :]], axis=1).astype(jnp.float32)
    merged = (ga * ya + gb * yb).astype(jnp.bfloat16)
    mixed = jnp.dot(merged, wo_ref[...], preferred_element_type=jnp.float32)
    x1 = x_ref[rows, :] + gt1_ref[0] * (_rms(mixed, NORM_EPS) * gpm_ref[...])
    x1_ref[rows, :] = x1
    return _rms(x1, NORM_EPS) * gpf_ref[...] * (1.0 + sc2_ref[0]) + sh2_ref[0]


def _mix_route(t, h2, wr_ref, br_ref, ls_ref, tw_ref, cnt_ref):
    tm = TOKEN_TILE
    rows = slice(t * tm, (t + 1) * tm)
    logits =lax.dot_general(wr_ref[...], h2, (((1,), (1,)), ((), ())), precision=_HI,
                             preferred_element_type=jnp.float32) + br_ref[...]
    eid = lax.broadcasted_iota(jnp.int32, logits.shape, 0)
    work = logits
    sels, vals = [], []
    for _ in range(TOP_K):
        mx = jnp.max(work, axis=0, keepdims=True)
        idx = jnp.min(jnp.where(work == mx, eid, N_EXPERTS), axis=0, keepdims=True)
        sel = eid == idx
        work = jnp.where(sel, -jnp.inf, work)
        sels.append(sel)
        vals.append(mx)
    ex = [jnp.exp(v - vals[0]) for v in vals]
    den = ex[0] + ex[1] + ex[2] + ex[3]
    onehot = jnp.zeros(logits.shape, jnp.float32)
    for sel in sels:
        onehot = jnp.where(sel, 1.0, onehot)
    r = lax.broadcasted_iota(jnp.int32, (tm, tm), 0)
    c = lax.broadcasted_iota(jnp.int32, (tm, tm), 1)
    tri = jnp.where(r < c, 1.0, 0.0).astype(jnp.bfloat16)
    rank = jnp.dot(onehot.astype(jnp.bfloat16), tri, preferred_element_type=jnp.float32)
    counts = jnp.broadcast_to(jnp.sum(onehot, axis=1, keepdims=True), (N_EXPERTS, LANES))
    rounded = jnp.floor((counts + (ROW_ALIGN - 1)) * (1.0 / ROW_ALIGN)) * ROW_ALIGN
    er = lax.broadcasted_iota(jnp.int32, (N_EXPERTS, N_EXPERTS), 0)
    ec = lax.broadcasted_iota(jnp.int32, (N_EXPERTS, N_EXPERTS), 1)
    below = jnp.where(ec < er, 1.0, 0.0).astype(jnp.bfloat16)
    offset = jnp.dot(below, rounded.astype(jnp.bfloat16), preferred_element_type=jnp.float32)
    slot_of = rank + offset[:, 0:1]
    cnt_ref[t] = rounded
    slot_id = lax.broadcasted_iota(jnp.int32, (LOCAL_ROWS, tm), 0)
    place = jnp.zeros((LOCAL_ROWS, tm), jnp.float32)
    for kk in range(TOP_K):
        ls = jnp.sum(jnp.where(sels[kk], slot_of, 0.0), axis=0, keepdims=True).astype(jnp.int32)
        ls_ref[kk:kk + 1, rows] = ls
        tw_ref[kk:kk + 1, rows] = ex[kk] / den
        place = jnp.where(slot_id == ls, 1.0, place)
    return place.astype(jnp.bfloat16)


def _mix(x2, o, p, z, wa, wb, wo, g_post_mix, mod3, g_pre_ffn, w_router_t, b_router, seq):
    T, D = x2.shape
    sub = min(MIX_TILES, seq // TOKEN_TILE)
    tm = sub * TOKEN_TILE
    per_b = seq // tm
    E = N_EXPERTS
    row = lambda n: pl.BlockSpec((1, n), lambda i: (0, 0))
    mod_spec = lambda col: pl.BlockSpec((1, 1, D), lambda i: (i // per_b, 0, col))
    gate = lambda cb: pl.BlockSpec((tm, COL_TILE), lambda i: (i, cb))
    full = lambda a: pl.BlockSpec(a.shape, lambda i: (0,) * a.ndim)
    tile = pl.BlockSpec((tm, D), lambda i: (i, 0))
    small = pl.BlockSpec((TOP_K, tm), lambda i: (0, i))
    return pl.pallas_call(
        _mix_kernel,
        grid=(T // tm,),
        in_specs=[tile, tile,
                  pl.BlockSpec((tm, POOL_WIDTH), lambda i: (i, 0)),
                  gate(COL_G), gate(COL_G + 1), gate(COL_G + 2), gate(COL_G + 3),
                  full(wa), full(wb), full(wo), row(D), mod_spec(2),
                  row(D), mod_spec(4), mod_spec(3), full(w_router_t),
                  pl.BlockSpec((E, 1), lambda i: (0, 0))],
        out_specs=[tile, pl.BlockSpec((sub * LOCAL_ROWS, D), lambda i: (i, 0)), small, small,
                   pl.BlockSpec((sub, E, LANES), lambda i: (i, 0, 0))],
        out_shape=[jax.ShapeDtypeStruct((T, D), jnp.float32),
                   jax.ShapeDtypeStruct((T // TOKEN_TILE * LOCAL_ROWS, D), jnp.bfloat16),
                   jax.ShapeDtypeStruct((TOP_K, T), jnp.int32),
                   jax.ShapeDtypeStruct((TOP_K, T), jnp.float32),
                   jax.ShapeDtypeStruct((T // TOKEN_TILE, E, LANES), jnp.float32)],
        compiler_params=_params(("parallel",)),
        name="mix_tail",
    )(x2, o, p, z, z, z, z, wa, wb, wo, g_post_mix, mod3, g_pre_ffn, mod3, mod3, w_router_t,
      b_router.reshape(E, 1))


def _chunk(ref, row):
    return ref.at[pl.ds(pl.multiple_of(row, ROW_ALIGN), ROW_ALIGN), :]


def _dispatch_kernel(chunk_slot_ref, n_chunk_ref, zero_off_ref, zero_len_ref, ntile_ref,
                     xl_ref, xs_ref, zero_ref, sem, zsem, *, zero_per_step):
    i = pl.program_id(0)

    @pl.when(i == 0)
    def _():
        zero_ref[...] = jnp.zeros_like(zero_ref)

    def copies(act):
        def body(k, _):
            act(pltpu.make_async_copy(_chunk(xl_ref, k * ROW_ALIGN),
                                      _chunk(xs_ref, chunk_slot_ref[i * BLOCK_CHUNKS + k]), sem))
            return 0
        lax.fori_loop(0, n_chunk_ref[i], body, 0)
        for r in range(zero_per_step):
            idx = i * zero_per_step + r

            def zbody(k, _, idx=idx):
                act(pltpu.make_async_copy(_chunk(zero_ref, 0),
                                          _chunk(xs_ref, zero_off_ref[idx] + k * ROW_ALIGN), zsem))
                return 0
            lax.fori_loop(0, zero_len_ref[idx] // ROW_ALIGN, zbody, 0)

    def tails(act):
        def body(t, _):
            off = pl.multiple_of(t * MOE_TILE, MOE_TILE)
            act(pltpu.make_async_copy(zero_ref, xs_ref.at[pl.ds(off, MOE_TILE), :], zsem))
            return 0
        lax.fori_loop(ntile_ref[0], xs_ref.shape[0] // MOE_TILE, body, 0)

    for act in (lambda cp: cp.start(), lambda cp: cp.wait()):
        copies(act)
        pl.when(i == 0)(functools.partial(tails, act))


def _dispatch(chunk_slot, n_chunks, zero_off, zero_len, n_tiles, xs_local, n_slots):
    W = xs_local.shape[1]
    steps = n_chunks.shape[0]
    return pl.pallas_call(
        functools.partial(_dispatch_kernel, zero_per_step=zero_len.shape[0] // steps),
        grid_spec=pltpu.PrefetchScalarGridSpec(
            num_scalar_prefetch=5,
            grid=(steps,),
            in_specs=[pl.BlockSpec((LOCAL_ROWS, W), lambda i, *_: (i, 0))],
            out_specs=pl.BlockSpec(memory_space=pl.ANY),
            scratch_shapes=[pltpu.VMEM((MOE_TILE, W), xs_local.dtype),
                            pltpu.SemaphoreType.DMA, pltpu.SemaphoreType.DMA]),
        out_shape=jax.ShapeDtypeStruct((n_slots, W), xs_local.dtype),
        compiler_params=_params(("arbitrary",)),
        name="dispatch",
    )(chunk_slot, n_chunks, zero_off, zero_len, n_tiles, xs_local)


def _moe_kernel(te_ref, nt_ref, buf_ref, nxt_ref, rows_ref, x_ref, w1_hbm, b1_ref, w2_hbm, b2_ref, perm_ref, y_ref,
                w1raw_ref, w2raw_ref, w1s_ref, w2s_ref, sem):
    j = pl.program_id(0)
    prev = te_ref[jnp.maximum(j - 1, 0)]
    fresh = (j == 0) | (te_ref[j] != prev)
    nblk = w1s_ref.shape[1] // MXU_DIM

    def weight_copies(e, b):
        return (pltpu.make_async_copy(w1_hbm.at[e], w1raw_ref.at[b], sem.at[0, b]),
                pltpu.make_async_copy(w2_hbm.at[e], w2raw_ref.at[b], sem.at[1, b]))

    @pl.when(fresh & (j < nt_ref[0]))
    def _():
        e, b, nxt = te_ref[j], buf_ref[j], nxt_ref[j]

        @pl.when(j == 0)
        def _():
            for cp in weight_copies(e, b):
                cp.start()

        for cp in weight_copies(e, b):
            cp.wait()

        @pl.when(nxt >= 0)
        def _():
            for cp in weight_copies(nxt, 1 - b):
                cp.start()

        for c in range(nblk):
            cols = slice(c * MXU_DIM, (c + 1) * MXU_DIM)
            blk = w1raw_ref[b, :, cols].astype(jnp.bfloat16)
            w1s_ref[:, cols] = jnp.dot(blk, perm_ref[...],
                                       preferred_element_type=jnp.float32).astype(jnp.bfloat16)
        w2s_ref[...] = w2raw_ref[b].astype(jnp.bfloat16)

    def experts_mlp(rows):
        z = jnp.dot(x_ref[:rows, :], w1s_ref[...], preferred_element_type=jnp.float32) + b1_ref[0]
        acts = []
        for c in range(nblk):
            gate = jnp.minimum(z[:, c * MXU_DIM:c * MXU_DIM + LANES], SWIGLU_LIMIT)
            up = jnp.clip(z[:, c * MXU_DIM + LANES:(c + 1) * MXU_DIM], -SWIGLU_LIMIT, SWIGLU_LIMIT)
            acts.append(gate * jax.nn.sigmoid(SWIGLU_ALPHA * gate) * (up + 1.0))
        a = jnp.concatenate(acts, axis=1).astype(jnp.bfloat16)
        y = jnp.dot(a, w2s_ref[...], preferred_element_type=jnp.float32) + b2_ref[0]
        y_ref[:rows, :] = y.astype(y_ref.dtype)
        if rows < y_ref.shape[0]:
            y_ref[rows:, :] = jnp.zeros((y_ref.shape[0] - rows, y_ref.shape[1]), y_ref.dtype)

    half = y_ref.shape[0] // 2
    active = j < nt_ref[0]
    pl.when(active & (rows_ref[j] > half))(lambda: experts_mlp(2 * half))
    pl.when(active & (rows_ref[j] <= half))(lambda: experts_mlp(half))

    @pl.when(j >= nt_ref[0])
    def _():
        y_ref[...] = jnp.zeros_like(y_ref)


def _regroup_perm():
    src = np.arange(MXU_DIM)
    dst = np.where(src % 2 == 0, src // 2, LANES + src // 2)
    perm = np.zeros((MXU_DIM, MXU_DIM), np.float32)
    perm[src, dst] = 1.0
    return jnp.asarray(perm, jnp.bfloat16)


def _moe(tile_expert, n_tiles, tile_buf, tile_next, tile_rows, xs, w1, b1g, w2, b2):
    n_slots, W = xs.shape
    tm = MOE_TILE
    E, D, F2 = w1.shape
    F = w2.shape[1]
    xmap = lambda j, te, nt, *_: (jnp.minimum(j, nt[0] - 1), 0)
    emap = lambda j, te, *_: (te[j], 0, 0)
    return pl.pallas_call(
        _moe_kernel,
        grid_spec=pltpu.PrefetchScalarGridSpec(
            num_scalar_prefetch=5,
            grid=(n_slots // tm,),
            in_specs=[pl.BlockSpec((tm, W), xmap),
                      pl.BlockSpec(memory_space=pl.ANY),
                      pl.BlockSpec((1, 1, F2), emap),
                      pl.BlockSpec(memory_space=pl.ANY),
                      pl.BlockSpec((1, 1, D), emap),
                      pl.BlockSpec((MXU_DIM, MXU_DIM), lambda j, *_: (0, 0))],
            out_specs=pl.BlockSpec((tm, W), lambda j, *_: (j, 0)),
            scratch_shapes=[pltpu.VMEM((2, D, F2), w1.dtype), pltpu.VMEM((2, F, D), w2.dtype),
                            pltpu.VMEM((D, F2), jnp.bfloat16), pltpu.VMEM((F, D), jnp.bfloat16),
                            pltpu.SemaphoreType.DMA((2, 2))]),
        out_shape=jax.ShapeDtypeStruct((n_slots, W), xs.dtype),
        compiler_params=_params(("arbitrary",)),
        name="moe_experts",
    )(tile_expert, n_tiles, tile_buf, tile_next, tile_rows, xs, w1, b1g, w2, b2.reshape(E, 1, D),
      _regroup_perm())


def _combine_kernel(chunk_slot_ref, n_chunk_ref, ys_ref, ls_ref, w_ref, x1_ref, gt2_ref, g_ref,
                    o_ref, yl_ref, sem):
    i = pl.program_id(0)
    tc = x1_ref.shape[0]

    def pieces(tile, act):
        buf = tile % 2

        def body(k, _):
            act(pltpu.make_async_copy(_chunk(ys_ref, chunk_slot_ref[tile * BLOCK_CHUNKS + k]),
                                      _chunk(yl_ref.at[buf], k * ROW_ALIGN), sem.at[buf]))
            return 0
        lax.fori_loop(0, n_chunk_ref[tile], body, 0)

    def fetch(tile):
        yl_ref[tile % 2, TOP_K * tc:, :] = jnp.zeros((LOCAL_ROWS - TOP_K * tc, yl_ref.shape[2]), yl_ref.dtype)
        pieces(tile, lambda cp: cp.start())

    pl.when(i == 0)(lambda: fetch(i))
    pl.when(i + 1 < pl.num_programs(0))(lambda: fetch(i + 1))
    slot_id = lax.broadcasted_iota(jnp.int32, (tc, LOCAL_ROWS), 1)
    ls = ls_ref[...]
    w = w_ref[...]
    mix = jnp.zeros((tc, LOCAL_ROWS), jnp.float32)
    for k in range(TOP_K):
        mix = jnp.where(slot_id == ls[:, k:k + 1], w[:, k:k + 1], mix)
    hi = mix.astype(jnp.bfloat16)
    lo = (mix - hi.astype(jnp.float32)).astype(jnp.bfloat16)
    pieces(i, lambda cp: cp.wait())
    y = yl_ref[i % 2]
    f = (jnp.dot(hi, y, preferred_element_type=jnp.float32)
         + jnp.dot(lo, y, preferred_element_type=jnp.float32))
    o_ref[...] = x1_ref[...] + gt2_ref[0] * (_rms(f, NORM_EPS) * g_ref[...])


def _combine(chunk_slot, n_chunks, ys, ls_t, topw_t, x1, mod3, g_post_ffn, seq):
    T, D = x1.shape
    tc = TOKEN_TILE
    per_b = seq // tc
    tile = pl.BlockSpec((tc, D), lambda i, *_: (i, 0))
    small = pl.BlockSpec((tc, TOP_K), lambda i, *_: (i, 0))
    return pl.pallas_call(
        _combine_kernel,
        grid_spec=pltpu.PrefetchScalarGridSpec(
            num_scalar_prefetch=2,
            grid=(T // tc,),
            in_specs=[pl.BlockSpec(memory_space=pl.ANY), small, small, tile,
                      pl.BlockSpec((1, 1, D), lambda i, *_: (i // per_b, 0, 5)),
                      pl.BlockSpec((1, D), lambda i, *_: (0, 0))],
            out_specs=tile,
            scratch_shapes=[pltpu.VMEM((2, LOCAL_ROWS, ys.shape[1]), ys.dtype),
                            pltpu.SemaphoreType.DMA((2,))]),
        out_shape=jax.ShapeDtypeStruct((T, D), jnp.float32),
        compiler_params=_params(("arbitrary",)),
        name="combine",
    )(chunk_slot, n_chunks, ys, ls_t, topw_t, x1, mod3, g_post_ffn)


def kernel(x, c, positions, w_ada, b_ada, g_pre_mix, w_in, lambda_q1, lambda_k1, lambda_q2, lambda_k2,
           g_subln, w_pool, pool_scale, w_proj_a, w_proj_b, w_out, g_post_mix, g_pre_ffn,
           w_router, b_router, w_exp1, b_exp1, w_exp2, b_exp2, g_post_ffn):
    B, S, D = x.shape
    T = B * S
    bf = jnp.bfloat16
    tabs = _rope_tables(positions)
    for l in range(w_ada.shape[0]):
        x2 = x.reshape(T, D)
        mod3 = _ada(c, w_ada[l], b_ada[l]).reshape(B, 1, N_MOD * D)
        n_qk = 2 * DIFF_WIDTH
        w_in_bf = jnp.concatenate([_permute_qk_columns(w_in[l][:, :n_qk]), w_in[l][:, n_qk:]], axis=1).astype(bf)
        z = _in_proj(x2, g_pre_mix[l][None], mod3, w_in_bf, tabs, S)
        lams = [v[l][None] for v in (lambda_q1, lambda_k1, lambda_q2, lambda_k2)]
        o = _diff_attention(z, lams, g_subln[l][None], B, S)
        p = _pool(z, w_pool[l], pool_scale[l][None], B, S)
        x1, xs_local, ls, topw, cnt = _mix(
            x2, o, p, z, w_proj_a[l].astype(bf), w_proj_b[l].astype(bf), w_out[l].astype(bf),
            g_post_mix[l][None], mod3, g_pre_ffn[l][None], w_router[l].T, b_router[l], S)

        i32 = jnp.int32
        n = cnt[:, :, 0].astype(i32)
        n_tok_tiles = n.shape[0]
        counts = jnp.sum(n, axis=0)
        padded = (counts + MOE_TILE - 1) // MOE_TILE * MOE_TILE
        gend = jnp.cumsum(padded).astype(i32)
        gstart = gend - padded
        local_end = jnp.cumsum(n, axis=1)
        slot_off = gstart[None, :] + jnp.cumsum(n, axis=0) - n
        chunk_row = jnp.arange(BLOCK_CHUNKS, dtype=i32) * ROW_ALIGN
        piece = jnp.sum(local_end[:, None, :] <= chunk_row[None, :, None], axis=2)
        in_piece = piece[:, :, None] == jnp.arange(N_EXPERTS, dtype=i32)[None, None, :]
        shift = jnp.sum(jnp.where(in_piece, (slot_off - (local_end - n))[:, None, :], 0), axis=2)
        chunk_slot = (shift + chunk_row[None, :]).reshape(-1).astype(i32)
        n_chunks = (local_end[:, -1] // ROW_ALIGN).astype(i32)
        steps_pad = -N_EXPERTS % n_tok_tiles
        group_pad_off = jnp.pad(gstart + counts, (0, steps_pad)).astype(i32)
        group_pad_len = jnp.pad(padded - counts, (0, steps_pad)).astype(i32)
        n_slots = -(-(T * TOP_K + n_tok_tiles * N_EXPERTS * ROW_ALIGN) // MOE_TILE) * MOE_TILE + N_EXPERTS * MOE_TILE
        tile_start = jnp.arange(n_slots // MOE_TILE, dtype=i32) * MOE_TILE
        tile_expert = jnp.minimum(jnp.sum(tile_start[:, None] >= gend[None, :], axis=1), N_EXPERTS - 1)
        n_tiles = gend[-1:] // MOE_TILE
        has_tiles = padded > 0
        expert_ids = jnp.arange(N_EXPERTS, dtype=i32)
        later = lax.cummin(jnp.where(has_tiles, expert_ids, N_EXPERTS), reverse=True)
        next_expert = jnp.concatenate([later[1:], jnp.full((1,), N_EXPERTS, i32)])
        next_expert = jnp.where(next_expert < N_EXPERTS, next_expert, -1)
        of_tile = tile_expert[:, None] == expert_ids[None, :]
        pick = lambda per_expert: jnp.sum(jnp.where(of_tile, per_expert[None, :], 0), axis=1).astype(i32)
        tile_buf = pick((jnp.cumsum(has_tiles.astype(i32)) - 1) % 2)
        tile_next = pick(next_expert)
        tile_rows = jnp.clip(pick(gstart + counts) - tile_start, 0, MOE_TILE).astype(i32)

        xs = _dispatch(chunk_slot, n_chunks, group_pad_off, group_pad_len, n_tiles, xs_local, n_slots)
        half = D_FF
        b1g = b_exp1[l].reshape(N_EXPERTS, half // LANES, LANES, 2).transpose(0, 1, 3, 2)
        b1g = b1g.reshape(N_EXPERTS, 1, 2 * half)
        ys = _moe(tile_expert.astype(i32), n_tiles, tile_buf, tile_next, tile_rows, xs,
                  w_exp1[l], b1g, w_exp2[l], b_exp2[l])
        x = _combine(chunk_slot, n_chunks, ys, ls.T, topw.T, x1, mod3, g_post_ffn[l][None], S)
        x = x.reshape(B, S, D)
    return x
```

```python
import functools
import math

import numpy as np
import jax
import jax.numpy as jnp
from jax import lax
from jax.experimental import pallas as pl
from jax.experimental.pallas import tpu as pltpu

D_MODEL = 1024
N_HEADS = 8
HEAD_DIM = 64
V_DIM = 2 * HEAD_DIM
DIFF_WIDTH = N_HEADS * V_DIM
POOL_WINDOWS = (2, 4, 8, 16)
POOL_GROUP_DIM = 128
POOL_WIDTH = len(POOL_WINDOWS) * POOL_GROUP_DIM
IN_COLS = 3 * DIFF_WIDTH + POOL_WIDTH + 2 * D_MODEL
ROPE_THETA = 500000.0
ROT_DIM = HEAD_DIM // 4
ROT_HALF = ROT_DIM // 2
N_EXPERTS = 32
TOP_K = 4
D_FF = D_MODEL
SWIGLU_ALPHA = 1.702
SWIGLU_LIMIT = 7.0
NORM_EPS = 1e-6
SUBLN_EPS = 1e-5
N_MOD = 6
NEG_BIG = -1e30
LAMBDA_INIT = 0.8 - 0.6 * math.exp(-0.3 * 0)

LANES = 128
MXU_DIM = 256
VMEM_LIMIT = 56 * 1024 * 1024

COL_TILE = 512
IN_ROW_TILE = 2048
IN_ROW_CHUNK = 256
COL_K = DIFF_WIDTH // COL_TILE
COL_V = 2 * DIFF_WIDTH // COL_TILE
COL_G = (3 * DIFF_WIDTH + POOL_WIDTH) // COL_TILE
ATT_BLOCK = 512
ATT_HEADS = 4
MOE_TILE = 512
TOKEN_TILE = 256
MIX_TILES = 2
ROW_ALIGN = 16
LOCAL_ROWS = TOP_K * TOKEN_TILE + N_EXPERTS * ROW_ALIGN
BLOCK_CHUNKS = LOCAL_ROWS // ROW_ALIGN

_HI = lax.Precision.HIGHEST


def _params(sem, vmem=VMEM_LIMIT):
    return pltpu.CompilerParams(dimension_semantics=sem, vmem_limit_bytes=vmem)


def _rms(x, eps):
    return x * lax.rsqrt(jnp.mean(x * x, axis=-1, keepdims=True) + eps)


def _ada_kernel(c_ref, w_ref, b_ref, o_ref):
    c = c_ref[...]
    s = c * jax.nn.sigmoid(c)
    o_ref[...] = jnp.dot(s, w_ref[...], precision=_HI, preferred_element_type=jnp.float32) + b_ref[...]


def _ada(c, w_ada, b_ada):
    B, D = c.shape
    N = w_ada.shape[1]
    tn = 1536
    return pl.pallas_call(
        _ada_kernel,
        grid=(N // tn,),
        in_specs=[pl.BlockSpec((B, D), lambda j: (0, 0)),
                  pl.BlockSpec((D, tn), lambda j: (0, j)),
                  pl.BlockSpec((1, tn), lambda j: (0, j))],
        out_specs=pl.BlockSpec((B, tn), lambda j: (0, j)),
        out_shape=jax.ShapeDtypeStruct((B, N), jnp.float32),
        compiler_params=_params(("parallel",)),
        name="ada",
    )(c, w_ada, b_ada.reshape(1, N))


def _rope_tab_kernel(pos_ref, invf_ref, phase_ref, c_ref, s_ref):
    pos = pos_ref[...].astype(jnp.float32)
    cs = jnp.cos(pos * invf_ref[...] - phase_ref[...])
    rolled = pltpu.roll(cs, LANES // 2, 1)
    lane = lax.broadcasted_iota(jnp.int32, cs.shape, 1)
    lower = lane < LANES // 2
    rot = lane % (LANES // 2) < ROT_DIM
    c_ref[...] = jnp.where(lower, cs, rolled)
    s_ref[...] = jnp.where(rot, jnp.where(lower, -rolled, cs), 0.0)


def _rope_tables(positions):
    T = positions.size
    tm = min(T, 2048)
    lane = np.arange(LANES)
    rot = lane % (LANES // 2) < ROT_DIM
    inv = ROPE_THETA ** (-(np.arange(ROT_HALF, dtype=np.float64) * 2.0 / ROT_DIM))
    invf = np.where(rot, inv[lane % ROT_HALF], 0.0).astype(np.float32)[None, :]
    phase = np.where(rot & (lane >= LANES // 2), np.pi / 2, 0.0).astype(np.float32)[None, :]
    row = pl.BlockSpec((1, LANES), lambda i: (0, 0))
    tab = pl.BlockSpec((tm, LANES), lambda i: (i, 0))
    sds = jax.ShapeDtypeStruct((T, LANES), jnp.float32)
    return pl.pallas_call(
        _rope_tab_kernel,
        grid=(T // tm,),
        in_specs=[pl.BlockSpec((tm, 1), lambda i: (i, 0)), row, row],
        out_specs=[tab, tab],
        out_shape=[sds, sds],
        compiler_params=_params(("parallel",)),
        name="rope_tables",
    )(positions.reshape(T, 1), jnp.asarray(invf), jnp.asarray(phase))


def _permute_qk_columns(w):
    D, n = w.shape
    w = w.reshape(D, n // V_DIM, 2, 4, 2, ROT_HALF)
    return w.transpose(0, 1, 4, 3, 2, 5).reshape(D, n)


def _in_proj_kernel(x_ref, g_ref, sc_ref, sh_ref, w_ref, c_ref, s_ref, z_ref, h_ref):
    j = pl.program_id(1)
    tm = x_ref.shape[0]
    rows = min(tm, IN_ROW_CHUNK)

    @pl.when(j == 0)
    def _():
        h = _rms(x_ref[...], NORM_EPS) * g_ref[...] * (1.0 + sc_ref[0]) + sh_ref[0]
        h_ref[...] = h.astype(h_ref.dtype)

    def chunks(epilogue):
        for r in range(tm // rows):
            sl = slice(r * rows, (r + 1) * rows)
            z = jnp.dot(h_ref[sl, :], w_ref[...], preferred_element_type=jnp.float32)
            z_ref[sl, :] = epilogue(z, sl).astype(z_ref.dtype)

    @pl.when(j < COL_V)
    def _():
        scale = jnp.where(j < COL_K, HEAD_DIM ** -0.5 * math.log2(math.e), 1.0)

        def rope(z, sl):
            c = c_ref[sl, :] * scale
            s = s_ref[sl, :] * scale
            parts = []
            for g in range(COL_TILE // LANES):
                zg = z[:, g * LANES:(g + 1) * LANES]
                parts.append(zg * c + pltpu.roll(zg, LANES // 2, 1) * s)
            return jnp.concatenate(parts, axis=1)

        chunks(rope)

    @pl.when((j >= COL_V) & (j < COL_G))
    def _():
        chunks(lambda z, sl: z)

    @pl.when(j >= COL_G)
    def _():
        chunks(lambda z, sl: 0.5 * jnp.tanh(0.5 * z) + 0.5)


def _in_proj(x2, g_pre, mod3, w_in_bf, tabs, seq):
    T, D = x2.shape
    tm = min(seq, IN_ROW_TILE)
    per_b = seq // tm
    mod_spec = lambda col: pl.BlockSpec((1, 1, D), lambda i, j: (i // per_b, 0, col))
    tab = pl.BlockSpec((tm, LANES), lambda i, j: (i, 0))
    return pl.pallas_call(
        _in_proj_kernel,
        grid=(T // tm, IN_COLS // COL_TILE),
        in_specs=[pl.BlockSpec((tm, D), lambda i, j: (i, 0)),
                  pl.BlockSpec((1, D), lambda i, j: (0, 0)),
                  mod_spec(1), mod_spec(0),
                  pl.BlockSpec((D, COL_TILE), lambda i, j: (0, j)),
                  tab, tab],
        out_specs=pl.BlockSpec((tm, COL_TILE), lambda i, j: (i, j)),
        out_shape=jax.ShapeDtypeStruct((T, IN_COLS), jnp.bfloat16),
        scratch_shapes=[pltpu.VMEM((tm, D), jnp.bfloat16)],
        compiler_params=_params(("parallel", "arbitrary")),
        name="in_proj",
    )(x2, g_pre, mod3, mod3, w_in_bf, *tabs)


def _attn_kernel(lq1_ref, lk1_ref, lq2_ref, lk2_ref, gs_ref, q_ref, k_ref, v_ref, o_ref, vt_ref, *acc_refs):
    i = pl.program_id(2)
    blk = q_ref.shape[0]
    nblk = v_ref.shape[0] // blk
    chains = [(h, comp) for h in range(ATT_HEADS) for comp in range(2)]

    @pl.when(i == 0)
    def _():
        for c in range(nblk):
            for h in range(ATT_HEADS):
                v = v_ref[c * blk:(c + 1) * blk, h * V_DIM:(h + 1) * V_DIM]
                vt_ref[c, h] = v.astype(jnp.float32).T.astype(vt_ref.dtype)

    lam = (jnp.exp(jnp.sum(lq1_ref[...] * lk1_ref[...], axis=-1, keepdims=True))
           - jnp.exp(jnp.sum(lq2_ref[...] * lk2_ref[...], axis=-1, keepdims=True))
           + LAMBDA_INIT)
    lane_comp = lax.broadcasted_iota(jnp.int32, (blk, V_DIM), 1) // ROT_HALF % 2
    qs = []
    for h, comp in chains:
        q = q_ref[:, h * V_DIM:(h + 1) * V_DIM]
        qs.append(jnp.where(lane_comp == comp, q, jnp.zeros_like(q)))
    nt = (((1,), (1,)), ((), ()))
    for acc_ref in acc_refs:
        acc_ref[...] = jnp.zeros_like(acc_ref)

    def step(c, carry, mask=None):
        off = pl.multiple_of(c * blk, blk)
        scores = []
        for n, (h, comp) in enumerate(chains):
            k = k_ref[pl.ds(off, blk), h * V_DIM:(h + 1) * V_DIM]
            scores.append(lax.dot_general(k, qs[n], nt, preferred_element_type=jnp.float32))
        out, probs, alphas = [], [], []
        for n, s in enumerate(scores):
            m, l = carry[n]
            if mask is not None:
                s = jnp.where(mask, s, NEG_BIG)
            m_new = jnp.maximum(m, jnp.max(s, axis=0, keepdims=True))
            alpha = jnp.exp2(m - m_new)
            p = jnp.exp2(s - m_new)
            out.append((m_new, alpha * l + jnp.sum(p, axis=0, keepdims=True)))
            probs.append(p.astype(vt_ref.dtype))
            alphas.append(alpha)
        for n, (h, comp) in enumerate(chains):
            pv = jnp.dot(vt_ref[c, h], probs[n], preferred_element_type=jnp.float32)
            acc_refs[n][...] = alphas[n] * acc_refs[n][...] + pv
        return tuple(out)

    init = tuple((jnp.full((1, blk), NEG_BIG, jnp.float32), jnp.zeros((1, blk), jnp.float32))
                 for _ in chains)
    carry = lax.fori_loop(0, i, step, init)

    key = lax.broadcasted_iota(jnp.int32, (blk, blk), 0)
    qry = lax.broadcasted_iota(jnp.int32, (blk, blk), 1)
    carry = step(i, carry, key <= qry)
    for h in range(ATT_HEADS):
        (_, l1), (_, l2) = carry[2 * h], carry[2 * h + 1]
        ot = acc_refs[2 * h][...] / l1 - lam * (acc_refs[2 * h + 1][...] / l2)
        o = _rms(ot.T, SUBLN_EPS) * gs_ref[...] * (1.0 - LAMBDA_INIT)
        o_ref[:, h * V_DIM:(h + 1) * V_DIM] = o.astype(o_ref.dtype)


def _diff_attention(z, lams, g_subln, batch, seq):
    T = z.shape[0]
    blk = min(ATT_BLOCK, seq)
    nq = seq // blk
    width = ATT_HEADS * V_DIM
    vec = pl.BlockSpec((1, HEAD_DIM), lambda b, h, i: (0, 0))
    kcol = DIFF_WIDTH // width
    return pl.pallas_call(
        _attn_kernel,
        grid=(batch, N_HEADS // ATT_HEADS, nq),
        in_specs=[vec, vec, vec, vec,
                  pl.BlockSpec((1, V_DIM), lambda b, h, i: (0, 0)),
                  pl.BlockSpec((blk, width), lambda b, h, i: (b * nq + i, h)),
                  pl.BlockSpec((seq, width), lambda b, h, i: (b, kcol + h)),
                  pl.BlockSpec((seq, width), lambda b, h, i: (b, 2 * kcol + h))],
        out_specs=pl.BlockSpec((blk, width), lambda b, h, i: (b * nq + i, h)),
        out_shape=jax.ShapeDtypeStruct((T, DIFF_WIDTH), jnp.bfloat16),
        scratch_shapes=[pltpu.VMEM((nq, ATT_HEADS, V_DIM, blk), jnp.bfloat16)]
        + [pltpu.VMEM((V_DIM, blk), jnp.float32)] * (2 * ATT_HEADS),
        compiler_params=_params(("parallel", "parallel", "arbitrary")),
        name="diff_attn",
    )(*lams, g_subln, z, z, z)


def _pool_kernel(u_ref, w_ref, ps_ref, o_ref):
    g = pl.program_id(1)
    u = u_ref[...].astype(jnp.float32)
    t = lax.broadcasted_iota(jnp.int32, u.shape, 0)

    def shifted(x, k):
        return jnp.where(t >= k, pltpu.roll(x, k, 0), 0.0)

    s = u
    d = jnp.zeros_like(u)
    for gi, w in enumerate(POOL_WINDOWS):
        s = s + shifted(s, w // 2)
        cnt = jnp.minimum(t + 1, w).astype(jnp.float32)
        d = jnp.where(g == gi, s / cnt - u, d)
    y = jnp.dot(d.astype(jnp.bfloat16), w_ref[0].astype(jnp.bfloat16), preferred_element_type=jnp.float32)
    o_ref[...] = (y * ps_ref[...]).astype(o_ref.dtype)


def _pool(z, w_pool, pool_scale, batch, seq):
    T = z.shape[0]
    G = len(POOL_WINDOWS)
    ucol = 3 * DIFF_WIDTH // POOL_GROUP_DIM
    return pl.pallas_call(
        _pool_kernel,
        grid=(batch, G),
        in_specs=[pl.BlockSpec((seq, POOL_GROUP_DIM), lambda b, g: (b, ucol + g)),
                  pl.BlockSpec((1, POOL_GROUP_DIM, POOL_GROUP_DIM), lambda b, g: (g, 0, 0)),
                  pl.BlockSpec((1, POOL_GROUP_DIM), lambda b, g: (0, g))],
        out_specs=pl.BlockSpec((seq, POOL_GROUP_DIM), lambda b, g: (b, g)),
        out_shape=jax.ShapeDtypeStruct((T, POOL_WIDTH), jnp.bfloat16),
        compiler_params=_params(("parallel", "parallel")),
        name="pool",
    )(z, w_pool, pool_scale)


def _mix_kernel(x_ref, o_ref, p_ref, ga0, ga1, gb0, gb1, wa_ref, wb_ref, wo_ref, gpm_ref, gt1_ref,
                gpf_ref, sc2_ref, sh2_ref, wr_ref, br_ref,
                x1_ref, h2_ref, ls_ref, tw_ref, cnt_ref):
    tiles = range(x_ref.shape[0] // TOKEN_TILE)
    h2s = [_mix_front(t, x_ref, o_ref, p_ref, ga0, ga1, gb0, gb1, wa_ref, wb_ref, wo_ref, gpm_ref, gt1_ref,
                      gpf_ref, sc2_ref, sh2_ref, x1_ref) for t in tiles]
    for t in tiles:
        h2_ref[t * TOKEN_TILE:(t + 1) * TOKEN_TILE, :] = h2s[t].astype(h2_ref.dtype)
    for t in tiles:
        _mix_route(t, h2s[t], wr_ref, br_ref, ls_ref, tw_ref, cnt_ref)


def _mix_front(t, x_ref, o_ref, p_ref, ga0, ga1, gb0, gb1, wa_ref, wb_ref, wo_ref, gpm_ref, gt1_ref,
               gpf_ref, sc2_ref, sh2_ref, x1_ref):
    rows = slice(t * TOKEN_TILE, (t + 1) * TOKEN_TILE)
    ya = jnp.dot(o_ref[rows, :], wa_ref[...], preferred_element_type=jnp.float32)
    yb = jnp.dot(p_ref[rows, :], wb_ref[...], preferred_element_type=jnp.float32)
    ga = jnp.concatenate([ga0[rows, :], ga1[rows, :]], axis=1).astype(jnp.float32)
    gb = jnp.concatenate([gb0[rows, :], gb1[rows, :]], axis=1).astype(jnp.float32)
    merged = (ga * ya + gb * yb).astype(jnp.bfloat16)
    mixed = jnp.dot(merged, wo_ref[...], preferred_element_type=jnp.float32)
    x1 = x_ref[rows, :] + gt1_ref[0] * (_rms(mixed, NORM_EPS) * gpm_ref[...])
    x1_ref[rows, :] = x1
    return _rms(x1, NORM_EPS) * gpf_ref[...] * (1.0 + sc2_ref[0]) + sh2_ref[0]


def _mix_route(t, h2, wr_ref, br_ref, ls_ref, tw_ref, cnt_ref):
    tm = TOKEN_TILE
    rows = slice(t * tm, (t + 1) * tm)
    logits =lax.dot_general(wr_ref[...], h2, (((1,), (1,)), ((), ())), precision=_HI,
                             preferred_element_type=jnp.float32) + br_ref[...]
    eid = lax.broadcasted_iota(jnp.int32, logits.shape, 0)
    work = logits
    sels, vals = [], []
    for _ in range(TOP_K):
        mx = jnp.max(work, axis=0, keepdims=True)
        idx = jnp.min(jnp.where(work == mx, eid, N_EXPERTS), axis=0, keepdims=True)
        sel = eid == idx
        work = jnp.where(sel, -jnp.inf, work)
        sels.append(sel)
        vals.append(mx)
    ex = [jnp.exp(v - vals[0]) for v in vals]
    den = ex[0] + ex[1] + ex[2] + ex[3]
    onehot = jnp.zeros(logits.shape, jnp.float32)
    for sel in sels:
        onehot = jnp.where(sel, 1.0, onehot)
    r = lax.broadcasted_iota(jnp.int32, (tm, tm), 0)
    c = lax.broadcasted_iota(jnp.int32, (tm, tm), 1)
    tri = jnp.where(r < c, 1.0, 0.0).astype(jnp.bfloat16)
    rank = jnp.dot(onehot.astype(jnp.bfloat16), tri, preferred_element_type=jnp.float32)
    counts = jnp.broadcast_to(jnp.sum(onehot, axis=1, keepdims=True), (N_EXPERTS, LANES))
    rounded = jnp.floor((counts + (ROW_ALIGN - 1)) * (1.0 / ROW_ALIGN)) * ROW_ALIGN
    er = lax.broadcasted_iota(jnp.int32, (N_EXPERTS, N_EXPERTS), 0)
    ec = lax.broadcasted_iota(jnp.int32, (N_EXPERTS, N_EXPERTS), 1)
    below = jnp.where(ec < er, 1.0, 0.0).astype(jnp.bfloat16)
    offset = jnp.dot(below, rounded.astype(jnp.bfloat16), preferred_element_type=jnp.float32)
    slot_of = rank + offset[:, 0:1]
    cnt_ref[t] = rounded
    for kk in range(TOP_K):
        ls = jnp.sum(jnp.where(sels[kk], slot_of, 0.0), axis=0, keepdims=True).astype(jnp.int32)
        ls_ref[kk:kk + 1, rows] = ls
        tw_ref[kk:kk + 1, rows] = ex[kk] / den


def _mix(x2, o, p, z, wa, wb, wo, g_post_mix, mod3, g_pre_ffn, w_router_t, b_router, seq):
    T, D = x2.shape
    sub = min(MIX_TILES, seq // TOKEN_TILE)
    tm = sub * TOKEN_TILE
    per_b = seq // tm
    E = N_EXPERTS
    row = lambda n: pl.BlockSpec((1, n), lambda i: (0, 0))
    mod_spec = lambda col: pl.BlockSpec((1, 1, D), lambda i: (i // per_b, 0, col))
    gate = lambda cb: pl.BlockSpec((tm, COL_TILE), lambda i: (i, cb))
    full = lambda a: pl.BlockSpec(a.shape, lambda i: (0,) * a.ndim)
    tile = pl.BlockSpec((tm, D), lambda i: (i, 0))
    small = pl.BlockSpec((TOP_K, tm), lambda i: (0, i))
    return pl.pallas_call(
        _mix_kernel,
        grid=(T // tm,),
        in_specs=[tile, tile,
                  pl.BlockSpec((tm, POOL_WIDTH), lambda i: (i, 0)),
                  gate(COL_G), gate(COL_G + 1), gate(COL_G + 2), gate(COL_G + 3),
                  full(wa), full(wb), full(wo), row(D), mod_spec(2),
                  row(D), mod_spec(4), mod_spec(3), full(w_router_t),
                  pl.BlockSpec((E, 1), lambda i: (0, 0))],
        out_specs=[tile, tile, small, small,
                   pl.BlockSpec((sub, E, LANES), lambda i: (i, 0, 0))],
        out_shape=[jax.ShapeDtypeStruct((T, D), jnp.float32),
                   jax.ShapeDtypeStruct((T, D), jnp.bfloat16),
                   jax.ShapeDtypeStruct((TOP_K, T), jnp.int32),
                   jax.ShapeDtypeStruct((TOP_K, T), jnp.float32),
                   jax.ShapeDtypeStruct((T // TOKEN_TILE, E, LANES), jnp.float32)],
        compiler_params=_params(("parallel",)),
        name="mix_tail",
    )(x2, o, p, z, z, z, z, wa, wb, wo, g_post_mix, mod3, g_pre_ffn, mod3, mod3, w_router_t,
      b_router.reshape(E, 1))


def _chunk(ref, row):
    return ref.at[pl.ds(pl.multiple_of(row, ROW_ALIGN), ROW_ALIGN), :]


def _dispatch_kernel(chunk_slot_ref, n_chunk_ref, zero_off_ref, zero_len_ref, ntile_ref,
                     h_ref, ls_ref, xs_ref, xl_ref, zero_ref, sem, zsem, *, zero_per_step):
    i = pl.program_id(0)
    last = pl.num_programs(0) - 1

    @pl.when(i == 0)
    def _():
        zero_ref[...] = jnp.zeros_like(zero_ref)

    slot_id = lax.broadcasted_iota(jnp.int32, (LOCAL_ROWS, h_ref.shape[0]), 0)
    place = jnp.zeros(slot_id.shape, jnp.float32)
    for kk in range(TOP_K):
        place = jnp.where(slot_id == ls_ref[kk:kk + 1, :], 1.0, place)
    xl = jnp.dot(place.astype(jnp.bfloat16), h_ref[...], preferred_element_type=jnp.float32)
    xl_ref[i % 2] = xl.astype(xl_ref.dtype)

    def rows_out(tile, act):
        buf = tile % 2

        def body(k, _):
            act(pltpu.make_async_copy(_chunk(xl_ref.at[buf], k * ROW_ALIGN),
                                      _chunk(xs_ref, chunk_slot_ref[tile * BLOCK_CHUNKS + k]), sem.at[buf]))
            return 0
        lax.fori_loop(0, n_chunk_ref[tile], body, 0)

    def copies(act):
        for r in range(zero_per_step):
            idx = i * zero_per_step + r

            def zbody(k, _, idx=idx):
                act(pltpu.make_async_copy(_chunk(zero_ref, 0),
                                          _chunk(xs_ref, zero_off_ref[idx] + k * ROW_ALIGN), zsem))
                return 0
            lax.fori_loop(0, zero_len_ref[idx] // ROW_ALIGN, zbody, 0)

    def tails(act):
        def body(t, _):
            off = pl.multiple_of(t * MOE_TILE, MOE_TILE)
            act(pltpu.make_async_copy(zero_ref, xs_ref.at[pl.ds(off, MOE_TILE), :], zsem))
            return 0
        lax.fori_loop(ntile_ref[0], xs_ref.shape[0] // MOE_TILE, body, 0)

    start, wait = (lambda cp: cp.start()), (lambda cp: cp.wait())
    rows_out(i, start)
    copies(start)
    pl.when(i == 0)(functools.partial(tails, start))
    pl.when(i > 0)(lambda: rows_out(i - 1, wait))
    copies(wait)
    pl.when(i == 0)(functools.partial(tails, wait))
    pl.when(i == last)(lambda: rows_out(i, wait))


def _dispatch(chunk_slot, n_chunks, zero_off, zero_len, n_tiles, h2, ls, n_slots):
    T, W = h2.shape
    tc = TOKEN_TILE
    steps = n_chunks.shape[0]
    return pl.pallas_call(
        functools.partial(_dispatch_kernel, zero_per_step=zero_len.shape[0] // steps),
        grid_spec=pltpu.PrefetchScalarGridSpec(
            num_scalar_prefetch=5,
            grid=(steps,),
            in_specs=[pl.BlockSpec((tc, W), lambda i, *_: (i, 0)),
                      pl.BlockSpec((TOP_K, tc), lambda i, *_: (0, i))],
            out_specs=pl.BlockSpec(memory_space=pl.ANY),
            scratch_shapes=[pltpu.VMEM((2, LOCAL_ROWS, W), h2.dtype),
                            pltpu.VMEM((MOE_TILE, W), h2.dtype),
                            pltpu.SemaphoreType.DMA((2,)), pltpu.SemaphoreType.DMA]),
        out_shape=jax.ShapeDtypeStruct((n_slots, W), h2.dtype),
        compiler_params=_params(("arbitrary",)),
        name="dispatch",
    )(chunk_slot, n_chunks, zero_off, zero_len, n_tiles, h2, ls)


def _moe_kernel(te_ref, nt_ref, buf_ref, nxt_ref, rows_ref, x_ref, w1_hbm, b1_ref, w2_hbm, b2_ref, perm_ref, y_ref,
                w1raw_ref, w2raw_ref, w1s_ref, w2s_ref, sem):
    j = pl.program_id(0)
    prev = te_ref[jnp.maximum(j - 1, 0)]
    fresh = (j == 0) | (te_ref[j] != prev)
    nblk = w1s_ref.shape[1] // MXU_DIM

    def weight_copies(e, b):
        return (pltpu.make_async_copy(w1_hbm.at[e], w1raw_ref.at[b], sem.at[0, b]),
                pltpu.make_async_copy(w2_hbm.at[e], w2raw_ref.at[b], sem.at[1, b]))

    @pl.when(fresh & (j < nt_ref[0]))
    def _():
        e, b, nxt = te_ref[j], buf_ref[j], nxt_ref[j]

        @pl.when(j == 0)
        def _():
            for cp in weight_copies(e, b):
                cp.start()

        for cp in weight_copies(e, b):
            cp.wait()

        @pl.when(nxt >= 0)
        def _():
            for cp in weight_copies(nxt, 1 - b):
                cp.start()

        for c in range(nblk):
            cols = slice(c * MXU_DIM, (c + 1) * MXU_DIM)
            blk = w1raw_ref[b, :, cols].astype(jnp.bfloat16)
            w1s_ref[:, cols] = jnp.dot(blk, perm_ref[...],
                                       preferred_element_type=jnp.float32).astype(jnp.bfloat16)
        w2s_ref[...] = w2raw_ref[b].astype(jnp.bfloat16)

    def experts_mlp(rows):
        z = jnp.dot(x_ref[:rows, :], w1s_ref[...], preferred_element_type=jnp.float32) + b1_ref[0]
        acts = []
        for c in range(nblk):
            gate = jnp.minimum(z[:, c * MXU_DIM:c * MXU_DIM + LANES], SWIGLU_LIMIT)
            up = jnp.clip(z[:, c * MXU_DIM + LANES:(c + 1) * MXU_DIM], -SWIGLU_LIMIT, SWIGLU_LIMIT)
            acts.append(gate * jax.nn.sigmoid(SWIGLU_ALPHA * gate) * (up + 1.0))
        a = jnp.concatenate(acts, axis=1).astype(jnp.bfloat16)
        y = jnp.dot(a, w2s_ref[...], preferred_element_type=jnp.float32) + b2_ref[0]
        y_ref[:rows, :] = y.astype(y_ref.dtype)
        if rows < y_ref.shape[0]:
            y_ref[rows:, :] = jnp.zeros((y_ref.shape[0] - rows, y_ref.shape[1]), y_ref.dtype)

    half = y_ref.shape[0] // 2
    active = j < nt_ref[0]
    pl.when(active & (rows_ref[j] > half))(lambda: experts_mlp(2 * half))
    pl.when(active & (rows_ref[j] <= half))(lambda: experts_mlp(half))

    @pl.when(j >= nt_ref[0])
    def _():
        y_ref[...] = jnp.zeros_like(y_ref)


def _regroup_perm():
    src = np.arange(MXU_DIM)
    dst = np.where(src % 2 == 0, src // 2, LANES + src // 2)
    perm = np.zeros((MXU_DIM, MXU_DIM), np.float32)
    perm[src, dst] = 1.0
    return jnp.asarray(perm, jnp.bfloat16)


def _moe(tile_expert, n_tiles, tile_buf, tile_next, tile_rows, xs, w1, b1g, w2, b2):
    n_slots, W = xs.shape
    tm = MOE_TILE
    E, D, F2 = w1.shape
    F = w2.shape[1]
    xmap = lambda j, te, nt, *_: (jnp.minimum(j, nt[0] - 1), 0)
    emap = lambda j, te, *_: (te[j], 0, 0)
    return pl.pallas_call(
        _moe_kernel,
        grid_spec=pltpu.PrefetchScalarGridSpec(
            num_scalar_prefetch=5,
            grid=(n_slots // tm,),
            in_specs=[pl.BlockSpec((tm, W), xmap),
                      pl.BlockSpec(memory_space=pl.ANY),
                      pl.BlockSpec((1, 1, F2), emap),
                      pl.BlockSpec(memory_space=pl.ANY),
                      pl.BlockSpec((1, 1, D), emap),
                      pl.BlockSpec((MXU_DIM, MXU_DIM), lambda j, *_: (0, 0))],
            out_specs=pl.BlockSpec((tm, W), lambda j, *_: (j, 0)),
            scratch_shapes=[pltpu.VMEM((2, D, F2), w1.dtype), pltpu.VMEM((2, F, D), w2.dtype),
                            pltpu.VMEM((D, F2), jnp.bfloat16), pltpu.VMEM((F, D), jnp.bfloat16),
                            pltpu.SemaphoreType.DMA((2, 2))]),
        out_shape=jax.ShapeDtypeStruct((n_slots, W), xs.dtype),
        compiler_params=_params(("arbitrary",)),
        name="moe_experts",
    )(tile_expert, n_tiles, tile_buf, tile_next, tile_rows, xs, w1, b1g, w2, b2.reshape(E, 1, D),
      _regroup_perm())


def _combine_kernel(chunk_slot_ref, n_chunk_ref, ys_ref, ls_ref, w_ref, x1_ref, gt2_ref, g_ref,
                    o_ref, yl_ref, sem):
    i = pl.program_id(0)
    tc = x1_ref.shape[0]

    def pieces(tile, act):
        buf = tile % 2

        def body(k, _):
            act(pltpu.make_async_copy(_chunk(ys_ref, chunk_slot_ref[tile * BLOCK_CHUNKS + k]),
                                      _chunk(yl_ref.at[buf], k * ROW_ALIGN), sem.at[buf]))
            return 0
        lax.fori_loop(0, n_chunk_ref[tile], body, 0)

    def fetch(tile):
        yl_ref[tile % 2, TOP_K * tc:, :] = jnp.zeros((LOCAL_ROWS - TOP_K * tc, yl_ref.shape[2]), yl_ref.dtype)
        pieces(tile, lambda cp: cp.start())

    pl.when(i == 0)(lambda: fetch(i))
    pl.when(i + 1 < pl.num_programs(0))(lambda: fetch(i + 1))
    slot_id = lax.broadcasted_iota(jnp.int32, (tc, LOCAL_ROWS), 1)
    ls = ls_ref[...]
    w = w_ref[...]
    mix = jnp.zeros((tc, LOCAL_ROWS), jnp.float32)
    for k in range(TOP_K):
        mix = jnp.where(slot_id == ls[:, k:k + 1], w[:, k:k + 1], mix)
    pieces(i, lambda cp: cp.wait())
    f = jnp.dot(mix.astype(jnp.bfloat16), yl_ref[i % 2], preferred_element_type=jnp.float32)
    o_ref[...] = x1_ref[...] + gt2_ref[0] * (_rms(f, NORM_EPS) * g_ref[...])


def _combine(chunk_slot, n_chunks, ys, ls_t, topw_t, x1, mod3, g_post_ffn, seq):
    T, D = x1.shape
    tc = TOKEN_TILE
    per_b = seq // tc
    tile = pl.BlockSpec((tc, D), lambda i, *_: (i, 0))
    small = pl.BlockSpec((tc, TOP_K), lambda i, *_: (i, 0))
    return pl.pallas_call(
        _combine_kernel,
        grid_spec=pltpu.PrefetchScalarGridSpec(
            num_scalar_prefetch=2,
            grid=(T // tc,),
            in_specs=[pl.BlockSpec(memory_space=pl.ANY), small, small, tile,
                      pl.BlockSpec((1, 1, D), lambda i, *_: (i // per_b, 0, 5)),
                      pl.BlockSpec((1, D), lambda i, *_: (0, 0))],
            out_specs=tile,
            scratch_shapes=[pltpu.VMEM((2, LOCAL_ROWS, ys.shape[1]), ys.dtype),
                            pltpu.SemaphoreType.DMA((2,))]),
        out_shape=jax.ShapeDtypeStruct((T, D), jnp.float32),
        compiler_params=_params(("arbitrary",)),
        name="combine",
    )(chunk_slot, n_chunks, ys, ls_t, topw_t, x1, mod3, g_post_ffn)


def kernel(x, c, positions, w_ada, b_ada, g_pre_mix, w_in, lambda_q1, lambda_k1, lambda_q2, lambda_k2,
           g_subln, w_pool, pool_scale, w_proj_a, w_proj_b, w_out, g_post_mix, g_pre_ffn,
           w_router, b_router, w_exp1, b_exp1, w_exp2, b_exp2, g_post_ffn):
    B, S, D = x.shape
    T = B * S
    bf = jnp.bfloat16
    tabs = _rope_tables(positions)
    for l in range(w_ada.shape[0]):
        x2 = x.reshape(T, D)
        mod3 = _ada(c, w_ada[l], b_ada[l]).reshape(B, 1, N_MOD * D)
        n_qk = 2 * DIFF_WIDTH
        w_in_bf = jnp.concatenate([_permute_qk_columns(w_in[l][:, :n_qk]), w_in[l][:, n_qk:]], axis=1).astype(bf)
        z = _in_proj(x2, g_pre_mix[l][None], mod3, w_in_bf, tabs, S)
        lams = [v[l][None] for v in (lambda_q1, lambda_k1, lambda_q2, lambda_k2)]
        o = _diff_attention(z, lams, g_subln[l][None], B, S)
        p = _pool(z, w_pool[l], pool_scale[l][None], B, S)
        x1, h2, ls, topw, cnt = _mix(
            x2, o, p, z, w_proj_a[l].astype(bf), w_proj_b[l].astype(bf), w_out[l].astype(bf),
            g_post_mix[l][None], mod3, g_pre_ffn[l][None], w_router[l].T, b_router[l], S)

        i32 = jnp.int32
        n = cnt[:, :, 0].astype(i32)
        n_tok_tiles = n.shape[0]
        counts = jnp.sum(n, axis=0)
        padded = (counts + MOE_TILE - 1) // MOE_TILE * MOE_TILE
        gend = jnp.cumsum(padded).astype(i32)
        gstart = gend - padded
        local_end = jnp.cumsum(n, axis=1)
        slot_off = gstart[None, :] + jnp.cumsum(n, axis=0) - n
        chunk_row = jnp.arange(BLOCK_CHUNKS, dtype=i32) * ROW_ALIGN
        piece = jnp.sum(local_end[:, None, :] <= chunk_row[None, :, None], axis=2)
        in_piece = piece[:, :, None] == jnp.arange(N_EXPERTS, dtype=i32)[None, None, :]
        shift = jnp.sum(jnp.where(in_piece, (slot_off - (local_end - n))[:, None, :], 0), axis=2)
        chunk_slot = (shift + chunk_row[None, :]).reshape(-1).astype(i32)
        n_chunks = (local_end[:, -1] // ROW_ALIGN).astype(i32)
        steps_pad = -N_EXPERTS % n_tok_tiles
        group_pad_off = jnp.pad(gstart + counts, (0, steps_pad)).astype(i32)
        group_pad_len = jnp.pad(padded - counts, (0, steps_pad)).astype(i32)
        n_slots = -(-(T * TOP_K + n_tok_tiles * N_EXPERTS * ROW_ALIGN) // MOE_TILE) * MOE_TILE + N_EXPERTS * MOE_TILE
        tile_start = jnp.arange(n_slots // MOE_TILE, dtype=i32) * MOE_TILE
        tile_expert = jnp.minimum(jnp.sum(tile_start[:, None] >= gend[None, :], axis=1), N_EXPERTS - 1)
        n_tiles = gend[-1:] // MOE_TILE
        has_tiles = padded > 0
        expert_ids = jnp.arange(N_EXPERTS, dtype=i32)
        later = lax.cummin(jnp.where(has_tiles, expert_ids, N_EXPERTS), reverse=True)
        next_expert = jnp.concatenate([later[1:], jnp.full((1,), N_EXPERTS, i32)])
        next_expert = jnp.where(next_expert < N_EXPERTS, next_expert, -1)
        of_tile = tile_expert[:, None] == expert_ids[None, :]
        pick = lambda per_expert: jnp.sum(jnp.where(of_tile, per_expert[None, :], 0), axis=1).astype(i32)
        tile_buf = pick((jnp.cumsum(has_tiles.astype(i32)) - 1) % 2)
        tile_next = pick(next_expert)
        tile_rows = jnp.clip(pick(gstart + counts) - tile_start, 0, MOE_TILE).astype(i32)

        xs = _dispatch(chunk_slot, n_chunks, group_pad_off, group_pad_len, n_tiles, h2, ls, n_slots)
        half = D_FF
        b1g = b_exp1[l].reshape(N_EXPERTS, half // LANES, LANES, 2).transpose(0, 1, 3, 2)
        b1g = b1g.reshape(N_EXPERTS, 1, 2 * half)
        ys = _moe(tile_expert.astype(i32), n_tiles, tile_buf, tile_next, tile_rows, xs,
                  w_exp1[l], b1g, w_exp2[l], b_exp2[l])
        x = _combine(chunk_slot, n_chunks, ys, ls.T, topw.T, x1, mod3, g_post_ffn[l][None], S)
        x = x.reshape(B, S, D)
    return x
```

```python
import functools
import math

import numpy as np
import jax
import jax.numpy as jnp
from jax import lax
from jax.experimental import pallas as pl
from jax.experimental.pallas import tpu as pltpu

D_MODEL = 1024
N_HEADS = 8
HEAD_DIM = 64
V_DIM = 2 * HEAD_DIM
DIFF_WIDTH = N_HEADS * V_DIM
POOL_WINDOWS = (2, 4, 8, 16)
POOL_GROUP_DIM = 128
POOL_WIDTH = len(POOL_WINDOWS) * POOL_GROUP_DIM
IN_COLS = 3 * DIFF_WIDTH + POOL_WIDTH + 2 * D_MODEL
ROPE_THETA = 500000.0
ROT_DIM = HEAD_DIM // 4
ROT_HALF = ROT_DIM // 2
N_EXPERTS = 32
TOP_K = 4
D_FF = D_MODEL
SWIGLU_ALPHA = 1.702
SWIGLU_LIMIT = 7.0
NORM_EPS = 1e-6
SUBLN_EPS = 1e-5
N_MOD = 6
NEG_BIG = -1e30

LANES = 128
MXU_DIM = 256
VMEM_LIMIT = 56 * 1024 * 1024

ADA_COL_TILE = 1536
ROPE_ROW_TILE = 2048
COL_TILE = 512
IN_ROW_TILE = 2048
IN_ROW_CHUNK = 512
COL_K = DIFF_WIDTH // COL_TILE
COL_V = 2 * DIFF_WIDTH // COL_TILE
COL_G = (3 * DIFF_WIDTH + POOL_WIDTH) // COL_TILE
ATT_BLOCK = 512
ATT_HEADS = 4
MOE_TILE = 512
TOKEN_TILE = 256
MIX_TILES = 2
ROW_ALIGN = 16
LOCAL_ROWS = TOP_K * TOKEN_TILE + N_EXPERTS * ROW_ALIGN
BLOCK_CHUNKS = LOCAL_ROWS // ROW_ALIGN

_HI = lax.Precision.HIGHEST


def _params(sem, vmem=VMEM_LIMIT):
    return pltpu.CompilerParams(dimension_semantics=sem, vmem_limit_bytes=vmem)


def _rms(x, eps):
    return x * lax.rsqrt(jnp.mean(x * x, axis=-1, keepdims=True) + eps)


def _ada_kernel(c_ref, w_ref, b_ref, o_ref):
    c = c_ref[...]
    s = c * jax.nn.sigmoid(c)
    o_ref[...] = jnp.dot(s, w_ref[...], precision=_HI, preferred_element_type=jnp.float32) + b_ref[...]


def _ada(c, w_ada, b_ada):
    B, D = c.shape
    N = w_ada.shape[1]
    tn = ADA_COL_TILE
    return pl.pallas_call(
        _ada_kernel,
        grid=(N // tn,),
        in_specs=[pl.BlockSpec((B, D), lambda j: (0, 0)),
                  pl.BlockSpec((D, tn), lambda j: (0, j)),
                  pl.BlockSpec((1, tn), lambda j: (0, j))],
        out_specs=pl.BlockSpec((B, tn), lambda j: (0, j)),
        out_shape=jax.ShapeDtypeStruct((B, N), jnp.float32),
        compiler_params=_params(("parallel",)),
        name="ada",
    )(c, w_ada, b_ada.reshape(1, N))


def _rope_tab_kernel(pos_ref, invf_ref, phase_ref, c_ref, s_ref):
    pos = pos_ref[...].astype(jnp.float32)
    cs = jnp.cos(pos * invf_ref[...] - phase_ref[...])
    rolled = pltpu.roll(cs, LANES // 2, 1)
    lane = lax.broadcasted_iota(jnp.int32, cs.shape, 1)
    lower = lane < LANES // 2
    rot = lane % (LANES // 2) < ROT_DIM
    c_ref[...] = jnp.where(lower, cs, rolled)
    s_ref[...] = jnp.where(rot, jnp.where(lower, -rolled, cs), 0.0)


def _rope_tables(positions):
    T = positions.size
    tm = min(T, ROPE_ROW_TILE)
    lane = np.arange(LANES)
    rot = lane % (LANES // 2) < ROT_DIM
    inv = ROPE_THETA ** (-(np.arange(ROT_HALF, dtype=np.float64) * 2.0 / ROT_DIM))
    invf = np.where(rot, inv[lane % ROT_HALF], 0.0).astype(np.float32)[None, :]
    phase = np.where(rot & (lane >= LANES // 2), np.pi / 2, 0.0).astype(np.float32)[None, :]
    row = pl.BlockSpec((1, LANES), lambda i: (0, 0))
    tab = pl.BlockSpec((tm, LANES), lambda i: (i, 0))
    sds = jax.ShapeDtypeStruct((T, LANES), jnp.float32)
    return pl.pallas_call(
        _rope_tab_kernel,
        grid=(T // tm,),
        in_specs=[pl.BlockSpec((tm, 1), lambda i: (i, 0)), row, row],
        out_specs=[tab, tab],
        out_shape=[sds, sds],
        compiler_params=_params(("parallel",)),
        name="rope_tables",
    )(positions.reshape(T, 1), jnp.asarray(invf), jnp.asarray(phase))


def _permute_qk_columns(w):
    D, n = w.shape
    w = w.reshape(D, n // V_DIM, 2, 4, 2, ROT_HALF)
    return w.transpose(0, 1, 4, 3, 2, 5).reshape(D, n)


def _in_proj_kernel(x_ref, g_ref, sc_ref, sh_ref, w_ref, c_ref, s_ref, z_ref, h_ref):
    j = pl.program_id(1)
    tm = x_ref.shape[0]
    rows = min(tm, IN_ROW_CHUNK)

    @pl.when(j == 0)
    def _():
        h = _rms(x_ref[...], NORM_EPS) * g_ref[...] * (1.0 + sc_ref[0]) + sh_ref[0]
        h_ref[...] = h.astype(h_ref.dtype)

    def chunks(epilogue):
        for r in range(tm // rows):
            sl = slice(r * rows, (r + 1) * rows)
            z = jnp.dot(h_ref[sl, :], w_ref[...], preferred_element_type=jnp.float32)
            z_ref[sl, :] = epilogue(z, sl).astype(z_ref.dtype)

    @pl.when(j < COL_V)
    def _():
        scale = jnp.where(j < COL_K, HEAD_DIM ** -0.5 * math.log2(math.e), 1.0)

        def rope(z, sl):
            c = c_ref[sl, :] * scale
            s = s_ref[sl, :] * scale
            parts = []
            for g in range(COL_TILE // LANES):
                zg = z[:, g * LANES:(g + 1) * LANES]
                parts.append(zg * c + pltpu.roll(zg, LANES // 2, 1) * s)
            return jnp.concatenate(parts, axis=1)

        chunks(rope)

    @pl.when((j >= COL_V) & (j < COL_G))
    def _():
        chunks(lambda z, sl: z)

    @pl.when(j >= COL_G)
    def _():
        chunks(lambda z, sl: 0.5 * jnp.tanh(0.5 * z) + 0.5)


def _in_proj(x2, g_pre, mod3, w_in_bf, tabs, seq):
    T, D = x2.shape
    tm = min(seq, IN_ROW_TILE)
    per_b = seq // tm
    mod_spec = lambda col: pl.BlockSpec((1, 1, D), lambda i, j: (i // per_b, 0, col))
    tab = pl.BlockSpec((tm, LANES), lambda i, j: (i, 0))
    return pl.pallas_call(
        _in_proj_kernel,
        grid=(T // tm, IN_COLS // COL_TILE),
        in_specs=[pl.BlockSpec((tm, D), lambda i, j: (i, 0)),
                  pl.BlockSpec((1, D), lambda i, j: (0, 0)),
                  mod_spec(1), mod_spec(0),
                  pl.BlockSpec((D, COL_TILE), lambda i, j: (0, j)),
                  tab, tab],
        out_specs=pl.BlockSpec((tm, COL_TILE), lambda i, j: (i, j)),
        out_shape=jax.ShapeDtypeStruct((T, IN_COLS), jnp.bfloat16),
        scratch_shapes=[pltpu.VMEM((tm, D), jnp.bfloat16)],
        compiler_params=_params(("parallel", "arbitrary")),
        name="in_proj",
    )(x2, g_pre, mod3, mod3, w_in_bf, *tabs)


def _attn_kernel(lq1_ref, lk1_ref, lq2_ref, lk2_ref, gs_ref, q_ref, k_ref, v_ref, o_ref, vt_ref, *acc_refs,
                 lam_init):
    i = pl.program_id(2)
    blk = q_ref.shape[0]
    nblk = v_ref.shape[0] // blk
    chains = [(h, comp) for h in range(ATT_HEADS) for comp in range(2)]

    @pl.when(i == 0)
    def _():
        for c in range(nblk):
            for h in range(ATT_HEADS):
                v = v_ref[c * blk:(c + 1) * blk, h * V_DIM:(h + 1) * V_DIM]
                vt_ref[c, h] = v.astype(jnp.float32).T.astype(vt_ref.dtype)

    lam = (jnp.exp(jnp.sum(lq1_ref[...] * lk1_ref[...], axis=-1, keepdims=True))
           - jnp.exp(jnp.sum(lq2_ref[...] * lk2_ref[...], axis=-1, keepdims=True))
           + lam_init)
    lane_comp = lax.broadcasted_iota(jnp.int32, (blk, V_DIM), 1) // ROT_HALF % 2
    qs = []
    for h, comp in chains:
        q = q_ref[:, h * V_DIM:(h + 1) * V_DIM]
        qs.append(jnp.where(lane_comp == comp, q, jnp.zeros_like(q)))
    nt = (((1,), (1,)), ((), ()))
    for acc_ref in acc_refs:
        acc_ref[...] = jnp.zeros_like(acc_ref)

    def step(c, carry, mask=None):
        off = pl.multiple_of(c * blk, blk)
        scores = []
        for n, (h, comp) in enumerate(chains):
            k = k_ref[pl.ds(off, blk), h * V_DIM:(h + 1) * V_DIM]
            scores.append(lax.dot_general(k, qs[n], nt, preferred_element_type=jnp.float32))
        out, probs, alphas = [], [], []
        for n, s in enumerate(scores):
            m, l = carry[n]
            if mask is not None:
                s = jnp.where(mask, s, NEG_BIG)
            m_new = jnp.maximum(m, jnp.max(s, axis=0, keepdims=True))
            alpha = jnp.exp2(m - m_new)
            p = jnp.exp2(s - m_new)
            out.append((m_new, alpha * l + jnp.sum(p, axis=0, keepdims=True)))
            probs.append(p.astype(vt_ref.dtype))
            alphas.append(alpha)
        for n, (h, comp) in enumerate(chains):
            pv = jnp.dot(vt_ref[c, h], probs[n], preferred_element_type=jnp.float32)
            acc_refs[n][...] = alphas[n] * acc_refs[n][...] + pv
        return tuple(out)

    init = tuple((jnp.full((1, blk), NEG_BIG, jnp.float32), jnp.zeros((1, blk), jnp.float32))
                 for _ in chains)
    carry = lax.fori_loop(0, i, step, init)

    key = lax.broadcasted_iota(jnp.int32, (blk, blk), 0)
    qry = lax.broadcasted_iota(jnp.int32, (blk, blk), 1)
    carry = step(i, carry, key <= qry)
    for h in range(ATT_HEADS):
        (_, l1), (_, l2) = carry[2 * h], carry[2 * h + 1]
        ot = acc_refs[2 * h][...] / l1 - lam * (acc_refs[2 * h + 1][...] / l2)
        o = _rms(ot.T, SUBLN_EPS) * gs_ref[...] * (1.0 - lam_init)
        o_ref[:, h * V_DIM:(h + 1) * V_DIM] = o.astype(o_ref.dtype)


def _diff_attention(z, lams, g_subln, lam_init, batch, seq):
    T = z.shape[0]
    blk = min(ATT_BLOCK, seq)
    nq = seq // blk
    width = ATT_HEADS * V_DIM
    vec = pl.BlockSpec((1, HEAD_DIM), lambda b, h, i: (0, 0))
    kcol = DIFF_WIDTH // width
    return pl.pallas_call(
        functools.partial(_attn_kernel, lam_init=lam_init),
        grid=(batch, N_HEADS // ATT_HEADS, nq),
        in_specs=[vec, vec, vec, vec,
                  pl.BlockSpec((1, V_DIM), lambda b, h, i: (0, 0)),
                  pl.BlockSpec((blk, width), lambda b, h, i: (b * nq + i, h)),
                  pl.BlockSpec((seq, width), lambda b, h, i: (b, kcol + h)),
                  pl.BlockSpec((seq, width), lambda b, h, i: (b, 2 * kcol + h))],
        out_specs=pl.BlockSpec((blk, width), lambda b, h, i: (b * nq + i, h)),
        out_shape=jax.ShapeDtypeStruct((T, DIFF_WIDTH), jnp.bfloat16),
        scratch_shapes=[pltpu.VMEM((nq, ATT_HEADS, V_DIM, blk), jnp.bfloat16)]
        + [pltpu.VMEM((V_DIM, blk), jnp.float32)] * (2 * ATT_HEADS),
        compiler_params=_params(("parallel", "parallel", "arbitrary")),
        name="diff_attn",
    )(*lams, g_subln, z, z, z)


def _pool_kernel(u_ref, w_ref, ps_ref, o_ref):
    g = pl.program_id(1)
    u = u_ref[...].astype(jnp.float32)
    t = lax.broadcasted_iota(jnp.int32, u.shape, 0)

    def shifted(x, k):
        return jnp.where(t >= k, pltpu.roll(x, k, 0), 0.0)

    s = u
    d = jnp.zeros_like(u)
    for gi, w in enumerate(POOL_WINDOWS):
        s = s + shifted(s, w // 2)
        cnt = jnp.minimum(t + 1, w).astype(jnp.float32)
        d = jnp.where(g == gi, s / cnt - u, d)
    y = jnp.dot(d.astype(jnp.bfloat16), w_ref[0].astype(jnp.bfloat16), preferred_element_type=jnp.float32)
    o_ref[...] = (y * ps_ref[...]).astype(o_ref.dtype)


def _pool(z, w_pool, pool_scale, batch, seq):
    T = z.shape[0]
    G = len(POOL_WINDOWS)
    ucol = 3 * DIFF_WIDTH // POOL_GROUP_DIM
    return pl.pallas_call(
        _pool_kernel,
        grid=(batch, G),
        in_specs=[pl.BlockSpec((seq, POOL_GROUP_DIM), lambda b, g: (b, ucol + g)),
                  pl.BlockSpec((1, POOL_GROUP_DIM, POOL_GROUP_DIM), lambda b, g: (g, 0, 0)),
                  pl.BlockSpec((1, POOL_GROUP_DIM), lambda b, g: (0, g))],
        out_specs=pl.BlockSpec((seq, POOL_GROUP_DIM), lambda b, g: (b, g)),
        out_shape=jax.ShapeDtypeStruct((T, POOL_WIDTH), jnp.bfloat16),
        compiler_params=_params(("parallel", "parallel")),
        name="pool",
    )(z, w_pool, pool_scale)


def _mix_kernel(x_ref, o_ref, p_ref, ga0, ga1, gb0, gb1, wa_ref, wb_ref, wo_ref, gpm_ref, gt1_ref,
                gpf_ref, sc2_ref, sh2_ref, wr_ref, br_ref,
                x1_ref, h2_ref, ls_ref, tw_ref, cnt_ref):
    tiles = range(x_ref.shape[0] // TOKEN_TILE)
    h2s = [_mix_front(t, x_ref, o_ref, p_ref, ga0, ga1, gb0, gb1, wa_ref, wb_ref, wo_ref, gpm_ref, gt1_ref,
                      gpf_ref, sc2_ref, sh2_ref, x1_ref) for t in tiles]
    for t in tiles:
        h2_ref[t * TOKEN_TILE:(t + 1) * TOKEN_TILE, :] = h2s[t].astype(h2_ref.dtype)
    for t in tiles:
        _mix_route(t, h2s[t], wr_ref, br_ref, ls_ref, tw_ref, cnt_ref)


def _mix_front(t, x_ref, o_ref, p_ref, ga0, ga1, gb0, gb1, wa_ref, wb_ref, wo_ref, gpm_ref, gt1_ref,
               gpf_ref, sc2_ref, sh2_ref, x1_ref):
    rows = slice(t * TOKEN_TILE, (t + 1) * TOKEN_TILE)
    ya = jnp.dot(o_ref[rows, :], wa_ref[...], preferred_element_type=jnp.float32)
    yb = jnp.dot(p_ref[rows, :], wb_ref[...], preferred_element_type=jnp.float32)
    ga = jnp.concatenate([ga0[rows, :], ga1[rows, :]], axis=1).astype(jnp.float32)
    gb = jnp.concatenate([gb0[rows, :], gb1[rows, :]], axis=1).astype(jnp.float32)
    merged = (ga * ya + gb * yb).astype(jnp.bfloat16)
    mixed = jnp.dot(merged, wo_ref[...], preferred_element_type=jnp.float32)
    x1 = x_ref[rows, :] + gt1_ref[0] * (_rms(mixed, NORM_EPS) * gpm_ref[...])
    x1_ref[rows, :] = x1
    return _rms(x1, NORM_EPS) * gpf_ref[...] * (1.0 + sc2_ref[0]) + sh2_ref[0]


def _mix_route(t, h2, wr_ref, br_ref, ls_ref, tw_ref, cnt_ref):
    tm = TOKEN_TILE
    rows = slice(t * tm, (t + 1) * tm)
    logits =lax.dot_general(wr_ref[...], h2, (((1,), (1,)), ((), ())), precision=_HI,
                             preferred_element_type=jnp.float32) + br_ref[...]
    eid = lax.broadcasted_iota(jnp.int32, logits.shape, 0)
    work = logits
    sels, vals = [], []
    for _ in range(TOP_K):
        mx = jnp.max(work, axis=0, keepdims=True)
        idx = jnp.min(jnp.where(work == mx, eid, N_EXPERTS), axis=0, keepdims=True)
        sel = eid == idx
        work = jnp.where(sel, -jnp.inf, work)
        sels.append(sel)
        vals.append(mx)
    ex = [jnp.exp(v - vals[0]) for v in vals]
    den = ex[0] + ex[1] + ex[2] + ex[3]
    onehot = jnp.zeros(logits.shape, jnp.float32)
    for sel in sels:
        onehot = jnp.where(sel, 1.0, onehot)
    r = lax.broadcasted_iota(jnp.int32, (tm, tm), 0)
    c = lax.broadcasted_iota(jnp.int32, (tm, tm), 1)
    tri = jnp.where(r < c, 1.0, 0.0).astype(jnp.bfloat16)
    rank = jnp.dot(onehot.astype(jnp.bfloat16), tri, preferred_element_type=jnp.float32)
    counts = jnp.broadcast_to(jnp.sum(onehot, axis=1, keepdims=True), (N_EXPERTS, LANES))
    rounded = jnp.floor((counts + (ROW_ALIGN - 1)) * (1.0 / ROW_ALIGN)) * ROW_ALIGN
    er = lax.broadcasted_iota(jnp.int32, (N_EXPERTS, N_EXPERTS), 0)
    ec = lax.broadcasted_iota(jnp.int32, (N_EXPERTS, N_EXPERTS), 1)
    below = jnp.where(ec < er, 1.0, 0.0).astype(jnp.bfloat16)
    offset = jnp.dot(below, rounded.astype(jnp.bfloat16), preferred_element_type=jnp.float32)
    slot_of = rank + offset[:, 0:1]
    cnt_ref[t] = rounded
    for kk in range(TOP_K):
        ls = jnp.sum(jnp.where(sels[kk], slot_of, 0.0), axis=0, keepdims=True).astype(jnp.int32)
        ls_ref[kk:kk + 1, rows] = ls
        tw_ref[kk:kk + 1, rows] = ex[kk] / den


def _mix(x2, o, p, z, wa, wb, wo, g_post_mix, mod3, g_pre_ffn, w_router_t, b_router, seq):
    T, D = x2.shape
    sub = min(MIX_TILES, seq // TOKEN_TILE)
    tm = sub * TOKEN_TILE
    per_b = seq // tm
    E = N_EXPERTS
    row = lambda n: pl.BlockSpec((1, n), lambda i: (0, 0))
    mod_spec = lambda col: pl.BlockSpec((1, 1, D), lambda i: (i // per_b, 0, col))
    gate = lambda cb: pl.BlockSpec((tm, COL_TILE), lambda i: (i, cb))
    full = lambda a: pl.BlockSpec(a.shape, lambda i: (0,) * a.ndim)
    tile = pl.BlockSpec((tm, D), lambda i: (i, 0))
    small = pl.BlockSpec((TOP_K, tm), lambda i: (0, i))
    return pl.pallas_call(
        _mix_kernel,
        grid=(T // tm,),
        in_specs=[tile, tile,
                  pl.BlockSpec((tm, POOL_WIDTH), lambda i: (i, 0)),
                  gate(COL_G), gate(COL_G + 1), gate(COL_G + 2), gate(COL_G + 3),
                  full(wa), full(wb), full(wo), row(D), mod_spec(2),
                  row(D), mod_spec(4), mod_spec(3), full(w_router_t),
                  pl.BlockSpec((E, 1), lambda i: (0, 0))],
        out_specs=[tile, tile, small, small,
                   pl.BlockSpec((sub, E, LANES), lambda i: (i, 0, 0))],
        out_shape=[jax.ShapeDtypeStruct((T, D), jnp.float32),
                   jax.ShapeDtypeStruct((T, D), jnp.bfloat16),
                   jax.ShapeDtypeStruct((TOP_K, T), jnp.int32),
                   jax.ShapeDtypeStruct((TOP_K, T), jnp.float32),
                   jax.ShapeDtypeStruct((T // TOKEN_TILE, E, LANES), jnp.float32)],
        compiler_params=_params(("parallel",)),
        name="mix_tail",
    )(x2, o, p, z, z, z, z, wa, wb, wo, g_post_mix, mod3, g_pre_ffn, mod3, mod3, w_router_t,
      b_router.reshape(E, 1))


def _chunk(ref, row):
    return ref.at[pl.ds(pl.multiple_of(row, ROW_ALIGN), ROW_ALIGN), :]


def _dispatch_kernel(chunk_slot_ref, n_chunk_ref, zero_off_ref, zero_len_ref, ntile_ref,
                     h_ref, ls_ref, xs_ref, xl_ref, zero_ref, sem, zsem, *, zero_per_step):
    i = pl.program_id(0)
    last = pl.num_programs(0) - 1

    @pl.when(i == 0)
    def _():
        zero_ref[...] = jnp.zeros_like(zero_ref)

    slot_id = lax.broadcasted_iota(jnp.int32, (LOCAL_ROWS, h_ref.shape[0]), 0)
    place = jnp.zeros(slot_id.shape, jnp.float32)
    for kk in range(TOP_K):
        place = jnp.where(slot_id == ls_ref[kk:kk + 1, :], 1.0, place)
    xl = jnp.dot(place.astype(jnp.bfloat16), h_ref[...], preferred_element_type=jnp.float32)
    xl_ref[i % 2] = xl.astype(xl_ref.dtype)

    def rows_out(tile, act):
        buf = tile % 2

        def body(k, _):
            act(pltpu.make_async_copy(_chunk(xl_ref.at[buf], k * ROW_ALIGN),
                                      _chunk(xs_ref, chunk_slot_ref[tile * BLOCK_CHUNKS + k]), sem.at[buf]))
            return 0
        lax.fori_loop(0, n_chunk_ref[tile], body, 0)

    def copies(act):
        for r in range(zero_per_step):
            idx = i * zero_per_step + r

            def zbody(k, _, idx=idx):
                act(pltpu.make_async_copy(_chunk(zero_ref, 0),
                                          _chunk(xs_ref, zero_off_ref[idx] + k * ROW_ALIGN), zsem))
                return 0
            lax.fori_loop(0, zero_len_ref[idx] // ROW_ALIGN, zbody, 0)

    def tails(act):
        def body(t, _):
            off = pl.multiple_of(t * MOE_TILE, MOE_TILE)
            act(pltpu.make_async_copy(zero_ref, xs_ref.at[pl.ds(off, MOE_TILE), :], zsem))
            return 0
        lax.fori_loop(ntile_ref[0], xs_ref.shape[0] // MOE_TILE, body, 0)

    start, wait = (lambda cp: cp.start()), (lambda cp: cp.wait())
    rows_out(i, start)
    copies(start)
    pl.when(i == 0)(functools.partial(tails, start))
    pl.when(i > 0)(lambda: rows_out(i - 1, wait))
    copies(wait)
    pl.when(i == 0)(functools.partial(tails, wait))
    pl.when(i == last)(lambda: rows_out(i, wait))


def _dispatch(chunk_slot, n_chunks, zero_off, zero_len, n_tiles, h2, ls, n_slots):
    T, W = h2.shape
    tc = TOKEN_TILE
    steps = n_chunks.shape[0]
    return pl.pallas_call(
        functools.partial(_dispatch_kernel, zero_per_step=zero_len.shape[0] // steps),
        grid_spec=pltpu.PrefetchScalarGridSpec(
            num_scalar_prefetch=5,
            grid=(steps,),
            in_specs=[pl.BlockSpec((tc, W), lambda i, *_: (i, 0)),
                      pl.BlockSpec((TOP_K, tc), lambda i, *_: (0, i))],
            out_specs=pl.BlockSpec(memory_space=pl.ANY),
            scratch_shapes=[pltpu.VMEM((2, LOCAL_ROWS, W), h2.dtype),
                            pltpu.VMEM((MOE_TILE, W), h2.dtype),
                            pltpu.SemaphoreType.DMA((2,)), pltpu.SemaphoreType.DMA]),
        out_shape=jax.ShapeDtypeStruct((n_slots, W), h2.dtype),
        compiler_params=_params(("arbitrary",)),
        name="dispatch",
    )(chunk_slot, n_chunks, zero_off, zero_len, n_tiles, h2, ls)


def _moe_kernel(te_ref, nt_ref, buf_ref, nxt_ref, rows_ref, x_ref, w1_hbm, b1_ref, w2_hbm, b2_ref, perm_ref, y_ref,
                w1raw_ref, w2raw_ref, w1s_ref, w2s_ref, sem):
    j = pl.program_id(0)
    prev = te_ref[jnp.maximum(j - 1, 0)]
    fresh = (j == 0) | (te_ref[j] != prev)
    nblk = w1s_ref.shape[1] // MXU_DIM

    def weight_copies(e, b):
        return (pltpu.make_async_copy(w1_hbm.at[e], w1raw_ref.at[b], sem.at[0, b]),
                pltpu.make_async_copy(w2_hbm.at[e], w2raw_ref.at[b], sem.at[1, b]))

    @pl.when(fresh & (j < nt_ref[0]))
    def _():
        e, b, nxt = te_ref[j], buf_ref[j], nxt_ref[j]

        @pl.when(j == 0)
        def _():
            for cp in weight_copies(e, b):
                cp.start()

        for cp in weight_copies(e, b):
            cp.wait()

        @pl.when(nxt >= 0)
        def _():
            for cp in weight_copies(nxt, 1 - b):
                cp.start()

        for c in range(nblk):
            cols = slice(c * MXU_DIM, (c + 1) * MXU_DIM)
            blk = w1raw_ref[b, :, cols].astype(jnp.bfloat16)
            w1s_ref[:, cols] = jnp.dot(blk, perm_ref[...],
                                       preferred_element_type=jnp.float32).astype(jnp.bfloat16)
        w2s_ref[...] = w2raw_ref[b].astype(jnp.bfloat16)

    def experts_mlp(rows):
        z = jnp.dot(x_ref[:rows, :], w1s_ref[...], preferred_element_type=jnp.float32) + b1_ref[0]
        acts = []
        for c in range(nblk):
            gate = jnp.minimum(z[:, c * MXU_DIM:c * MXU_DIM + LANES], SWIGLU_LIMIT)
            up = jnp.clip(z[:, c * MXU_DIM + LANES:(c + 1) * MXU_DIM], -SWIGLU_LIMIT, SWIGLU_LIMIT)
            acts.append(gate * jax.nn.sigmoid(SWIGLU_ALPHA * gate) * (up + 1.0))
        a = jnp.concatenate(acts, axis=1).astype(jnp.bfloat16)
        y = jnp.dot(a, w2s_ref[...], preferred_element_type=jnp.float32) + b2_ref[0]
        y_ref[:rows, :] = y.astype(y_ref.dtype)
        if rows < y_ref.shape[0]:
            y_ref[rows:, :] = jnp.zeros((y_ref.shape[0] - rows, y_ref.shape[1]), y_ref.dtype)

    half = y_ref.shape[0] // 2
    active = j < nt_ref[0]
    pl.when(active & (rows_ref[j] > half))(lambda: experts_mlp(2 * half))
    pl.when(active & (rows_ref[j] <= half))(lambda: experts_mlp(half))

    @pl.when(j >= nt_ref[0])
    def _():
        y_ref[...] = jnp.zeros_like(y_ref)


def _regroup_perm():
    src = np.arange(MXU_DIM)
    dst = np.where(src % 2 == 0, src // 2, LANES + src // 2)
    perm = np.zeros((MXU_DIM, MXU_DIM), np.float32)
    perm[src, dst] = 1.0
    return jnp.asarray(perm, jnp.bfloat16)


def _moe(tile_expert, n_tiles, tile_buf, tile_next, tile_rows, xs, w1, b1g, w2, b2):
    n_slots, W = xs.shape
    tm = MOE_TILE
    E, D, F2 = w1.shape
    F = w2.shape[1]
    xmap = lambda j, te, nt, *_: (jnp.minimum(j, nt[0] - 1), 0)
    emap = lambda j, te, *_: (te[j], 0, 0)
    return pl.pallas_call(
        _moe_kernel,
        grid_spec=pltpu.PrefetchScalarGridSpec(
            num_scalar_prefetch=5,
            grid=(n_slots // tm,),
            in_specs=[pl.BlockSpec((tm, W), xmap),
                      pl.BlockSpec(memory_space=pl.ANY),
                      pl.BlockSpec((1, 1, F2), emap),
                      pl.BlockSpec(memory_space=pl.ANY),
                      pl.BlockSpec((1, 1, D), emap),
                      pl.BlockSpec((MXU_DIM, MXU_DIM), lambda j, *_: (0, 0))],
            out_specs=pl.BlockSpec((tm, W), lambda j, *_: (j, 0)),
            scratch_shapes=[pltpu.VMEM((2, D, F2), w1.dtype), pltpu.VMEM((2, F, D), w2.dtype),
                            pltpu.VMEM((D, F2), jnp.bfloat16), pltpu.VMEM((F, D), jnp.bfloat16),
                            pltpu.SemaphoreType.DMA((2, 2))]),
        out_shape=jax.ShapeDtypeStruct((n_slots, W), xs.dtype),
        compiler_params=_params(("arbitrary",)),
        name="moe_experts",
    )(tile_expert, n_tiles, tile_buf, tile_next, tile_rows, xs, w1, b1g, w2, b2.reshape(E, 1, D),
      _regroup_perm())


def _combine_kernel(chunk_slot_ref, n_chunk_ref, ys_ref, ls_ref, w_ref, x1_ref, gt2_ref, g_ref,
                    o_ref, yl_ref, sem):
    i = pl.program_id(0)
    tc = x1_ref.shape[0]

    def pieces(tile, act):
        buf = tile % 2

        def body(k, _):
            act(pltpu.make_async_copy(_chunk(ys_ref, chunk_slot_ref[tile * BLOCK_CHUNKS + k]),
                                      _chunk(yl_ref.at[buf], k * ROW_ALIGN), sem.at[buf]))
            return 0
        lax.fori_loop(0, n_chunk_ref[tile], body, 0)

    def fetch(tile):
        yl_ref[tile % 2, TOP_K * tc:, :] = jnp.zeros((LOCAL_ROWS - TOP_K * tc, yl_ref.shape[2]), yl_ref.dtype)
        pieces(tile, lambda cp: cp.start())

    pl.when(i == 0)(lambda: fetch(i))
    pl.when(i + 1 < pl.num_programs(0))(lambda: fetch(i + 1))
    slot_id = lax.broadcasted_iota(jnp.int32, (tc, LOCAL_ROWS), 1)
    ls = ls_ref[...]
    w = w_ref[...]
    mix = jnp.zeros((tc, LOCAL_ROWS), jnp.float32)
    for k in range(TOP_K):
        mix = jnp.where(slot_id == ls[:, k:k + 1], w[:, k:k + 1], mix)
    pieces(i, lambda cp: cp.wait())
    f = jnp.dot(mix.astype(jnp.bfloat16), yl_ref[i % 2], preferred_element_type=jnp.float32)
    o_ref[...] = x1_ref[...] + gt2_ref[0] * (_rms(f, NORM_EPS) * g_ref[...])


def _combine(chunk_slot, n_chunks, ys, ls_t, topw_t, x1, mod3, g_post_ffn, seq):
    T, D = x1.shape
    tc = TOKEN_TILE
    per_b = seq // tc
    tile = pl.BlockSpec((tc, D), lambda i, *_: (i, 0))
    small = pl.BlockSpec((tc, TOP_K), lambda i, *_: (i, 0))
    return pl.pallas_call(
        _combine_kernel,
        grid_spec=pltpu.PrefetchScalarGridSpec(
            num_scalar_prefetch=2,
            grid=(T // tc,),
            in_specs=[pl.BlockSpec(memory_space=pl.ANY), small, small, tile,
                      pl.BlockSpec((1, 1, D), lambda i, *_: (i // per_b, 0, 5)),
                      pl.BlockSpec((1, D), lambda i, *_: (0, 0))],
            out_specs=tile,
            scratch_shapes=[pltpu.VMEM((2, LOCAL_ROWS, ys.shape[1]), ys.dtype),
                            pltpu.SemaphoreType.DMA((2,))]),
        out_shape=jax.ShapeDtypeStruct((T, D), jnp.float32),
        compiler_params=_params(("arbitrary",)),
        name="combine",
    )(chunk_slot, n_chunks, ys, ls_t, topw_t, x1, mod3, g_post_ffn)


def kernel(x, c, positions, w_ada, b_ada, g_pre_mix, w_in, lambda_q1, lambda_k1, lambda_q2, lambda_k2,
           g_subln, w_pool, pool_scale, w_proj_a, w_proj_b, w_out, g_post_mix, g_pre_ffn,
           w_router, b_router, w_exp1, b_exp1, w_exp2, b_exp2, g_post_ffn):
    B, S, D = x.shape
    T = B * S
    bf = jnp.bfloat16
    tabs = _rope_tables(positions)
    for l in range(w_ada.shape[0]):
        x2 = x.reshape(T, D)
        mod3 = _ada(c, w_ada[l], b_ada[l]).reshape(B, 1, N_MOD * D)
        n_qk = 2 * DIFF_WIDTH
        w_in_bf = jnp.concatenate([_permute_qk_columns(w_in[l][:, :n_qk]), w_in[l][:, n_qk:]], axis=1).astype(bf)
        z = _in_proj(x2, g_pre_mix[l][None], mod3, w_in_bf, tabs, S)
        lams = [v[l][None] for v in (lambda_q1, lambda_k1, lambda_q2, lambda_k2)]
        lam_init = 0.8 - 0.6 * math.exp(-0.3 * l)
        o = _diff_attention(z, lams, g_subln[l][None], lam_init, B, S)
        p = _pool(z, w_pool[l], pool_scale[l][None], B, S)
        x1, h2, ls, topw, cnt = _mix(
            x2, o, p, z, w_proj_a[l].astype(bf), w_proj_b[l].astype(bf), w_out[l].astype(bf),
            g_post_mix[l][None], mod3, g_pre_ffn[l][None], w_router[l].T, b_router[l], S)

        i32 = jnp.int32
        n = cnt[:, :, 0].astype(i32)
        n_tok_tiles = n.shape[0]
        counts = jnp.sum(n, axis=0)
        padded = (counts + MOE_TILE - 1) // MOE_TILE * MOE_TILE
        gend = jnp.cumsum(padded).astype(i32)
        gstart = gend - padded
        local_end = jnp.cumsum(n, axis=1)
        slot_off = gstart[None, :] + jnp.cumsum(n, axis=0) - n
        chunk_row = jnp.arange(BLOCK_CHUNKS, dtype=i32) * ROW_ALIGN
        piece = jnp.sum(local_end[:, None, :] <= chunk_row[None, :, None], axis=2)
        in_piece = piece[:, :, None] == jnp.arange(N_EXPERTS, dtype=i32)[None, None, :]
        shift = jnp.sum(jnp.where(in_piece, (slot_off - (local_end - n))[:, None, :], 0), axis=2)
        chunk_slot = (shift + chunk_row[None, :]).reshape(-1).astype(i32)
        n_chunks = (local_end[:, -1] // ROW_ALIGN).astype(i32)
        steps_pad = -N_EXPERTS % n_tok_tiles
        group_pad_off = jnp.pad(gstart + counts, (0, steps_pad)).astype(i32)
        group_pad_len = jnp.pad(padded - counts, (0, steps_pad)).astype(i32)
        n_slots = -(-(T * TOP_K + n_tok_tiles * N_EXPERTS * ROW_ALIGN) // MOE_TILE) * MOE_TILE + N_EXPERTS * MOE_TILE
        tile_start = jnp.arange(n_slots // MOE_TILE, dtype=i32) * MOE_TILE
        tile_expert = jnp.minimum(jnp.sum(tile_start[:, None] >= gend[None, :], axis=1), N_EXPERTS - 1)
        n_tiles = gend[-1:] // MOE_TILE
        has_tiles = padded > 0
        expert_ids = jnp.arange(N_EXPERTS, dtype=i32)
        later = lax.cummin(jnp.where(has_tiles, expert_ids, N_EXPERTS), reverse=True)
        next_expert = jnp.concatenate([later[1:], jnp.full((1,), N_EXPERTS, i32)])
        next_expert = jnp.where(next_expert < N_EXPERTS, next_expert, -1)
        of_tile = tile_expert[:, None] == expert_ids[None, :]
        pick = lambda per_expert: jnp.sum(jnp.where(of_tile, per_expert[None, :], 0), axis=1).astype(i32)
        tile_buf = pick((jnp.cumsum(has_tiles.astype(i32)) - 1) % 2)
        tile_next = pick(next_expert)
        tile_rows = jnp.clip(pick(gstart + counts) - tile_start, 0, MOE_TILE).astype(i32)

        xs = _dispatch(chunk_slot, n_chunks, group_pad_off, group_pad_len, n_tiles, h2, ls, n_slots)
        half = D_FF
        b1g = b_exp1[l].reshape(N_EXPERTS, half // LANES, LANES, 2).transpose(0, 1, 3, 2)
        b1g = b1g.reshape(N_EXPERTS, 1, 2 * half)
        ys = _moe(tile_expert.astype(i32), n_tiles, tile_buf, tile_next, tile_rows, xs,
                  w_exp1[l], b1g, w_exp2[l], b_exp2[l])
        x = _combine(chunk_slot, n_chunks, ys, ls.T, topw.T, x1, mod3, g_post_ffn[l][None], S)
        x = x.reshape(B, S, D)
    return x
```

```python
import functools
import math

import numpy as np
import jax
import jax.numpy as jnp
from jax import lax
from jax.experimental import pallas as pl
from jax.experimental.pallas import tpu as pltpu

D_MODEL = 1024
N_HEADS = 8
HEAD_DIM = 64
V_DIM = 2 * HEAD_DIM
DIFF_WIDTH = N_HEADS * V_DIM
POOL_WINDOWS = (2, 4, 8, 16)
POOL_GROUP_DIM = 128
POOL_WIDTH = len(POOL_WINDOWS) * POOL_GROUP_DIM
IN_COLS = 3 * DIFF_WIDTH + POOL_WIDTH + 2 * D_MODEL
ROPE_THETA = 500000.0
ROT_DIM = HEAD_DIM // 4
ROT_HALF = ROT_DIM // 2
N_EXPERTS = 32
TOP_K = 4
D_FF = D_MODEL
SWIGLU_ALPHA = 1.702
SWIGLU_LIMIT = 7.0
NORM_EPS = 1e-6
SUBLN_EPS = 1e-5
N_MOD = 6
NEG_BIG = -1e30

LANES = 128
MXU_DIM = 256
VMEM_LIMIT = 56 * 1024 * 1024

ADA_COL_TILE = 1536
ROPE_ROW_TILE = 2048
COL_TILE = 512
IN_ROW_TILE = 512
COL_K = DIFF_WIDTH // COL_TILE
COL_V = 2 * DIFF_WIDTH // COL_TILE
COL_G = (3 * DIFF_WIDTH + POOL_WIDTH) // COL_TILE
ATT_BLOCK = 512
ATT_HEADS = 4
MOE_TILE = 512
TOKEN_TILE = 256
MIX_TILES = 2
ROW_ALIGN = 16
LOCAL_ROWS = TOP_K * TOKEN_TILE + N_EXPERTS * ROW_ALIGN
BLOCK_CHUNKS = LOCAL_ROWS // ROW_ALIGN

_HI = lax.Precision.HIGHEST


def _params(sem, vmem=VMEM_LIMIT):
    return pltpu.CompilerParams(dimension_semantics=sem, vmem_limit_bytes=vmem)


def _rms(x, eps):
    return x * lax.rsqrt(jnp.mean(x * x, axis=-1, keepdims=True) + eps)


def _ada_kernel(c_ref, w_ref, b_ref, o_ref):
    c = c_ref[...]
    s = c * jax.nn.sigmoid(c)
    o_ref[...] = jnp.dot(s, w_ref[...], precision=_HI, preferred_element_type=jnp.float32) + b_ref[...]


def _ada(c, w_ada, b_ada):
    B, D = c.shape
    N = w_ada.shape[1]
    tn = ADA_COL_TILE
    return pl.pallas_call(
        _ada_kernel,
        grid=(N // tn,),
        in_specs=[pl.BlockSpec((B, D), lambda j: (0, 0)),
                  pl.BlockSpec((D, tn), lambda j: (0, j)),
                  pl.BlockSpec((1, tn), lambda j: (0, j))],
        out_specs=pl.BlockSpec((B, tn), lambda j: (0, j)),
        out_shape=jax.ShapeDtypeStruct((B, N), jnp.float32),
        compiler_params=_params(("parallel",)),
        name="ada",
    )(c, w_ada, b_ada.reshape(1, N))


def _rope_tab_kernel(pos_ref, invf_ref, phase_ref, c_ref, s_ref):
    pos = pos_ref[...].astype(jnp.float32)
    cs = jnp.cos(pos * invf_ref[...] - phase_ref[...])
    rolled = pltpu.roll(cs, LANES // 2, 1)
    lane = lax.broadcasted_iota(jnp.int32, cs.shape, 1)
    lower = lane < LANES // 2
    rot = lane % (LANES // 2) < ROT_DIM
    c_ref[...] = jnp.where(lower, cs, rolled)
    s_ref[...] = jnp.where(rot, jnp.where(lower, -rolled, cs), 0.0)


def _rope_tables(positions):
    T = positions.size
    tm = min(T, ROPE_ROW_TILE)
    lane = np.arange(LANES)
    rot = lane % (LANES // 2) < ROT_DIM
    inv = ROPE_THETA ** (-(np.arange(ROT_HALF, dtype=np.float64) * 2.0 / ROT_DIM))
    invf = np.where(rot, inv[lane % ROT_HALF], 0.0).astype(np.float32)[None, :]
    phase = np.where(rot & (lane >= LANES // 2), np.pi / 2, 0.0).astype(np.float32)[None, :]
    row = pl.BlockSpec((1, LANES), lambda i: (0, 0))
    tab = pl.BlockSpec((tm, LANES), lambda i: (i, 0))
    sds = jax.ShapeDtypeStruct((T, LANES), jnp.float32)
    return pl.pallas_call(
        _rope_tab_kernel,
        grid=(T // tm,),
        in_specs=[pl.BlockSpec((tm, 1), lambda i: (i, 0)), row, row],
        out_specs=[tab, tab],
        out_shape=[sds, sds],
        compiler_params=_params(("parallel",)),
        name="rope_tables",
    )(positions.reshape(T, 1), jnp.asarray(invf), jnp.asarray(phase))


def _permute_qk_columns(w):
    D, n = w.shape
    w = w.reshape(D, n // V_DIM, 2, 4, 2, ROT_HALF)
    return w.transpose(0, 1, 4, 3, 2, 5).reshape(D, n)


def _in_proj_kernel(x_ref, g_ref, sc_ref, sh_ref, w_ref, c_ref, s_ref, z_ref):
    h = (_rms(x_ref[...], NORM_EPS) * g_ref[...] * (1.0 + sc_ref[0]) + sh_ref[0]).astype(jnp.bfloat16)
    c, s = c_ref[...], s_ref[...]
    q_scale = HEAD_DIM ** -0.5 * math.log2(math.e)
    cq, sq = c * q_scale, s * q_scale
    for j in range(IN_COLS // COL_TILE):
        cols = slice(j * COL_TILE, (j + 1) * COL_TILE)
        z = jnp.dot(h, w_ref[:, cols], preferred_element_type=jnp.float32)
        if j < COL_V:
            cj, sj = (cq, sq) if j < COL_K else (c, s)
            parts = []
            for g in range(COL_TILE // LANES):
                zg = z[:, g * LANES:(g + 1) * LANES]
                parts.append(zg * cj + pltpu.roll(zg, LANES // 2, 1) * sj)
            z = jnp.concatenate(parts, axis=1)
        elif j >= COL_G:
            z = 0.5 * jnp.tanh(0.5 * z) + 0.5
        z_ref[:, cols] = z.astype(z_ref.dtype)


def _in_proj(x2, g_pre, mod3, w_in_bf, tabs, seq):
    T, D = x2.shape
    tm = min(seq, IN_ROW_TILE)
    per_b = seq // tm
    mod_spec = lambda col: pl.BlockSpec((1, 1, D), lambda i: (i // per_b, 0, col))
    tab = pl.BlockSpec((tm, LANES), lambda i: (i, 0))
    return pl.pallas_call(
        _in_proj_kernel,
        grid=(T // tm,),
        in_specs=[pl.BlockSpec((tm, D), lambda i: (i, 0)),
                  pl.BlockSpec((1, D), lambda i: (0, 0)),
                  mod_spec(1), mod_spec(0),
                  pl.BlockSpec((D, IN_COLS), lambda i: (0, 0)),
                  tab, tab],
        out_specs=pl.BlockSpec((tm, IN_COLS), lambda i: (i, 0)),
        out_shape=jax.ShapeDtypeStruct((T, IN_COLS), jnp.bfloat16),
        compiler_params=_params(("parallel",)),
        name="in_proj",
    )(x2, g_pre, mod3, mod3, w_in_bf, *tabs)


def _attn_kernel(lq1_ref, lk1_ref, lq2_ref, lk2_ref, gs_ref, q_ref, k_ref, v_ref, o_ref, vt_ref, *acc_refs,
                 lam_init):
    i = pl.program_id(2)
    blk = q_ref.shape[0]
    nblk = v_ref.shape[0] // blk
    chains = [(h, comp) for h in range(ATT_HEADS) for comp in range(2)]

    @pl.when(i == 0)
    def _():
        for c in range(nblk):
            for h in range(ATT_HEADS):
                v = v_ref[c * blk:(c + 1) * blk, h * V_DIM:(h + 1) * V_DIM]
                vt_ref[c, h] = v.astype(jnp.float32).T.astype(vt_ref.dtype)

    lam = (jnp.exp(jnp.sum(lq1_ref[...] * lk1_ref[...], axis=-1, keepdims=True))
           - jnp.exp(jnp.sum(lq2_ref[...] * lk2_ref[...], axis=-1, keepdims=True))
           + lam_init)
    lane_comp = lax.broadcasted_iota(jnp.int32, (blk, V_DIM), 1) // ROT_HALF % 2
    qs = []
    for h, comp in chains:
        q = q_ref[:, h * V_DIM:(h + 1) * V_DIM]
        qs.append(jnp.where(lane_comp == comp, q, jnp.zeros_like(q)))
    nt = (((1,), (1,)), ((), ()))
    for acc_ref in acc_refs:
        acc_ref[...] = jnp.zeros_like(acc_ref)

    def step(c, carry, mask=None):
        off = pl.multiple_of(c * blk, blk)
        scores = []
        for n, (h, comp) in enumerate(chains):
            k = k_ref[pl.ds(off, blk), h * V_DIM:(h + 1) * V_DIM]
            scores.append(lax.dot_general(k, qs[n], nt, preferred_element_type=jnp.float32))
        out, probs, alphas = [], [], []
        for n, s in enumerate(scores):
            m, l = carry[n]
            if mask is not None:
                s = jnp.where(mask, s, NEG_BIG)
            m_new = jnp.maximum(m, jnp.max(s, axis=0, keepdims=True))
            alpha = jnp.exp2(m - m_new)
            p = jnp.exp2(s - m_new)
            out.append((m_new, alpha * l + jnp.sum(p, axis=0, keepdims=True)))
            probs.append(p.astype(vt_ref.dtype))
            alphas.append(alpha)
        for n, (h, comp) in enumerate(chains):
            pv = jnp.dot(vt_ref[c, h], probs[n], preferred_element_type=jnp.float32)
            acc_refs[n][...] = alphas[n] * acc_refs[n][...] + pv
        return tuple(out)

    init = tuple((jnp.full((1, blk), NEG_BIG, jnp.float32), jnp.zeros((1, blk), jnp.float32))
                 for _ in chains)
    carry = lax.fori_loop(0, i, step, init)

    key = lax.broadcasted_iota(jnp.int32, (blk, blk), 0)
    qry = lax.broadcasted_iota(jnp.int32, (blk, blk), 1)
    carry = step(i, carry, key <= qry)
    for h in range(ATT_HEADS):
        (_, l1), (_, l2) = carry[2 * h], carry[2 * h + 1]
        ot = acc_refs[2 * h][...] / l1 - lam * (acc_refs[2 * h + 1][...] / l2)
        o = _rms(ot.T, SUBLN_EPS) * gs_ref[...] * (1.0 - lam_init)
        o_ref[:, h * V_DIM:(h + 1) * V_DIM] = o.astype(o_ref.dtype)


def _diff_attention(z, lams, g_subln, lam_init, batch, seq):
    T = z.shape[0]
    blk = min(ATT_BLOCK, seq)
    nq = seq // blk
    width = ATT_HEADS * V_DIM
    vec = pl.BlockSpec((1, HEAD_DIM), lambda b, h, i: (0, 0))
    kcol = DIFF_WIDTH // width
    return pl.pallas_call(
        functools.partial(_attn_kernel, lam_init=lam_init),
        grid=(batch, N_HEADS // ATT_HEADS, nq),
        in_specs=[vec, vec, vec, vec,
                  pl.BlockSpec((1, V_DIM), lambda b, h, i: (0, 0)),
                  pl.BlockSpec((blk, width), lambda b, h, i: (b * nq + i, h)),
                  pl.BlockSpec((seq, width), lambda b, h, i: (b, kcol + h)),
                  pl.BlockSpec((seq, width), lambda b, h, i: (b, 2 * kcol + h))],
        out_specs=pl.BlockSpec((blk, width), lambda b, h, i: (b * nq + i, h)),
        out_shape=jax.ShapeDtypeStruct((T, DIFF_WIDTH), jnp.bfloat16),
        scratch_shapes=[pltpu.VMEM((nq, ATT_HEADS, V_DIM, blk), jnp.bfloat16)]
        + [pltpu.VMEM((V_DIM, blk), jnp.float32)] * (2 * ATT_HEADS),
        compiler_params=_params(("parallel", "parallel", "arbitrary")),
        name="diff_attn",
    )(*lams, g_subln, z, z, z)


def _pool_kernel(u_ref, w_ref, ps_ref, o_ref):
    g = pl.program_id(1)
    u = u_ref[...].astype(jnp.float32)
    t = lax.broadcasted_iota(jnp.int32, u.shape, 0)

    def shifted(x, k):
        return jnp.where(t >= k, pltpu.roll(x, k, 0), 0.0)

    s = u
    d = jnp.zeros_like(u)
    for gi, w in enumerate(POOL_WINDOWS):
        s = s + shifted(s, w // 2)
        cnt = jnp.minimum(t + 1, w).astype(jnp.float32)
        d = jnp.where(g == gi, s / cnt - u, d)
    y = jnp.dot(d.astype(jnp.bfloat16), w_ref[0].astype(jnp.bfloat16), preferred_element_type=jnp.float32)
    o_ref[...] = (y * ps_ref[...]).astype(o_ref.dtype)


def _pool(z, w_pool, pool_scale, batch, seq):
    T = z.shape[0]
    G = len(POOL_WINDOWS)
    ucol = 3 * DIFF_WIDTH // POOL_GROUP_DIM
    return pl.pallas_call(
        _pool_kernel,
        grid=(batch, G),
        in_specs=[pl.BlockSpec((seq, POOL_GROUP_DIM), lambda b, g: (b, ucol + g)),
                  pl.BlockSpec((1, POOL_GROUP_DIM, POOL_GROUP_DIM), lambda b, g: (g, 0, 0)),
                  pl.BlockSpec((1, POOL_GROUP_DIM), lambda b, g: (0, g))],
        out_specs=pl.BlockSpec((seq, POOL_GROUP_DIM), lambda b, g: (b, g)),
        out_shape=jax.ShapeDtypeStruct((T, POOL_WIDTH), jnp.bfloat16),
        compiler_params=_params(("parallel", "parallel")),
        name="pool",
    )(z, w_pool, pool_scale)


def _mix_kernel(x_ref, o_ref, p_ref, ga0, ga1, gb0, gb1, wa_ref, wb_ref, wo_ref, gpm_ref, gt1_ref,
                gpf_ref, sc2_ref, sh2_ref, wr_ref, br_ref,
                x1_ref, h2_ref, ls_ref, tw_ref, cnt_ref):
    tiles = range(x_ref.shape[0] // TOKEN_TILE)
    h2s = [_mix_front(t, x_ref, o_ref, p_ref, ga0, ga1, gb0, gb1, wa_ref, wb_ref, wo_ref, gpm_ref, gt1_ref,
                      gpf_ref, sc2_ref, sh2_ref, x1_ref) for t in tiles]
    for t in tiles:
        h2_ref[t * TOKEN_TILE:(t + 1) * TOKEN_TILE, :] = h2s[t].astype(h2_ref.dtype)
    for t in tiles:
        _mix_route(t, h2s[t], wr_ref, br_ref, ls_ref, tw_ref, cnt_ref)


def _mix_front(t, x_ref, o_ref, p_ref, ga0, ga1, gb0, gb1, wa_ref, wb_ref, wo_ref, gpm_ref, gt1_ref,
               gpf_ref, sc2_ref, sh2_ref, x1_ref):
    rows = slice(t * TOKEN_TILE, (t + 1) * TOKEN_TILE)
    ya = jnp.dot(o_ref[rows, :], wa_ref[...], preferred_element_type=jnp.float32)
    yb = jnp.dot(p_ref[rows, :], wb_ref[...], preferred_element_type=jnp.float32)
    ga = jnp.concatenate([ga0[rows, :], ga1[rows, :]], axis=1).astype(jnp.float32)
    gb = jnp.concatenate([gb0[rows, :], gb1[rows, :]], axis=1).astype(jnp.float32)
    merged = (ga * ya + gb * yb).astype(jnp.bfloat16)
    mixed = jnp.dot(merged, wo_ref[...], preferred_element_type=jnp.float32)
    x1 = x_ref[rows, :] + gt1_ref[0] * (_rms(mixed, NORM_EPS) * gpm_ref[...])
    x1_ref[rows, :] = x1
    return _rms(x1, NORM_EPS) * gpf_ref[...] * (1.0 + sc2_ref[0]) + sh2_ref[0]


def _mix_route(t, h2, wr_ref, br_ref, ls_ref, tw_ref, cnt_ref):
    tm = TOKEN_TILE
    rows = slice(t * tm, (t + 1) * tm)
    logits =lax.dot_general(wr_ref[...], h2, (((1,), (1,)), ((), ())), precision=_HI,
                             preferred_element_type=jnp.float32) + br_ref[...]
    eid = lax.broadcasted_iota(jnp.int32, logits.shape, 0)
    work = logits
    sels, vals = [], []
    for _ in range(TOP_K):
        mx = jnp.max(work, axis=0, keepdims=True)
        idx = jnp.min(jnp.where(work == mx, eid, N_EXPERTS), axis=0, keepdims=True)
        sel = eid == idx
        work = jnp.where(sel, -jnp.inf, work)
        sels.append(sel)
        vals.append(mx)
    ex = [jnp.exp(v - vals[0]) for v in vals]
    den = ex[0] + ex[1] + ex[2] + ex[3]
    onehot = jnp.zeros(logits.shape, jnp.float32)
    for sel in sels:
        onehot = jnp.where(sel, 1.0, onehot)
    r = lax.broadcasted_iota(jnp.int32, (tm, tm), 0)
    c = lax.broadcasted_iota(jnp.int32, (tm, tm), 1)
    tri = jnp.where(r < c, 1.0, 0.0).astype(jnp.bfloat16)
    rank = jnp.dot(onehot.astype(jnp.bfloat16), tri, preferred_element_type=jnp.float32)
    counts = jnp.broadcast_to(jnp.sum(onehot, axis=1, keepdims=True), (N_EXPERTS, LANES))
    rounded = jnp.floor((counts + (ROW_ALIGN - 1)) * (1.0 / ROW_ALIGN)) * ROW_ALIGN
    er = lax.broadcasted_iota(jnp.int32, (N_EXPERTS, N_EXPERTS), 0)
    ec = lax.broadcasted_iota(jnp.int32, (N_EXPERTS, N_EXPERTS), 1)
    below = jnp.where(ec < er, 1.0, 0.0).astype(jnp.bfloat16)
    offset = jnp.dot(below, rounded.astype(jnp.bfloat16), preferred_element_type=jnp.float32)
    slot_of = rank + offset[:, 0:1]
    cnt_ref[t] = rounded
    for kk in range(TOP_K):
        ls = jnp.sum(jnp.where(sels[kk], slot_of, 0.0), axis=0, keepdims=True).astype(jnp.int32)
        ls_ref[kk:kk + 1, rows] = ls
        tw_ref[kk:kk + 1, rows] = ex[kk] / den


def _mix(x2, o, p, z, wa, wb, wo, g_post_mix, mod3, g_pre_ffn, w_router_t, b_router, seq):
    T, D = x2.shape
    sub = min(MIX_TILES, seq // TOKEN_TILE)
    tm = sub * TOKEN_TILE
    per_b = seq // tm
    E = N_EXPERTS
    row = lambda n: pl.BlockSpec((1, n), lambda i: (0, 0))
    mod_spec = lambda col: pl.BlockSpec((1, 1, D), lambda i: (i // per_b, 0, col))
    gate = lambda cb: pl.BlockSpec((tm, COL_TILE), lambda i: (i, cb))
    full = lambda a: pl.BlockSpec(a.shape, lambda i: (0,) * a.ndim)
    tile = pl.BlockSpec((tm, D), lambda i: (i, 0))
    small = pl.BlockSpec((TOP_K, tm), lambda i: (0, i))
    return pl.pallas_call(
        _mix_kernel,
        grid=(T // tm,),
        in_specs=[tile, tile,
                  pl.BlockSpec((tm, POOL_WIDTH), lambda i: (i, 0)),
                  gate(COL_G), gate(COL_G + 1), gate(COL_G + 2), gate(COL_G + 3),
                  full(wa), full(wb), full(wo), row(D), mod_spec(2),
                  row(D), mod_spec(4), mod_spec(3), full(w_router_t),
                  pl.BlockSpec((E, 1), lambda i: (0, 0))],
        out_specs=[tile, tile, small, small,
                   pl.BlockSpec((sub, E, LANES), lambda i: (i, 0, 0))],
        out_shape=[jax.ShapeDtypeStruct((T, D), jnp.float32),
                   jax.ShapeDtypeStruct((T, D), jnp.bfloat16),
                   jax.ShapeDtypeStruct((TOP_K, T), jnp.int32),
                   jax.ShapeDtypeStruct((TOP_K, T), jnp.float32),
                   jax.ShapeDtypeStruct((T // TOKEN_TILE, E, LANES), jnp.float32)],
        compiler_params=_params(("parallel",)),
        name="mix_tail",
    )(x2, o, p, z, z, z, z, wa, wb, wo, g_post_mix, mod3, g_pre_ffn, mod3, mod3, w_router_t,
      b_router.reshape(E, 1))


def _chunk(ref, row):
    return ref.at[pl.ds(pl.multiple_of(row, ROW_ALIGN), ROW_ALIGN), :]


def _dispatch_kernel(chunk_slot_ref, n_chunk_ref, zero_off_ref, zero_len_ref, ntile_ref,
                     h_ref, ls_ref, xs_ref, xl_ref, zero_ref, sem, zsem, *, zero_per_step):
    i = pl.program_id(0)
    last = pl.num_programs(0) - 1

    @pl.when(i == 0)
    def _():
        zero_ref[...] = jnp.zeros_like(zero_ref)

    slot_id = lax.broadcasted_iota(jnp.int32, (LOCAL_ROWS, h_ref.shape[0]), 0)
    place = jnp.zeros(slot_id.shape, jnp.float32)
    for kk in range(TOP_K):
        place = jnp.where(slot_id == ls_ref[kk:kk + 1, :], 1.0, place)
    xl = jnp.dot(place.astype(jnp.bfloat16), h_ref[...], preferred_element_type=jnp.float32)
    xl_ref[i % 2] = xl.astype(xl_ref.dtype)

    def rows_out(tile, act):
        buf = tile % 2

        def body(k, _):
            act(pltpu.make_async_copy(_chunk(xl_ref.at[buf], k * ROW_ALIGN),
                                      _chunk(xs_ref, chunk_slot_ref[tile * BLOCK_CHUNKS + k]), sem.at[buf]))
            return 0
        lax.fori_loop(0, n_chunk_ref[tile], body, 0)

    def copies(act):
        for r in range(zero_per_step):
            idx = i * zero_per_step + r

            def zbody(k, _, idx=idx):
                act(pltpu.make_async_copy(_chunk(zero_ref, 0),
                                          _chunk(xs_ref, zero_off_ref[idx] + k * ROW_ALIGN), zsem))
                return 0
            lax.fori_loop(0, zero_len_ref[idx] // ROW_ALIGN, zbody, 0)

    def tails(act):
        def body(t, _):
            off = pl.multiple_of(t * MOE_TILE, MOE_TILE)
            act(pltpu.make_async_copy(zero_ref, xs_ref.at[pl.ds(off, MOE_TILE), :], zsem))
            return 0
        lax.fori_loop(ntile_ref[0], xs_ref.shape[0] // MOE_TILE, body, 0)

    start, wait = (lambda cp: cp.start()), (lambda cp: cp.wait())
    rows_out(i, start)
    copies(start)
    pl.when(i == 0)(functools.partial(tails, start))
    pl.when(i > 0)(lambda: rows_out(i - 1, wait))
    copies(wait)
    pl.when(i == 0)(functools.partial(tails, wait))
    pl.when(i == last)(lambda: rows_out(i, wait))


def _dispatch(chunk_slot, n_chunks, zero_off, zero_len, n_tiles, h2, ls, n_slots):
    T, W = h2.shape
    tc = TOKEN_TILE
    steps = n_chunks.shape[0]
    return pl.pallas_call(
        functools.partial(_dispatch_kernel, zero_per_step=zero_len.shape[0] // steps),
        grid_spec=pltpu.PrefetchScalarGridSpec(
            num_scalar_prefetch=5,
            grid=(steps,),
            in_specs=[pl.BlockSpec((tc, W), lambda i, *_: (i, 0)),
                      pl.BlockSpec((TOP_K, tc), lambda i, *_: (0, i))],
            out_specs=pl.BlockSpec(memory_space=pl.ANY),
            scratch_shapes=[pltpu.VMEM((2, LOCAL_ROWS, W), h2.dtype),
                            pltpu.VMEM((MOE_TILE, W), h2.dtype),
                            pltpu.SemaphoreType.DMA((2,)), pltpu.SemaphoreType.DMA]),
        out_shape=jax.ShapeDtypeStruct((n_slots, W), h2.dtype),
        compiler_params=_params(("arbitrary",)),
        name="dispatch",
    )(chunk_slot, n_chunks, zero_off, zero_len, n_tiles, h2, ls)


def _moe_kernel(te_ref, nt_ref, buf_ref, nxt_ref, rows_ref, x_ref, w1_hbm, b1_ref, w2_hbm, b2_ref, perm_ref, y_ref,
                w1raw_ref, w2raw_ref, w1s_ref, w2s_ref, sem):
    j = pl.program_id(0)
    prev = te_ref[jnp.maximum(j - 1, 0)]
    fresh = (j == 0) | (te_ref[j] != prev)
    nblk = w1s_ref.shape[1] // MXU_DIM

    def weight_copies(e, b):
        return (pltpu.make_async_copy(w1_hbm.at[e], w1raw_ref.at[b], sem.at[0, b]),
                pltpu.make_async_copy(w2_hbm.at[e], w2raw_ref.at[b], sem.at[1, b]))

    @pl.when(fresh & (j < nt_ref[0]))
    def _():
        e, b, nxt = te_ref[j], buf_ref[j], nxt_ref[j]

        @pl.when(j == 0)
        def _():
            for cp in weight_copies(e, b):
                cp.start()

        for cp in weight_copies(e, b):
            cp.wait()

        @pl.when(nxt >= 0)
        def _():
            for cp in weight_copies(nxt, 1 - b):
                cp.start()

        for c in range(nblk):
            cols = slice(c * MXU_DIM, (c + 1) * MXU_DIM)
            blk = w1raw_ref[b, :, cols].astype(jnp.bfloat16)
            w1s_ref[:, cols] = jnp.dot(blk, perm_ref[...],
                                       preferred_element_type=jnp.float32).astype(jnp.bfloat16)
        w2s_ref[...] = w2raw_ref[b].astype(jnp.bfloat16)

    def experts_mlp(rows):
        z = jnp.dot(x_ref[:rows, :], w1s_ref[...], preferred_element_type=jnp.float32) + b1_ref[0]
        acts = []
        for c in range(nblk):
            gate = jnp.minimum(z[:, c * MXU_DIM:c * MXU_DIM + LANES], SWIGLU_LIMIT)
            up = jnp.clip(z[:, c * MXU_DIM + LANES:(c + 1) * MXU_DIM], -SWIGLU_LIMIT, SWIGLU_LIMIT)
            acts.append(gate * jax.nn.sigmoid(SWIGLU_ALPHA * gate) * (up + 1.0))
        a = jnp.concatenate(acts, axis=1).astype(jnp.bfloat16)
        y = jnp.dot(a, w2s_ref[...], preferred_element_type=jnp.float32) + b2_ref[0]
        y_ref[:rows, :] = y.astype(y_ref.dtype)
        if rows < y_ref.shape[0]:
            y_ref[rows:, :] = jnp.zeros((y_ref.shape[0] - rows, y_ref.shape[1]), y_ref.dtype)

    half = y_ref.shape[0] // 2
    active = j < nt_ref[0]
    pl.when(active & (rows_ref[j] > half))(lambda: experts_mlp(2 * half))
    pl.when(active & (rows_ref[j] <= half))(lambda: experts_mlp(half))

    @pl.when(j >= nt_ref[0])
    def _():
        y_ref[...] = jnp.zeros_like(y_ref)


def _regroup_perm():
    src = np.arange(MXU_DIM)
    dst = np.where(src % 2 == 0, src // 2, LANES + src // 2)
    perm = np.zeros((MXU_DIM, MXU_DIM), np.float32)
    perm[src, dst] = 1.0
    return jnp.asarray(perm, jnp.bfloat16)


def _moe(tile_expert, n_tiles, tile_buf, tile_next, tile_rows, xs, w1, b1g, w2, b2):
    n_slots, W = xs.shape
    tm = MOE_TILE
    E, D, F2 = w1.shape
    F = w2.shape[1]
    xmap = lambda j, te, nt, *_: (jnp.minimum(j, nt[0] - 1), 0)
    emap = lambda j, te, *_: (te[j], 0, 0)
    return pl.pallas_call(
        _moe_kernel,
        grid_spec=pltpu.PrefetchScalarGridSpec(
            num_scalar_prefetch=5,
            grid=(n_slots // tm,),
            in_specs=[pl.BlockSpec((tm, W), xmap),
                      pl.BlockSpec(memory_space=pl.ANY),
                      pl.BlockSpec((1, 1, F2), emap),
                      pl.BlockSpec(memory_space=pl.ANY),
                      pl.BlockSpec((1, 1, D), emap),
                      pl.BlockSpec((MXU_DIM, MXU_DIM), lambda j, *_: (0, 0))],
            out_specs=pl.BlockSpec((tm, W), lambda j, *_: (j, 0)),
            scratch_shapes=[pltpu.VMEM((2, D, F2), w1.dtype), pltpu.VMEM((2, F, D), w2.dtype),
                            pltpu.VMEM((D, F2), jnp.bfloat16), pltpu.VMEM((F, D), jnp.bfloat16),
                            pltpu.SemaphoreType.DMA((2, 2))]),
        out_shape=jax.ShapeDtypeStruct((n_slots, W), xs.dtype),
        compiler_params=_params(("arbitrary",)),
        name="moe_experts",
    )(tile_expert, n_tiles, tile_buf, tile_next, tile_rows, xs, w1, b1g, w2, b2.reshape(E, 1, D),
      _regroup_perm())


def _combine_kernel(chunk_slot_ref, n_chunk_ref, ys_ref, ls_ref, w_ref, x1_ref, gt2_ref, g_ref,
                    o_ref, yl_ref, sem):
    i = pl.program_id(0)
    tc = x1_ref.shape[0]

    def pieces(tile, act):
        buf = tile % 2

        def body(k, _):
            act(pltpu.make_async_copy(_chunk(ys_ref, chunk_slot_ref[tile * BLOCK_CHUNKS + k]),
                                      _chunk(yl_ref.at[buf], k * ROW_ALIGN), sem.at[buf]))
            return 0
        lax.fori_loop(0, n_chunk_ref[tile], body, 0)

    def fetch(tile):
        yl_ref[tile % 2, TOP_K * tc:, :] = jnp.zeros((LOCAL_ROWS - TOP_K * tc, yl_ref.shape[2]), yl_ref.dtype)
        pieces(tile, lambda cp: cp.start())

    pl.when(i == 0)(lambda: fetch(i))
    pl.when(i + 1 < pl.num_programs(0))(lambda: fetch(i + 1))
    slot_id = lax.broadcasted_iota(jnp.int32, (tc, LOCAL_ROWS), 1)
    ls = ls_ref[...]
    w = w_ref[...]
    mix = jnp.zeros((tc, LOCAL_ROWS), jnp.float32)
    for k in range(TOP_K):
        mix = jnp.where(slot_id == ls[:, k:k + 1], w[:, k:k + 1], mix)
    pieces(i, lambda cp: cp.wait())
    f = jnp.dot(mix.astype(jnp.bfloat16), yl_ref[i % 2], preferred_element_type=jnp.float32)
    o_ref[...] = x1_ref[...] + gt2_ref[0] * (_rms(f, NORM_EPS) * g_ref[...])


def _combine(chunk_slot, n_chunks, ys, ls_t, topw_t, x1, mod3, g_post_ffn, seq):
    T, D = x1.shape
    tc = TOKEN_TILE
    per_b = seq // tc
    tile = pl.BlockSpec((tc, D), lambda i, *_: (i, 0))
    small = pl.BlockSpec((tc, TOP_K), lambda i, *_: (i, 0))
    return pl.pallas_call(
        _combine_kernel,
        grid_spec=pltpu.PrefetchScalarGridSpec(
            num_scalar_prefetch=2,
            grid=(T // tc,),
            in_specs=[pl.BlockSpec(memory_space=pl.ANY), small, small, tile,
                      pl.BlockSpec((1, 1, D), lambda i, *_: (i // per_b, 0, 5)),
                      pl.BlockSpec((1, D), lambda i, *_: (0, 0))],
            out_specs=tile,
            scratch_shapes=[pltpu.VMEM((2, LOCAL_ROWS, ys.shape[1]), ys.dtype),
                            pltpu.SemaphoreType.DMA((2,))]),
        out_shape=jax.ShapeDtypeStruct((T, D), jnp.float32),
        compiler_params=_params(("arbitrary",)),
        name="combine",
    )(chunk_slot, n_chunks, ys, ls_t, topw_t, x1, mod3, g_post_ffn)


def kernel(x, c, positions, w_ada, b_ada, g_pre_mix, w_in, lambda_q1, lambda_k1, lambda_q2, lambda_k2,
           g_subln, w_pool, pool_scale, w_proj_a, w_proj_b, w_out, g_post_mix, g_pre_ffn,
           w_router, b_router, w_exp1, b_exp1, w_exp2, b_exp2, g_post_ffn):
    B, S, D = x.shape
    T = B * S
    bf = jnp.bfloat16
    tabs = _rope_tables(positions)
    for l in range(w_ada.shape[0]):
        x2 = x.reshape(T, D)
        mod3 = _ada(c, w_ada[l], b_ada[l]).reshape(B, 1, N_MOD * D)
        n_qk = 2 * DIFF_WIDTH
        w_in_bf = jnp.concatenate([_permute_qk_columns(w_in[l][:, :n_qk]), w_in[l][:, n_qk:]], axis=1).astype(bf)
        z = _in_proj(x2, g_pre_mix[l][None], mod3, w_in_bf, tabs, S)
        lams = [v[l][None] for v in (lambda_q1, lambda_k1, lambda_q2, lambda_k2)]
        lam_init = 0.8 - 0.6 * math.exp(-0.3 * l)
        o = _diff_attention(z, lams, g_subln[l][None], lam_init, B, S)
        p = _pool(z, w_pool[l], pool_scale[l][None], B, S)
        x1, h2, ls, topw, cnt = _mix(
            x2, o, p, z, w_proj_a[l].astype(bf), w_proj_b[l].astype(bf), w_out[l].astype(bf),
            g_post_mix[l][None], mod3, g_pre_ffn[l][None], w_router[l].T, b_router[l], S)

        i32 = jnp.int32
        n = cnt[:, :, 0].astype(i32)
        n_tok_tiles = n.shape[0]
        counts = jnp.sum(n, axis=0)
        padded = (counts + MOE_TILE - 1) // MOE_TILE * MOE_TILE
        gend = jnp.cumsum(padded).astype(i32)
        gstart = gend - padded
        local_end = jnp.cumsum(n, axis=1)
        slot_off = gstart[None, :] + jnp.cumsum(n, axis=0) - n
        chunk_row = jnp.arange(BLOCK_CHUNKS, dtype=i32) * ROW_ALIGN
        piece = jnp.sum(local_end[:, None, :] <= chunk_row[None, :, None], axis=2)
        in_piece = piece[:, :, None] == jnp.arange(N_EXPERTS, dtype=i32)[None, None, :]
        shift = jnp.sum(jnp.where(in_piece, (slot_off - (local_end - n))[:, None, :], 0), axis=2)
        chunk_slot = (shift + chunk_row[None, :]).reshape(-1).astype(i32)
        n_chunks = (local_end[:, -1] // ROW_ALIGN).astype(i32)
        steps_pad = -N_EXPERTS % n_tok_tiles
        group_pad_off = jnp.pad(gstart + counts, (0, steps_pad)).astype(i32)
        group_pad_len = jnp.pad(padded - counts, (0, steps_pad)).astype(i32)
        n_slots = -(-(T * TOP_K + n_tok_tiles * N_EXPERTS * ROW_ALIGN) // MOE_TILE) * MOE_TILE + N_EXPERTS * MOE_TILE
        tile_start = jnp.arange(n_slots // MOE_TILE, dtype=i32) * MOE_TILE
        tile_expert = jnp.minimum(jnp.sum(tile_start[:, None] >= gend[None, :], axis=1), N_EXPERTS - 1)
        n_tiles = gend[-1:] // MOE_TILE
        has_tiles = padded > 0
        expert_ids = jnp.arange(N_EXPERTS, dtype=i32)
        later = lax.cummin(jnp.where(has_tiles, expert_ids, N_EXPERTS), reverse=True)
        next_expert = jnp.concatenate([later[1:], jnp.full((1,), N_EXPERTS, i32)])
        next_expert = jnp.where(next_expert < N_EXPERTS, next_expert, -1)
        of_tile = tile_expert[:, None] == expert_ids[None, :]
        pick = lambda per_expert: jnp.sum(jnp.where(of_tile, per_expert[None, :], 0), axis=1).astype(i32)
        tile_buf = pick((jnp.cumsum(has_tiles.astype(i32)) - 1) % 2)
        tile_next = pick(next_expert)
        tile_rows = jnp.clip(pick(gstart + counts) - tile_start, 0, MOE_TILE).astype(i32)

        xs = _dispatch(chunk_slot, n_chunks, group_pad_off, group_pad_len, n_tiles, h2, ls, n_slots)
        half = D_FF
        b1g = b_exp1[l].reshape(N_EXPERTS, half // LANES, LANES, 2).transpose(0, 1, 3, 2)
        b1g = b1g.reshape(N_EXPERTS, 1, 2 * half)
        ys = _moe(tile_expert.astype(i32), n_tiles, tile_buf, tile_next, tile_rows, xs,
                  w_exp1[l], b1g, w_exp2[l], b_exp2[l])
        x = _combine(chunk_slot, n_chunks, ys, ls.T, topw.T, x1, mod3, g_post_ffn[l][None], S)
        x = x.reshape(B, S, D)
    return x
```

```python
import functools
import math

import numpy as np
import jax
import jax.numpy as jnp
from jax import lax
from jax.experimental import pallas as pl
from jax.experimental.pallas import tpu as pltpu

D_MODEL = 1024
N_HEADS = 8
HEAD_DIM = 64
V_DIM = 2 * HEAD_DIM
DIFF_WIDTH = N_HEADS * V_DIM
POOL_WINDOWS = (2, 4, 8, 16)
POOL_GROUP_DIM = 128
POOL_WIDTH = len(POOL_WINDOWS) * POOL_GROUP_DIM
IN_COLS = 3 * DIFF_WIDTH + POOL_WIDTH + 2 * D_MODEL
ROPE_THETA = 500000.0
ROT_DIM = HEAD_DIM // 4
ROT_HALF = ROT_DIM // 2
N_EXPERTS = 32
TOP_K = 4
D_FF = D_MODEL
SWIGLU_ALPHA = 1.702
SWIGLU_LIMIT = 7.0
NORM_EPS = 1e-6
SUBLN_EPS = 1e-5
N_MOD = 6
NEG_BIG = -1e30

LANES = 128
MXU_DIM = 256
VMEM_LIMIT = 56 * 1024 * 1024

ADA_COL_TILE = 1536
ROPE_ROW_TILE = 2048
COL_TILE = 512
IN_ROW_TILE = 512
COL_K = DIFF_WIDTH // COL_TILE
COL_V = 2 * DIFF_WIDTH // COL_TILE
COL_G = (3 * DIFF_WIDTH + POOL_WIDTH) // COL_TILE
ATT_BLOCK = 512
ATT_HEADS = 4
MOE_TILE = 512
TOKEN_TILE = 256
MIX_TILES = 2
ROW_ALIGN = 16
LOCAL_ROWS = TOP_K * TOKEN_TILE + N_EXPERTS * ROW_ALIGN
BLOCK_CHUNKS = LOCAL_ROWS // ROW_ALIGN

_HI = lax.Precision.HIGHEST


def _params(sem, vmem=VMEM_LIMIT):
    return pltpu.CompilerParams(dimension_semantics=sem, vmem_limit_bytes=vmem)


def _rms(x, eps):
    return x * lax.rsqrt(jnp.mean(x * x, axis=-1, keepdims=True) + eps)


def _ada_kernel(c_ref, w_ref, b_ref, o_ref):
    c = c_ref[...]
    s = c * jax.nn.sigmoid(c)
    o_ref[...] = jnp.dot(s, w_ref[...], precision=_HI, preferred_element_type=jnp.float32) + b_ref[...]


def _ada(c, w_ada, b_ada):
    B, D = c.shape
    N = w_ada.shape[1]
    tn = ADA_COL_TILE
    return pl.pallas_call(
        _ada_kernel,
        grid=(N // tn,),
        in_specs=[pl.BlockSpec((B, D), lambda j: (0, 0)),
                  pl.BlockSpec((D, tn), lambda j: (0, j)),
                  pl.BlockSpec((1, tn), lambda j: (0, j))],
        out_specs=pl.BlockSpec((B, tn), lambda j: (0, j)),
        out_shape=jax.ShapeDtypeStruct((B, N), jnp.float32),
        compiler_params=_params(("parallel",)),
        name="ada",
    )(c, w_ada, b_ada.reshape(1, N))


def _rope_tab_kernel(pos_ref, invf_ref, phase_ref, c_ref, s_ref):
    pos = pos_ref[...].astype(jnp.float32)
    cs = jnp.cos(pos * invf_ref[...] - phase_ref[...])
    rolled = pltpu.roll(cs, LANES // 2, 1)
    lane = lax.broadcasted_iota(jnp.int32, cs.shape, 1)
    lower = lane < LANES // 2
    rot = lane % (LANES // 2) < ROT_DIM
    c_ref[...] = jnp.where(lower, cs, rolled)
    s_ref[...] = jnp.where(rot, jnp.where(lower, -rolled, cs), 0.0)


def _rope_tables(positions):
    T = positions.size
    tm = min(T, ROPE_ROW_TILE)
    lane = np.arange(LANES)
    rot = lane % (LANES // 2) < ROT_DIM
    inv = ROPE_THETA ** (-(np.arange(ROT_HALF, dtype=np.float64) * 2.0 / ROT_DIM))
    invf = np.where(rot, inv[lane % ROT_HALF], 0.0).astype(np.float32)[None, :]
    phase = np.where(rot & (lane >= LANES // 2), np.pi / 2, 0.0).astype(np.float32)[None, :]
    row = pl.BlockSpec((1, LANES), lambda i: (0, 0))
    tab = pl.BlockSpec((tm, LANES), lambda i: (i, 0))
    sds = jax.ShapeDtypeStruct((T, LANES), jnp.float32)
    return pl.pallas_call(
        _rope_tab_kernel,
        grid=(T // tm,),
        in_specs=[pl.BlockSpec((tm, 1), lambda i: (i, 0)), row, row],
        out_specs=[tab, tab],
        out_shape=[sds, sds],
        compiler_params=_params(("parallel",)),
        name="rope_tables",
    )(positions.reshape(T, 1), jnp.asarray(invf), jnp.asarray(phase))


def _permute_qk_columns(w):
    D, n = w.shape
    w = w.reshape(D, n // V_DIM, 2, 4, 2, ROT_HALF)
    return w.transpose(0, 1, 4, 3, 2, 5).reshape(D, n)


def _in_proj_kernel(x_ref, g_ref, sc_ref, sh_ref, w_ref, c_ref, s_ref, z_ref):
    h = (_rms(x_ref[...], NORM_EPS) * g_ref[...] * (1.0 + sc_ref[0]) + sh_ref[0]).astype(jnp.bfloat16)
    c, s = c_ref[...], s_ref[...]
    q_scale = HEAD_DIM ** -0.5 * math.log2(math.e)
    cq, sq = c * q_scale, s * q_scale
    for j in range(IN_COLS // COL_TILE):
        cols = slice(j * COL_TILE, (j + 1) * COL_TILE)
        z = jnp.dot(h, w_ref[:, cols], preferred_element_type=jnp.float32)
        if j < COL_V:
            cj, sj = (cq, sq) if j < COL_K else (c, s)
            parts = []
            for g in range(COL_TILE // LANES):
                zg = z[:, g * LANES:(g + 1) * LANES]
                parts.append(zg * cj + pltpu.roll(zg, LANES // 2, 1) * sj)
            z = jnp.concatenate(parts, axis=1)
        elif j >= COL_G:
            z = 0.5 * jnp.tanh(0.5 * z) + 0.5
        z_ref[:, cols] = z.astype(z_ref.dtype)


def _in_proj(x2, g_pre, mod3, w_in_bf, tabs, seq):
    T, D = x2.shape
    tm = min(seq, IN_ROW_TILE)
    per_b = seq // tm
    mod_spec = lambda col: pl.BlockSpec((1, 1, D), lambda i: (i // per_b, 0, col))
    tab = pl.BlockSpec((tm, LANES), lambda i: (i, 0))
    return pl.pallas_call(
        _in_proj_kernel,
        grid=(T // tm,),
        in_specs=[pl.BlockSpec((tm, D), lambda i: (i, 0)),
                  pl.BlockSpec((1, D), lambda i: (0, 0)),
                  mod_spec(1), mod_spec(0),
                  pl.BlockSpec((D, IN_COLS), lambda i: (0, 0)),
                  tab, tab],
        out_specs=pl.BlockSpec((tm, IN_COLS), lambda i: (i, 0)),
        out_shape=jax.ShapeDtypeStruct((T, IN_COLS), jnp.bfloat16),
        compiler_params=_params(("parallel",)),
        name="in_proj",
    )(x2, g_pre, mod3, mod3, w_in_bf, *tabs)


def _attn_kernel(lq1_ref, lk1_ref, lq2_ref, lk2_ref, gs_ref, q_ref, k_ref, v_ref, o_ref, vt_ref, *acc_refs,
                 lam_init):
    i = pl.program_id(2)
    blk = q_ref.shape[0]
    nblk = v_ref.shape[0] // blk
    chains = [(h, comp) for h in range(ATT_HEADS) for comp in range(2)]

    @pl.when(i == 0)
    def _():
        for c in range(nblk):
            for h in range(ATT_HEADS):
                v = v_ref[c * blk:(c + 1) * blk, h * V_DIM:(h + 1) * V_DIM]
                vt_ref[c, h] = v.astype(jnp.float32).T.astype(vt_ref.dtype)

    lam = (jnp.exp(jnp.sum(lq1_ref[...] * lk1_ref[...], axis=-1, keepdims=True))
           - jnp.exp(jnp.sum(lq2_ref[...] * lk2_ref[...], axis=-1, keepdims=True))
           + lam_init)
    lane_comp = lax.broadcasted_iota(jnp.int32, (blk, V_DIM), 1) // ROT_HALF % 2
    qs = []
    for h, comp in chains:
        q = q_ref[:, h * V_DIM:(h + 1) * V_DIM]
        qs.append(jnp.where(lane_comp == comp, q, jnp.zeros_like(q)))
    nt = (((1,), (1,)), ((), ()))
    for acc_ref in acc_refs:
        acc_ref[...] = jnp.zeros_like(acc_ref)

    def step(c, carry, mask=None):
        off = pl.multiple_of(c * blk, blk)
        scores = []
        for n, (h, comp) in enumerate(chains):
            k = k_ref[pl.ds(off, blk), h * V_DIM:(h + 1) * V_DIM]
            scores.append(lax.dot_general(k, qs[n], nt, preferred_element_type=jnp.float32))
        out, probs, alphas = [], [], []
        for n, s in enumerate(scores):
            m, l = carry[n]
            if mask is not None:
                s = jnp.where(mask, s, NEG_BIG)
            m_new = jnp.maximum(m, jnp.max(s, axis=0, keepdims=True))
            alpha = jnp.exp2(m - m_new)
            p = jnp.exp2(s - m_new)
            out.append((m_new, alpha * l + jnp.sum(p, axis=0, keepdims=True)))
            probs.append(p.astype(vt_ref.dtype))
            alphas.append(alpha)
        for n, (h, comp) in enumerate(chains):
            pv = jnp.dot(vt_ref[c, h], probs[n], preferred_element_type=jnp.float32)
            acc_refs[n][...] = alphas[n] * acc_refs[n][...] + pv
        return tuple(out)

    init = tuple((jnp.full((1, blk), NEG_BIG, jnp.float32), jnp.zeros((1, blk), jnp.float32))
                 for _ in chains)
    carry = lax.fori_loop(0, i, step, init)

    key = lax.broadcasted_iota(jnp.int32, (blk, blk), 0)
    qry = lax.broadcasted_iota(jnp.int32, (blk, blk), 1)
    carry = step(i, carry, key <= qry)
    for h in range(ATT_HEADS):
        (_, l1), (_, l2) = carry[2 * h], carry[2 * h + 1]
        ot = acc_refs[2 * h][...] / l1 - lam * (acc_refs[2 * h + 1][...] / l2)
        o = _rms(ot.T, SUBLN_EPS) * gs_ref[...] * (1.0 - lam_init)
        o_ref[:, h * V_DIM:(h + 1) * V_DIM] = o.astype(o_ref.dtype)


def _diff_attention(z, lams, g_subln, lam_init, batch, seq):
    T = z.shape[0]
    blk = min(ATT_BLOCK, seq)
    nq = seq // blk
    width = ATT_HEADS * V_DIM
    vec = pl.BlockSpec((1, HEAD_DIM), lambda b, h, i: (0, 0))
    kcol = DIFF_WIDTH // width
    return pl.pallas_call(
        functools.partial(_attn_kernel, lam_init=lam_init),
        grid=(batch, N_HEADS // ATT_HEADS, nq),
        in_specs=[vec, vec, vec, vec,
                  pl.BlockSpec((1, V_DIM), lambda b, h, i: (0, 0)),
                  pl.BlockSpec((blk, width), lambda b, h, i: (b * nq + i, h)),
                  pl.BlockSpec((seq, width), lambda b, h, i: (b, kcol + h)),
                  pl.BlockSpec((seq, width), lambda b, h, i: (b, 2 * kcol + h))],
        out_specs=pl.BlockSpec((blk, width), lambda b, h, i: (b * nq + i, h)),
        out_shape=jax.ShapeDtypeStruct((T, DIFF_WIDTH), jnp.bfloat16),
        scratch_shapes=[pltpu.VMEM((nq, ATT_HEADS, V_DIM, blk), jnp.bfloat16)]
        + [pltpu.VMEM((V_DIM, blk), jnp.float32)] * (2 * ATT_HEADS),
        compiler_params=_params(("parallel", "parallel", "arbitrary")),
        name="diff_attn",
    )(*lams, g_subln, z, z, z)


def _pool_kernel(u_ref, w_ref, ps_ref, o_ref):
    t = lax.broadcasted_iota(jnp.int32, (u_ref.shape[0], POOL_GROUP_DIM), 0)

    def shifted(x, k):
        return jnp.where(t >= k, pltpu.roll(x, k, 0), 0.0)

    for g, window in enumerate(POOL_WINDOWS):
        cols = slice(g * POOL_GROUP_DIM, (g + 1) * POOL_GROUP_DIM)
        u = u_ref[:, cols].astype(jnp.float32)
        s, k = u, 1
        while k < window:
            s = s + shifted(s, k)
            k *= 2
        d = s / jnp.minimum(t + 1, window).astype(jnp.float32) - u
        y = jnp.dot(d.astype(jnp.bfloat16), w_ref[g].astype(jnp.bfloat16), preferred_element_type=jnp.float32)
        o_ref[:, cols] = (y * ps_ref[:, cols]).astype(o_ref.dtype)


def _pool(z, w_pool, pool_scale, batch, seq):
    T = z.shape[0]
    G = len(POOL_WINDOWS)
    ucol = 3 * DIFF_WIDTH // POOL_WIDTH
    return pl.pallas_call(
        _pool_kernel,
        grid=(batch,),
        in_specs=[pl.BlockSpec((seq, POOL_WIDTH), lambda b: (b, ucol)),
                  pl.BlockSpec((G, POOL_GROUP_DIM, POOL_GROUP_DIM), lambda b: (0, 0, 0)),
                  pl.BlockSpec((1, POOL_WIDTH), lambda b: (0, 0))],
        out_specs=pl.BlockSpec((seq, POOL_WIDTH), lambda b: (b, 0)),
        out_shape=jax.ShapeDtypeStruct((T, POOL_WIDTH), jnp.bfloat16),
        compiler_params=_params(("parallel",)),
        name="pool",
    )(z, w_pool, pool_scale)


def _mix_kernel(x_ref, o_ref, p_ref, ga0, ga1, gb0, gb1, wa_ref, wb_ref, wo_ref, gpm_ref, gt1_ref,
                gpf_ref, sc2_ref, sh2_ref, wr_ref, br_ref,
                x1_ref, h2_ref, ls_ref, tw_ref, cnt_ref):
    tiles = range(x_ref.shape[0] // TOKEN_TILE)
    h2s = [_mix_front(t, x_ref, o_ref, p_ref, ga0, ga1, gb0, gb1, wa_ref, wb_ref, wo_ref, gpm_ref, gt1_ref,
                      gpf_ref, sc2_ref, sh2_ref, x1_ref) for t in tiles]
    for t in tiles:
        h2_ref[t * TOKEN_TILE:(t + 1) * TOKEN_TILE, :] = h2s[t].astype(h2_ref.dtype)
    for t in tiles:
        _mix_route(t, h2s[t], wr_ref, br_ref, ls_ref, tw_ref, cnt_ref)


def _mix_front(t, x_ref, o_ref, p_ref, ga0, ga1, gb0, gb1, wa_ref, wb_ref, wo_ref, gpm_ref, gt1_ref,
               gpf_ref, sc2_ref, sh2_ref, x1_ref):
    rows = slice(t * TOKEN_TILE, (t + 1) * TOKEN_TILE)
    ya = jnp.dot(o_ref[rows, :], wa_ref[...], preferred_element_type=jnp.float32)
    yb = jnp.dot(p_ref[rows, :], wb_ref[...], preferred_element_type=jnp.float32)
    ga = jnp.concatenate([ga0[rows, :], ga1[rows, :]], axis=1).astype(jnp.float32)
    gb = jnp.concatenate([gb0[rows, :], gb1[rows, :]], axis=1).astype(jnp.float32)
    merged = (ga * ya + gb * yb).astype(jnp.bfloat16)
    mixed = jnp.dot(merged, wo_ref[...], preferred_element_type=jnp.float32)
    x1 = x_ref[rows, :] + gt1_ref[0] * (_rms(mixed, NORM_EPS) * gpm_ref[...])
    x1_ref[rows, :] = x1
    return _rms(x1, NORM_EPS) * gpf_ref[...] * (1.0 + sc2_ref[0]) + sh2_ref[0]


def _mix_route(t, h2, wr_ref, br_ref, ls_ref, tw_ref, cnt_ref):
    tm = TOKEN_TILE
    rows = slice(t * tm, (t + 1) * tm)
    logits =lax.dot_general(wr_ref[...], h2, (((1,), (1,)), ((), ())), precision=_HI,
                             preferred_element_type=jnp.float32) + br_ref[...]
    eid = lax.broadcasted_iota(jnp.int32, logits.shape, 0)
    work = logits
    sels, vals = [], []
    for _ in range(TOP_K):
        mx = jnp.max(work, axis=0, keepdims=True)
        idx = jnp.min(jnp.where(work == mx, eid, N_EXPERTS), axis=0, keepdims=True)
        sel = eid == idx
        work = jnp.where(sel, -jnp.inf, work)
        sels.append(sel)
        vals.append(mx)
    ex = [jnp.exp(v - vals[0]) for v in vals]
    den = ex[0] + ex[1] + ex[2] + ex[3]
    onehot = jnp.zeros(logits.shape, jnp.float32)
    for sel in sels:
        onehot = jnp.where(sel, 1.0, onehot)
    r = lax.broadcasted_iota(jnp.int32, (tm, tm), 0)
    c = lax.broadcasted_iota(jnp.int32, (tm, tm), 1)
    tri = jnp.where(r < c, 1.0, 0.0).astype(jnp.bfloat16)
    rank = jnp.dot(onehot.astype(jnp.bfloat16), tri, preferred_element_type=jnp.float32)
    counts = jnp.broadcast_to(jnp.sum(onehot, axis=1, keepdims=True), (N_EXPERTS, LANES))
    rounded = jnp.floor((counts + (ROW_ALIGN - 1)) * (1.0 / ROW_ALIGN)) * ROW_ALIGN
    er = lax.broadcasted_iota(jnp.int32, (N_EXPERTS, N_EXPERTS), 0)
    ec = lax.broadcasted_iota(jnp.int32, (N_EXPERTS, N_EXPERTS), 1)
    below = jnp.where(ec < er, 1.0, 0.0).astype(jnp.bfloat16)
    offset = jnp.dot(below, rounded.astype(jnp.bfloat16), preferred_element_type=jnp.float32)
    slot_of = rank + offset[:, 0:1]
    cnt_ref[t] = rounded
    for kk in range(TOP_K):
        ls = jnp.sum(jnp.where(sels[kk], slot_of, 0.0), axis=0, keepdims=True).astype(jnp.int32)
        ls_ref[kk:kk + 1, rows] = ls
        tw_ref[kk:kk + 1, rows] = ex[kk] / den


def _mix(x2, o, p, z, wa, wb, wo, g_post_mix, mod3, g_pre_ffn, w_router_t, b_router, seq):
    T, D = x2.shape
    sub = min(MIX_TILES, seq // TOKEN_TILE)
    tm = sub * TOKEN_TILE
    per_b = seq // tm
    E = N_EXPERTS
    row = lambda n: pl.BlockSpec((1, n), lambda i: (0, 0))
    mod_spec = lambda col: pl.BlockSpec((1, 1, D), lambda i: (i // per_b, 0, col))
    gate = lambda cb: pl.BlockSpec((tm, COL_TILE), lambda i: (i, cb))
    full = lambda a: pl.BlockSpec(a.shape, lambda i: (0,) * a.ndim)
    tile = pl.BlockSpec((tm, D), lambda i: (i, 0))
    small = pl.BlockSpec((TOP_K, tm), lambda i: (0, i))
    return pl.pallas_call(
        _mix_kernel,
        grid=(T // tm,),
        in_specs=[tile, tile,
                  pl.BlockSpec((tm, POOL_WIDTH), lambda i: (i, 0)),
                  gate(COL_G), gate(COL_G + 1), gate(COL_G + 2), gate(COL_G + 3),
                  full(wa), full(wb), full(wo), row(D), mod_spec(2),
                  row(D), mod_spec(4), mod_spec(3), full(w_router_t),
                  pl.BlockSpec((E, 1), lambda i: (0, 0))],
        out_specs=[tile, tile, small, small,
                   pl.BlockSpec((sub, E, LANES), lambda i: (i, 0, 0))],
        out_shape=[jax.ShapeDtypeStruct((T, D), jnp.float32),
                   jax.ShapeDtypeStruct((T, D), jnp.bfloat16),
                   jax.ShapeDtypeStruct((TOP_K, T), jnp.int32),
                   jax.ShapeDtypeStruct((TOP_K, T), jnp.float32),
                   jax.ShapeDtypeStruct((T // TOKEN_TILE, E, LANES), jnp.float32)],
        compiler_params=_params(("parallel",)),
        name="mix_tail",
    )(x2, o, p, z, z, z, z, wa, wb, wo, g_post_mix, mod3, g_pre_ffn, mod3, mod3, w_router_t,
      b_router.reshape(E, 1))


def _chunk(ref, row):
    return ref.at[pl.ds(pl.multiple_of(row, ROW_ALIGN), ROW_ALIGN), :]


def _dispatch_kernel(chunk_slot_ref, n_chunk_ref, zero_off_ref, zero_len_ref, ntile_ref,
                     h_ref, ls_ref, xs_ref, xl_ref, zero_ref, sem, zsem, *, zero_per_step):
    i = pl.program_id(0)
    last = pl.num_programs(0) - 1

    @pl.when(i == 0)
    def _():
        zero_ref[...] = jnp.zeros_like(zero_ref)

    slot_id = lax.broadcasted_iota(jnp.int32, (LOCAL_ROWS, h_ref.shape[0]), 0)
    place = jnp.zeros(slot_id.shape, jnp.float32)
    for kk in range(TOP_K):
        place = jnp.where(slot_id == ls_ref[kk:kk + 1, :], 1.0, place)
    xl = jnp.dot(place.astype(jnp.bfloat16), h_ref[...], preferred_element_type=jnp.float32)
    xl_ref[i % 2] = xl.astype(xl_ref.dtype)

    def rows_out(tile, act):
        buf = tile % 2

        def body(k, _):
            act(pltpu.make_async_copy(_chunk(xl_ref.at[buf], k * ROW_ALIGN),
                                      _chunk(xs_ref, chunk_slot_ref[tile * BLOCK_CHUNKS + k]), sem.at[buf]))
            return 0
        lax.fori_loop(0, n_chunk_ref[tile], body, 0)

    def copies(act):
        for r in range(zero_per_step):
            idx = i * zero_per_step + r

            def zbody(k, _, idx=idx):
                act(pltpu.make_async_copy(_chunk(zero_ref, 0),
                                          _chunk(xs_ref, zero_off_ref[idx] + k * ROW_ALIGN), zsem))
                return 0
            lax.fori_loop(0, zero_len_ref[idx] // ROW_ALIGN, zbody, 0)

    def tails(act):
        def body(t, _):
            off = pl.multiple_of(t * MOE_TILE, MOE_TILE)
            act(pltpu.make_async_copy(zero_ref, xs_ref.at[pl.ds(off, MOE_TILE), :], zsem))
            return 0
        lax.fori_loop(ntile_ref[0], xs_ref.shape[0] // MOE_TILE, body, 0)

    start, wait = (lambda cp: cp.start()), (lambda cp: cp.wait())
    rows_out(i, start)
    copies(start)
    pl.when(i == 0)(functools.partial(tails, start))
    pl.when(i > 0)(lambda: rows_out(i - 1, wait))
    copies(wait)
    pl.when(i == 0)(functools.partial(tails, wait))
    pl.when(i == last)(lambda: rows_out(i, wait))


def _dispatch(chunk_slot, n_chunks, zero_off, zero_len, n_tiles, h2, ls, n_slots):
    T, W = h2.shape
    tc = TOKEN_TILE
    steps = n_chunks.shape[0]
    return pl.pallas_call(
        functools.partial(_dispatch_kernel, zero_per_step=zero_len.shape[0] // steps),
        grid_spec=pltpu.PrefetchScalarGridSpec(
            num_scalar_prefetch=5,
            grid=(steps,),
            in_specs=[pl.BlockSpec((tc, W), lambda i, *_: (i, 0)),
                      pl.BlockSpec((TOP_K, tc), lambda i, *_: (0, i))],
            out_specs=pl.BlockSpec(memory_space=pl.ANY),
            scratch_shapes=[pltpu.VMEM((2, LOCAL_ROWS, W), h2.dtype),
                            pltpu.VMEM((MOE_TILE, W), h2.dtype),
                            pltpu.SemaphoreType.DMA((2,)), pltpu.SemaphoreType.DMA]),
        out_shape=jax.ShapeDtypeStruct((n_slots, W), h2.dtype),
        compiler_params=_params(("arbitrary",)),
        name="dispatch",
    )(chunk_slot, n_chunks, zero_off, zero_len, n_tiles, h2, ls)


def _moe_kernel(te_ref, nt_ref, buf_ref, nxt_ref, rows_ref, x_ref, w1_hbm, b1_ref, w2_hbm, b2_ref, perm_ref, y_ref,
                w1raw_ref, w2raw_ref, w1s_ref, w2s_ref, sem):
    j = pl.program_id(0)
    prev = te_ref[jnp.maximum(j - 1, 0)]
    fresh = (j == 0) | (te_ref[j] != prev)
    nblk = w1s_ref.shape[1] // MXU_DIM

    def weight_copies(e, b):
        return (pltpu.make_async_copy(w1_hbm.at[e], w1raw_ref.at[b], sem.at[0, b]),
                pltpu.make_async_copy(w2_hbm.at[e], w2raw_ref.at[b], sem.at[1, b]))

    @pl.when(fresh & (j < nt_ref[0]))
    def _():
        e, b, nxt = te_ref[j], buf_ref[j], nxt_ref[j]

        @pl.when(j == 0)
        def _():
            for cp in weight_copies(e, b):
                cp.start()

        for cp in weight_copies(e, b):
            cp.wait()

        @pl.when(nxt >= 0)
        def _():
            for cp in weight_copies(nxt, 1 - b):
                cp.start()

        for c in range(nblk):
            cols = slice(c * MXU_DIM, (c + 1) * MXU_DIM)
            blk = w1raw_ref[b, :, cols].astype(jnp.bfloat16)
            w1s_ref[:, cols] = jnp.dot(blk, perm_ref[...],
                                       preferred_element_type=jnp.float32).astype(jnp.bfloat16)
        w2s_ref[...] = w2raw_ref[b].astype(jnp.bfloat16)

    def experts_mlp(rows):
        z = jnp.dot(x_ref[:rows, :], w1s_ref[...], preferred_element_type=jnp.float32) + b1_ref[0]
        acts = []
        for c in range(nblk):
            gate = jnp.minimum(z[:, c * MXU_DIM:c * MXU_DIM + LANES], SWIGLU_LIMIT)
            up = jnp.clip(z[:, c * MXU_DIM + LANES:(c + 1) * MXU_DIM], -SWIGLU_LIMIT, SWIGLU_LIMIT)
            acts.append(gate * jax.nn.sigmoid(SWIGLU_ALPHA * gate) * (up + 1.0))
        a = jnp.concatenate(acts, axis=1).astype(jnp.bfloat16)
        y = jnp.dot(a, w2s_ref[...], preferred_element_type=jnp.float32) + b2_ref[0]
        y_ref[:rows, :] = y.astype(y_ref.dtype)
        if rows < y_ref.shape[0]:
            y_ref[rows:, :] = jnp.zeros((y_ref.shape[0] - rows, y_ref.shape[1]), y_ref.dtype)

    half = y_ref.shape[0] // 2
    active = j < nt_ref[0]
    pl.when(active & (rows_ref[j] > half))(lambda: experts_mlp(2 * half))
    pl.when(active & (rows_ref[j] <= half))(lambda: experts_mlp(half))

    @pl.when(j >= nt_ref[0])
    def _():
        y_ref[...] = jnp.zeros_like(y_ref)


def _regroup_perm():
    src = np.arange(MXU_DIM)
    dst = np.where(src % 2 == 0, src // 2, LANES + src // 2)
    perm = np.zeros((MXU_DIM, MXU_DIM), np.float32)
    perm[src, dst] = 1.0
    return jnp.asarray(perm, jnp.bfloat16)


def _moe(tile_expert, n_tiles, tile_buf, tile_next, tile_rows, xs, w1, b1g, w2, b2):
    n_slots, W = xs.shape
    tm = MOE_TILE
    E, D, F2 = w1.shape
    F = w2.shape[1]
    xmap = lambda j, te, nt, *_: (jnp.minimum(j, nt[0] - 1), 0)
    emap = lambda j, te, *_: (te[j], 0, 0)
    return pl.pallas_call(
        _moe_kernel,
        grid_spec=pltpu.PrefetchScalarGridSpec(
            num_scalar_prefetch=5,
            grid=(n_slots // tm,),
            in_specs=[pl.BlockSpec((tm, W), xmap),
                      pl.BlockSpec(memory_space=pl.ANY),
                      pl.BlockSpec((1, 1, F2), emap),
                      pl.BlockSpec(memory_space=pl.ANY),
                      pl.BlockSpec((1, 1, D), emap),
                      pl.BlockSpec((MXU_DIM, MXU_DIM), lambda j, *_: (0, 0))],
            out_specs=pl.BlockSpec((tm, W), lambda j, *_: (j, 0)),
            scratch_shapes=[pltpu.VMEM((2, D, F2), w1.dtype), pltpu.VMEM((2, F, D), w2.dtype),
                            pltpu.VMEM((D, F2), jnp.bfloat16), pltpu.VMEM((F, D), jnp.bfloat16),
                            pltpu.SemaphoreType.DMA((2, 2))]),
        out_shape=jax.ShapeDtypeStruct((n_slots, W), xs.dtype),
        compiler_params=_params(("arbitrary",)),
        name="moe_experts",
    )(tile_expert, n_tiles, tile_buf, tile_next, tile_rows, xs, w1, b1g, w2, b2.reshape(E, 1, D),
      _regroup_perm())


def _combine_kernel(chunk_slot_ref, n_chunk_ref, ys_ref, ls_ref, w_ref, x1_ref, gt2_ref, g_ref,
                    o_ref, yl_ref, sem):
    i = pl.program_id(0)
    tc = x1_ref.shape[0]

    def pieces(tile, act):
        buf = tile % 2

        def body(k, _):
            act(pltpu.make_async_copy(_chunk(ys_ref, chunk_slot_ref[tile * BLOCK_CHUNKS + k]),
                                      _chunk(yl_ref.at[buf], k * ROW_ALIGN), sem.at[buf]))
            return 0
        lax.fori_loop(0, n_chunk_ref[tile], body, 0)

    def fetch(tile):
        yl_ref[tile % 2, TOP_K * tc:, :] = jnp.zeros((LOCAL_ROWS - TOP_K * tc, yl_ref.shape[2]), yl_ref.dtype)
        pieces(tile, lambda cp: cp.start())

    pl.when(i == 0)(lambda: fetch(i))
    pl.when(i + 1 < pl.num_programs(0))(lambda: fetch(i + 1))
    slot_id = lax.broadcasted_iota(jnp.int32, (tc, LOCAL_ROWS), 1)
    ls = ls_ref[...]
    w = w_ref[...]
    mix = jnp.zeros((tc, LOCAL_ROWS), jnp.float32)
    for k in range(TOP_K):
        mix = jnp.where(slot_id == ls[:, k:k + 1], w[:, k:k + 1], mix)
    pieces(i, lambda cp: cp.wait())
    f = jnp.dot(mix.astype(jnp.bfloat16), yl_ref[i % 2], preferred_element_type=jnp.float32)
    o_ref[...] = x1_ref[...] + gt2_ref[0] * (_rms(f, NORM_EPS) * g_ref[...])


def _combine(chunk_slot, n_chunks, ys, ls_t, topw_t, x1, mod3, g_post_ffn, seq):
    T, D = x1.shape
    tc = TOKEN_TILE
    per_b = seq // tc
    tile = pl.BlockSpec((tc, D), lambda i, *_: (i, 0))
    small = pl.BlockSpec((tc, TOP_K), lambda i, *_: (i, 0))
    return pl.pallas_call(
        _combine_kernel,
        grid_spec=pltpu.PrefetchScalarGridSpec(
            num_scalar_prefetch=2,
            grid=(T // tc,),
            in_specs=[pl.BlockSpec(memory_space=pl.ANY), small, small, tile,
                      pl.BlockSpec((1, 1, D), lambda i, *_: (i // per_b, 0, 5)),
                      pl.BlockSpec((1, D), lambda i, *_: (0, 0))],
            out_specs=tile,
            scratch_shapes=[pltpu.VMEM((2, LOCAL_ROWS, ys.shape[1]), ys.dtype),
                            pltpu.SemaphoreType.DMA((2,))]),
        out_shape=jax.ShapeDtypeStruct((T, D), jnp.float32),
        compiler_params=_params(("arbitrary",)),
        name="combine",
    )(chunk_slot, n_chunks, ys, ls_t, topw_t, x1, mod3, g_post_ffn)


def kernel(x, c, positions, w_ada, b_ada, g_pre_mix, w_in, lambda_q1, lambda_k1, lambda_q2, lambda_k2,
           g_subln, w_pool, pool_scale, w_proj_a, w_proj_b, w_out, g_post_mix, g_pre_ffn,
           w_router, b_router, w_exp1, b_exp1, w_exp2, b_exp2, g_post_ffn):
    B, S, D = x.shape
    T = B * S
    bf = jnp.bfloat16
    tabs = _rope_tables(positions)
    for l in range(w_ada.shape[0]):
        x2 = x.reshape(T, D)
        mod3 = _ada(c, w_ada[l], b_ada[l]).reshape(B, 1, N_MOD * D)
        n_qk = 2 * DIFF_WIDTH
        w_in_bf = jnp.concatenate([_permute_qk_columns(w_in[l][:, :n_qk]), w_in[l][:, n_qk:]], axis=1).astype(bf)
        z = _in_proj(x2, g_pre_mix[l][None], mod3, w_in_bf, tabs, S)
        lams = [v[l][None] for v in (lambda_q1, lambda_k1, lambda_q2, lambda_k2)]
        lam_init = 0.8 - 0.6 * math.exp(-0.3 * l)
        o = _diff_attention(z, lams, g_subln[l][None], lam_init, B, S)
        p = _pool(z, w_pool[l], pool_scale[l][None], B, S)
        x1, h2, ls, topw, cnt = _mix(
            x2, o, p, z, w_proj_a[l].astype(bf), w_proj_b[l].astype(bf), w_out[l].astype(bf),
            g_post_mix[l][None], mod3, g_pre_ffn[l][None], w_router[l].T, b_router[l], S)

        i32 = jnp.int32
        n = cnt[:, :, 0].astype(i32)
        n_tok_tiles = n.shape[0]
        counts = jnp.sum(n, axis=0)
        padded = (counts + MOE_TILE - 1) // MOE_TILE * MOE_TILE
        gend = jnp.cumsum(padded).astype(i32)
        gstart = gend - padded
        local_end = jnp.cumsum(n, axis=1)
        slot_off = gstart[None, :] + jnp.cumsum(n, axis=0) - n
        chunk_row = jnp.arange(BLOCK_CHUNKS, dtype=i32) * ROW_ALIGN
        piece = jnp.sum(local_end[:, None, :] <= chunk_row[None, :, None], axis=2)
        in_piece = piece[:, :, None] == jnp.arange(N_EXPERTS, dtype=i32)[None, None, :]
        shift = jnp.sum(jnp.where(in_piece, (slot_off - (local_end - n))[:, None, :], 0), axis=2)
        chunk_slot = (shift + chunk_row[None, :]).reshape(-1).astype(i32)
        n_chunks = (local_end[:, -1] // ROW_ALIGN).astype(i32)
        steps_pad = -N_EXPERTS % n_tok_tiles
        group_pad_off = jnp.pad(gstart + counts, (0, steps_pad)).astype(i32)
        group_pad_len = jnp.pad(padded - counts, (0, steps_pad)).astype(i32)
        n_slots = -(-(T * TOP_K + n_tok_tiles * N_EXPERTS * ROW_ALIGN) // MOE_TILE) * MOE_TILE + N_EXPERTS * MOE_TILE
        tile_start = jnp.arange(n_slots // MOE_TILE, dtype=i32) * MOE_TILE
        tile_expert = jnp.minimum(jnp.sum(tile_start[:, None] >= gend[None, :], axis=1), N_EXPERTS - 1)
        n_tiles = gend[-1:] // MOE_TILE
        has_tiles = padded > 0
        expert_ids = jnp.arange(N_EXPERTS, dtype=i32)
        later = lax.cummin(jnp.where(has_tiles, expert_ids, N_EXPERTS), reverse=True)
        next_expert = jnp.concatenate([later[1:], jnp.full((1,), N_EXPERTS, i32)])
        next_expert = jnp.where(next_expert < N_EXPERTS, next_expert, -1)
        of_tile = tile_expert[:, None] == expert_ids[None, :]
        pick = lambda per_expert: jnp.sum(jnp.where(of_tile, per_expert[None, :], 0), axis=1).astype(i32)
        tile_buf = pick((jnp.cumsum(has_tiles.astype(i32)) - 1) % 2)
        tile_next = pick(next_expert)
        tile_rows = jnp.clip(pick(gstart + counts) - tile_start, 0, MOE_TILE).astype(i32)

        xs = _dispatch(chunk_slot, n_chunks, group_pad_off, group_pad_len, n_tiles, h2, ls, n_slots)
        half = D_FF
        b1g = b_exp1[l].reshape(N_EXPERTS, half // LANES, LANES, 2).transpose(0, 1, 3, 2)
        b1g = b1g.reshape(N_EXPERTS, 1, 2 * half)
        ys = _moe(tile_expert.astype(i32), n_tiles, tile_buf, tile_next, tile_rows, xs,
                  w_exp1[l], b1g, w_exp2[l], b_exp2[l])
        x = _combine(chunk_slot, n_chunks, ys, ls.T, topw.T, x1, mod3, g_post_ffn[l][None], S)
        x = x.reshape(B, S, D)
    return x
```

```python
import functools
import math

import numpy as np
import jax
import jax.numpy as jnp
from jax import lax
from jax.experimental import pallas as pl
from jax.experimental.pallas import tpu as pltpu

D_MODEL = 1024
N_HEADS = 8
HEAD_DIM = 64
V_DIM = 2 * HEAD_DIM
DIFF_WIDTH = N_HEADS * V_DIM
POOL_WINDOWS = (2, 4, 8, 16)
POOL_GROUP_DIM = 128
POOL_WIDTH = len(POOL_WINDOWS) * POOL_GROUP_DIM
IN_COLS = 3 * DIFF_WIDTH + POOL_WIDTH + 2 * D_MODEL
ROPE_THETA = 500000.0
ROT_DIM = HEAD_DIM // 4
ROT_HALF = ROT_DIM // 2
N_EXPERTS = 32
TOP_K = 4
D_FF = D_MODEL
SWIGLU_ALPHA = 1.702
SWIGLU_LIMIT = 7.0
NORM_EPS = 1e-6
SUBLN_EPS = 1e-5
N_MOD = 6
NEG_BIG = -1e30

LANES = 128
MXU_DIM = 256
VMEM_LIMIT = 56 * 1024 * 1024

ADA_COL_TILE = 1536
ROPE_ROW_TILE = 2048
COL_TILE = 512
IN_ROW_TILE = 512
COL_K = DIFF_WIDTH // COL_TILE
COL_V = 2 * DIFF_WIDTH // COL_TILE
COL_G = (3 * DIFF_WIDTH + POOL_WIDTH) // COL_TILE
ATT_BLOCK = 512
ATT_HEADS = 4
MOE_TILE = 512
MOE_ROW_STEPS = 4
TOKEN_TILE = 256
MIX_TILES = 2
ROW_ALIGN = 16
LOCAL_ROWS = TOP_K * TOKEN_TILE + N_EXPERTS * ROW_ALIGN
BLOCK_CHUNKS = LOCAL_ROWS // ROW_ALIGN

_HI = lax.Precision.HIGHEST


def _params(sem, vmem=VMEM_LIMIT):
    return pltpu.CompilerParams(dimension_semantics=sem, vmem_limit_bytes=vmem)


def _rms(x, eps):
    return x * lax.rsqrt(jnp.mean(x * x, axis=-1, keepdims=True) + eps)


def _ada_kernel(c_ref, w_ref, b_ref, o_ref):
    c = c_ref[...]
    s = c * jax.nn.sigmoid(c)
    o_ref[...] = jnp.dot(s, w_ref[...], precision=_HI, preferred_element_type=jnp.float32) + b_ref[...]


def _ada(c, w_ada, b_ada):
    B, D = c.shape
    N = w_ada.shape[1]
    tn = ADA_COL_TILE
    return pl.pallas_call(
        _ada_kernel,
        grid=(N // tn,),
        in_specs=[pl.BlockSpec((B, D), lambda j: (0, 0)),
                  pl.BlockSpec((D, tn), lambda j: (0, j)),
                  pl.BlockSpec((1, tn), lambda j: (0, j))],
        out_specs=pl.BlockSpec((B, tn), lambda j: (0, j)),
        out_shape=jax.ShapeDtypeStruct((B, N), jnp.float32),
        compiler_params=_params(("parallel",)),
        name="ada",
    )(c, w_ada, b_ada.reshape(1, N))


def _rope_tab_kernel(pos_ref, invf_ref, phase_ref, c_ref, s_ref):
    pos = pos_ref[...].astype(jnp.float32)
    cs = jnp.cos(pos * invf_ref[...] - phase_ref[...])
    rolled = pltpu.roll(cs, LANES // 2, 1)
    lane = lax.broadcasted_iota(jnp.int32, cs.shape, 1)
    lower = lane < LANES // 2
    rot = lane % (LANES // 2) < ROT_DIM
    c_ref[...] = jnp.where(lower, cs, rolled)
    s_ref[...] = jnp.where(rot, jnp.where(lower, -rolled, cs), 0.0)


def _rope_tables(positions):
    T = positions.size
    tm = min(T, ROPE_ROW_TILE)
    lane = np.arange(LANES)
    rot = lane % (LANES // 2) < ROT_DIM
    inv = ROPE_THETA ** (-(np.arange(ROT_HALF, dtype=np.float64) * 2.0 / ROT_DIM))
    invf = np.where(rot, inv[lane % ROT_HALF], 0.0).astype(np.float32)[None, :]
    phase = np.where(rot & (lane >= LANES // 2), np.pi / 2, 0.0).astype(np.float32)[None, :]
    row = pl.BlockSpec((1, LANES), lambda i: (0, 0))
    tab = pl.BlockSpec((tm, LANES), lambda i: (i, 0))
    sds = jax.ShapeDtypeStruct((T, LANES), jnp.float32)
    return pl.pallas_call(
        _rope_tab_kernel,
        grid=(T // tm,),
        in_specs=[pl.BlockSpec((tm, 1), lambda i: (i, 0)), row, row],
        out_specs=[tab, tab],
        out_shape=[sds, sds],
        compiler_params=_params(("parallel",)),
        name="rope_tables",
    )(positions.reshape(T, 1), jnp.asarray(invf), jnp.asarray(phase))


def _permute_qk_columns(w):
    D, n = w.shape
    w = w.reshape(D, n // V_DIM, 2, 4, 2, ROT_HALF)
    return w.transpose(0, 1, 4, 3, 2, 5).reshape(D, n)


def _in_proj_kernel(x_ref, g_ref, sc_ref, sh_ref, w_ref, c_ref, s_ref, z_ref):
    h = (_rms(x_ref[...], NORM_EPS) * g_ref[...] * (1.0 + sc_ref[0]) + sh_ref[0]).astype(jnp.bfloat16)
    c, s = c_ref[...], s_ref[...]
    q_scale = HEAD_DIM ** -0.5 * math.log2(math.e)
    cq, sq = c * q_scale, s * q_scale
    for j in range(IN_COLS // COL_TILE):
        cols = slice(j * COL_TILE, (j + 1) * COL_TILE)
        z = jnp.dot(h, w_ref[:, cols], preferred_element_type=jnp.float32)
        if j < COL_V:
            cj, sj = (cq, sq) if j < COL_K else (c, s)
            parts = []
            for g in range(COL_TILE // LANES):
                zg = z[:, g * LANES:(g + 1) * LANES]
                parts.append(zg * cj + pltpu.roll(zg, LANES // 2, 1) * sj)
            z = jnp.concatenate(parts, axis=1)
        elif j >= COL_G:
            z = 0.5 * jnp.tanh(0.5 * z) + 0.5
        z_ref[:, cols] = z.astype(z_ref.dtype)


def _in_proj(x2, g_pre, mod3, w_in_bf, tabs, seq):
    T, D = x2.shape
    tm = min(seq, IN_ROW_TILE)
    per_b = seq // tm
    mod_spec = lambda col: pl.BlockSpec((1, 1, D), lambda i: (i // per_b, 0, col))
    tab = pl.BlockSpec((tm, LANES), lambda i: (i, 0))
    return pl.pallas_call(
        _in_proj_kernel,
        grid=(T // tm,),
        in_specs=[pl.BlockSpec((tm, D), lambda i: (i, 0)),
                  pl.BlockSpec((1, D), lambda i: (0, 0)),
                  mod_spec(1), mod_spec(0),
                  pl.BlockSpec((D, IN_COLS), lambda i: (0, 0)),
                  tab, tab],
        out_specs=pl.BlockSpec((tm, IN_COLS), lambda i: (i, 0)),
        out_shape=jax.ShapeDtypeStruct((T, IN_COLS), jnp.bfloat16),
        compiler_params=_params(("parallel",)),
        name="in_proj",
    )(x2, g_pre, mod3, mod3, w_in_bf, *tabs)


def _attn_kernel(lq1_ref, lk1_ref, lq2_ref, lk2_ref, gs_ref, q_ref, k_ref, v_ref, o_ref, vt_ref, *acc_refs,
                 lam_init):
    i = pl.program_id(2)
    blk = q_ref.shape[0]
    nblk = v_ref.shape[0] // blk
    chains = [(h, comp) for h in range(ATT_HEADS) for comp in range(2)]

    @pl.when(i == 0)
    def _():
        for c in range(nblk):
            for h in range(ATT_HEADS):
                v = v_ref[c * blk:(c + 1) * blk, h * V_DIM:(h + 1) * V_DIM]
                vt_ref[c, h] = v.astype(jnp.float32).T.astype(vt_ref.dtype)

    lam = (jnp.exp(jnp.sum(lq1_ref[...] * lk1_ref[...], axis=-1, keepdims=True))
           - jnp.exp(jnp.sum(lq2_ref[...] * lk2_ref[...], axis=-1, keepdims=True))
           + lam_init)
    lane_comp = lax.broadcasted_iota(jnp.int32, (blk, V_DIM), 1) // ROT_HALF % 2
    qs = []
    for h, comp in chains:
        q = q_ref[:, h * V_DIM:(h + 1) * V_DIM]
        qs.append(jnp.where(lane_comp == comp, q, jnp.zeros_like(q)))
    nt = (((1,), (1,)), ((), ()))
    for acc_ref in acc_refs:
        acc_ref[...] = jnp.zeros_like(acc_ref)

    def step(c, carry, mask=None):
        off = pl.multiple_of(c * blk, blk)
        scores = []
        for n, (h, comp) in enumerate(chains):
            k = k_ref[pl.ds(off, blk), h * V_DIM:(h + 1) * V_DIM]
            scores.append(lax.dot_general(k, qs[n], nt, preferred_element_type=jnp.float32))
        out, probs, alphas = [], [], []
        for n, s in enumerate(scores):
            m, l = carry[n]
            if mask is not None:
                s = jnp.where(mask, s, NEG_BIG)
            m_new = jnp.maximum(m, jnp.max(s, axis=0, keepdims=True))
            alpha = jnp.exp2(m - m_new)
            p = jnp.exp2(s - m_new)
            out.append((m_new, alpha * l + jnp.sum(p, axis=0, keepdims=True)))
            probs.append(p.astype(vt_ref.dtype))
            alphas.append(alpha)
        for n, (h, comp) in enumerate(chains):
            pv = jnp.dot(vt_ref[c, h], probs[n], preferred_element_type=jnp.float32)
            acc_refs[n][...] = alphas[n] * acc_refs[n][...] + pv
        return tuple(out)

    init = tuple((jnp.full((1, blk), NEG_BIG, jnp.float32), jnp.zeros((1, blk), jnp.float32))
                 for _ in chains)
    carry = lax.fori_loop(0, i, step, init)

    key = lax.broadcasted_iota(jnp.int32, (blk, blk), 0)
    qry = lax.broadcasted_iota(jnp.int32, (blk, blk), 1)
    carry = step(i, carry, key <= qry)
    for h in range(ATT_HEADS):
        (_, l1), (_, l2) = carry[2 * h], carry[2 * h + 1]
        ot = acc_refs[2 * h][...] / l1 - lam * (acc_refs[2 * h + 1][...] / l2)
        o = _rms(ot.T, SUBLN_EPS) * gs_ref[...] * (1.0 - lam_init)
        o_ref[:, h * V_DIM:(h + 1) * V_DIM] = o.astype(o_ref.dtype)


def _diff_attention(z, lams, g_subln, lam_init, batch, seq):
    T = z.shape[0]
    blk = min(ATT_BLOCK, seq)
    nq = seq // blk
    width = ATT_HEADS * V_DIM
    vec = pl.BlockSpec((1, HEAD_DIM), lambda b, h, i: (0, 0))
    kcol = DIFF_WIDTH // width
    return pl.pallas_call(
        functools.partial(_attn_kernel, lam_init=lam_init),
        grid=(batch, N_HEADS // ATT_HEADS, nq),
        in_specs=[vec, vec, vec, vec,
                  pl.BlockSpec((1, V_DIM), lambda b, h, i: (0, 0)),
                  pl.BlockSpec((blk, width), lambda b, h, i: (b * nq + i, h)),
                  pl.BlockSpec((seq, width), lambda b, h, i: (b, kcol + h)),
                  pl.BlockSpec((seq, width), lambda b, h, i: (b, 2 * kcol + h))],
        out_specs=pl.BlockSpec((blk, width), lambda b, h, i: (b * nq + i, h)),
        out_shape=jax.ShapeDtypeStruct((T, DIFF_WIDTH), jnp.bfloat16),
        scratch_shapes=[pltpu.VMEM((nq, ATT_HEADS, V_DIM, blk), jnp.bfloat16)]
        + [pltpu.VMEM((V_DIM, blk), jnp.float32)] * (2 * ATT_HEADS),
        compiler_params=_params(("parallel", "parallel", "arbitrary")),
        name="diff_attn",
    )(*lams, g_subln, z, z, z)


def _pool_kernel(u_ref, w_ref, ps_ref, o_ref):
    t = lax.broadcasted_iota(jnp.int32, (u_ref.shape[0], POOL_GROUP_DIM), 0)

    def shifted(x, k):
        return jnp.where(t >= k, pltpu.roll(x, k, 0), 0.0)

    for g, window in enumerate(POOL_WINDOWS):
        cols = slice(g * POOL_GROUP_DIM, (g + 1) * POOL_GROUP_DIM)
        u = u_ref[:, cols].astype(jnp.float32)
        s, k = u, 1
        while k < window:
            s = s + shifted(s, k)
            k *= 2
        d = s / jnp.minimum(t + 1, window).astype(jnp.float32) - u
        y = jnp.dot(d.astype(jnp.bfloat16), w_ref[g].astype(jnp.bfloat16), preferred_element_type=jnp.float32)
        o_ref[:, cols] = (y * ps_ref[:, cols]).astype(o_ref.dtype)


def _pool(z, w_pool, pool_scale, batch, seq):
    T = z.shape[0]
    G = len(POOL_WINDOWS)
    ucol = 3 * DIFF_WIDTH // POOL_WIDTH
    return pl.pallas_call(
        _pool_kernel,
        grid=(batch,),
        in_specs=[pl.BlockSpec((seq, POOL_WIDTH), lambda b: (b, ucol)),
                  pl.BlockSpec((G, POOL_GROUP_DIM, POOL_GROUP_DIM), lambda b: (0, 0, 0)),
                  pl.BlockSpec((1, POOL_WIDTH), lambda b: (0, 0))],
        out_specs=pl.BlockSpec((seq, POOL_WIDTH), lambda b: (b, 0)),
        out_shape=jax.ShapeDtypeStruct((T, POOL_WIDTH), jnp.bfloat16),
        compiler_params=_params(("parallel",)),
        name="pool",
    )(z, w_pool, pool_scale)


def _mix_kernel(x_ref, o_ref, p_ref, ga0, ga1, gb0, gb1, wa_ref, wb_ref, wo_ref, gpm_ref, gt1_ref,
                gpf_ref, sc2_ref, sh2_ref, wr_ref, br_ref,
                x1_ref, h2_ref, ls_ref, tw_ref, cnt_ref):
    tiles = range(x_ref.shape[0] // TOKEN_TILE)
    h2s = [_mix_front(t, x_ref, o_ref, p_ref, ga0, ga1, gb0, gb1, wa_ref, wb_ref, wo_ref, gpm_ref, gt1_ref,
                      gpf_ref, sc2_ref, sh2_ref, x1_ref) for t in tiles]
    for t in tiles:
        h2_ref[t * TOKEN_TILE:(t + 1) * TOKEN_TILE, :] = h2s[t].astype(h2_ref.dtype)
    for t in tiles:
        _mix_route(t, h2s[t], wr_ref, br_ref, ls_ref, tw_ref, cnt_ref)


def _mix_front(t, x_ref, o_ref, p_ref, ga0, ga1, gb0, gb1, wa_ref, wb_ref, wo_ref, gpm_ref, gt1_ref,
               gpf_ref, sc2_ref, sh2_ref, x1_ref):
    rows = slice(t * TOKEN_TILE, (t + 1) * TOKEN_TILE)
    ya = jnp.dot(o_ref[rows, :], wa_ref[...], preferred_element_type=jnp.float32)
    yb = jnp.dot(p_ref[rows, :], wb_ref[...], preferred_element_type=jnp.float32)
    ga = jnp.concatenate([ga0[rows, :], ga1[rows, :]], axis=1).astype(jnp.float32)
    gb = jnp.concatenate([gb0[rows, :], gb1[rows, :]], axis=1).astype(jnp.float32)
    merged = (ga * ya + gb * yb).astype(jnp.bfloat16)
    mixed = jnp.dot(merged, wo_ref[...], preferred_element_type=jnp.float32)
    x1 = x_ref[rows, :] + gt1_ref[0] * (_rms(mixed, NORM_EPS) * gpm_ref[...])
    x1_ref[rows, :] = x1
    return _rms(x1, NORM_EPS) * gpf_ref[...] * (1.0 + sc2_ref[0]) + sh2_ref[0]


def _mix_route(t, h2, wr_ref, br_ref, ls_ref, tw_ref, cnt_ref):
    tm = TOKEN_TILE
    rows = slice(t * tm, (t + 1) * tm)
    logits =lax.dot_general(wr_ref[...], h2, (((1,), (1,)), ((), ())), precision=_HI,
                             preferred_element_type=jnp.float32) + br_ref[...]
    eid = lax.broadcasted_iota(jnp.int32, logits.shape, 0)
    work = logits
    sels, vals = [], []
    for _ in range(TOP_K):
        mx = jnp.max(work, axis=0, keepdims=True)
        idx = jnp.min(jnp.where(work == mx, eid, N_EXPERTS), axis=0, keepdims=True)
        sel = eid == idx
        work = jnp.where(sel, -jnp.inf, work)
        sels.append(sel)
        vals.append(mx)
    ex = [jnp.exp(v - vals[0]) for v in vals]
    den = ex[0] + ex[1] + ex[2] + ex[3]
    onehot = jnp.zeros(logits.shape, jnp.float32)
    for sel in sels:
        onehot = jnp.where(sel, 1.0, onehot)
    r = lax.broadcasted_iota(jnp.int32, (tm, tm), 0)
    c = lax.broadcasted_iota(jnp.int32, (tm, tm), 1)
    tri = jnp.where(r < c, 1.0, 0.0).astype(jnp.bfloat16)
    rank = jnp.dot(onehot.astype(jnp.bfloat16), tri, preferred_element_type=jnp.float32)
    counts = jnp.broadcast_to(jnp.sum(onehot, axis=1, keepdims=True), (N_EXPERTS, LANES))
    rounded = jnp.floor((counts + (ROW_ALIGN - 1)) * (1.0 / ROW_ALIGN)) * ROW_ALIGN
    er = lax.broadcasted_iota(jnp.int32, (N_EXPERTS, N_EXPERTS), 0)
    ec = lax.broadcasted_iota(jnp.int32, (N_EXPERTS, N_EXPERTS), 1)
    below = jnp.where(ec < er, 1.0, 0.0).astype(jnp.bfloat16)
    offset = jnp.dot(below, rounded.astype(jnp.bfloat16), preferred_element_type=jnp.float32)
    slot_of = rank + offset[:, 0:1]
    cnt_ref[t] = rounded
    for kk in range(TOP_K):
        ls = jnp.sum(jnp.where(sels[kk], slot_of, 0.0), axis=0, keepdims=True).astype(jnp.int32)
        ls_ref[kk:kk + 1, rows] = ls
        tw_ref[kk:kk + 1, rows] = ex[kk] / den


def _mix(x2, o, p, z, wa, wb, wo, g_post_mix, mod3, g_pre_ffn, w_router_t, b_router, seq):
    T, D = x2.shape
    sub = min(MIX_TILES, seq // TOKEN_TILE)
    tm = sub * TOKEN_TILE
    per_b = seq // tm
    E = N_EXPERTS
    row = lambda n: pl.BlockSpec((1, n), lambda i: (0, 0))
    mod_spec = lambda col: pl.BlockSpec((1, 1, D), lambda i: (i // per_b, 0, col))
    gate = lambda cb: pl.BlockSpec((tm, COL_TILE), lambda i: (i, cb))
    full = lambda a: pl.BlockSpec(a.shape, lambda i: (0,) * a.ndim)
    tile = pl.BlockSpec((tm, D), lambda i: (i, 0))
    small = pl.BlockSpec((TOP_K, tm), lambda i: (0, i))
    return pl.pallas_call(
        _mix_kernel,
        grid=(T // tm,),
        in_specs=[tile, tile,
                  pl.BlockSpec((tm, POOL_WIDTH), lambda i: (i, 0)),
                  gate(COL_G), gate(COL_G + 1), gate(COL_G + 2), gate(COL_G + 3),
                  full(wa), full(wb), full(wo), row(D), mod_spec(2),
                  row(D), mod_spec(4), mod_spec(3), full(w_router_t),
                  pl.BlockSpec((E, 1), lambda i: (0, 0))],
        out_specs=[tile, tile, small, small,
                   pl.BlockSpec((sub, E, LANES), lambda i: (i, 0, 0))],
        out_shape=[jax.ShapeDtypeStruct((T, D), jnp.float32),
                   jax.ShapeDtypeStruct((T, D), jnp.bfloat16),
                   jax.ShapeDtypeStruct((TOP_K, T), jnp.int32),
                   jax.ShapeDtypeStruct((TOP_K, T), jnp.float32),
                   jax.ShapeDtypeStruct((T // TOKEN_TILE, E, LANES), jnp.float32)],
        compiler_params=_params(("parallel",)),
        name="mix_tail",
    )(x2, o, p, z, z, z, z, wa, wb, wo, g_post_mix, mod3, g_pre_ffn, mod3, mod3, w_router_t,
      b_router.reshape(E, 1))


def _chunk(ref, row):
    return ref.at[pl.ds(pl.multiple_of(row, ROW_ALIGN), ROW_ALIGN), :]


def _dispatch_kernel(chunk_slot_ref, n_chunk_ref, zero_off_ref, zero_len_ref, ntile_ref,
                     h_ref, ls_ref, xs_ref, xl_ref, zero_ref, sem, zsem, *, zero_per_step):
    i = pl.program_id(0)
    last = pl.num_programs(0) - 1

    @pl.when(i == 0)
    def _():
        zero_ref[...] = jnp.zeros_like(zero_ref)

    slot_id = lax.broadcasted_iota(jnp.int32, (LOCAL_ROWS, h_ref.shape[0]), 0)
    place = jnp.zeros(slot_id.shape, jnp.float32)
    for kk in range(TOP_K):
        place = jnp.where(slot_id == ls_ref[kk:kk + 1, :], 1.0, place)
    xl = jnp.dot(place.astype(jnp.bfloat16), h_ref[...], preferred_element_type=jnp.float32)
    xl_ref[i % 2] = xl.astype(xl_ref.dtype)

    def rows_out(tile, act):
        buf = tile % 2

        def body(k, _):
            act(pltpu.make_async_copy(_chunk(xl_ref.at[buf], k * ROW_ALIGN),
                                      _chunk(xs_ref, chunk_slot_ref[tile * BLOCK_CHUNKS + k]), sem.at[buf]))
            return 0
        lax.fori_loop(0, n_chunk_ref[tile], body, 0)

    def copies(act):
        for r in range(zero_per_step):
            idx = i * zero_per_step + r

            def zbody(k, _, idx=idx):
                act(pltpu.make_async_copy(_chunk(zero_ref, 0),
                                          _chunk(xs_ref, zero_off_ref[idx] + k * ROW_ALIGN), zsem))
                return 0
            lax.fori_loop(0, zero_len_ref[idx] // ROW_ALIGN, zbody, 0)

    def tails(act):
        def body(t, _):
            off = pl.multiple_of(t * MOE_TILE, MOE_TILE)
            act(pltpu.make_async_copy(zero_ref, xs_ref.at[pl.ds(off, MOE_TILE), :], zsem))
            return 0
        lax.fori_loop(ntile_ref[0], xs_ref.shape[0] // MOE_TILE, body, 0)

    start, wait = (lambda cp: cp.start()), (lambda cp: cp.wait())
    rows_out(i, start)
    copies(start)
    pl.when(i == 0)(functools.partial(tails, start))
    pl.when(i > 0)(lambda: rows_out(i - 1, wait))
    copies(wait)
    pl.when(i == 0)(functools.partial(tails, wait))
    pl.when(i == last)(lambda: rows_out(i, wait))


def _dispatch(chunk_slot, n_chunks, zero_off, zero_len, n_tiles, h2, ls, n_slots):
    T, W = h2.shape
    tc = TOKEN_TILE
    steps = n_chunks.shape[0]
    return pl.pallas_call(
        functools.partial(_dispatch_kernel, zero_per_step=zero_len.shape[0] // steps),
        grid_spec=pltpu.PrefetchScalarGridSpec(
            num_scalar_prefetch=5,
            grid=(steps,),
            in_specs=[pl.BlockSpec((tc, W), lambda i, *_: (i, 0)),
                      pl.BlockSpec((TOP_K, tc), lambda i, *_: (0, i))],
            out_specs=pl.BlockSpec(memory_space=pl.ANY),
            scratch_shapes=[pltpu.VMEM((2, LOCAL_ROWS, W), h2.dtype),
                            pltpu.VMEM((MOE_TILE, W), h2.dtype),
                            pltpu.SemaphoreType.DMA((2,)), pltpu.SemaphoreType.DMA]),
        out_shape=jax.ShapeDtypeStruct((n_slots, W), h2.dtype),
        compiler_params=_params(("arbitrary",)),
        name="dispatch",
    )(chunk_slot, n_chunks, zero_off, zero_len, n_tiles, h2, ls)


def _moe_kernel(te_ref, nt_ref, buf_ref, nxt_ref, rows_ref, x_ref, w1_hbm, b1_ref, w2_hbm, b2_ref, perm_ref, y_ref,
                w1raw_ref, w2raw_ref, w1s_ref, w2s_ref, sem):
    j = pl.program_id(0)
    prev = te_ref[jnp.maximum(j - 1, 0)]
    fresh = (j == 0) | (te_ref[j] != prev)
    nblk = w1s_ref.shape[1] // MXU_DIM

    def weight_copies(e, b):
        return (pltpu.make_async_copy(w1_hbm.at[e], w1raw_ref.at[b], sem.at[0, b]),
                pltpu.make_async_copy(w2_hbm.at[e], w2raw_ref.at[b], sem.at[1, b]))

    @pl.when(fresh & (j < nt_ref[0]))
    def _():
        e, b, nxt = te_ref[j], buf_ref[j], nxt_ref[j]

        @pl.when(j == 0)
        def _():
            for cp in weight_copies(e, b):
                cp.start()

        for cp in weight_copies(e, b):
            cp.wait()

        @pl.when(nxt >= 0)
        def _():
            for cp in weight_copies(nxt, 1 - b):
                cp.start()

        for c in range(nblk):
            cols = slice(c * MXU_DIM, (c + 1) * MXU_DIM)
            blk = w1raw_ref[b, :, cols].astype(jnp.bfloat16)
            w1s_ref[:, cols] = jnp.dot(blk, perm_ref[...],
                                       preferred_element_type=jnp.float32).astype(jnp.bfloat16)
        w2s_ref[...] = w2raw_ref[b].astype(jnp.bfloat16)

    def experts_mlp(rows):
        z = jnp.dot(x_ref[:rows, :], w1s_ref[...], preferred_element_type=jnp.float32) + b1_ref[0]
        acts = []
        for c in range(nblk):
            gate = jnp.minimum(z[:, c * MXU_DIM:c * MXU_DIM + LANES], SWIGLU_LIMIT)
            up = jnp.clip(z[:, c * MXU_DIM + LANES:(c + 1) * MXU_DIM], -SWIGLU_LIMIT, SWIGLU_LIMIT)
            acts.append(gate * jax.nn.sigmoid(SWIGLU_ALPHA * gate) * (up + 1.0))
        a = jnp.concatenate(acts, axis=1).astype(jnp.bfloat16)
        y = jnp.dot(a, w2s_ref[...], preferred_element_type=jnp.float32) + b2_ref[0]
        y_ref[:rows, :] = y.astype(y_ref.dtype)
        if rows < y_ref.shape[0]:
            y_ref[rows:, :] = jnp.zeros((y_ref.shape[0] - rows, y_ref.shape[1]), y_ref.dtype)

    active = j < nt_ref[0]
    step = y_ref.shape[0] // MOE_ROW_STEPS
    for q in range(1, MOE_ROW_STEPS + 1):
        fits = rows_ref[j] <= q * step
        if q > 1:
            fits = fits & (rows_ref[j] > (q - 1) * step)
        pl.when(active & fits)(functools.partial(experts_mlp, q * step))

    @pl.when(j >= nt_ref[0])
    def _():
        y_ref[...] = jnp.zeros_like(y_ref)


def _regroup_perm():
    src = np.arange(MXU_DIM)
    dst = np.where(src % 2 == 0, src // 2, LANES + src // 2)
    perm = np.zeros((MXU_DIM, MXU_DIM), np.float32)
    perm[src, dst] = 1.0
    return jnp.asarray(perm, jnp.bfloat16)


def _moe(tile_expert, n_tiles, tile_buf, tile_next, tile_rows, xs, w1, b1g, w2, b2):
    n_slots, W = xs.shape
    tm = MOE_TILE
    E, D, F2 = w1.shape
    F = w2.shape[1]
    xmap = lambda j, te, nt, *_: (jnp.minimum(j, nt[0] - 1), 0)
    emap = lambda j, te, *_: (te[j], 0, 0)
    return pl.pallas_call(
        _moe_kernel,
        grid_spec=pltpu.PrefetchScalarGridSpec(
            num_scalar_prefetch=5,
            grid=(n_slots // tm,),
            in_specs=[pl.BlockSpec((tm, W), xmap),
                      pl.BlockSpec(memory_space=pl.ANY),
                      pl.BlockSpec((1, 1, F2), emap),
                      pl.BlockSpec(memory_space=pl.ANY),
                      pl.BlockSpec((1, 1, D), emap),
                      pl.BlockSpec((MXU_DIM, MXU_DIM), lambda j, *_: (0, 0))],
            out_specs=pl.BlockSpec((tm, W), lambda j, *_: (j, 0)),
            scratch_shapes=[pltpu.VMEM((2, D, F2), w1.dtype), pltpu.VMEM((2, F, D), w2.dtype),
                            pltpu.VMEM((D, F2), jnp.bfloat16), pltpu.VMEM((F, D), jnp.bfloat16),
                            pltpu.SemaphoreType.DMA((2, 2))]),
        out_shape=jax.ShapeDtypeStruct((n_slots, W), xs.dtype),
        compiler_params=_params(("arbitrary",)),
        name="moe_experts",
    )(tile_expert, n_tiles, tile_buf, tile_next, tile_rows, xs, w1, b1g, w2, b2.reshape(E, 1, D),
      _regroup_perm())


def _combine_kernel(chunk_slot_ref, n_chunk_ref, ys_ref, ls_ref, w_ref, x1_ref, gt2_ref, g_ref,
                    o_ref, yl_ref, sem):
    i = pl.program_id(0)
    tc = x1_ref.shape[0]

    def pieces(tile, act):
        buf = tile % 2

        def body(k, _):
            act(pltpu.make_async_copy(_chunk(ys_ref, chunk_slot_ref[tile * BLOCK_CHUNKS + k]),
                                      _chunk(yl_ref.at[buf], k * ROW_ALIGN), sem.at[buf]))
            return 0
        lax.fori_loop(0, n_chunk_ref[tile], body, 0)

    def fetch(tile):
        yl_ref[tile % 2, TOP_K * tc:, :] = jnp.zeros((LOCAL_ROWS - TOP_K * tc, yl_ref.shape[2]), yl_ref.dtype)
        pieces(tile, lambda cp: cp.start())

    pl.when(i == 0)(lambda: fetch(i))
    pl.when(i + 1 < pl.num_programs(0))(lambda: fetch(i + 1))
    slot_id = lax.broadcasted_iota(jnp.int32, (tc, LOCAL_ROWS), 1)
    ls = ls_ref[...]
    w = w_ref[...]
    mix = jnp.zeros((tc, LOCAL_ROWS), jnp.float32)
    for k in range(TOP_K):
        mix = jnp.where(slot_id == ls[:, k:k + 1], w[:, k:k + 1], mix)
    pieces(i, lambda cp: cp.wait())
    f = jnp.dot(mix.astype(jnp.bfloat16), yl_ref[i % 2], preferred_element_type=jnp.float32)
    o_ref[...] = x1_ref[...] + gt2_ref[0] * (_rms(f, NORM_EPS) * g_ref[...])


def _combine(chunk_slot, n_chunks, ys, ls_t, topw_t, x1, mod3, g_post_ffn, seq):
    T, D = x1.shape
    tc = TOKEN_TILE
    per_b = seq // tc
    tile = pl.BlockSpec((tc, D), lambda i, *_: (i, 0))
    small = pl.BlockSpec((tc, TOP_K), lambda i, *_: (i, 0))
    return pl.pallas_call(
        _combine_kernel,
        grid_spec=pltpu.PrefetchScalarGridSpec(
            num_scalar_prefetch=2,
            grid=(T // tc,),
            in_specs=[pl.BlockSpec(memory_space=pl.ANY), small, small, tile,
                      pl.BlockSpec((1, 1, D), lambda i, *_: (i // per_b, 0, 5)),
                      pl.BlockSpec((1, D), lambda i, *_: (0, 0))],
            out_specs=tile,
            scratch_shapes=[pltpu.VMEM((2, LOCAL_ROWS, ys.shape[1]), ys.dtype),
                            pltpu.SemaphoreType.DMA((2,))]),
        out_shape=jax.ShapeDtypeStruct((T, D), jnp.float32),
        compiler_params=_params(("arbitrary",)),
        name="combine",
    )(chunk_slot, n_chunks, ys, ls_t, topw_t, x1, mod3, g_post_ffn)


def kernel(x, c, positions, w_ada, b_ada, g_pre_mix, w_in, lambda_q1, lambda_k1, lambda_q2, lambda_k2,
           g_subln, w_pool, pool_scale, w_proj_a, w_proj_b, w_out, g_post_mix, g_pre_ffn,
           w_router, b_router, w_exp1, b_exp1, w_exp2, b_exp2, g_post_ffn):
    B, S, D = x.shape
    T = B * S
    bf = jnp.bfloat16
    tabs = _rope_tables(positions)
    for l in range(w_ada.shape[0]):
        x2 = x.reshape(T, D)
        mod3 = _ada(c, w_ada[l], b_ada[l]).reshape(B, 1, N_MOD * D)
        n_qk = 2 * DIFF_WIDTH
        w_in_bf = jnp.concatenate([_permute_qk_columns(w_in[l][:, :n_qk]), w_in[l][:, n_qk:]], axis=1).astype(bf)
        z = _in_proj(x2, g_pre_mix[l][None], mod3, w_in_bf, tabs, S)
        lams = [v[l][None] for v in (lambda_q1, lambda_k1, lambda_q2, lambda_k2)]
        lam_init = 0.8 - 0.6 * math.exp(-0.3 * l)
        o = _diff_attention(z, lams, g_subln[l][None], lam_init, B, S)
        p = _pool(z, w_pool[l], pool_scale[l][None], B, S)
        x1, h2, ls, topw, cnt = _mix(
            x2, o, p, z, w_proj_a[l].astype(bf), w_proj_b[l].astype(bf), w_out[l].astype(bf),
            g_post_mix[l][None], mod3, g_pre_ffn[l][None], w_router[l].T, b_router[l], S)

        i32 = jnp.int32
        n = cnt[:, :, 0].astype(i32)
        n_tok_tiles = n.shape[0]
        counts = jnp.sum(n, axis=0)
        padded = (counts + MOE_TILE - 1) // MOE_TILE * MOE_TILE
        gend = jnp.cumsum(padded).astype(i32)
        gstart = gend - padded
        local_end = jnp.cumsum(n, axis=1)
        slot_off = gstart[None, :] + jnp.cumsum(n, axis=0) - n
        chunk_row = jnp.arange(BLOCK_CHUNKS, dtype=i32) * ROW_ALIGN
        piece = jnp.sum(local_end[:, None, :] <= chunk_row[None, :, None], axis=2)
        in_piece = piece[:, :, None] == jnp.arange(N_EXPERTS, dtype=i32)[None, None, :]
        shift = jnp.sum(jnp.where(in_piece, (slot_off - (local_end - n))[:, None, :], 0), axis=2)
        chunk_slot = (shift + chunk_row[None, :]).reshape(-1).astype(i32)
        n_chunks = (local_end[:, -1] // ROW_ALIGN).astype(i32)
        steps_pad = -N_EXPERTS % n_tok_tiles
        group_pad_off = jnp.pad(gstart + counts, (0, steps_pad)).astype(i32)
        group_pad_len = jnp.pad(padded - counts, (0, steps_pad)).astype(i32)
        n_slots = -(-(T * TOP_K + n_tok_tiles * N_EXPERTS * ROW_ALIGN) // MOE_TILE) * MOE_TILE + N_EXPERTS * MOE_TILE
        tile_start = jnp.arange(n_slots // MOE_TILE, dtype=i32) * MOE_TILE
        tile_expert = jnp.minimum(jnp.sum(tile_start[:, None] >= gend[None, :], axis=1), N_EXPERTS - 1)
        n_tiles = gend[-1:] // MOE_TILE
        has_tiles = padded > 0
        expert_ids = jnp.arange(N_EXPERTS, dtype=i32)
        later = lax.cummin(jnp.where(has_tiles, expert_ids, N_EXPERTS), reverse=True)
        next_expert = jnp.concatenate([later[1:], jnp.full((1,), N_EXPERTS, i32)])
        next_expert = jnp.where(next_expert < N_EXPERTS, next_expert, -1)
        of_tile = tile_expert[:, None] == expert_ids[None, :]
        pick = lambda per_expert: jnp.sum(jnp.where(of_tile, per_expert[None, :], 0), axis=1).astype(i32)
        tile_buf = pick((jnp.cumsum(has_tiles.astype(i32)) - 1) % 2)
        tile_next = pick(next_expert)
        tile_rows = jnp.clip(pick(gstart + counts) - tile_start, 0, MOE_TILE).astype(i32)

        xs = _dispatch(chunk_slot, n_chunks, group_pad_off, group_pad_len, n_tiles, h2, ls, n_slots)
        half = D_FF
        b1g = b_exp1[l].reshape(N_EXPERTS, half // LANES, LANES, 2).transpose(0, 1, 3, 2)
        b1g = b1g.reshape(N_EXPERTS, 1, 2 * half)
        ys = _moe(tile_expert.astype(i32), n_tiles, tile_buf, tile_next, tile_rows, xs,
                  w_exp1[l], b1g, w_exp2[l], b_exp2[l])
        x = _combine(chunk_slot, n_chunks, ys, ls.T, topw.T, x1, mod3, g_post_ffn[l][None], S)
        x = x.reshape(B, S, D)
    return x
```

```python
import functools
import math

import numpy as np
import jax
import jax.numpy as jnp
from jax import lax
from jax.experimental import pallas as pl
from jax.experimental.pallas import tpu as pltpu

D_MODEL = 1024
N_HEADS = 8
HEAD_DIM = 64
V_DIM = 2 * HEAD_DIM
DIFF_WIDTH = N_HEADS * V_DIM
POOL_WINDOWS = (2, 4, 8, 16)
POOL_GROUP_DIM = 128
POOL_WIDTH = len(POOL_WINDOWS) * POOL_GROUP_DIM
IN_COLS = 3 * DIFF_WIDTH + POOL_WIDTH + 2 * D_MODEL
ROPE_THETA = 500000.0
ROT_DIM = HEAD_DIM // 4
ROT_HALF = ROT_DIM // 2
N_EXPERTS = 32
TOP_K = 4
D_FF = D_MODEL
SWIGLU_ALPHA = 1.702
SWIGLU_LIMIT = 7.0
NORM_EPS = 1e-6
SUBLN_EPS = 1e-5
N_MOD = 6
NEG_BIG = -1e30

LANES = 128
MXU_DIM = 256
VMEM_LIMIT = 56 * 1024 * 1024

ADA_COL_TILE = 1536
ROPE_ROW_TILE = 2048
COL_TILE = 512
IN_ROW_TILE = 512
COL_K = DIFF_WIDTH // COL_TILE
COL_V = 2 * DIFF_WIDTH // COL_TILE
COL_G = (3 * DIFF_WIDTH + POOL_WIDTH) // COL_TILE
ATT_BLOCK = 512
ATT_HEADS = 4
MOE_TILE = 512
TOKEN_TILE = 256
MIX_TILES = 2
ROW_ALIGN = 16
LOCAL_ROWS = TOP_K * TOKEN_TILE + N_EXPERTS * ROW_ALIGN
BLOCK_CHUNKS = LOCAL_ROWS // ROW_ALIGN
CHUNK_UNROLL = 4

_HI = lax.Precision.HIGHEST


def _params(sem, vmem=VMEM_LIMIT):
    return pltpu.CompilerParams(dimension_semantics=sem, vmem_limit_bytes=vmem)


def _rms(x, eps):
    return x * lax.rsqrt(jnp.mean(x * x, axis=-1, keepdims=True) + eps)


def _ada_kernel(c_ref, w_ref, b_ref, o_ref):
    c = c_ref[...]
    s = c * jax.nn.sigmoid(c)
    o_ref[...] = jnp.dot(s, w_ref[...], precision=_HI, preferred_element_type=jnp.float32) + b_ref[...]


def _ada(c, w_ada, b_ada):
    B, D = c.shape
    N = w_ada.shape[1]
    tn = ADA_COL_TILE
    return pl.pallas_call(
        _ada_kernel,
        grid=(N // tn,),
        in_specs=[pl.BlockSpec((B, D), lambda j: (0, 0)),
                  pl.BlockSpec((D, tn), lambda j: (0, j)),
                  pl.BlockSpec((1, tn), lambda j: (0, j))],
        out_specs=pl.BlockSpec((B, tn), lambda j: (0, j)),
        out_shape=jax.ShapeDtypeStruct((B, N), jnp.float32),
        compiler_params=_params(("parallel",)),
        name="ada",
    )(c, w_ada, b_ada.reshape(1, N))


def _rope_tab_kernel(pos_ref, invf_ref, phase_ref, c_ref, s_ref):
    pos = pos_ref[...].astype(jnp.float32)
    cs = jnp.cos(pos * invf_ref[...] - phase_ref[...])
    rolled = pltpu.roll(cs, LANES // 2, 1)
    lane = lax.broadcasted_iota(jnp.int32, cs.shape, 1)
    lower = lane < LANES // 2
    rot = lane % (LANES // 2) < ROT_DIM
    c_ref[...] = jnp.where(lower, cs, rolled)
    s_ref[...] = jnp.where(rot, jnp.where(lower, -rolled, cs), 0.0)


def _rope_tables(positions):
    T = positions.size
    tm = min(T, ROPE_ROW_TILE)
    lane = np.arange(LANES)
    rot = lane % (LANES // 2) < ROT_DIM
    inv = ROPE_THETA ** (-(np.arange(ROT_HALF, dtype=np.float64) * 2.0 / ROT_DIM))
    invf = np.where(rot, inv[lane % ROT_HALF], 0.0).astype(np.float32)[None, :]
    phase = np.where(rot & (lane >= LANES // 2), np.pi / 2, 0.0).astype(np.float32)[None, :]
    row = pl.BlockSpec((1, LANES), lambda i: (0, 0))
    tab = pl.BlockSpec((tm, LANES), lambda i: (i, 0))
    sds = jax.ShapeDtypeStruct((T, LANES), jnp.float32)
    return pl.pallas_call(
        _rope_tab_kernel,
        grid=(T // tm,),
        in_specs=[pl.BlockSpec((tm, 1), lambda i: (i, 0)), row, row],
        out_specs=[tab, tab],
        out_shape=[sds, sds],
        compiler_params=_params(("parallel",)),
        name="rope_tables",
    )(positions.reshape(T, 1), jnp.asarray(invf), jnp.asarray(phase))


def _permute_qk_columns(w):
    D, n = w.shape
    w = w.reshape(D, n // V_DIM, 2, 4, 2, ROT_HALF)
    return w.transpose(0, 1, 4, 3, 2, 5).reshape(D, n)


def _in_proj_kernel(x_ref, g_ref, sc_ref, sh_ref, w_ref, c_ref, s_ref, z_ref):
    h = (_rms(x_ref[...], NORM_EPS) * g_ref[...] * (1.0 + sc_ref[0]) + sh_ref[0]).astype(jnp.bfloat16)
    c, s = c_ref[...], s_ref[...]
    q_scale = HEAD_DIM ** -0.5 * math.log2(math.e)
    cq, sq = c * q_scale, s * q_scale
    for j in range(IN_COLS // COL_TILE):
        cols = slice(j * COL_TILE, (j + 1) * COL_TILE)
        z = jnp.dot(h, w_ref[:, cols], preferred_element_type=jnp.float32)
        if j < COL_V:
            cj, sj = (cq, sq) if j < COL_K else (c, s)
            parts = []
            for g in range(COL_TILE // LANES):
                zg = z[:, g * LANES:(g + 1) * LANES]
                parts.append(zg * cj + pltpu.roll(zg, LANES // 2, 1) * sj)
            z = jnp.concatenate(parts, axis=1)
        elif j >= COL_G:
            z = 0.5 * jnp.tanh(0.5 * z) + 0.5
        z_ref[:, cols] = z.astype(z_ref.dtype)


def _in_proj(x2, g_pre, mod3, w_in_bf, tabs, seq):
    T, D = x2.shape
    tm = min(seq, IN_ROW_TILE)
    per_b = seq // tm
    mod_spec = lambda col: pl.BlockSpec((1, 1, D), lambda i: (i // per_b, 0, col))
    tab = pl.BlockSpec((tm, LANES), lambda i: (i, 0))
    return pl.pallas_call(
        _in_proj_kernel,
        grid=(T // tm,),
        in_specs=[pl.BlockSpec((tm, D), lambda i: (i, 0)),
                  pl.BlockSpec((1, D), lambda i: (0, 0)),
                  mod_spec(1), mod_spec(0),
                  pl.BlockSpec((D, IN_COLS), lambda i: (0, 0)),
                  tab, tab],
        out_specs=pl.BlockSpec((tm, IN_COLS), lambda i: (i, 0)),
        out_shape=jax.ShapeDtypeStruct((T, IN_COLS), jnp.bfloat16),
        compiler_params=_params(("parallel",)),
        name="in_proj",
    )(x2, g_pre, mod3, mod3, w_in_bf, *tabs)


def _attn_kernel(lq1_ref, lk1_ref, lq2_ref, lk2_ref, gs_ref, q_ref, k_ref, v_ref, o_ref, vt_ref, *acc_refs,
                 lam_init):
    i = pl.program_id(2)
    blk = q_ref.shape[0]
    nblk = v_ref.shape[0] // blk
    chains = [(h, comp) for h in range(ATT_HEADS) for comp in range(2)]

    @pl.when(i == 0)
    def _():
        for c in range(nblk):
            for h in range(ATT_HEADS):
                v = v_ref[c * blk:(c + 1) * blk, h * V_DIM:(h + 1) * V_DIM]
                vt_ref[c, h] = v.astype(jnp.float32).T.astype(vt_ref.dtype)

    lam = (jnp.exp(jnp.sum(lq1_ref[...] * lk1_ref[...], axis=-1, keepdims=True))
           - jnp.exp(jnp.sum(lq2_ref[...] * lk2_ref[...], axis=-1, keepdims=True))
           + lam_init)
    lane_comp = lax.broadcasted_iota(jnp.int32, (blk, V_DIM), 1) // ROT_HALF % 2
    qs = []
    for h, comp in chains:
        q = q_ref[:, h * V_DIM:(h + 1) * V_DIM]
        qs.append(jnp.where(lane_comp == comp, q, jnp.zeros_like(q)))
    nt = (((1,), (1,)), ((), ()))
    for acc_ref in acc_refs:
        acc_ref[...] = jnp.zeros_like(acc_ref)

    def step(c, carry, mask=None):
        off = pl.multiple_of(c * blk, blk)
        scores = []
        for n, (h, comp) in enumerate(chains):
            k = k_ref[pl.ds(off, blk), h * V_DIM:(h + 1) * V_DIM]
            scores.append(lax.dot_general(k, qs[n], nt, preferred_element_type=jnp.float32))
        out, probs, alphas = [], [], []
        for n, s in enumerate(scores):
            m, l = carry[n]
            if mask is not None:
                s = jnp.where(mask, s, NEG_BIG)
            m_new = jnp.maximum(m, jnp.max(s, axis=0, keepdims=True))
            alpha = jnp.exp2(m - m_new)
            p = jnp.exp2(s - m_new)
            out.append((m_new, alpha * l + jnp.sum(p, axis=0, keepdims=True)))
            probs.append(p.astype(vt_ref.dtype))
            alphas.append(alpha)
        for n, (h, comp) in enumerate(chains):
            pv = jnp.dot(vt_ref[c, h], probs[n], preferred_element_type=jnp.float32)
            acc_refs[n][...] = alphas[n] * acc_refs[n][...] + pv
        return tuple(out)

    init = tuple((jnp.full((1, blk), NEG_BIG, jnp.float32), jnp.zeros((1, blk), jnp.float32))
                 for _ in chains)
    carry = lax.fori_loop(0, i, step, init)

    key = lax.broadcasted_iota(jnp.int32, (blk, blk), 0)
    qry = lax.broadcasted_iota(jnp.int32, (blk, blk), 1)
    carry = step(i, carry, key <= qry)
    for h in range(ATT_HEADS):
        (_, l1), (_, l2) = carry[2 * h], carry[2 * h + 1]
        ot = acc_refs[2 * h][...] / l1 - lam * (acc_refs[2 * h + 1][...] / l2)
        o = _rms(ot.T, SUBLN_EPS) * gs_ref[...] * (1.0 - lam_init)
        o_ref[:, h * V_DIM:(h + 1) * V_DIM] = o.astype(o_ref.dtype)


def _diff_attention(z, lams, g_subln, lam_init, batch, seq):
    T = z.shape[0]
    blk = min(ATT_BLOCK, seq)
    nq = seq // blk
    width = ATT_HEADS * V_DIM
    vec = pl.BlockSpec((1, HEAD_DIM), lambda b, h, i: (0, 0))
    kcol = DIFF_WIDTH // width
    return pl.pallas_call(
        functools.partial(_attn_kernel, lam_init=lam_init),
        grid=(batch, N_HEADS // ATT_HEADS, nq),
        in_specs=[vec, vec, vec, vec,
                  pl.BlockSpec((1, V_DIM), lambda b, h, i: (0, 0)),
                  pl.BlockSpec((blk, width), lambda b, h, i: (b * nq + i, h)),
                  pl.BlockSpec((seq, width), lambda b, h, i: (b, kcol + h)),
                  pl.BlockSpec((seq, width), lambda b, h, i: (b, 2 * kcol + h))],
        out_specs=pl.BlockSpec((blk, width), lambda b, h, i: (b * nq + i, h)),
        out_shape=jax.ShapeDtypeStruct((T, DIFF_WIDTH), jnp.bfloat16),
        scratch_shapes=[pltpu.VMEM((nq, ATT_HEADS, V_DIM, blk), jnp.bfloat16)]
        + [pltpu.VMEM((V_DIM, blk), jnp.float32)] * (2 * ATT_HEADS),
        compiler_params=_params(("parallel", "parallel", "arbitrary")),
        name="diff_attn",
    )(*lams, g_subln, z, z, z)


def _pool_kernel(u_ref, w_ref, ps_ref, o_ref):
    t = lax.broadcasted_iota(jnp.int32, (u_ref.shape[0], POOL_GROUP_DIM), 0)

    def shifted(x, k):
        return jnp.where(t >= k, pltpu.roll(x, k, 0), 0.0)

    for g, window in enumerate(POOL_WINDOWS):
        cols = slice(g * POOL_GROUP_DIM, (g + 1) * POOL_GROUP_DIM)
        u = u_ref[:, cols].astype(jnp.float32)
        s, k = u, 1
        while k < window:
            s = s + shifted(s, k)
            k *= 2
        d = s / jnp.minimum(t + 1, window).astype(jnp.float32) - u
        y = jnp.dot(d.astype(jnp.bfloat16), w_ref[g].astype(jnp.bfloat16), preferred_element_type=jnp.float32)
        o_ref[:, cols] = (y * ps_ref[:, cols]).astype(o_ref.dtype)


def _pool(z, w_pool, pool_scale, batch, seq):
    T = z.shape[0]
    G = len(POOL_WINDOWS)
    ucol = 3 * DIFF_WIDTH // POOL_WIDTH
    return pl.pallas_call(
        _pool_kernel,
        grid=(batch,),
        in_specs=[pl.BlockSpec((seq, POOL_WIDTH), lambda b: (b, ucol)),
                  pl.BlockSpec((G, POOL_GROUP_DIM, POOL_GROUP_DIM), lambda b: (0, 0, 0)),
                  pl.BlockSpec((1, POOL_WIDTH), lambda b: (0, 0))],
        out_specs=pl.BlockSpec((seq, POOL_WIDTH), lambda b: (b, 0)),
        out_shape=jax.ShapeDtypeStruct((T, POOL_WIDTH), jnp.bfloat16),
        compiler_params=_params(("parallel",)),
        name="pool",
    )(z, w_pool, pool_scale)


def _mix_kernel(x_ref, o_ref, p_ref, ga0, ga1, gb0, gb1, wa_ref, wb_ref, wo_ref, gpm_ref, gt1_ref,
                gpf_ref, sc2_ref, sh2_ref, wr_ref, br_ref,
                x1_ref, h2_ref, ls_ref, tw_ref, cnt_ref):
    tiles = range(x_ref.shape[0] // TOKEN_TILE)
    h2s = [_mix_front(t, x_ref, o_ref, p_ref, ga0, ga1, gb0, gb1, wa_ref, wb_ref, wo_ref, gpm_ref, gt1_ref,
                      gpf_ref, sc2_ref, sh2_ref, x1_ref) for t in tiles]
    for t in tiles:
        h2_ref[t * TOKEN_TILE:(t + 1) * TOKEN_TILE, :] = h2s[t].astype(h2_ref.dtype)
    for t in tiles:
        _mix_route(t, h2s[t], wr_ref, br_ref, ls_ref, tw_ref, cnt_ref)


def _mix_front(t, x_ref, o_ref, p_ref, ga0, ga1, gb0, gb1, wa_ref, wb_ref, wo_ref, gpm_ref, gt1_ref,
               gpf_ref, sc2_ref, sh2_ref, x1_ref):
    rows = slice(t * TOKEN_TILE, (t + 1) * TOKEN_TILE)
    ya = jnp.dot(o_ref[rows, :], wa_ref[...], preferred_element_type=jnp.float32)
    yb = jnp.dot(p_ref[rows, :], wb_ref[...], preferred_element_type=jnp.float32)
    ga = jnp.concatenate([ga0[rows, :], ga1[rows, :]], axis=1).astype(jnp.float32)
    gb = jnp.concatenate([gb0[rows, :], gb1[rows, :]], axis=1).astype(jnp.float32)
    merged = (ga * ya + gb * yb).astype(jnp.bfloat16)
    mixed = jnp.dot(merged, wo_ref[...], preferred_element_type=jnp.float32)
    x1 = x_ref[rows, :] + gt1_ref[0] * (_rms(mixed, NORM_EPS) * gpm_ref[...])
    x1_ref[rows, :] = x1
    return _rms(x1, NORM_EPS) * gpf_ref[...] * (1.0 + sc2_ref[0]) + sh2_ref[0]


def _mix_route(t, h2, wr_ref, br_ref, ls_ref, tw_ref, cnt_ref):
    tm = TOKEN_TILE
    rows = slice(t * tm, (t + 1) * tm)
    logits =lax.dot_general(wr_ref[...], h2, (((1,), (1,)), ((), ())), precision=_HI,
                             preferred_element_type=jnp.float32) + br_ref[...]
    eid = lax.broadcasted_iota(jnp.int32, logits.shape, 0)
    work = logits
    sels, vals = [], []
    for _ in range(TOP_K):
        mx = jnp.max(work, axis=0, keepdims=True)
        idx = jnp.min(jnp.where(work == mx, eid, N_EXPERTS), axis=0, keepdims=True)
        sel = eid == idx
        work = jnp.where(sel, -jnp.inf, work)
        sels.append(sel)
        vals.append(mx)
    ex = [jnp.exp(v - vals[0]) for v in vals]
    den = ex[0] + ex[1] + ex[2] + ex[3]
    onehot = jnp.zeros(logits.shape, jnp.float32)
    for sel in sels:
        onehot = jnp.where(sel, 1.0, onehot)
    r = lax.broadcasted_iota(jnp.int32, (tm, tm), 0)
    c = lax.broadcasted_iota(jnp.int32, (tm, tm), 1)
    tri = jnp.where(r < c, 1.0, 0.0).astype(jnp.bfloat16)
    rank = jnp.dot(onehot.astype(jnp.bfloat16), tri, preferred_element_type=jnp.float32)
    counts = jnp.broadcast_to(jnp.sum(onehot, axis=1, keepdims=True), (N_EXPERTS, LANES))
    rounded = jnp.floor((counts + (ROW_ALIGN - 1)) * (1.0 / ROW_ALIGN)) * ROW_ALIGN
    er = lax.broadcasted_iota(jnp.int32, (N_EXPERTS, N_EXPERTS), 0)
    ec = lax.broadcasted_iota(jnp.int32, (N_EXPERTS, N_EXPERTS), 1)
    below = jnp.where(ec < er, 1.0, 0.0).astype(jnp.bfloat16)
    offset = jnp.dot(below, rounded.astype(jnp.bfloat16), preferred_element_type=jnp.float32)
    slot_of = rank + offset[:, 0:1]
    cnt_ref[t] = rounded
    for kk in range(TOP_K):
        ls = jnp.sum(jnp.where(sels[kk], slot_of, 0.0), axis=0, keepdims=True).astype(jnp.int32)
        ls_ref[kk:kk + 1, rows] = ls
        tw_ref[kk:kk + 1, rows] = ex[kk] / den


def _mix(x2, o, p, z, wa, wb, wo, g_post_mix, mod3, g_pre_ffn, w_router_t, b_router, seq):
    T, D = x2.shape
    sub = min(MIX_TILES, seq // TOKEN_TILE)
    tm = sub * TOKEN_TILE
    per_b = seq // tm
    E = N_EXPERTS
    row = lambda n: pl.BlockSpec((1, n), lambda i: (0, 0))
    mod_spec = lambda col: pl.BlockSpec((1, 1, D), lambda i: (i // per_b, 0, col))
    gate = lambda cb: pl.BlockSpec((tm, COL_TILE), lambda i: (i, cb))
    full = lambda a: pl.BlockSpec(a.shape, lambda i: (0,) * a.ndim)
    tile = pl.BlockSpec((tm, D), lambda i: (i, 0))
    small = pl.BlockSpec((TOP_K, tm), lambda i: (0, i))
    return pl.pallas_call(
        _mix_kernel,
        grid=(T // tm,),
        in_specs=[tile, tile,
                  pl.BlockSpec((tm, POOL_WIDTH), lambda i: (i, 0)),
                  gate(COL_G), gate(COL_G + 1), gate(COL_G + 2), gate(COL_G + 3),
                  full(wa), full(wb), full(wo), row(D), mod_spec(2),
                  row(D), mod_spec(4), mod_spec(3), full(w_router_t),
                  pl.BlockSpec((E, 1), lambda i: (0, 0))],
        out_specs=[tile, tile, small, small,
                   pl.BlockSpec((sub, E, LANES), lambda i: (i, 0, 0))],
        out_shape=[jax.ShapeDtypeStruct((T, D), jnp.float32),
                   jax.ShapeDtypeStruct((T, D), jnp.bfloat16),
                   jax.ShapeDtypeStruct((TOP_K, T), jnp.int32),
                   jax.ShapeDtypeStruct((TOP_K, T), jnp.float32),
                   jax.ShapeDtypeStruct((T // TOKEN_TILE, E, LANES), jnp.float32)],
        compiler_params=_params(("parallel",)),
        name="mix_tail",
    )(x2, o, p, z, z, z, z, wa, wb, wo, g_post_mix, mod3, g_pre_ffn, mod3, mod3, w_router_t,
      b_router.reshape(E, 1))


def _chunk(ref, row):
    return ref.at[pl.ds(pl.multiple_of(row, ROW_ALIGN), ROW_ALIGN), :]


def _for_chunks(n, fn):
    def body(g, _):
        for u in range(CHUNK_UNROLL):
            fn(g * CHUNK_UNROLL + u)
        return 0
    lax.fori_loop(0, n // CHUNK_UNROLL, body, 0)
    lax.fori_loop(n // CHUNK_UNROLL * CHUNK_UNROLL, n, lambda k, _: (fn(k), 0)[1], 0)


def _dispatch_kernel(chunk_slot_ref, n_chunk_ref, zero_off_ref, zero_len_ref, ntile_ref,
                     h_ref, ls_ref, xs_ref, xl_ref, zero_ref, sem, zsem, *, zero_per_step):
    i = pl.program_id(0)
    last = pl.num_programs(0) - 1

    @pl.when(i == 0)
    def _():
        zero_ref[...] = jnp.zeros_like(zero_ref)

    slot_id = lax.broadcasted_iota(jnp.int32, (LOCAL_ROWS, h_ref.shape[0]), 0)
    place = jnp.zeros(slot_id.shape, jnp.float32)
    for kk in range(TOP_K):
        place = jnp.where(slot_id == ls_ref[kk:kk + 1, :], 1.0, place)
    xl = jnp.dot(place.astype(jnp.bfloat16), h_ref[...], preferred_element_type=jnp.float32)
    xl_ref[i % 2] = xl.astype(xl_ref.dtype)

    def rows_out(tile, act):
        buf = tile % 2

        def copy(k):
            act(pltpu.make_async_copy(_chunk(xl_ref.at[buf], k * ROW_ALIGN),
                                      _chunk(xs_ref, chunk_slot_ref[tile * BLOCK_CHUNKS + k]), sem.at[buf]))

        _for_chunks(n_chunk_ref[tile], copy)

    def copies(act):
        for r in range(zero_per_step):
            idx = i * zero_per_step + r

            def zbody(k, _, idx=idx):
                act(pltpu.make_async_copy(_chunk(zero_ref, 0),
                                          _chunk(xs_ref, zero_off_ref[idx] + k * ROW_ALIGN), zsem))
                return 0
            lax.fori_loop(0, zero_len_ref[idx] // ROW_ALIGN, zbody, 0)

    def tails(act):
        def body(t, _):
            off = pl.multiple_of(t * MOE_TILE, MOE_TILE)
            act(pltpu.make_async_copy(zero_ref, xs_ref.at[pl.ds(off, MOE_TILE), :], zsem))
            return 0
        lax.fori_loop(ntile_ref[0], xs_ref.shape[0] // MOE_TILE, body, 0)

    start, wait = (lambda cp: cp.start()), (lambda cp: cp.wait())
    rows_out(i, start)
    copies(start)
    pl.when(i == 0)(functools.partial(tails, start))
    pl.when(i > 0)(lambda: rows_out(i - 1, wait))
    copies(wait)
    pl.when(i == 0)(functools.partial(tails, wait))
    pl.when(i == last)(lambda: rows_out(i, wait))


def _dispatch(chunk_slot, n_chunks, zero_off, zero_len, n_tiles, h2, ls, n_slots):
    T, W = h2.shape
    tc = TOKEN_TILE
    steps = n_chunks.shape[0]
    return pl.pallas_call(
        functools.partial(_dispatch_kernel, zero_per_step=zero_len.shape[0] // steps),
        grid_spec=pltpu.PrefetchScalarGridSpec(
            num_scalar_prefetch=5,
            grid=(steps,),
            in_specs=[pl.BlockSpec((tc, W), lambda i, *_: (i, 0)),
                      pl.BlockSpec((TOP_K, tc), lambda i, *_: (0, i))],
            out_specs=pl.BlockSpec(memory_space=pl.ANY),
            scratch_shapes=[pltpu.VMEM((2, LOCAL_ROWS, W), h2.dtype),
                            pltpu.VMEM((MOE_TILE, W), h2.dtype),
                            pltpu.SemaphoreType.DMA((2,)), pltpu.SemaphoreType.DMA]),
        out_shape=jax.ShapeDtypeStruct((n_slots, W), h2.dtype),
        compiler_params=_params(("arbitrary",)),
        name="dispatch",
    )(chunk_slot, n_chunks, zero_off, zero_len, n_tiles, h2, ls)


def _moe_kernel(te_ref, nt_ref, buf_ref, nxt_ref, rows_ref, x_ref, w1_hbm, b1_ref, w2_hbm, b2_ref, perm_ref, y_ref,
                w1raw_ref, w2raw_ref, w1s_ref, w2s_ref, sem):
    j = pl.program_id(0)
    prev = te_ref[jnp.maximum(j - 1, 0)]
    fresh = (j == 0) | (te_ref[j] != prev)
    nblk = w1s_ref.shape[1] // MXU_DIM

    def weight_copies(e, b):
        return (pltpu.make_async_copy(w1_hbm.at[e], w1raw_ref.at[b], sem.at[0, b]),
                pltpu.make_async_copy(w2_hbm.at[e], w2raw_ref.at[b], sem.at[1, b]))

    @pl.when(fresh & (j < nt_ref[0]))
    def _():
        e, b, nxt = te_ref[j], buf_ref[j], nxt_ref[j]

        @pl.when(j == 0)
        def _():
            for cp in weight_copies(e, b):
                cp.start()

        for cp in weight_copies(e, b):
            cp.wait()

        @pl.when(nxt >= 0)
        def _():
            for cp in weight_copies(nxt, 1 - b):
                cp.start()

        for c in range(nblk):
            cols = slice(c * MXU_DIM, (c + 1) * MXU_DIM)
            blk = w1raw_ref[b, :, cols].astype(jnp.bfloat16)
            w1s_ref[:, cols] = jnp.dot(blk, perm_ref[...],
                                       preferred_element_type=jnp.float32).astype(jnp.bfloat16)
        w2s_ref[...] = w2raw_ref[b].astype(jnp.bfloat16)

    def experts_mlp(rows):
        z = jnp.dot(x_ref[:rows, :], w1s_ref[...], preferred_element_type=jnp.float32) + b1_ref[0]
        acts = []
        for c in range(nblk):
            gate = jnp.minimum(z[:, c * MXU_DIM:c * MXU_DIM + LANES], SWIGLU_LIMIT)
            up = jnp.clip(z[:, c * MXU_DIM + LANES:(c + 1) * MXU_DIM], -SWIGLU_LIMIT, SWIGLU_LIMIT)
            acts.append(gate * jax.nn.sigmoid(SWIGLU_ALPHA * gate) * (up + 1.0))
        a = jnp.concatenate(acts, axis=1).astype(jnp.bfloat16)
        y = jnp.dot(a, w2s_ref[...], preferred_element_type=jnp.float32) + b2_ref[0]
        y_ref[:rows, :] = y.astype(y_ref.dtype)
        if rows < y_ref.shape[0]:
            y_ref[rows:, :] = jnp.zeros((y_ref.shape[0] - rows, y_ref.shape[1]), y_ref.dtype)

    half = y_ref.shape[0] // 2
    active = j < nt_ref[0]
    pl.when(active & (rows_ref[j] > half))(lambda: experts_mlp(2 * half))
    pl.when(active & (rows_ref[j] <= half))(lambda: experts_mlp(half))

    @pl.when(j >= nt_ref[0])
    def _():
        y_ref[...] = jnp.zeros_like(y_ref)


def _regroup_perm():
    src = np.arange(MXU_DIM)
    dst = np.where(src % 2 == 0, src // 2, LANES + src // 2)
    perm = np.zeros((MXU_DIM, MXU_DIM), np.float32)
    perm[src, dst] = 1.0
    return jnp.asarray(perm, jnp.bfloat16)


def _moe(tile_expert, n_tiles, tile_buf, tile_next, tile_rows, xs, w1, b1g, w2, b2):
    n_slots, W = xs.shape
    tm = MOE_TILE
    E, D, F2 = w1.shape
    F = w2.shape[1]
    xmap = lambda j, te, nt, *_: (jnp.minimum(j, nt[0] - 1), 0)
    emap = lambda j, te, *_: (te[j], 0, 0)
    return pl.pallas_call(
        _moe_kernel,
        grid_spec=pltpu.PrefetchScalarGridSpec(
            num_scalar_prefetch=5,
            grid=(n_slots // tm,),
            in_specs=[pl.BlockSpec((tm, W), xmap),
                      pl.BlockSpec(memory_space=pl.ANY),
                      pl.BlockSpec((1, 1, F2), emap),
                      pl.BlockSpec(memory_space=pl.ANY),
                      pl.BlockSpec((1, 1, D), emap),
                      pl.BlockSpec((MXU_DIM, MXU_DIM), lambda j, *_: (0, 0))],
            out_specs=pl.BlockSpec((tm, W), lambda j, *_: (j, 0)),
            scratch_shapes=[pltpu.VMEM((2, D, F2), w1.dtype), pltpu.VMEM((2, F, D), w2.dtype),
                            pltpu.VMEM((D, F2), jnp.bfloat16), pltpu.VMEM((F, D), jnp.bfloat16),
                            pltpu.SemaphoreType.DMA((2, 2))]),
        out_shape=jax.ShapeDtypeStruct((n_slots, W), xs.dtype),
        compiler_params=_params(("arbitrary",)),
        name="moe_experts",
    )(tile_expert, n_tiles, tile_buf, tile_next, tile_rows, xs, w1, b1g, w2, b2.reshape(E, 1, D),
      _regroup_perm())


def _combine_kernel(chunk_slot_ref, n_chunk_ref, ys_ref, ls_ref, w_ref, x1_ref, gt2_ref, g_ref,
                    o_ref, yl_ref, sem):
    i = pl.program_id(0)
    tc = x1_ref.shape[0]

    def pieces(tile, act):
        buf = tile % 2

        def copy(k):
            act(pltpu.make_async_copy(_chunk(ys_ref, chunk_slot_ref[tile * BLOCK_CHUNKS + k]),
                                      _chunk(yl_ref.at[buf], k * ROW_ALIGN), sem.at[buf]))

        _for_chunks(n_chunk_ref[tile], copy)

    def fetch(tile):
        yl_ref[tile % 2, TOP_K * tc:, :] = jnp.zeros((LOCAL_ROWS - TOP_K * tc, yl_ref.shape[2]), yl_ref.dtype)
        pieces(tile, lambda cp: cp.start())

    pl.when(i == 0)(lambda: fetch(i))
    pl.when(i + 1 < pl.num_programs(0))(lambda: fetch(i + 1))
    slot_id = lax.broadcasted_iota(jnp.int32, (tc, LOCAL_ROWS), 1)
    ls = ls_ref[...]
    w = w_ref[...]
    mix = jnp.zeros((tc, LOCAL_ROWS), jnp.float32)
    for k in range(TOP_K):
        mix = jnp.where(slot_id == ls[:, k:k + 1], w[:, k:k + 1], mix)
    pieces(i, lambda cp: cp.wait())
    f = jnp.dot(mix.astype(jnp.bfloat16), yl_ref[i % 2], preferred_element_type=jnp.float32)
    o_ref[...] = x1_ref[...] + gt2_ref[0] * (_rms(f, NORM_EPS) * g_ref[...])


def _combine(chunk_slot, n_chunks, ys, ls_t, topw_t, x1, mod3, g_post_ffn, seq):
    T, D = x1.shape
    tc = TOKEN_TILE
    per_b = seq // tc
    tile = pl.BlockSpec((tc, D), lambda i, *_: (i, 0))
    small = pl.BlockSpec((tc, TOP_K), lambda i, *_: (i, 0))
    return pl.pallas_call(
        _combine_kernel,
        grid_spec=pltpu.PrefetchScalarGridSpec(
            num_scalar_prefetch=2,
            grid=(T // tc,),
            in_specs=[pl.BlockSpec(memory_space=pl.ANY), small, small, tile,
                      pl.BlockSpec((1, 1, D), lambda i, *_: (i // per_b, 0, 5)),
                      pl.BlockSpec((1, D), lambda i, *_: (0, 0))],
            out_specs=tile,
            scratch_shapes=[pltpu.VMEM((2, LOCAL_ROWS, ys.shape[1]), ys.dtype),
                            pltpu.SemaphoreType.DMA((2,))]),
        out_shape=jax.ShapeDtypeStruct((T, D), jnp.float32),
        compiler_params=_params(("arbitrary",)),
        name="combine",
    )(chunk_slot, n_chunks, ys, ls_t, topw_t, x1, mod3, g_post_ffn)


def kernel(x, c, positions, w_ada, b_ada, g_pre_mix, w_in, lambda_q1, lambda_k1, lambda_q2, lambda_k2,
           g_subln, w_pool, pool_scale, w_proj_a, w_proj_b, w_out, g_post_mix, g_pre_ffn,
           w_router, b_router, w_exp1, b_exp1, w_exp2, b_exp2, g_post_ffn):
    B, S, D = x.shape
    T = B * S
    bf = jnp.bfloat16
    tabs = _rope_tables(positions)
    for l in range(w_ada.shape[0]):
        x2 = x.reshape(T, D)
        mod3 = _ada(c, w_ada[l], b_ada[l]).reshape(B, 1, N_MOD * D)
        n_qk = 2 * DIFF_WIDTH
        w_in_bf = jnp.concatenate([_permute_qk_columns(w_in[l][:, :n_qk]), w_in[l][:, n_qk:]], axis=1).astype(bf)
        z = _in_proj(x2, g_pre_mix[l][None], mod3, w_in_bf, tabs, S)
        lams = [v[l][None] for v in (lambda_q1, lambda_k1, lambda_q2, lambda_k2)]
        lam_init = 0.8 - 0.6 * math.exp(-0.3 * l)
        o = _diff_attention(z, lams, g_subln[l][None], lam_init, B, S)
        p = _pool(z, w_pool[l], pool_scale[l][None], B, S)
        x1, h2, ls, topw, cnt = _mix(
            x2, o, p, z, w_proj_a[l].astype(bf), w_proj_b[l].astype(bf), w_out[l].astype(bf),
            g_post_mix[l][None], mod3, g_pre_ffn[l][None], w_router[l].T, b_router[l], S)

        i32 = jnp.int32
        n = cnt[:, :, 0].astype(i32)
        n_tok_tiles = n.shape[0]
        counts = jnp.sum(n, axis=0)
        padded = (counts + MOE_TILE - 1) // MOE_TILE * MOE_TILE
        gend = jnp.cumsum(padded).astype(i32)
        gstart = gend - padded
        local_end = jnp.cumsum(n, axis=1)
        slot_off = gstart[None, :] + jnp.cumsum(n, axis=0) - n
        chunk_row = jnp.arange(BLOCK_CHUNKS, dtype=i32) * ROW_ALIGN
        piece = jnp.sum(local_end[:, None, :] <= chunk_row[None, :, None], axis=2)
        in_piece = piece[:, :, None] == jnp.arange(N_EXPERTS, dtype=i32)[None, None, :]
        shift = jnp.sum(jnp.where(in_piece, (slot_off - (local_end - n))[:, None, :], 0), axis=2)
        chunk_slot = (shift + chunk_row[None, :]).reshape(-1).astype(i32)
        n_chunks = (local_end[:, -1] // ROW_ALIGN).astype(i32)
        steps_pad = -N_EXPERTS % n_tok_tiles
        group_pad_off = jnp.pad(gstart + counts, (0, steps_pad)).astype(i32)
        group_pad_len = jnp.pad(padded - counts, (0, steps_pad)).astype(i32)
        n_slots = -(-(T * TOP_K + n_tok_tiles * N_EXPERTS * ROW_ALIGN) // MOE_TILE) * MOE_TILE + N_EXPERTS * MOE_TILE
        tile_start = jnp.arange(n_slots // MOE_TILE, dtype=i32) * MOE_TILE
        tile_expert = jnp.minimum(jnp.sum(tile_start[:, None] >= gend[None, :], axis=1), N_EXPERTS - 1)
        n_tiles = gend[-1:] // MOE_TILE
        has_tiles = padded > 0
        expert_ids = jnp.arange(N_EXPERTS, dtype=i32)
        later = lax.cummin(jnp.where(has_tiles, expert_ids, N_EXPERTS), reverse=True)
        next_expert = jnp.concatenate([later[1:], jnp.full((1,), N_EXPERTS, i32)])
        next_expert = jnp.where(next_expert < N_EXPERTS, next_expert, -1)
        of_tile = tile_expert[:, None] == expert_ids[None, :]
        pick = lambda per_expert: jnp.sum(jnp.where(of_tile, per_expert[None, :], 0), axis=1).astype(i32)
        tile_buf = pick((jnp.cumsum(has_tiles.astype(i32)) - 1) % 2)
        tile_next = pick(next_expert)
        tile_rows = jnp.clip(pick(gstart + counts) - tile_start, 0, MOE_TILE).astype(i32)

        xs = _dispatch(chunk_slot, n_chunks, group_pad_off, group_pad_len, n_tiles, h2, ls, n_slots)
        half = D_FF
        b1g = b_exp1[l].reshape(N_EXPERTS, half // LANES, LANES, 2).transpose(0, 1, 3, 2)
        b1g = b1g.reshape(N_EXPERTS, 1, 2 * half)
        ys = _moe(tile_expert.astype(i32), n_tiles, tile_buf, tile_next, tile_rows, xs,
                  w_exp1[l], b1g, w_exp2[l], b_exp2[l])
        x = _combine(chunk_slot, n_chunks, ys, ls.T, topw.T, x1, mod3, g_post_ffn[l][None], S)
        x = x.reshape(B, S, D)
    return x
```

```python
import functools
import math

import numpy as np
import jax
import jax.numpy as jnp
from jax import lax
from jax.experimental import pallas as pl
from jax.experimental.pallas import tpu as pltpu

D_MODEL = 1024
N_HEADS = 8
HEAD_DIM = 64
V_DIM = 2 * HEAD_DIM
DIFF_WIDTH = N_HEADS * V_DIM
POOL_WINDOWS = (2, 4, 8, 16)
POOL_GROUP_DIM = 128
POOL_WIDTH = len(POOL_WINDOWS) * POOL_GROUP_DIM
IN_COLS = 3 * DIFF_WIDTH + POOL_WIDTH + 2 * D_MODEL
ROPE_THETA = 500000.0
ROT_DIM = HEAD_DIM // 4
ROT_HALF = ROT_DIM // 2
N_EXPERTS = 32
TOP_K = 4
D_FF = D_MODEL
SWIGLU_ALPHA = 1.702
SWIGLU_LIMIT = 7.0
NORM_EPS = 1e-6
SUBLN_EPS = 1e-5
N_MOD = 6
NEG_BIG = -1e30

LANES = 128
MXU_DIM = 256
VMEM_LIMIT = 56 * 1024 * 1024

ADA_COL_TILE = 1536
ROPE_ROW_TILE = 2048
COL_TILE = 512
IN_ROW_TILE = 512
COL_K = DIFF_WIDTH // COL_TILE
COL_V = 2 * DIFF_WIDTH // COL_TILE
COL_G = (3 * DIFF_WIDTH + POOL_WIDTH) // COL_TILE
ATT_BLOCK = 512
ATT_HEADS = 4
MOE_TILE = 512
TOKEN_TILE = 256
MIX_TILES = 2
ROW_ALIGN = 16
LOCAL_ROWS = TOP_K * TOKEN_TILE + N_EXPERTS * ROW_ALIGN
BLOCK_CHUNKS = LOCAL_ROWS // ROW_ALIGN
CHUNK_UNROLL = 4

_HI = lax.Precision.HIGHEST


def _params(sem, vmem=VMEM_LIMIT):
    return pltpu.CompilerParams(dimension_semantics=sem, vmem_limit_bytes=vmem)


def _rms(x, eps):
    return x * lax.rsqrt(jnp.mean(x * x, axis=-1, keepdims=True) + eps)


def _ada_kernel(c_ref, w_ref, b_ref, o_ref):
    c = c_ref[...]
    s = c * jax.nn.sigmoid(c)
    o_ref[...] = jnp.dot(s, w_ref[...], precision=_HI, preferred_element_type=jnp.float32) + b_ref[...]


def _ada(c, w_ada, b_ada):
    B, D = c.shape
    N = w_ada.shape[1]
    tn = ADA_COL_TILE
    return pl.pallas_call(
        _ada_kernel,
        grid=(N // tn,),
        in_specs=[pl.BlockSpec((B, D), lambda j: (0, 0)),
                  pl.BlockSpec((D, tn), lambda j: (0, j)),
                  pl.BlockSpec((1, tn), lambda j: (0, j))],
        out_specs=pl.BlockSpec((B, tn), lambda j: (0, j)),
        out_shape=jax.ShapeDtypeStruct((B, N), jnp.float32),
        compiler_params=_params(("parallel",)),
        name="ada",
    )(c, w_ada, b_ada.reshape(1, N))


def _rope_tab_kernel(pos_ref, inv_ref, c_ref, s_ref):
    lane = lax.broadcasted_iota(jnp.int32, (LANES, LANES), 1)
    lower = lane < LANES // 2
    rot = lane % (LANES // 2) < ROT_DIM
    reps = LANES // ROT_HALF
    for r in range(pos_ref.shape[0]):
        ang = inv_ref[...] * pos_ref[r:r + 1, :].astype(jnp.float32)
        cos_t = jnp.tile(jnp.cos(ang), (reps, 1)).T
        sin_t = jnp.tile(jnp.sin(ang), (reps, 1)).T
        rows = slice(r * LANES, (r + 1) * LANES)
        c_ref[rows, :] = jnp.where(rot, cos_t, 1.0)
        s_ref[rows, :] = jnp.where(rot, jnp.where(lower, -sin_t, sin_t), 0.0)


def _rope_tables(positions):
    T = positions.size
    tm = min(T, ROPE_ROW_TILE)
    inv = ROPE_THETA ** (-(np.arange(ROT_HALF, dtype=np.float64) * 2.0 / ROT_DIM))
    inv = np.broadcast_to(inv.astype(np.float32)[:, None], (ROT_HALF, LANES))
    tab = pl.BlockSpec((tm, LANES), lambda i: (i, 0))
    sds = jax.ShapeDtypeStruct((T, LANES), jnp.float32)
    return pl.pallas_call(
        _rope_tab_kernel,
        grid=(T // tm,),
        in_specs=[pl.BlockSpec((tm // LANES, LANES), lambda i: (i, 0)),
                  pl.BlockSpec((ROT_HALF, LANES), lambda i: (0, 0))],
        out_specs=[tab, tab],
        out_shape=[sds, sds],
        compiler_params=_params(("parallel",)),
        name="rope_tables",
    )(positions.reshape(T // LANES, LANES), jnp.asarray(inv))


def _permute_qk_columns(w):
    D, n = w.shape
    w = w.reshape(D, n // V_DIM, 2, 4, 2, ROT_HALF)
    return w.transpose(0, 1, 4, 3, 2, 5).reshape(D, n)


def _in_proj_kernel(x_ref, g_ref, sc_ref, sh_ref, w_ref, c_ref, s_ref, z_ref):
    h = (_rms(x_ref[...], NORM_EPS) * g_ref[...] * (1.0 + sc_ref[0]) + sh_ref[0]).astype(jnp.bfloat16)
    c, s = c_ref[...], s_ref[...]
    q_scale = HEAD_DIM ** -0.5 * math.log2(math.e)
    cq, sq = c * q_scale, s * q_scale
    for j in range(IN_COLS // COL_TILE):
        cols = slice(j * COL_TILE, (j + 1) * COL_TILE)
        z = jnp.dot(h, w_ref[:, cols], preferred_element_type=jnp.float32)
        if j < COL_V:
            cj, sj = (cq, sq) if j < COL_K else (c, s)
            parts = []
            for g in range(COL_TILE // LANES):
                zg = z[:, g * LANES:(g + 1) * LANES]
                parts.append(zg * cj + pltpu.roll(zg, LANES // 2, 1) * sj)
            z = jnp.concatenate(parts, axis=1)
        elif j >= COL_G:
            z = 0.5 * jnp.tanh(0.5 * z) + 0.5
        z_ref[:, cols] = z.astype(z_ref.dtype)


def _in_proj(x2, g_pre, mod3, w_in_bf, tabs, seq):
    T, D = x2.shape
    tm = min(seq, IN_ROW_TILE)
    per_b = seq // tm
    mod_spec = lambda col: pl.BlockSpec((1, 1, D), lambda i: (i // per_b, 0, col))
    tab = pl.BlockSpec((tm, LANES), lambda i: (i, 0))
    return pl.pallas_call(
        _in_proj_kernel,
        grid=(T // tm,),
        in_specs=[pl.BlockSpec((tm, D), lambda i: (i, 0)),
                  pl.BlockSpec((1, D), lambda i: (0, 0)),
                  mod_spec(1), mod_spec(0),
                  pl.BlockSpec((D, IN_COLS), lambda i: (0, 0)),
                  tab, tab],
        out_specs=pl.BlockSpec((tm, IN_COLS), lambda i: (i, 0)),
        out_shape=jax.ShapeDtypeStruct((T, IN_COLS), jnp.bfloat16),
        compiler_params=_params(("parallel",)),
        name="in_proj",
    )(x2, g_pre, mod3, mod3, w_in_bf, *tabs)


def _attn_kernel(lq1_ref, lk1_ref, lq2_ref, lk2_ref, gs_ref, q_ref, k_ref, v_ref, o_ref, vt_ref, *acc_refs,
                 lam_init):
    i = pl.program_id(2)
    blk = q_ref.shape[0]
    nblk = v_ref.shape[0] // blk
    chains = [(h, comp) for h in range(ATT_HEADS) for comp in range(2)]

    @pl.when(i == 0)
    def _():
        for c in range(nblk):
            for h in range(ATT_HEADS):
                v = v_ref[c * blk:(c + 1) * blk, h * V_DIM:(h + 1) * V_DIM]
                vt_ref[c, h] = v.astype(jnp.float32).T.astype(vt_ref.dtype)

    lam = (jnp.exp(jnp.sum(lq1_ref[...] * lk1_ref[...], axis=-1, keepdims=True))
           - jnp.exp(jnp.sum(lq2_ref[...] * lk2_ref[...], axis=-1, keepdims=True))
           + lam_init)
    lane_comp = lax.broadcasted_iota(jnp.int32, (blk, V_DIM), 1) // ROT_HALF % 2
    qs = []
    for h, comp in chains:
        q = q_ref[:, h * V_DIM:(h + 1) * V_DIM]
        qs.append(jnp.where(lane_comp == comp, q, jnp.zeros_like(q)))
    nt = (((1,), (1,)), ((), ()))
    for acc_ref in acc_refs:
        acc_ref[...] = jnp.zeros_like(acc_ref)

    def step(c, carry, mask=None):
        off = pl.multiple_of(c * blk, blk)
        scores = []
        for n, (h, comp) in enumerate(chains):
            k = k_ref[pl.ds(off, blk), h * V_DIM:(h + 1) * V_DIM]
            scores.append(lax.dot_general(k, qs[n], nt, preferred_element_type=jnp.float32))
        out, probs, alphas = [], [], []
        for n, s in enumerate(scores):
            m, l = carry[n]
            if mask is not None:
                s = jnp.where(mask, s, NEG_BIG)
            m_new = jnp.maximum(m, jnp.max(s, axis=0, keepdims=True))
            alpha = jnp.exp2(m - m_new)
            p = jnp.exp2(s - m_new)
            out.append((m_new, alpha * l + jnp.sum(p, axis=0, keepdims=True)))
            probs.append(p.astype(vt_ref.dtype))
            alphas.append(alpha)
        for n, (h, comp) in enumerate(chains):
            pv = jnp.dot(vt_ref[c, h], probs[n], preferred_element_type=jnp.float32)
            acc_refs[n][...] = alphas[n] * acc_refs[n][...] + pv
        return tuple(out)

    init = tuple((jnp.full((1, blk), NEG_BIG, jnp.float32), jnp.zeros((1, blk), jnp.float32))
                 for _ in chains)
    carry = lax.fori_loop(0, i, step, init)

    key = lax.broadcasted_iota(jnp.int32, (blk, blk), 0)
    qry = lax.broadcasted_iota(jnp.int32, (blk, blk), 1)
    carry = step(i, carry, key <= qry)
    for h in range(ATT_HEADS):
        (_, l1), (_, l2) = carry[2 * h], carry[2 * h + 1]
        ot = acc_refs[2 * h][...] / l1 - lam * (acc_refs[2 * h + 1][...] / l2)
        o = _rms(ot.T, SUBLN_EPS) * gs_ref[...] * (1.0 - lam_init)
        o_ref[:, h * V_DIM:(h + 1) * V_DIM] = o.astype(o_ref.dtype)


def _diff_attention(z, lams, g_subln, lam_init, batch, seq):
    T = z.shape[0]
    blk = min(ATT_BLOCK, seq)
    nq = seq // blk
    width = ATT_HEADS * V_DIM
    vec = pl.BlockSpec((1, HEAD_DIM), lambda b, h, i: (0, 0))
    kcol = DIFF_WIDTH // width
    return pl.pallas_call(
        functools.partial(_attn_kernel, lam_init=lam_init),
        grid=(batch, N_HEADS // ATT_HEADS, nq),
        in_specs=[vec, vec, vec, vec,
                  pl.BlockSpec((1, V_DIM), lambda b, h, i: (0, 0)),
                  pl.BlockSpec((blk, width), lambda b, h, i: (b * nq + i, h)),
                  pl.BlockSpec((seq, width), lambda b, h, i: (b, kcol + h)),
                  pl.BlockSpec((seq, width), lambda b, h, i: (b, 2 * kcol + h))],
        out_specs=pl.BlockSpec((blk, width), lambda b, h, i: (b * nq + i, h)),
        out_shape=jax.ShapeDtypeStruct((T, DIFF_WIDTH), jnp.bfloat16),
        scratch_shapes=[pltpu.VMEM((nq, ATT_HEADS, V_DIM, blk), jnp.bfloat16)]
        + [pltpu.VMEM((V_DIM, blk), jnp.float32)] * (2 * ATT_HEADS),
        compiler_params=_params(("parallel", "parallel", "arbitrary")),
        name="diff_attn",
    )(*lams, g_subln, z, z, z)


def _pool_kernel(u_ref, w_ref, ps_ref, o_ref):
    t = lax.broadcasted_iota(jnp.int32, (u_ref.shape[0], POOL_GROUP_DIM), 0)

    def shifted(x, k):
        return jnp.where(t >= k, pltpu.roll(x, k, 0), 0.0)

    for g, window in enumerate(POOL_WINDOWS):
        cols = slice(g * POOL_GROUP_DIM, (g + 1) * POOL_GROUP_DIM)
        u = u_ref[:, cols].astype(jnp.float32)
        s, k = u, 1
        while k < window:
            s = s + shifted(s, k)
            k *= 2
        d = s / jnp.minimum(t + 1, window).astype(jnp.float32) - u
        y = jnp.dot(d.astype(jnp.bfloat16), w_ref[g].astype(jnp.bfloat16), preferred_element_type=jnp.float32)
        o_ref[:, cols] = (y * ps_ref[:, cols]).astype(o_ref.dtype)


def _pool(z, w_pool, pool_scale, batch, seq):
    T = z.shape[0]
    G = len(POOL_WINDOWS)
    ucol = 3 * DIFF_WIDTH // POOL_WIDTH
    return pl.pallas_call(
        _pool_kernel,
        grid=(batch,),
        in_specs=[pl.BlockSpec((seq, POOL_WIDTH), lambda b: (b, ucol)),
                  pl.BlockSpec((G, POOL_GROUP_DIM, POOL_GROUP_DIM), lambda b: (0, 0, 0)),
                  pl.BlockSpec((1, POOL_WIDTH), lambda b: (0, 0))],
        out_specs=pl.BlockSpec((seq, POOL_WIDTH), lambda b: (b, 0)),
        out_shape=jax.ShapeDtypeStruct((T, POOL_WIDTH), jnp.bfloat16),
        compiler_params=_params(("parallel",)),
        name="pool",
    )(z, w_pool, pool_scale)


def _mix_kernel(x_ref, o_ref, p_ref, ga0, ga1, gb0, gb1, wa_ref, wb_ref, wo_ref, gpm_ref, gt1_ref,
                gpf_ref, sc2_ref, sh2_ref, wr_ref, br_ref,
                x1_ref, h2_ref, ls_ref, tw_ref, cnt_ref):
    tiles = range(x_ref.shape[0] // TOKEN_TILE)
    h2s = [_mix_front(t, x_ref, o_ref, p_ref, ga0, ga1, gb0, gb1, wa_ref, wb_ref, wo_ref, gpm_ref, gt1_ref,
                      gpf_ref, sc2_ref, sh2_ref, x1_ref) for t in tiles]
    for t in tiles:
        h2_ref[t * TOKEN_TILE:(t + 1) * TOKEN_TILE, :] = h2s[t].astype(h2_ref.dtype)
    for t in tiles:
        _mix_route(t, h2s[t], wr_ref, br_ref, ls_ref, tw_ref, cnt_ref)


def _mix_front(t, x_ref, o_ref, p_ref, ga0, ga1, gb0, gb1, wa_ref, wb_ref, wo_ref, gpm_ref, gt1_ref,
               gpf_ref, sc2_ref, sh2_ref, x1_ref):
    rows = slice(t * TOKEN_TILE, (t + 1) * TOKEN_TILE)
    ya = jnp.dot(o_ref[rows, :], wa_ref[...], preferred_element_type=jnp.float32)
    yb = jnp.dot(p_ref[rows, :], wb_ref[...], preferred_element_type=jnp.float32)
    ga = jnp.concatenate([ga0[rows, :], ga1[rows, :]], axis=1).astype(jnp.float32)
    gb = jnp.concatenate([gb0[rows, :], gb1[rows, :]], axis=1).astype(jnp.float32)
    merged = (ga * ya + gb * yb).astype(jnp.bfloat16)
    mixed = jnp.dot(merged, wo_ref[...], preferred_element_type=jnp.float32)
    x1 = x_ref[rows, :] + gt1_ref[0] * (_rms(mixed, NORM_EPS) * gpm_ref[...])
    x1_ref[rows, :] = x1
    return _rms(x1, NORM_EPS) * gpf_ref[...] * (1.0 + sc2_ref[0]) + sh2_ref[0]


def _mix_route(t, h2, wr_ref, br_ref, ls_ref, tw_ref, cnt_ref):
    tm = TOKEN_TILE
    rows = slice(t * tm, (t + 1) * tm)
    logits =lax.dot_general(wr_ref[...], h2, (((1,), (1,)), ((), ())), precision=_HI,
                             preferred_element_type=jnp.float32) + br_ref[...]
    eid = lax.broadcasted_iota(jnp.int32, logits.shape, 0)
    work = logits
    sels, vals = [], []
    for _ in range(TOP_K):
        mx = jnp.max(work, axis=0, keepdims=True)
        idx = jnp.min(jnp.where(work == mx, eid, N_EXPERTS), axis=0, keepdims=True)
        sel = eid == idx
        work = jnp.where(sel, -jnp.inf, work)
        sels.append(sel)
        vals.append(mx)
    ex = [jnp.exp(v - vals[0]) for v in vals]
    den = ex[0] + ex[1] + ex[2] + ex[3]
    onehot = jnp.zeros(logits.shape, jnp.float32)
    for sel in sels:
        onehot = jnp.where(sel, 1.0, onehot)
    r = lax.broadcasted_iota(jnp.int32, (tm, tm), 0)
    c = lax.broadcasted_iota(jnp.int32, (tm, tm), 1)
    tri = jnp.where(r < c, 1.0, 0.0).astype(jnp.bfloat16)
    rank = jnp.dot(onehot.astype(jnp.bfloat16), tri, preferred_element_type=jnp.float32)
    counts = jnp.broadcast_to(jnp.sum(onehot, axis=1, keepdims=True), (N_EXPERTS, LANES))
    rounded = jnp.floor((counts + (ROW_ALIGN - 1)) * (1.0 / ROW_ALIGN)) * ROW_ALIGN
    er = lax.broadcasted_iota(jnp.int32, (N_EXPERTS, N_EXPERTS), 0)
    ec = lax.broadcasted_iota(jnp.int32, (N_EXPERTS, N_EXPERTS), 1)
    below = jnp.where(ec < er, 1.0, 0.0).astype(jnp.bfloat16)
    offset = jnp.dot(below, rounded.astype(jnp.bfloat16), preferred_element_type=jnp.float32)
    slot_of = rank + offset[:, 0:1]
    cnt_ref[t] = rounded
    for kk in range(TOP_K):
        ls = jnp.sum(jnp.where(sels[kk], slot_of, 0.0), axis=0, keepdims=True).astype(jnp.int32)
        ls_ref[kk:kk + 1, rows] = ls
        tw_ref[kk:kk + 1, rows] = ex[kk] / den


def _mix(x2, o, p, z, wa, wb, wo, g_post_mix, mod3, g_pre_ffn, w_router_t, b_router, seq):
    T, D = x2.shape
    sub = min(MIX_TILES, seq // TOKEN_TILE)
    tm = sub * TOKEN_TILE
    per_b = seq // tm
    E = N_EXPERTS
    row = lambda n: pl.BlockSpec((1, n), lambda i: (0, 0))
    mod_spec = lambda col: pl.BlockSpec((1, 1, D), lambda i: (i // per_b, 0, col))
    gate = lambda cb: pl.BlockSpec((tm, COL_TILE), lambda i: (i, cb))
    full = lambda a: pl.BlockSpec(a.shape, lambda i: (0,) * a.ndim)
    tile = pl.BlockSpec((tm, D), lambda i: (i, 0))
    small = pl.BlockSpec((TOP_K, tm), lambda i: (0, i))
    return pl.pallas_call(
        _mix_kernel,
        grid=(T // tm,),
        in_specs=[tile, tile,
                  pl.BlockSpec((tm, POOL_WIDTH), lambda i: (i, 0)),
                  gate(COL_G), gate(COL_G + 1), gate(COL_G + 2), gate(COL_G + 3),
                  full(wa), full(wb), full(wo), row(D), mod_spec(2),
                  row(D), mod_spec(4), mod_spec(3), full(w_router_t),
                  pl.BlockSpec((E, 1), lambda i: (0, 0))],
        out_specs=[tile, tile, small, small,
                   pl.BlockSpec((sub, E, LANES), lambda i: (i, 0, 0))],
        out_shape=[jax.ShapeDtypeStruct((T, D), jnp.float32),
                   jax.ShapeDtypeStruct((T, D), jnp.bfloat16),
                   jax.ShapeDtypeStruct((TOP_K, T), jnp.int32),
                   jax.ShapeDtypeStruct((TOP_K, T), jnp.float32),
                   jax.ShapeDtypeStruct((T // TOKEN_TILE, E, LANES), jnp.float32)],
        compiler_params=_params(("parallel",)),
        name="mix_tail",
    )(x2, o, p, z, z, z, z, wa, wb, wo, g_post_mix, mod3, g_pre_ffn, mod3, mod3, w_router_t,
      b_router.reshape(E, 1))


def _chunk(ref, row):
    return ref.at[pl.ds(pl.multiple_of(row, ROW_ALIGN), ROW_ALIGN), :]


def _for_chunks(n, fn):
    def body(g, _):
        for u in range(CHUNK_UNROLL):
            fn(g * CHUNK_UNROLL + u)
        return 0
    lax.fori_loop(0, n // CHUNK_UNROLL, body, 0)
    lax.fori_loop(n // CHUNK_UNROLL * CHUNK_UNROLL, n, lambda k, _: (fn(k), 0)[1], 0)


def _dispatch_kernel(chunk_slot_ref, n_chunk_ref, zero_off_ref, zero_len_ref, ntile_ref,
                     h_ref, ls_ref, xs_ref, xl_ref, zero_ref, sem, zsem, *, zero_per_step):
    i = pl.program_id(0)
    last = pl.num_programs(0) - 1

    @pl.when(i == 0)
    def _():
        zero_ref[...] = jnp.zeros_like(zero_ref)

    slot_id = lax.broadcasted_iota(jnp.int32, (LOCAL_ROWS, h_ref.shape[0]), 0)
    place = jnp.zeros(slot_id.shape, jnp.float32)
    for kk in range(TOP_K):
        place = jnp.where(slot_id == ls_ref[kk:kk + 1, :], 1.0, place)
    xl = jnp.dot(place.astype(jnp.bfloat16), h_ref[...], preferred_element_type=jnp.float32)
    xl_ref[i % 2] = xl.astype(xl_ref.dtype)

    def rows_out(tile, act):
        buf = tile % 2

        def copy(k):
            act(pltpu.make_async_copy(_chunk(xl_ref.at[buf], k * ROW_ALIGN),
                                      _chunk(xs_ref, chunk_slot_ref[tile * BLOCK_CHUNKS + k]), sem.at[buf]))

        _for_chunks(n_chunk_ref[tile], copy)

    def copies(act):
        for r in range(zero_per_step):
            idx = i * zero_per_step + r

            def zbody(k, _, idx=idx):
                act(pltpu.make_async_copy(_chunk(zero_ref, 0),
                                          _chunk(xs_ref, zero_off_ref[idx] + k * ROW_ALIGN), zsem))
                return 0
            lax.fori_loop(0, zero_len_ref[idx] // ROW_ALIGN, zbody, 0)

    def tails(act):
        def body(t, _):
            off = pl.multiple_of(t * MOE_TILE, MOE_TILE)
            act(pltpu.make_async_copy(zero_ref, xs_ref.at[pl.ds(off, MOE_TILE), :], zsem))
            return 0
        lax.fori_loop(ntile_ref[0], xs_ref.shape[0] // MOE_TILE, body, 0)

    start, wait = (lambda cp: cp.start()), (lambda cp: cp.wait())
    rows_out(i, start)
    copies(start)
    pl.when(i == 0)(functools.partial(tails, start))
    pl.when(i > 0)(lambda: rows_out(i - 1, wait))
    copies(wait)
    pl.when(i == 0)(functools.partial(tails, wait))
    pl.when(i == last)(lambda: rows_out(i, wait))


def _dispatch(chunk_slot, n_chunks, zero_off, zero_len, n_tiles, h2, ls, n_slots):
    T, W = h2.shape
    tc = TOKEN_TILE
    steps = n_chunks.shape[0]
    return pl.pallas_call(
        functools.partial(_dispatch_kernel, zero_per_step=zero_len.shape[0] // steps),
        grid_spec=pltpu.PrefetchScalarGridSpec(
            num_scalar_prefetch=5,
            grid=(steps,),
            in_specs=[pl.BlockSpec((tc, W), lambda i, *_: (i, 0)),
                      pl.BlockSpec((TOP_K, tc), lambda i, *_: (0, i))],
            out_specs=pl.BlockSpec(memory_space=pl.ANY),
            scratch_shapes=[pltpu.VMEM((2, LOCAL_ROWS, W), h2.dtype),
                            pltpu.VMEM((MOE_TILE, W), h2.dtype),
                            pltpu.SemaphoreType.DMA((2,)), pltpu.SemaphoreType.DMA]),
        out_shape=jax.ShapeDtypeStruct((n_slots, W), h2.dtype),
        compiler_params=_params(("arbitrary",)),
        name="dispatch",
    )(chunk_slot, n_chunks, zero_off, zero_len, n_tiles, h2, ls)


def _moe_kernel(te_ref, nt_ref, buf_ref, nxt_ref, rows_ref, x_ref, w1_hbm, b1_ref, w2_hbm, b2_ref, perm_ref, y_ref,
                w1raw_ref, w2raw_ref, w1s_ref, w2s_ref, sem):
    j = pl.program_id(0)
    prev = te_ref[jnp.maximum(j - 1, 0)]
    fresh = (j == 0) | (te_ref[j] != prev)
    nblk = w1s_ref.shape[1] // MXU_DIM

    def weight_copies(e, b):
        return (pltpu.make_async_copy(w1_hbm.at[e], w1raw_ref.at[b], sem.at[0, b]),
                pltpu.make_async_copy(w2_hbm.at[e], w2raw_ref.at[b], sem.at[1, b]))

    @pl.when(fresh & (j < nt_ref[0]))
    def _():
        e, b, nxt = te_ref[j], buf_ref[j], nxt_ref[j]

        @pl.when(j == 0)
        def _():
            for cp in weight_copies(e, b):
                cp.start()

        for cp in weight_copies(e, b):
            cp.wait()

        @pl.when(nxt >= 0)
        def _():
            for cp in weight_copies(nxt, 1 - b):
                cp.start()

        for c in range(nblk):
            cols = slice(c * MXU_DIM, (c + 1) * MXU_DIM)
            blk = w1raw_ref[b, :, cols].astype(jnp.bfloat16)
            w1s_ref[:, cols] = jnp.dot(blk, perm_ref[...],
                                       preferred_element_type=jnp.float32).astype(jnp.bfloat16)
        w2s_ref[...] = w2raw_ref[b].astype(jnp.bfloat16)

    def experts_mlp(rows):
        z = jnp.dot(x_ref[:rows, :], w1s_ref[...], preferred_element_type=jnp.float32) + b1_ref[0]
        acts = []
        for c in range(nblk):
            gate = jnp.minimum(z[:, c * MXU_DIM:c * MXU_DIM + LANES], SWIGLU_LIMIT)
            up = jnp.clip(z[:, c * MXU_DIM + LANES:(c + 1) * MXU_DIM], -SWIGLU_LIMIT, SWIGLU_LIMIT)
            acts.append(gate * jax.nn.sigmoid(SWIGLU_ALPHA * gate) * (up + 1.0))
        a = jnp.concatenate(acts, axis=1).astype(jnp.bfloat16)
        y = jnp.dot(a, w2s_ref[...], preferred_element_type=jnp.float32) + b2_ref[0]
        y_ref[:rows, :] = y.astype(y_ref.dtype)
        if rows < y_ref.shape[0]:
            y_ref[rows:, :] = jnp.zeros((y_ref.shape[0] - rows, y_ref.shape[1]), y_ref.dtype)

    half = y_ref.shape[0] // 2
    active = j < nt_ref[0]
    pl.when(active & (rows_ref[j] > half))(lambda: experts_mlp(2 * half))
    pl.when(active & (rows_ref[j] <= half))(lambda: experts_mlp(half))

    @pl.when(j >= nt_ref[0])
    def _():
        y_ref[...] = jnp.zeros_like(y_ref)


def _regroup_perm():
    src = np.arange(MXU_DIM)
    dst = np.where(src % 2 == 0, src // 2, LANES + src // 2)
    perm = np.zeros((MXU_DIM, MXU_DIM), np.float32)
    perm[src, dst] = 1.0
    return jnp.asarray(perm, jnp.bfloat16)


def _moe(tile_expert, n_tiles, tile_buf, tile_next, tile_rows, xs, w1, b1g, w2, b2):
    n_slots, W = xs.shape
    tm = MOE_TILE
    E, D, F2 = w1.shape
    F = w2.shape[1]
    xmap = lambda j, te, nt, *_: (jnp.minimum(j, nt[0] - 1), 0)
    emap = lambda j, te, *_: (te[j], 0, 0)
    return pl.pallas_call(
        _moe_kernel,
        grid_spec=pltpu.PrefetchScalarGridSpec(
            num_scalar_prefetch=5,
            grid=(n_slots // tm,),
            in_specs=[pl.BlockSpec((tm, W), xmap),
                      pl.BlockSpec(memory_space=pl.ANY),
                      pl.BlockSpec((1, 1, F2), emap),
                      pl.BlockSpec(memory_space=pl.ANY),
                      pl.BlockSpec((1, 1, D), emap),
                      pl.BlockSpec((MXU_DIM, MXU_DIM), lambda j, *_: (0, 0))],
            out_specs=pl.BlockSpec((tm, W), lambda j, *_: (j, 0)),
            scratch_shapes=[pltpu.VMEM((2, D, F2), w1.dtype), pltpu.VMEM((2, F, D), w2.dtype),
                            pltpu.VMEM((D, F2), jnp.bfloat16), pltpu.VMEM((F, D), jnp.bfloat16),
                            pltpu.SemaphoreType.DMA((2, 2))]),
        out_shape=jax.ShapeDtypeStruct((n_slots, W), xs.dtype),
        compiler_params=_params(("arbitrary",)),
        name="moe_experts",
    )(tile_expert, n_tiles, tile_buf, tile_next, tile_rows, xs, w1, b1g, w2, b2.reshape(E, 1, D),
      _regroup_perm())


def _combine_kernel(chunk_slot_ref, n_chunk_ref, ys_ref, ls_ref, w_ref, x1_ref, gt2_ref, g_ref,
                    o_ref, yl_ref, sem):
    i = pl.program_id(0)
    tc = x1_ref.shape[0]

    def pieces(tile, act):
        buf = tile % 2

        def copy(k):
            act(pltpu.make_async_copy(_chunk(ys_ref, chunk_slot_ref[tile * BLOCK_CHUNKS + k]),
                                      _chunk(yl_ref.at[buf], k * ROW_ALIGN), sem.at[buf]))

        _for_chunks(n_chunk_ref[tile], copy)

    def fetch(tile):
        yl_ref[tile % 2, TOP_K * tc:, :] = jnp.zeros((LOCAL_ROWS - TOP_K * tc, yl_ref.shape[2]), yl_ref.dtype)
        pieces(tile, lambda cp: cp.start())

    pl.when(i == 0)(lambda: fetch(i))
    pl.when(i + 1 < pl.num_programs(0))(lambda: fetch(i + 1))
    slot_id = lax.broadcasted_iota(jnp.int32, (tc, LOCAL_ROWS), 1)
    ls = ls_ref[...]
    w = w_ref[...]
    mix = jnp.zeros((tc, LOCAL_ROWS), jnp.float32)
    for k in range(TOP_K):
        mix = jnp.where(slot_id == ls[:, k:k + 1], w[:, k:k + 1], mix)
    pieces(i, lambda cp: cp.wait())
    f = jnp.dot(mix.astype(jnp.bfloat16), yl_ref[i % 2], preferred_element_type=jnp.float32)
    o_ref[...] = x1_ref[...] + gt2_ref[0] * (_rms(f, NORM_EPS) * g_ref[...])


def _combine(chunk_slot, n_chunks, ys, ls_t, topw_t, x1, mod3, g_post_ffn, seq):
    T, D = x1.shape
    tc = TOKEN_TILE
    per_b = seq // tc
    tile = pl.BlockSpec((tc, D), lambda i, *_: (i, 0))
    small = pl.BlockSpec((tc, TOP_K), lambda i, *_: (i, 0))
    return pl.pallas_call(
        _combine_kernel,
        grid_spec=pltpu.PrefetchScalarGridSpec(
            num_scalar_prefetch=2,
            grid=(T // tc,),
            in_specs=[pl.BlockSpec(memory_space=pl.ANY), small, small, tile,
                      pl.BlockSpec((1, 1, D), lambda i, *_: (i // per_b, 0, 5)),
                      pl.BlockSpec((1, D), lambda i, *_: (0, 0))],
            out_specs=tile,
            scratch_shapes=[pltpu.VMEM((2, LOCAL_ROWS, ys.shape[1]), ys.dtype),
                            pltpu.SemaphoreType.DMA((2,))]),
        out_shape=jax.ShapeDtypeStruct((T, D), jnp.float32),
        compiler_params=_params(("arbitrary",)),
        name="combine",
    )(chunk_slot, n_chunks, ys, ls_t, topw_t, x1, mod3, g_post_ffn)


def kernel(x, c, positions, w_ada, b_ada, g_pre_mix, w_in, lambda_q1, lambda_k1, lambda_q2, lambda_k2,
           g_subln, w_pool, pool_scale, w_proj_a, w_proj_b, w_out, g_post_mix, g_pre_ffn,
           w_router, b_router, w_exp1, b_exp1, w_exp2, b_exp2, g_post_ffn):
    B, S, D = x.shape
    T = B * S
    bf = jnp.bfloat16
    tabs = _rope_tables(positions)
    for l in range(w_ada.shape[0]):
        x2 = x.reshape(T, D)
        mod3 = _ada(c, w_ada[l], b_ada[l]).reshape(B, 1, N_MOD * D)
        n_qk = 2 * DIFF_WIDTH
        w_in_bf = jnp.concatenate([_permute_qk_columns(w_in[l][:, :n_qk]), w_in[l][:, n_qk:]], axis=1).astype(bf)
        z = _in_proj(x2, g_pre_mix[l][None], mod3, w_in_bf, tabs, S)
        lams = [v[l][None] for v in (lambda_q1, lambda_k1, lambda_q2, lambda_k2)]
        lam_init = 0.8 - 0.6 * math.exp(-0.3 * l)
        o = _diff_attention(z, lams, g_subln[l][None], lam_init, B, S)
        p = _pool(z, w_pool[l], pool_scale[l][None], B, S)
        x1, h2, ls, topw, cnt = _mix(
            x2, o, p, z, w_proj_a[l].astype(bf), w_proj_b[l].astype(bf), w_out[l].astype(bf),
            g_post_mix[l][None], mod3, g_pre_ffn[l][None], w_router[l].T, b_router[l], S)

        i32 = jnp.int32
        n = cnt[:, :, 0].astype(i32)
        n_tok_tiles = n.shape[0]
        counts = jnp.sum(n, axis=0)
        padded = (counts + MOE_TILE - 1) // MOE_TILE * MOE_TILE
        gend = jnp.cumsum(padded).astype(i32)
        gstart = gend - padded
        local_end = jnp.cumsum(n, axis=1)
        slot_off = gstart[None, :] + jnp.cumsum(n, axis=0) - n
        chunk_row = jnp.arange(BLOCK_CHUNKS, dtype=i32) * ROW_ALIGN
        piece = jnp.sum(local_end[:, None, :] <= chunk_row[None, :, None], axis=2)
        in_piece = piece[:, :, None] == jnp.arange(N_EXPERTS, dtype=i32)[None, None, :]
        shift = jnp.sum(jnp.where(in_piece, (slot_off - (local_end - n))[:, None, :], 0), axis=2)
        chunk_slot = (shift + chunk_row[None, :]).reshape(-1).astype(i32)
        n_chunks = (local_end[:, -1] // ROW_ALIGN).astype(i32)
        steps_pad = -N_EXPERTS % n_tok_tiles
        group_pad_off = jnp.pad(gstart + counts, (0, steps_pad)).astype(i32)
        group_pad_len = jnp.pad(padded - counts, (0, steps_pad)).astype(i32)
        n_slots = -(-(T * TOP_K + n_tok_tiles * N_EXPERTS * ROW_ALIGN) // MOE_TILE) * MOE_TILE + N_EXPERTS * MOE_TILE
        tile_start = jnp.arange(n_slots // MOE_TILE, dtype=i32) * MOE_TILE
        tile_expert = jnp.minimum(jnp.sum(tile_start[:, None] >= gend[None, :], axis=1), N_EXPERTS - 1)
        n_tiles = gend[-1:] // MOE_TILE
        has_tiles = padded > 0
        expert_ids = jnp.arange(N_EXPERTS, dtype=i32)
        later = lax.cummin(jnp.where(has_tiles, expert_ids, N_EXPERTS), reverse=True)
        next_expert = jnp.concatenate([later[1:], jnp.full((1,), N_EXPERTS, i32)])
        next_expert = jnp.where(next_expert < N_EXPERTS, next_expert, -1)
        of_tile = tile_expert[:, None] == expert_ids[None, :]
        pick = lambda per_expert: jnp.sum(jnp.where(of_tile, per_expert[None, :], 0), axis=1).astype(i32)
        tile_buf = pick((jnp.cumsum(has_tiles.astype(i32)) - 1) % 2)
        tile_next = pick(next_expert)
        tile_rows = jnp.clip(pick(gstart + counts) - tile_start, 0, MOE_TILE).astype(i32)

        xs = _dispatch(chunk_slot, n_chunks, group_pad_off, group_pad_len, n_tiles, h2, ls, n_slots)
        half = D_FF
        b1g = b_exp1[l].reshape(N_EXPERTS, half // LANES, LANES, 2).transpose(0, 1, 3, 2)
        b1g = b1g.reshape(N_EXPERTS, 1, 2 * half)
        ys = _moe(tile_expert.astype(i32), n_tiles, tile_buf, tile_next, tile_rows, xs,
                  w_exp1[l], b1g, w_exp2[l], b_exp2[l])
        x = _combine(chunk_slot, n_chunks, ys, ls.T, topw.T, x1, mod3, g_post_ffn[l][None], S)
        x = x.reshape(B, S, D)
    return x
```

```python
import functools
import math

import numpy as np
import jax
import jax.numpy as jnp
from jax import lax
from jax.experimental import pallas as pl
from jax.experimental.pallas import tpu as pltpu

D_MODEL = 1024
N_HEADS = 8
HEAD_DIM = 64
V_DIM = 2 * HEAD_DIM
DIFF_WIDTH = N_HEADS * V_DIM
POOL_WINDOWS = (2, 4, 8, 16)
POOL_GROUP_DIM = 128
POOL_WIDTH = len(POOL_WINDOWS) * POOL_GROUP_DIM
IN_COLS = 3 * DIFF_WIDTH + POOL_WIDTH + 2 * D_MODEL
ROPE_THETA = 500000.0
ROT_DIM = HEAD_DIM // 4
ROT_HALF = ROT_DIM // 2
N_EXPERTS = 32
TOP_K = 4
D_FF = D_MODEL
SWIGLU_ALPHA = 1.702
SWIGLU_LIMIT = 7.0
NORM_EPS = 1e-6
SUBLN_EPS = 1e-5
N_MOD = 6
NEG_BIG = -1e30

LANES = 128
MXU_DIM = 256
VMEM_LIMIT = 56 * 1024 * 1024

ADA_COL_TILE = 1536
ROPE_ROW_TILE = 2048
COL_TILE = 512
IN_ROW_TILE = 512
COL_K = DIFF_WIDTH // COL_TILE
COL_V = 2 * DIFF_WIDTH // COL_TILE
COL_G = (3 * DIFF_WIDTH + POOL_WIDTH) // COL_TILE
ATT_BLOCK = 512
ATT_HEADS = 4
MOE_TILE = 512
TOKEN_TILE = 256
MIX_TILES = 2
ROW_ALIGN = 16
LOCAL_ROWS = TOP_K * TOKEN_TILE + N_EXPERTS * ROW_ALIGN
BLOCK_CHUNKS = LOCAL_ROWS // ROW_ALIGN
CHUNK_UNROLL = 4

_HI = lax.Precision.HIGHEST


def _params(sem, vmem=VMEM_LIMIT):
    return pltpu.CompilerParams(dimension_semantics=sem, vmem_limit_bytes=vmem)


def _rms(x, eps):
    return x * lax.rsqrt(jnp.mean(x * x, axis=-1, keepdims=True) + eps)


def _ada_kernel(c_ref, w_ref, b_ref, o_ref):
    c = c_ref[...]
    s = c * jax.nn.sigmoid(c)
    o_ref[...] = jnp.dot(s, w_ref[...], precision=_HI, preferred_element_type=jnp.float32) + b_ref[...]


def _ada(c, w_ada, b_ada):
    B, D = c.shape
    N = w_ada.shape[1]
    tn = ADA_COL_TILE
    return pl.pallas_call(
        _ada_kernel,
        grid=(N // tn,),
        in_specs=[pl.BlockSpec((B, D), lambda j: (0, 0)),
                  pl.BlockSpec((D, tn), lambda j: (0, j)),
                  pl.BlockSpec((1, tn), lambda j: (0, j))],
        out_specs=pl.BlockSpec((B, tn), lambda j: (0, j)),
        out_shape=jax.ShapeDtypeStruct((B, N), jnp.float32),
        compiler_params=_params(("parallel",)),
        name="ada",
    )(c, w_ada, b_ada.reshape(1, N))


def _rope_tab_kernel(pos_ref, inv_ref, c_ref, s_ref):
    lane = lax.broadcasted_iota(jnp.int32, (LANES, LANES), 1)
    lower = lane < LANES // 2
    rot = lane % (LANES // 2) < ROT_DIM
    reps = LANES // ROT_HALF
    for r in range(pos_ref.shape[0]):
        ang = inv_ref[...] * pos_ref[r:r + 1, :].astype(jnp.float32)
        cos_t = jnp.tile(jnp.cos(ang), (reps, 1)).T
        sin_t = jnp.tile(jnp.sin(ang), (reps, 1)).T
        rows = slice(r * LANES, (r + 1) * LANES)
        c_ref[rows, :] = jnp.where(rot, cos_t, 1.0)
        s_ref[rows, :] = jnp.where(rot, jnp.where(lower, -sin_t, sin_t), 0.0)


def _rope_tables(positions):
    T = positions.size
    tm = min(T, ROPE_ROW_TILE)
    inv = ROPE_THETA ** (-(np.arange(ROT_HALF, dtype=np.float64) * 2.0 / ROT_DIM))
    inv = np.broadcast_to(inv.astype(np.float32)[:, None], (ROT_HALF, LANES))
    tab = pl.BlockSpec((tm, LANES), lambda i: (i, 0))
    sds = jax.ShapeDtypeStruct((T, LANES), jnp.float32)
    return pl.pallas_call(
        _rope_tab_kernel,
        grid=(T // tm,),
        in_specs=[pl.BlockSpec((tm // LANES, LANES), lambda i: (i, 0)),
                  pl.BlockSpec((ROT_HALF, LANES), lambda i: (0, 0))],
        out_specs=[tab, tab],
        out_shape=[sds, sds],
        compiler_params=_params(("parallel",)),
        name="rope_tables",
    )(positions.reshape(T // LANES, LANES), jnp.asarray(inv))


def _permute_qk_columns(w):
    D, n = w.shape
    w = w.reshape(D, n // V_DIM, 2, 4, 2, ROT_HALF)
    return w.transpose(0, 1, 4, 3, 2, 5).reshape(D, n)


def _in_proj_kernel(x_ref, g_ref, sc_ref, sh_ref, w_ref, c_ref, s_ref, z_ref):
    h = (_rms(x_ref[...], NORM_EPS) * g_ref[...] * (1.0 + sc_ref[0]) + sh_ref[0]).astype(jnp.bfloat16)
    c, s = c_ref[...], s_ref[...]
    q_scale = HEAD_DIM ** -0.5 * math.log2(math.e)
    cq, sq = c * q_scale, s * q_scale
    for j in range(IN_COLS // COL_TILE):
        cols = slice(j * COL_TILE, (j + 1) * COL_TILE)
        z = jnp.dot(h, w_ref[:, cols], preferred_element_type=jnp.float32)
        if j < COL_V:
            cj, sj = (cq, sq) if j < COL_K else (c, s)
            parts = []
            for g in range(COL_TILE // LANES):
                zg = z[:, g * LANES:(g + 1) * LANES]
                parts.append(zg * cj + pltpu.roll(zg, LANES // 2, 1) * sj)
            z = jnp.concatenate(parts, axis=1)
        elif j >= COL_G:
            z = 0.5 * jnp.tanh(0.5 * z) + 0.5
        z_ref[:, cols] = z.astype(z_ref.dtype)


def _in_proj(x2, g_pre, mod3, w_in_bf, tabs, seq):
    T, D = x2.shape
    tm = min(seq, IN_ROW_TILE)
    per_b = seq // tm
    mod_spec = lambda col: pl.BlockSpec((1, 1, D), lambda i: (i // per_b, 0, col))
    tab = pl.BlockSpec((tm, LANES), lambda i: (i, 0))
    return pl.pallas_call(
        _in_proj_kernel,
        grid=(T // tm,),
        in_specs=[pl.BlockSpec((tm, D), lambda i: (i, 0)),
                  pl.BlockSpec((1, D), lambda i: (0, 0)),
                  mod_spec(1), mod_spec(0),
                  pl.BlockSpec((D, IN_COLS), lambda i: (0, 0)),
                  tab, tab],
        out_specs=pl.BlockSpec((tm, IN_COLS), lambda i: (i, 0)),
        out_shape=jax.ShapeDtypeStruct((T, IN_COLS), jnp.bfloat16),
        compiler_params=_params(("parallel",)),
        name="in_proj",
    )(x2, g_pre, mod3, mod3, w_in_bf, *tabs)


def _attn_kernel(lq1_ref, lk1_ref, lq2_ref, lk2_ref, gs_ref, q_ref, k_ref, v_ref, o_ref, vt_ref, *acc_refs,
                 lam_init):
    i = pl.program_id(2)
    blk = q_ref.shape[0]
    nblk = v_ref.shape[0] // blk
    chains = [(h, comp) for h in range(ATT_HEADS) for comp in range(2)]

    @pl.when(i == 0)
    def _():
        for c in range(nblk):
            for h in range(ATT_HEADS):
                v = v_ref[c * blk:(c + 1) * blk, h * V_DIM:(h + 1) * V_DIM]
                vt_ref[c, h] = v.astype(jnp.float32).T.astype(vt_ref.dtype)

    lam = (jnp.exp(jnp.sum(lq1_ref[...] * lk1_ref[...], axis=-1, keepdims=True))
           - jnp.exp(jnp.sum(lq2_ref[...] * lk2_ref[...], axis=-1, keepdims=True))
           + lam_init)
    lane_comp = lax.broadcasted_iota(jnp.int32, (blk, V_DIM), 1) // ROT_HALF % 2
    qs = []
    for h, comp in chains:
        q = q_ref[:, h * V_DIM:(h + 1) * V_DIM]
        qs.append(jnp.where(lane_comp == comp, q, jnp.zeros_like(q)))
    nt = (((1,), (1,)), ((), ()))
    for acc_ref in acc_refs:
        acc_ref[...] = jnp.zeros_like(acc_ref)

    def step(c, carry, mask=None):
        off = pl.multiple_of(c * blk, blk)
        scores = []
        for n, (h, comp) in enumerate(chains):
            k = k_ref[pl.ds(off, blk), h * V_DIM:(h + 1) * V_DIM]
            scores.append(lax.dot_general(k, qs[n], nt, preferred_element_type=jnp.float32))
        out, probs, alphas = [], [], []
        for n, s in enumerate(scores):
            m, l = carry[n]
            if mask is not None:
                s = jnp.where(mask, s, NEG_BIG)
            m_new = jnp.maximum(m, jnp.max(s, axis=0, keepdims=True))
            alpha = jnp.exp2(m - m_new)
            p = jnp.exp2(s - m_new)
            out.append((m_new, alpha * l + jnp.sum(p, axis=0, keepdims=True)))
            probs.append(p.astype(vt_ref.dtype))
            alphas.append(alpha)
        for n, (h, comp) in enumerate(chains):
            pv = jnp.dot(vt_ref[c, h], probs[n], preferred_element_type=jnp.float32)
            acc_refs[n][...] = alphas[n] * acc_refs[n][...] + pv
        return tuple(out)

    init = tuple((jnp.full((1, blk), NEG_BIG, jnp.float32), jnp.zeros((1, blk), jnp.float32))
                 for _ in chains)
    carry = lax.fori_loop(0, i, step, init)

    key = lax.broadcasted_iota(jnp.int32, (blk, blk), 0)
    qry = lax.broadcasted_iota(jnp.int32, (blk, blk), 1)
    carry = step(i, carry, key <= qry)
    for h in range(ATT_HEADS):
        (_, l1), (_, l2) = carry[2 * h], carry[2 * h + 1]
        ot = acc_refs[2 * h][...] / l1 - lam * (acc_refs[2 * h + 1][...] / l2)
        o = _rms(ot.T, SUBLN_EPS) * gs_ref[...] * (1.0 - lam_init)
        o_ref[:, h * V_DIM:(h + 1) * V_DIM] = o.astype(o_ref.dtype)


def _diff_attention(z, lams, g_subln, lam_init, batch, seq):
    T = z.shape[0]
    blk = min(ATT_BLOCK, seq)
    nq = seq // blk
    width = ATT_HEADS * V_DIM
    vec = pl.BlockSpec((1, HEAD_DIM), lambda b, h, i: (0, 0))
    kcol = DIFF_WIDTH // width
    return pl.pallas_call(
        functools.partial(_attn_kernel, lam_init=lam_init),
        grid=(batch, N_HEADS // ATT_HEADS, nq),
        in_specs=[vec, vec, vec, vec,
                  pl.BlockSpec((1, V_DIM), lambda b, h, i: (0, 0)),
                  pl.BlockSpec((blk, width), lambda b, h, i: (b * nq + i, h)),
                  pl.BlockSpec((seq, width), lambda b, h, i: (b, kcol + h)),
                  pl.BlockSpec((seq, width), lambda b, h, i: (b, 2 * kcol + h))],
        out_specs=pl.BlockSpec((blk, width), lambda b, h, i: (b * nq + i, h)),
        out_shape=jax.ShapeDtypeStruct((T, DIFF_WIDTH), jnp.bfloat16),
        scratch_shapes=[pltpu.VMEM((nq, ATT_HEADS, V_DIM, blk), jnp.bfloat16)]
        + [pltpu.VMEM((V_DIM, blk), jnp.float32)] * (2 * ATT_HEADS),
        compiler_params=_params(("parallel", "parallel", "arbitrary")),
        name="diff_attn",
    )(*lams, g_subln, z, z, z)


def _pool_kernel(u_ref, w_ref, ps_ref, o_ref):
    t = lax.broadcasted_iota(jnp.int32, (u_ref.shape[0], POOL_GROUP_DIM), 0)

    def shifted(x, k):
        return jnp.where(t >= k, pltpu.roll(x, k, 0), 0.0)

    for g, window in enumerate(POOL_WINDOWS):
        cols = slice(g * POOL_GROUP_DIM, (g + 1) * POOL_GROUP_DIM)
        u = u_ref[:, cols].astype(jnp.float32)
        s, k = u, 1
        while k < window:
            s = s + shifted(s, k)
            k *= 2
        d = s / jnp.minimum(t + 1, window).astype(jnp.float32) - u
        y = jnp.dot(d.astype(jnp.bfloat16), w_ref[g].astype(jnp.bfloat16), preferred_element_type=jnp.float32)
        o_ref[:, cols] = (y * ps_ref[:, cols]).astype(o_ref.dtype)


def _pool(z, w_pool, pool_scale, batch, seq):
    T = z.shape[0]
    G = len(POOL_WINDOWS)
    ucol = 3 * DIFF_WIDTH // POOL_WIDTH
    return pl.pallas_call(
        _pool_kernel,
        grid=(batch,),
        in_specs=[pl.BlockSpec((seq, POOL_WIDTH), lambda b: (b, ucol)),
                  pl.BlockSpec((G, POOL_GROUP_DIM, POOL_GROUP_DIM), lambda b: (0, 0, 0)),
                  pl.BlockSpec((1, POOL_WIDTH), lambda b: (0, 0))],
        out_specs=pl.BlockSpec((seq, POOL_WIDTH), lambda b: (b, 0)),
        out_shape=jax.ShapeDtypeStruct((T, POOL_WIDTH), jnp.bfloat16),
        compiler_params=_params(("parallel",)),
        name="pool",
    )(z, w_pool, pool_scale)


def _mix_kernel(x_ref, o_ref, p_ref, ga0, ga1, gb0, gb1, wa_ref, wb_ref, wo_ref, gpm_ref, gt1_ref,
                gpf_ref, sc2_ref, sh2_ref, wr_ref, br_ref,
                x1_ref, h2_ref, ls_ref, tw_ref, cnt_ref):
    tiles = range(x_ref.shape[0] // TOKEN_TILE)
    h2s = [_mix_front(t, x_ref, o_ref, p_ref, ga0, ga1, gb0, gb1, wa_ref, wb_ref, wo_ref, gpm_ref, gt1_ref,
                      gpf_ref, sc2_ref, sh2_ref, x1_ref) for t in tiles]
    for t in tiles:
        h2_ref[t * TOKEN_TILE:(t + 1) * TOKEN_TILE, :] = h2s[t].astype(h2_ref.dtype)
    for t in tiles:
        _mix_route(t, h2s[t], wr_ref, br_ref, ls_ref, tw_ref, cnt_ref)


def _mix_front(t, x_ref, o_ref, p_ref, ga0, ga1, gb0, gb1, wa_ref, wb_ref, wo_ref, gpm_ref, gt1_ref,
               gpf_ref, sc2_ref, sh2_ref, x1_ref):
    rows = slice(t * TOKEN_TILE, (t + 1) * TOKEN_TILE)
    ya = jnp.dot(o_ref[rows, :], wa_ref[...], preferred_element_type=jnp.float32)
    yb = jnp.dot(p_ref[rows, :], wb_ref[...], preferred_element_type=jnp.float32)
    ga = jnp.concatenate([ga0[rows, :], ga1[rows, :]], axis=1).astype(jnp.float32)
    gb = jnp.concatenate([gb0[rows, :], gb1[rows, :]], axis=1).astype(jnp.float32)
    merged = (ga * ya + gb * yb).astype(jnp.bfloat16)
    mixed = jnp.dot(merged, wo_ref[...], preferred_element_type=jnp.float32)
    x1 = x_ref[rows, :] + gt1_ref[0] * (_rms(mixed, NORM_EPS) * gpm_ref[...])
    x1_ref[rows, :] = x1
    return _rms(x1, NORM_EPS) * gpf_ref[...] * (1.0 + sc2_ref[0]) + sh2_ref[0]


def _mix_route(t, h2, wr_ref, br_ref, ls_ref, tw_ref, cnt_ref):
    tm = TOKEN_TILE
    rows = slice(t * tm, (t + 1) * tm)
    logits =lax.dot_general(wr_ref[...], h2, (((1,), (1,)), ((), ())), precision=_HI,
                             preferred_element_type=jnp.float32) + br_ref[...]
    eid = lax.broadcasted_iota(jnp.int32, logits.shape, 0)
    work = logits
    sels, vals = [], []
    for _ in range(TOP_K):
        mx = jnp.max(work, axis=0, keepdims=True)
        idx = jnp.min(jnp.where(work == mx, eid, N_EXPERTS), axis=0, keepdims=True)
        sel = eid == idx
        work = jnp.where(sel, -jnp.inf, work)
        sels.append(sel)
        vals.append(mx)
    ex = [jnp.exp(v - vals[0]) for v in vals]
    den = ex[0] + ex[1] + ex[2] + ex[3]
    onehot = jnp.zeros(logits.shape, jnp.float32)
    for sel in sels:
        onehot = jnp.where(sel, 1.0, onehot)
    r = lax.broadcasted_iota(jnp.int32, (tm, tm), 0)
    c = lax.broadcasted_iota(jnp.int32, (tm, tm), 1)
    tri = jnp.where(r < c, 1.0, 0.0).astype(jnp.bfloat16)
    rank = jnp.dot(onehot.astype(jnp.bfloat16), tri, preferred_element_type=jnp.float32)
    counts = jnp.broadcast_to(jnp.sum(onehot, axis=1, keepdims=True), (N_EXPERTS, LANES))
    rounded = jnp.floor((counts + (ROW_ALIGN - 1)) * (1.0 / ROW_ALIGN)) * ROW_ALIGN
    er = lax.broadcasted_iota(jnp.int32, (N_EXPERTS, N_EXPERTS), 0)
    ec = lax.broadcasted_iota(jnp.int32, (N_EXPERTS, N_EXPERTS), 1)
    below = jnp.where(ec < er, 1.0, 0.0).astype(jnp.bfloat16)
    offset = jnp.dot(below, rounded.astype(jnp.bfloat16), preferred_element_type=jnp.float32)
    slot_of = rank + offset[:, 0:1]
    cnt_ref[t] = rounded
    for kk in range(TOP_K):
        ls = jnp.sum(jnp.where(sels[kk], slot_of, 0.0), axis=0, keepdims=True).astype(jnp.int32)
        ls_ref[kk:kk + 1, rows] = ls
        tw_ref[kk:kk + 1, rows] = ex[kk] / den


def _mix(x2, o, p, z, wa, wb, wo, g_post_mix, mod3, g_pre_ffn, w_router_t, b_router, seq):
    T, D = x2.shape
    sub = min(MIX_TILES, seq // TOKEN_TILE)
    tm = sub * TOKEN_TILE
    per_b = seq // tm
    E = N_EXPERTS
    row = lambda n: pl.BlockSpec((1, n), lambda i: (0, 0))
    mod_spec = lambda col: pl.BlockSpec((1, 1, D), lambda i: (i // per_b, 0, col))
    gate = lambda cb: pl.BlockSpec((tm, COL_TILE), lambda i: (i, cb))
    full = lambda a: pl.BlockSpec(a.shape, lambda i: (0,) * a.ndim)
    tile = pl.BlockSpec((tm, D), lambda i: (i, 0))
    small = pl.BlockSpec((TOP_K, tm), lambda i: (0, i))
    return pl.pallas_call(
        _mix_kernel,
        grid=(T // tm,),
        in_specs=[tile, tile,
                  pl.BlockSpec((tm, POOL_WIDTH), lambda i: (i, 0)),
                  gate(COL_G), gate(COL_G + 1), gate(COL_G + 2), gate(COL_G + 3),
                  full(wa), full(wb), full(wo), row(D), mod_spec(2),
                  row(D), mod_spec(4), mod_spec(3), full(w_router_t),
                  pl.BlockSpec((E, 1), lambda i: (0, 0))],
        out_specs=[tile, tile, small, small,
                   pl.BlockSpec((sub, E, LANES), lambda i: (i, 0, 0))],
        out_shape=[jax.ShapeDtypeStruct((T, D), jnp.float32),
                   jax.ShapeDtypeStruct((T, D), jnp.bfloat16),
                   jax.ShapeDtypeStruct((TOP_K, T), jnp.int32),
                   jax.ShapeDtypeStruct((TOP_K, T), jnp.float32),
                   jax.ShapeDtypeStruct((T // TOKEN_TILE, E, LANES), jnp.float32)],
        compiler_params=_params(("parallel",)),
        name="mix_tail",
    )(x2, o, p, z, z, z, z, wa, wb, wo, g_post_mix, mod3, g_pre_ffn, mod3, mod3, w_router_t,
      b_router.reshape(E, 1))


def _chunk(ref, row):
    return ref.at[pl.ds(pl.multiple_of(row, ROW_ALIGN), ROW_ALIGN), :]


def _for_chunks(n, fn):
    def body(g, _):
        for u in range(CHUNK_UNROLL):
            fn(g * CHUNK_UNROLL + u)
        return 0
    lax.fori_loop(0, n // CHUNK_UNROLL, body, 0)
    lax.fori_loop(n // CHUNK_UNROLL * CHUNK_UNROLL, n, lambda k, _: (fn(k), 0)[1], 0)


def _dispatch_kernel(chunk_slot_ref, n_chunk_ref, zero_off_ref, zero_len_ref, ntile_ref,
                     h_ref, ls_ref, xs_ref, xl_ref, zero_ref, sem, zsem, *, zero_per_step):
    i = pl.program_id(0)
    last = pl.num_programs(0) - 1

    @pl.when(i == 0)
    def _():
        zero_ref[...] = jnp.zeros_like(zero_ref)

    slot_id = lax.broadcasted_iota(jnp.int32, (LOCAL_ROWS, h_ref.shape[0]), 0)
    place = jnp.zeros(slot_id.shape, jnp.float32)
    for kk in range(TOP_K):
        place = jnp.where(slot_id == ls_ref[kk:kk + 1, :], 1.0, place)
    xl = jnp.dot(place.astype(jnp.bfloat16), h_ref[...], preferred_element_type=jnp.float32)
    xl_ref[i % 2] = xl.astype(xl_ref.dtype)

    def rows_out(tile, act):
        buf = tile % 2

        def copy(k):
            act(pltpu.make_async_copy(_chunk(xl_ref.at[buf], k * ROW_ALIGN),
                                      _chunk(xs_ref, chunk_slot_ref[tile * BLOCK_CHUNKS + k]), sem.at[buf]))

        _for_chunks(n_chunk_ref[tile], copy)

    def copies(act):
        for r in range(zero_per_step):
            idx = i * zero_per_step + r

            def zbody(k, _, idx=idx):
                act(pltpu.make_async_copy(_chunk(zero_ref, 0),
                                          _chunk(xs_ref, zero_off_ref[idx] + k * ROW_ALIGN), zsem))
                return 0
            lax.fori_loop(0, zero_len_ref[idx] // ROW_ALIGN, zbody, 0)

    def tails(act):
        def body(t, _):
            off = pl.multiple_of(t * MOE_TILE, MOE_TILE)
            act(pltpu.make_async_copy(zero_ref, xs_ref.at[pl.ds(off, MOE_TILE), :], zsem))
            return 0
        lax.fori_loop(ntile_ref[0], xs_ref.shape[0] // MOE_TILE, body, 0)

    start, wait = (lambda cp: cp.start()), (lambda cp: cp.wait())
    rows_out(i, start)
    copies(start)
    pl.when(i == 0)(functools.partial(tails, start))
    pl.when(i > 0)(lambda: rows_out(i - 1, wait))
    copies(wait)
    pl.when(i == 0)(functools.partial(tails, wait))
    pl.when(i == last)(lambda: rows_out(i, wait))


def _dispatch(chunk_slot, n_chunks, zero_off, zero_len, n_tiles, h2, ls, n_slots):
    T, W = h2.shape
    tc = TOKEN_TILE
    steps = n_chunks.shape[0]
    return pl.pallas_call(
        functools.partial(_dispatch_kernel, zero_per_step=zero_len.shape[0] // steps),
        grid_spec=pltpu.PrefetchScalarGridSpec(
            num_scalar_prefetch=5,
            grid=(steps,),
            in_specs=[pl.BlockSpec((tc, W), lambda i, *_: (i, 0)),
                      pl.BlockSpec((TOP_K, tc), lambda i, *_: (0, i))],
            out_specs=pl.BlockSpec(memory_space=pl.ANY),
            scratch_shapes=[pltpu.VMEM((2, LOCAL_ROWS, W), h2.dtype),
                            pltpu.VMEM((MOE_TILE, W), h2.dtype),
                            pltpu.SemaphoreType.DMA((2,)), pltpu.SemaphoreType.DMA]),
        out_shape=jax.ShapeDtypeStruct((n_slots, W), h2.dtype),
        compiler_params=_params(("arbitrary",)),
        name="dispatch",
    )(chunk_slot, n_chunks, zero_off, zero_len, n_tiles, h2, ls)


def _moe_kernel(te_ref, nt_ref, buf_ref, nxt_ref, rows_ref, x_ref, w1_hbm, b1_ref, w2_hbm, b2_ref, perm_ref, y_ref,
                w1raw_ref, w2raw_ref, w1s_ref, w2s_ref, sem):
    j = pl.program_id(0)
    prev = te_ref[jnp.maximum(j - 1, 0)]
    fresh = (j == 0) | (te_ref[j] != prev)
    nblk = w1s_ref.shape[1] // MXU_DIM

    def weight_copies(e, b):
        return (pltpu.make_async_copy(w1_hbm.at[e], w1raw_ref.at[b], sem.at[0, b]),
                pltpu.make_async_copy(w2_hbm.at[e], w2raw_ref.at[b], sem.at[1, b]))

    @pl.when(fresh & (j < nt_ref[0]))
    def _():
        e, b, nxt = te_ref[j], buf_ref[j], nxt_ref[j]

        @pl.when(j == 0)
        def _():
            for cp in weight_copies(e, b):
                cp.start()

        for cp in weight_copies(e, b):
            cp.wait()

        @pl.when(nxt >= 0)
        def _():
            for cp in weight_copies(nxt, 1 - b):
                cp.start()

        for c in range(nblk):
            cols = slice(c * MXU_DIM, (c + 1) * MXU_DIM)
            blk = w1raw_ref[b, :, cols].astype(jnp.bfloat16)
            w1s_ref[:, cols] = jnp.dot(blk, perm_ref[...],
                                       preferred_element_type=jnp.float32).astype(jnp.bfloat16)
        w2s_ref[...] = w2raw_ref[b].astype(jnp.bfloat16)

    def experts_mlp(rows):
        z = jnp.dot(x_ref[:rows, :], w1s_ref[...], preferred_element_type=jnp.float32) + b1_ref[0]
        acts = []
        for c in range(nblk):
            gate = jnp.minimum(z[:, c * MXU_DIM:c * MXU_DIM + LANES], SWIGLU_LIMIT)
            up = jnp.clip(z[:, c * MXU_DIM + LANES:(c + 1) * MXU_DIM], -SWIGLU_LIMIT, SWIGLU_LIMIT)
            acts.append(gate * jax.nn.sigmoid(SWIGLU_ALPHA * gate) * (up + 1.0))
        a = jnp.concatenate(acts, axis=1).astype(jnp.bfloat16)
        y = jnp.dot(a, w2s_ref[...], preferred_element_type=jnp.float32) + b2_ref[0]
        y_ref[:rows, :] = y.astype(y_ref.dtype)
        if rows < y_ref.shape[0]:
            y_ref[rows:, :] = jnp.zeros((y_ref.shape[0] - rows, y_ref.shape[1]), y_ref.dtype)

    half = y_ref.shape[0] // 2
    active = j < nt_ref[0]
    pl.when(active & (rows_ref[j] > half))(lambda: experts_mlp(2 * half))
    pl.when(active & (rows_ref[j] <= half))(lambda: experts_mlp(half))


def _regroup_perm():
    src = np.arange(MXU_DIM)
    dst = np.where(src % 2 == 0, src // 2, LANES + src // 2)
    perm = np.zeros((MXU_DIM, MXU_DIM), np.float32)
    perm[src, dst] = 1.0
    return jnp.asarray(perm, jnp.bfloat16)


def _moe(tile_expert, n_tiles, tile_buf, tile_next, tile_rows, xs, w1, b1g, w2, b2):
    n_slots, W = xs.shape
    tm = MOE_TILE
    E, D, F2 = w1.shape
    F = w2.shape[1]
    xmap = lambda j, te, nt, *_: (jnp.minimum(j, nt[0] - 1), 0)
    emap = lambda j, te, *_: (te[j], 0, 0)
    return pl.pallas_call(
        _moe_kernel,
        grid_spec=pltpu.PrefetchScalarGridSpec(
            num_scalar_prefetch=5,
            grid=(n_slots // tm,),
            in_specs=[pl.BlockSpec((tm, W), xmap),
                      pl.BlockSpec(memory_space=pl.ANY),
                      pl.BlockSpec((1, 1, F2), emap),
                      pl.BlockSpec(memory_space=pl.ANY),
                      pl.BlockSpec((1, 1, D), emap),
                      pl.BlockSpec((MXU_DIM, MXU_DIM), lambda j, *_: (0, 0))],
            out_specs=pl.BlockSpec((tm, W), xmap),
            scratch_shapes=[pltpu.VMEM((2, D, F2), w1.dtype), pltpu.VMEM((2, F, D), w2.dtype),
                            pltpu.VMEM((D, F2), jnp.bfloat16), pltpu.VMEM((F, D), jnp.bfloat16),
                            pltpu.SemaphoreType.DMA((2, 2))]),
        out_shape=jax.ShapeDtypeStruct((n_slots, W), xs.dtype),
        input_output_aliases={5: 0},
        compiler_params=_params(("arbitrary",)),
        name="moe_experts",
    )(tile_expert, n_tiles, tile_buf, tile_next, tile_rows, xs, w1, b1g, w2, b2.reshape(E, 1, D),
      _regroup_perm())


def _combine_kernel(chunk_slot_ref, n_chunk_ref, ys_ref, ls_ref, w_ref, x1_ref, gt2_ref, g_ref,
                    o_ref, yl_ref, sem):
    i = pl.program_id(0)
    tc = x1_ref.shape[0]

    def pieces(tile, act):
        buf = tile % 2

        def copy(k):
            act(pltpu.make_async_copy(_chunk(ys_ref, chunk_slot_ref[tile * BLOCK_CHUNKS + k]),
                                      _chunk(yl_ref.at[buf], k * ROW_ALIGN), sem.at[buf]))

        _for_chunks(n_chunk_ref[tile], copy)

    def fetch(tile):
        yl_ref[tile % 2, TOP_K * tc:, :] = jnp.zeros((LOCAL_ROWS - TOP_K * tc, yl_ref.shape[2]), yl_ref.dtype)
        pieces(tile, lambda cp: cp.start())

    pl.when(i == 0)(lambda: fetch(i))
    pl.when(i + 1 < pl.num_programs(0))(lambda: fetch(i + 1))
    slot_id = lax.broadcasted_iota(jnp.int32, (tc, LOCAL_ROWS), 1)
    ls = ls_ref[...]
    w = w_ref[...]
    mix = jnp.zeros((tc, LOCAL_ROWS), jnp.float32)
    for k in range(TOP_K):
        mix = jnp.where(slot_id == ls[:, k:k + 1], w[:, k:k + 1], mix)
    pieces(i, lambda cp: cp.wait())
    f = jnp.dot(mix.astype(jnp.bfloat16), yl_ref[i % 2], preferred_element_type=jnp.float32)
    o_ref[...] = x1_ref[...] + gt2_ref[0] * (_rms(f, NORM_EPS) * g_ref[...])


def _combine(chunk_slot, n_chunks, ys, ls_t, topw_t, x1, mod3, g_post_ffn, seq):
    T, D = x1.shape
    tc = TOKEN_TILE
    per_b = seq // tc
    tile = pl.BlockSpec((tc, D), lambda i, *_: (i, 0))
    small = pl.BlockSpec((tc, TOP_K), lambda i, *_: (i, 0))
    return pl.pallas_call(
        _combine_kernel,
        grid_spec=pltpu.PrefetchScalarGridSpec(
            num_scalar_prefetch=2,
            grid=(T // tc,),
            in_specs=[pl.BlockSpec(memory_space=pl.ANY), small, small, tile,
                      pl.BlockSpec((1, 1, D), lambda i, *_: (i // per_b, 0, 5)),
                      pl.BlockSpec((1, D), lambda i, *_: (0, 0))],
            out_specs=tile,
            scratch_shapes=[pltpu.VMEM((2, LOCAL_ROWS, ys.shape[1]), ys.dtype),
                            pltpu.SemaphoreType.DMA((2,))]),
        out_shape=jax.ShapeDtypeStruct((T, D), jnp.float32),
        compiler_params=_params(("arbitrary",)),
        name="combine",
    )(chunk_slot, n_chunks, ys, ls_t, topw_t, x1, mod3, g_post_ffn)


def kernel(x, c, positions, w_ada, b_ada, g_pre_mix, w_in, lambda_q1, lambda_k1, lambda_q2, lambda_k2,
           g_subln, w_pool, pool_scale, w_proj_a, w_proj_b, w_out, g_post_mix, g_pre_ffn,
           w_router, b_router, w_exp1, b_exp1, w_exp2, b_exp2, g_post_ffn):
    B, S, D = x.shape
    T = B * S
    bf = jnp.bfloat16
    tabs = _rope_tables(positions)
    for l in range(w_ada.shape[0]):
        x2 = x.reshape(T, D)
        mod3 = _ada(c, w_ada[l], b_ada[l]).reshape(B, 1, N_MOD * D)
        n_qk = 2 * DIFF_WIDTH
        w_in_bf = jnp.concatenate([_permute_qk_columns(w_in[l][:, :n_qk]), w_in[l][:, n_qk:]], axis=1).astype(bf)
        z = _in_proj(x2, g_pre_mix[l][None], mod3, w_in_bf, tabs, S)
        lams = [v[l][None] for v in (lambda_q1, lambda_k1, lambda_q2, lambda_k2)]
        lam_init = 0.8 - 0.6 * math.exp(-0.3 * l)
        o = _diff_attention(z, lams, g_subln[l][None], lam_init, B, S)
        p = _pool(z, w_pool[l], pool_scale[l][None], B, S)
        x1, h2, ls, topw, cnt = _mix(
            x2, o, p, z, w_proj_a[l].astype(bf), w_proj_b[l].astype(bf), w_out[l].astype(bf),
            g_post_mix[l][None], mod3, g_pre_ffn[l][None], w_router[l].T, b_router[l], S)

        i32 = jnp.int32
        n = cnt[:, :, 0].astype(i32)
        n_tok_tiles = n.shape[0]
        counts = jnp.sum(n, axis=0)
        padded = (counts + MOE_TILE - 1) // MOE_TILE * MOE_TILE
        gend = jnp.cumsum(padded).astype(i32)
        gstart = gend - padded
        local_end = jnp.cumsum(n, axis=1)
        slot_off = gstart[None, :] + jnp.cumsum(n, axis=0) - n
        chunk_row = jnp.arange(BLOCK_CHUNKS, dtype=i32) * ROW_ALIGN
        piece = jnp.sum(local_end[:, None, :] <= chunk_row[None, :, None], axis=2)
        in_piece = piece[:, :, None] == jnp.arange(N_EXPERTS, dtype=i32)[None, None, :]
        shift = jnp.sum(jnp.where(in_piece, (slot_off - (local_end - n))[:, None, :], 0), axis=2)
        chunk_slot = (shift + chunk_row[None, :]).reshape(-1).astype(i32)
        n_chunks = (local_end[:, -1] // ROW_ALIGN).astype(i32)
        steps_pad = -N_EXPERTS % n_tok_tiles
        group_pad_off = jnp.pad(gstart + counts, (0, steps_pad)).astype(i32)
        group_pad_len = jnp.pad(padded - counts, (0, steps_pad)).astype(i32)
        n_slots = -(-(T * TOP_K + n_tok_tiles * N_EXPERTS * ROW_ALIGN) // MOE_TILE) * MOE_TILE + N_EXPERTS * MOE_TILE
        tile_start = jnp.arange(n_slots // MOE_TILE, dtype=i32) * MOE_TILE
        tile_expert = jnp.minimum(jnp.sum(tile_start[:, None] >= gend[None, :], axis=1), N_EXPERTS - 1)
        n_tiles = gend[-1:] // MOE_TILE
        has_tiles = padded > 0
        expert_ids = jnp.arange(N_EXPERTS, dtype=i32)
        later = lax.cummin(jnp.where(has_tiles, expert_ids, N_EXPERTS), reverse=True)
        next_expert = jnp.concatenate([later[1:], jnp.full((1,), N_EXPERTS, i32)])
        next_expert = jnp.where(next_expert < N_EXPERTS, next_expert, -1)
        of_tile = tile_expert[:, None] == expert_ids[None, :]
        pick = lambda per_expert: jnp.sum(jnp.where(of_tile, per_expert[None, :], 0), axis=1).astype(i32)
        tile_buf = pick((jnp.cumsum(has_tiles.astype(i32)) - 1) % 2)
        tile_next = pick(next_expert)
        tile_rows = jnp.clip(pick(gstart + counts) - tile_start, 0, MOE_TILE).astype(i32)

        xs = _dispatch(chunk_slot, n_chunks, group_pad_off, group_pad_len, n_tiles, h2, ls, n_slots)
        half = D_FF
        b1g = b_exp1[l].reshape(N_EXPERTS, half // LANES, LANES, 2).transpose(0, 1, 3, 2)
        b1g = b1g.reshape(N_EXPERTS, 1, 2 * half)
        ys = _moe(tile_expert.astype(i32), n_tiles, tile_buf, tile_next, tile_rows, xs,
                  w_exp1[l], b1g, w_exp2[l], b_exp2[l])
        x = _combine(chunk_slot, n_chunks, ys, ls.T, topw.T, x1, mod3, g_post_ffn[l][None], S)
        x = x.reshape(B, S, D)
    return x
```

```python
import functools
import math

import numpy as np
import jax
import jax.numpy as jnp
from jax import lax
from jax.experimental import pallas as pl
from jax.experimental.pallas import tpu as pltpu

D_MODEL = 1024
N_HEADS = 8
HEAD_DIM = 64
V_DIM = 2 * HEAD_DIM
DIFF_WIDTH = N_HEADS * V_DIM
POOL_WINDOWS = (2, 4, 8, 16)
POOL_GROUP_DIM = 128
POOL_WIDTH = len(POOL_WINDOWS) * POOL_GROUP_DIM
IN_COLS = 3 * DIFF_WIDTH + POOL_WIDTH + 2 * D_MODEL
ROPE_THETA = 500000.0
ROT_DIM = HEAD_DIM // 4
ROT_HALF = ROT_DIM // 2
N_EXPERTS = 32
TOP_K = 4
D_FF = D_MODEL
SWIGLU_ALPHA = 1.702
SWIGLU_LIMIT = 7.0
NORM_EPS = 1e-6
SUBLN_EPS = 1e-5
N_MOD = 6
NEG_BIG = -1e30

LANES = 128
MXU_DIM = 256
VMEM_LIMIT = 56 * 1024 * 1024

ADA_COL_TILE = 1536
ROPE_ROW_TILE = 2048
COL_TILE = 512
IN_ROW_TILE = 512
COL_K = DIFF_WIDTH // COL_TILE
COL_V = 2 * DIFF_WIDTH // COL_TILE
COL_G = (3 * DIFF_WIDTH + POOL_WIDTH) // COL_TILE
ATT_BLOCK = 512
ATT_HEADS = 4
MOE_TILE = 512
TOKEN_TILE = 256
MIX_TILES = 2
ROW_ALIGN = 16
LOCAL_ROWS = TOP_K * TOKEN_TILE + N_EXPERTS * ROW_ALIGN
BLOCK_CHUNKS = LOCAL_ROWS // ROW_ALIGN
CHUNK_UNROLL = 4

_HI = lax.Precision.HIGHEST


def _params(sem, vmem=VMEM_LIMIT):
    return pltpu.CompilerParams(dimension_semantics=sem, vmem_limit_bytes=vmem)


def _rms(x, eps):
    return x * lax.rsqrt(jnp.mean(x * x, axis=-1, keepdims=True) + eps)


def _split_bf16(x):
    hi = x.astype(jnp.bfloat16)
    return jnp.stack([hi, (x - hi.astype(jnp.float32)).astype(jnp.bfloat16)])


def _ada_kernel(c_ref, w_ref, b_ref, o_ref):
    c = c_ref[...]
    s = c * jax.nn.sigmoid(c)
    o_ref[...] = jnp.dot(s, w_ref[...], precision=_HI, preferred_element_type=jnp.float32) + b_ref[...]


def _ada(c, w_ada, b_ada):
    B, D = c.shape
    N = w_ada.shape[1]
    tn = ADA_COL_TILE
    return pl.pallas_call(
        _ada_kernel,
        grid=(N // tn,),
        in_specs=[pl.BlockSpec((B, D), lambda j: (0, 0)),
                  pl.BlockSpec((D, tn), lambda j: (0, j)),
                  pl.BlockSpec((1, tn), lambda j: (0, j))],
        out_specs=pl.BlockSpec((B, tn), lambda j: (0, j)),
        out_shape=jax.ShapeDtypeStruct((B, N), jnp.float32),
        compiler_params=_params(("parallel",)),
        name="ada",
    )(c, w_ada, b_ada.reshape(1, N))


def _rope_tab_kernel(pos_ref, inv_ref, c_ref, s_ref):
    lane = lax.broadcasted_iota(jnp.int32, (LANES, LANES), 1)
    lower = lane < LANES // 2
    rot = lane % (LANES // 2) < ROT_DIM
    reps = LANES // ROT_HALF
    for r in range(pos_ref.shape[0]):
        ang = inv_ref[...] * pos_ref[r:r + 1, :].astype(jnp.float32)
        cos_t = jnp.tile(jnp.cos(ang), (reps, 1)).T
        sin_t = jnp.tile(jnp.sin(ang), (reps, 1)).T
        rows = slice(r * LANES, (r + 1) * LANES)
        c_ref[rows, :] = jnp.where(rot, cos_t, 1.0)
        s_ref[rows, :] = jnp.where(rot, jnp.where(lower, -sin_t, sin_t), 0.0)


def _rope_tables(positions):
    T = positions.size
    tm = min(T, ROPE_ROW_TILE)
    inv = ROPE_THETA ** (-(np.arange(ROT_HALF, dtype=np.float64) * 2.0 / ROT_DIM))
    inv = np.broadcast_to(inv.astype(np.float32)[:, None], (ROT_HALF, LANES))
    tab = pl.BlockSpec((tm, LANES), lambda i: (i, 0))
    sds = jax.ShapeDtypeStruct((T, LANES), jnp.float32)
    return pl.pallas_call(
        _rope_tab_kernel,
        grid=(T // tm,),
        in_specs=[pl.BlockSpec((tm // LANES, LANES), lambda i: (i, 0)),
                  pl.BlockSpec((ROT_HALF, LANES), lambda i: (0, 0))],
        out_specs=[tab, tab],
        out_shape=[sds, sds],
        compiler_params=_params(("parallel",)),
        name="rope_tables",
    )(positions.reshape(T // LANES, LANES), jnp.asarray(inv))


def _permute_qk_columns(w):
    D, n = w.shape
    w = w.reshape(D, n // V_DIM, 2, 4, 2, ROT_HALF)
    return w.transpose(0, 1, 4, 3, 2, 5).reshape(D, n)


def _in_proj_kernel(x_ref, g_ref, sc_ref, sh_ref, w_ref, c_ref, s_ref, z_ref):
    h = (_rms(x_ref[...], NORM_EPS) * g_ref[...] * (1.0 + sc_ref[0]) + sh_ref[0]).astype(jnp.bfloat16)
    c, s = c_ref[...], s_ref[...]
    q_scale = HEAD_DIM ** -0.5 * math.log2(math.e)
    cq, sq = c * q_scale, s * q_scale
    for j in range(IN_COLS // COL_TILE):
        cols = slice(j * COL_TILE, (j + 1) * COL_TILE)
        z = jnp.dot(h, w_ref[:, cols], preferred_element_type=jnp.float32)
        if j < COL_V:
            cj, sj = (cq, sq) if j < COL_K else (c, s)
            parts = []
            for g in range(COL_TILE // LANES):
                zg = z[:, g * LANES:(g + 1) * LANES]
                parts.append(zg * cj + pltpu.roll(zg, LANES // 2, 1) * sj)
            z = jnp.concatenate(parts, axis=1)
        elif j >= COL_G:
            z = 0.5 * jnp.tanh(0.5 * z) + 0.5
        z_ref[:, cols] = z.astype(z_ref.dtype)


def _in_proj(x2, g_pre, mod3, w_in_bf, tabs, seq):
    T, D = x2.shape
    tm = min(seq, IN_ROW_TILE)
    per_b = seq // tm
    mod_spec = lambda col: pl.BlockSpec((1, 1, D), lambda i: (i // per_b, 0, col))
    tab = pl.BlockSpec((tm, LANES), lambda i: (i, 0))
    return pl.pallas_call(
        _in_proj_kernel,
        grid=(T // tm,),
        in_specs=[pl.BlockSpec((tm, D), lambda i: (i, 0)),
                  pl.BlockSpec((1, D), lambda i: (0, 0)),
                  mod_spec(1), mod_spec(0),
                  pl.BlockSpec((D, IN_COLS), lambda i: (0, 0)),
                  tab, tab],
        out_specs=pl.BlockSpec((tm, IN_COLS), lambda i: (i, 0)),
        out_shape=jax.ShapeDtypeStruct((T, IN_COLS), jnp.bfloat16),
        compiler_params=_params(("parallel",)),
        name="in_proj",
    )(x2, g_pre, mod3, mod3, w_in_bf, *tabs)


def _attn_kernel(lq1_ref, lk1_ref, lq2_ref, lk2_ref, gs_ref, q_ref, k_ref, v_ref, o_ref, vt_ref, *acc_refs,
                 lam_init):
    i = pl.program_id(2)
    blk = q_ref.shape[0]
    nblk = v_ref.shape[0] // blk
    chains = [(h, comp) for h in range(ATT_HEADS) for comp in range(2)]

    @pl.when(i == 0)
    def _():
        for c in range(nblk):
            for h in range(ATT_HEADS):
                v = v_ref[c * blk:(c + 1) * blk, h * V_DIM:(h + 1) * V_DIM]
                vt_ref[c, h] = v.astype(jnp.float32).T.astype(vt_ref.dtype)

    lam = (jnp.exp(jnp.sum(lq1_ref[...] * lk1_ref[...], axis=-1, keepdims=True))
           - jnp.exp(jnp.sum(lq2_ref[...] * lk2_ref[...], axis=-1, keepdims=True))
           + lam_init)
    lane_comp = lax.broadcasted_iota(jnp.int32, (blk, V_DIM), 1) // ROT_HALF % 2
    qs = []
    for h, comp in chains:
        q = q_ref[:, h * V_DIM:(h + 1) * V_DIM]
        qs.append(jnp.where(lane_comp == comp, q, jnp.zeros_like(q)))
    nt = (((1,), (1,)), ((), ()))
    for acc_ref in acc_refs:
        acc_ref[...] = jnp.zeros_like(acc_ref)

    def step(c, carry, mask=None):
        off = pl.multiple_of(c * blk, blk)
        scores = []
        for n, (h, comp) in enumerate(chains):
            k = k_ref[pl.ds(off, blk), h * V_DIM:(h + 1) * V_DIM]
            scores.append(lax.dot_general(k, qs[n], nt, preferred_element_type=jnp.float32))
        out, probs, alphas = [], [], []
        for n, s in enumerate(scores):
            m, l = carry[n]
            if mask is not None:
                s = jnp.where(mask, s, NEG_BIG)
            m_new = jnp.maximum(m, jnp.max(s, axis=0, keepdims=True))
            alpha = jnp.exp2(m - m_new)
            p = jnp.exp2(s - m_new)
            out.append((m_new, alpha * l + jnp.sum(p, axis=0, keepdims=True)))
            probs.append(p.astype(vt_ref.dtype))
            alphas.append(alpha)
        for n, (h, comp) in enumerate(chains):
            pv = jnp.dot(vt_ref[c, h], probs[n], preferred_element_type=jnp.float32)
            acc_refs[n][...] = alphas[n] * acc_refs[n][...] + pv
        return tuple(out)

    init = tuple((jnp.full((1, blk), NEG_BIG, jnp.float32), jnp.zeros((1, blk), jnp.float32))
                 for _ in chains)
    carry = lax.fori_loop(0, i, step, init)

    key = lax.broadcasted_iota(jnp.int32, (blk, blk), 0)
    qry = lax.broadcasted_iota(jnp.int32, (blk, blk), 1)
    carry = step(i, carry, key <= qry)
    for h in range(ATT_HEADS):
        (_, l1), (_, l2) = carry[2 * h], carry[2 * h + 1]
        ot = acc_refs[2 * h][...] / l1 - lam * (acc_refs[2 * h + 1][...] / l2)
        o = _rms(ot.T, SUBLN_EPS) * gs_ref[...] * (1.0 - lam_init)
        o_ref[:, h * V_DIM:(h + 1) * V_DIM] = o.astype(o_ref.dtype)


def _diff_attention(z, lams, g_subln, lam_init, batch, seq):
    T = z.shape[0]
    blk = min(ATT_BLOCK, seq)
    nq = seq // blk
    width = ATT_HEADS * V_DIM
    vec = pl.BlockSpec((1, HEAD_DIM), lambda b, h, i: (0, 0))
    kcol = DIFF_WIDTH // width
    return pl.pallas_call(
        functools.partial(_attn_kernel, lam_init=lam_init),
        grid=(batch, N_HEADS // ATT_HEADS, nq),
        in_specs=[vec, vec, vec, vec,
                  pl.BlockSpec((1, V_DIM), lambda b, h, i: (0, 0)),
                  pl.BlockSpec((blk, width), lambda b, h, i: (b * nq + i, h)),
                  pl.BlockSpec((seq, width), lambda b, h, i: (b, kcol + h)),
                  pl.BlockSpec((seq, width), lambda b, h, i: (b, 2 * kcol + h))],
        out_specs=pl.BlockSpec((blk, width), lambda b, h, i: (b * nq + i, h)),
        out_shape=jax.ShapeDtypeStruct((T, DIFF_WIDTH), jnp.bfloat16),
        scratch_shapes=[pltpu.VMEM((nq, ATT_HEADS, V_DIM, blk), jnp.bfloat16)]
        + [pltpu.VMEM((V_DIM, blk), jnp.float32)] * (2 * ATT_HEADS),
        compiler_params=_params(("parallel", "parallel", "arbitrary")),
        name="diff_attn",
    )(*lams, g_subln, z, z, z)


def _pool_kernel(u_ref, w_ref, ps_ref, o_ref):
    t = lax.broadcasted_iota(jnp.int32, (u_ref.shape[0], POOL_GROUP_DIM), 0)

    def shifted(x, k):
        return jnp.where(t >= k, pltpu.roll(x, k, 0), 0.0)

    for g, window in enumerate(POOL_WINDOWS):
        cols = slice(g * POOL_GROUP_DIM, (g + 1) * POOL_GROUP_DIM)
        u = u_ref[:, cols].astype(jnp.float32)
        s, k = u, 1
        while k < window:
            s = s + shifted(s, k)
            k *= 2
        d = s / jnp.minimum(t + 1, window).astype(jnp.float32) - u
        y = jnp.dot(d.astype(jnp.bfloat16), w_ref[g].astype(jnp.bfloat16), preferred_element_type=jnp.float32)
        o_ref[:, cols] = (y * ps_ref[:, cols]).astype(o_ref.dtype)


def _pool(z, w_pool, pool_scale, batch, seq):
    T = z.shape[0]
    G = len(POOL_WINDOWS)
    ucol = 3 * DIFF_WIDTH // POOL_WIDTH
    return pl.pallas_call(
        _pool_kernel,
        grid=(batch,),
        in_specs=[pl.BlockSpec((seq, POOL_WIDTH), lambda b: (b, ucol)),
                  pl.BlockSpec((G, POOL_GROUP_DIM, POOL_GROUP_DIM), lambda b: (0, 0, 0)),
                  pl.BlockSpec((1, POOL_WIDTH), lambda b: (0, 0))],
        out_specs=pl.BlockSpec((seq, POOL_WIDTH), lambda b: (b, 0)),
        out_shape=jax.ShapeDtypeStruct((T, POOL_WIDTH), jnp.bfloat16),
        compiler_params=_params(("parallel",)),
        name="pool",
    )(z, w_pool, pool_scale)


def _mix_kernel(x_ref, o_ref, p_ref, ga0, ga1, gb0, gb1, wa_ref, wb_ref, wo_ref, gpm_ref, gt1_ref,
                gpf_ref, sc2_ref, sh2_ref, wr_ref, br_ref,
                x1_ref, h2_ref, ls_ref, tw_ref, cnt_ref):
    tiles = range(x_ref.shape[0] // TOKEN_TILE)
    h2s = [_mix_front(t, x_ref, o_ref, p_ref, ga0, ga1, gb0, gb1, wa_ref, wb_ref, wo_ref, gpm_ref, gt1_ref,
                      gpf_ref, sc2_ref, sh2_ref, x1_ref) for t in tiles]
    for t in tiles:
        h2_ref[t * TOKEN_TILE:(t + 1) * TOKEN_TILE, :] = h2s[t].astype(h2_ref.dtype)
    for t in tiles:
        _mix_route(t, h2s[t], wr_ref, br_ref, ls_ref, tw_ref, cnt_ref)


def _mix_front(t, x_ref, o_ref, p_ref, ga0, ga1, gb0, gb1, wa_ref, wb_ref, wo_ref, gpm_ref, gt1_ref,
               gpf_ref, sc2_ref, sh2_ref, x1_ref):
    rows = slice(t * TOKEN_TILE, (t + 1) * TOKEN_TILE)
    ya = jnp.dot(o_ref[rows, :], wa_ref[...], preferred_element_type=jnp.float32)
    yb = jnp.dot(p_ref[rows, :], wb_ref[...], preferred_element_type=jnp.float32)
    ga = jnp.concatenate([ga0[rows, :], ga1[rows, :]], axis=1).astype(jnp.float32)
    gb = jnp.concatenate([gb0[rows, :], gb1[rows, :]], axis=1).astype(jnp.float32)
    merged = (ga * ya + gb * yb).astype(jnp.bfloat16)
    mixed = jnp.dot(merged, wo_ref[...], preferred_element_type=jnp.float32)
    x1 = x_ref[rows, :] + gt1_ref[0] * (_rms(mixed, NORM_EPS) * gpm_ref[...])
    x1_ref[rows, :] = x1
    return _rms(x1, NORM_EPS) * gpf_ref[...] * (1.0 + sc2_ref[0]) + sh2_ref[0]


def _mix_route(t, h2, wr_ref, br_ref, ls_ref, tw_ref, cnt_ref):
    tm = TOKEN_TILE
    rows = slice(t * tm, (t + 1) * tm)
    h_hi = h2.astype(jnp.bfloat16)
    h_lo = (h2 - h_hi.astype(jnp.float32)).astype(jnp.bfloat16)
    nt = (((1,), (1,)), ((), ()))
    logits = (lax.dot_general(wr_ref[0], h_hi, nt, preferred_element_type=jnp.float32)
              + lax.dot_general(wr_ref[0], h_lo, nt, preferred_element_type=jnp.float32)
              + lax.dot_general(wr_ref[1], h_hi, nt, preferred_element_type=jnp.float32)
              + br_ref[...])
    eid = lax.broadcasted_iota(jnp.int32, logits.shape, 0)
    work = logits
    sels, vals = [], []
    for _ in range(TOP_K):
        mx = jnp.max(work, axis=0, keepdims=True)
        idx = jnp.min(jnp.where(work == mx, eid, N_EXPERTS), axis=0, keepdims=True)
        sel = eid == idx
        work = jnp.where(sel, -jnp.inf, work)
        sels.append(sel)
        vals.append(mx)
    ex = [jnp.exp(v - vals[0]) for v in vals]
    den = ex[0] + ex[1] + ex[2] + ex[3]
    onehot = jnp.zeros(logits.shape, jnp.float32)
    for sel in sels:
        onehot = jnp.where(sel, 1.0, onehot)
    r = lax.broadcasted_iota(jnp.int32, (tm, tm), 0)
    c = lax.broadcasted_iota(jnp.int32, (tm, tm), 1)
    tri = jnp.where(r < c, 1.0, 0.0).astype(jnp.bfloat16)
    rank = jnp.dot(onehot.astype(jnp.bfloat16), tri, preferred_element_type=jnp.float32)
    counts = jnp.broadcast_to(jnp.sum(onehot, axis=1, keepdims=True), (N_EXPERTS, LANES))
    rounded = jnp.floor((counts + (ROW_ALIGN - 1)) * (1.0 / ROW_ALIGN)) * ROW_ALIGN
    er = lax.broadcasted_iota(jnp.int32, (N_EXPERTS, N_EXPERTS), 0)
    ec = lax.broadcasted_iota(jnp.int32, (N_EXPERTS, N_EXPERTS), 1)
    below = jnp.where(ec < er, 1.0, 0.0).astype(jnp.bfloat16)
    offset = jnp.dot(below, rounded.astype(jnp.bfloat16), preferred_element_type=jnp.float32)
    slot_of = rank + offset[:, 0:1]
    cnt_ref[t] = rounded
    for kk in range(TOP_K):
        ls = jnp.sum(jnp.where(sels[kk], slot_of, 0.0), axis=0, keepdims=True).astype(jnp.int32)
        ls_ref[kk:kk + 1, rows] = ls
        tw_ref[kk:kk + 1, rows] = ex[kk] / den


def _mix(x2, o, p, z, wa, wb, wo, g_post_mix, mod3, g_pre_ffn, w_router_t, b_router, seq):
    T, D = x2.shape
    sub = min(MIX_TILES, seq // TOKEN_TILE)
    tm = sub * TOKEN_TILE
    per_b = seq // tm
    E = N_EXPERTS
    row = lambda n: pl.BlockSpec((1, n), lambda i: (0, 0))
    mod_spec = lambda col: pl.BlockSpec((1, 1, D), lambda i: (i // per_b, 0, col))
    gate = lambda cb: pl.BlockSpec((tm, COL_TILE), lambda i: (i, cb))
    full = lambda a: pl.BlockSpec(a.shape, lambda i: (0,) * a.ndim)
    tile = pl.BlockSpec((tm, D), lambda i: (i, 0))
    small = pl.BlockSpec((TOP_K, tm), lambda i: (0, i))
    return pl.pallas_call(
        _mix_kernel,
        grid=(T // tm,),
        in_specs=[tile, tile,
                  pl.BlockSpec((tm, POOL_WIDTH), lambda i: (i, 0)),
                  gate(COL_G), gate(COL_G + 1), gate(COL_G + 2), gate(COL_G + 3),
                  full(wa), full(wb), full(wo), row(D), mod_spec(2),
                  row(D), mod_spec(4), mod_spec(3), full(w_router_t),
                  pl.BlockSpec((E, 1), lambda i: (0, 0))],
        out_specs=[tile, tile, small, small,
                   pl.BlockSpec((sub, E, LANES), lambda i: (i, 0, 0))],
        out_shape=[jax.ShapeDtypeStruct((T, D), jnp.float32),
                   jax.ShapeDtypeStruct((T, D), jnp.bfloat16),
                   jax.ShapeDtypeStruct((TOP_K, T), jnp.int32),
                   jax.ShapeDtypeStruct((TOP_K, T), jnp.float32),
                   jax.ShapeDtypeStruct((T // TOKEN_TILE, E, LANES), jnp.float32)],
        compiler_params=_params(("parallel",)),
        name="mix_tail",
    )(x2, o, p, z, z, z, z, wa, wb, wo, g_post_mix, mod3, g_pre_ffn, mod3, mod3, w_router_t,
      b_router.reshape(E, 1))


def _chunk(ref, row):
    return ref.at[pl.ds(pl.multiple_of(row, ROW_ALIGN), ROW_ALIGN), :]


def _for_chunks(n, fn):
    def body(g, _):
        for u in range(CHUNK_UNROLL):
            fn(g * CHUNK_UNROLL + u)
        return 0
    lax.fori_loop(0, n // CHUNK_UNROLL, body, 0)
    lax.fori_loop(n // CHUNK_UNROLL * CHUNK_UNROLL, n, lambda k, _: (fn(k), 0)[1], 0)


def _dispatch_kernel(chunk_slot_ref, n_chunk_ref, zero_off_ref, zero_len_ref, ntile_ref,
                     h_ref, ls_ref, xs_ref, xl_ref, zero_ref, sem, zsem, *, zero_per_step):
    i = pl.program_id(0)
    last = pl.num_programs(0) - 1

    @pl.when(i == 0)
    def _():
        zero_ref[...] = jnp.zeros_like(zero_ref)

    slot_id = lax.broadcasted_iota(jnp.int32, (LOCAL_ROWS, h_ref.shape[0]), 0)
    place = jnp.zeros(slot_id.shape, jnp.float32)
    for kk in range(TOP_K):
        place = jnp.where(slot_id == ls_ref[kk:kk + 1, :], 1.0, place)
    xl = jnp.dot(place.astype(jnp.bfloat16), h_ref[...], preferred_element_type=jnp.float32)
    xl_ref[i % 2] = xl.astype(xl_ref.dtype)

    def rows_out(tile, act):
        buf = tile % 2

        def copy(k):
            act(pltpu.make_async_copy(_chunk(xl_ref.at[buf], k * ROW_ALIGN),
                                      _chunk(xs_ref, chunk_slot_ref[tile * BLOCK_CHUNKS + k]), sem.at[buf]))

        _for_chunks(n_chunk_ref[tile], copy)

    def copies(act):
        for r in range(zero_per_step):
            idx = i * zero_per_step + r

            def zbody(k, _, idx=idx):
                act(pltpu.make_async_copy(_chunk(zero_ref, 0),
                                          _chunk(xs_ref, zero_off_ref[idx] + k * ROW_ALIGN), zsem))
                return 0
            lax.fori_loop(0, zero_len_ref[idx] // ROW_ALIGN, zbody, 0)

    def tails(act):
        def body(t, _):
            off = pl.multiple_of(t * MOE_TILE, MOE_TILE)
            act(pltpu.make_async_copy(zero_ref, xs_ref.at[pl.ds(off, MOE_TILE), :], zsem))
            return 0
        lax.fori_loop(ntile_ref[0], xs_ref.shape[0] // MOE_TILE, body, 0)

    start, wait = (lambda cp: cp.start()), (lambda cp: cp.wait())
    rows_out(i, start)
    copies(start)
    pl.when(i == 0)(functools.partial(tails, start))
    pl.when(i > 0)(lambda: rows_out(i - 1, wait))
    copies(wait)
    pl.when(i == 0)(functools.partial(tails, wait))
    pl.when(i == last)(lambda: rows_out(i, wait))


def _dispatch(chunk_slot, n_chunks, zero_off, zero_len, n_tiles, h2, ls, n_slots):
    T, W = h2.shape
    tc = TOKEN_TILE
    steps = n_chunks.shape[0]
    return pl.pallas_call(
        functools.partial(_dispatch_kernel, zero_per_step=zero_len.shape[0] // steps),
        grid_spec=pltpu.PrefetchScalarGridSpec(
            num_scalar_prefetch=5,
            grid=(steps,),
            in_specs=[pl.BlockSpec((tc, W), lambda i, *_: (i, 0)),
                      pl.BlockSpec((TOP_K, tc), lambda i, *_: (0, i))],
            out_specs=pl.BlockSpec(memory_space=pl.ANY),
            scratch_shapes=[pltpu.VMEM((2, LOCAL_ROWS, W), h2.dtype),
                            pltpu.VMEM((MOE_TILE, W), h2.dtype),
                            pltpu.SemaphoreType.DMA((2,)), pltpu.SemaphoreType.DMA]),
        out_shape=jax.ShapeDtypeStruct((n_slots, W), h2.dtype),
        compiler_params=_params(("arbitrary",)),
        name="dispatch",
    )(chunk_slot, n_chunks, zero_off, zero_len, n_tiles, h2, ls)


def _moe_kernel(te_ref, nt_ref, buf_ref, nxt_ref, rows_ref, x_ref, w1_hbm, b1_ref, w2_hbm, b2_ref, perm_ref, y_ref,
                w1raw_ref, w2raw_ref, w1s_ref, w2s_ref, sem):
    j = pl.program_id(0)
    prev = te_ref[jnp.maximum(j - 1, 0)]
    fresh = (j == 0) | (te_ref[j] != prev)
    nblk = w1s_ref.shape[1] // MXU_DIM

    def weight_copies(e, b):
        return (pltpu.make_async_copy(w1_hbm.at[e], w1raw_ref.at[b], sem.at[0, b]),
                pltpu.make_async_copy(w2_hbm.at[e], w2raw_ref.at[b], sem.at[1, b]))

    @pl.when(fresh & (j < nt_ref[0]))
    def _():
        e, b, nxt = te_ref[j], buf_ref[j], nxt_ref[j]

        @pl.when(j == 0)
        def _():
            for cp in weight_copies(e, b):
                cp.start()

        for cp in weight_copies(e, b):
            cp.wait()

        @pl.when(nxt >= 0)
        def _():
            for cp in weight_copies(nxt, 1 - b):
                cp.start()

        for c in range(nblk):
            cols = slice(c * MXU_DIM, (c + 1) * MXU_DIM)
            blk = w1raw_ref[b, :, cols].astype(jnp.bfloat16)
            w1s_ref[:, cols] = jnp.dot(blk, perm_ref[...],
                                       preferred_element_type=jnp.float32).astype(jnp.bfloat16)
        w2s_ref[...] = w2raw_ref[b].astype(jnp.bfloat16)

    def experts_mlp(rows):
        z = jnp.dot(x_ref[:rows, :], w1s_ref[...], preferred_element_type=jnp.float32) + b1_ref[0]
        acts = []
        for c in range(nblk):
            gate = jnp.minimum(z[:, c * MXU_DIM:c * MXU_DIM + LANES], SWIGLU_LIMIT)
            up = jnp.clip(z[:, c * MXU_DIM + LANES:(c + 1) * MXU_DIM], -SWIGLU_LIMIT, SWIGLU_LIMIT)
            acts.append(gate * jax.nn.sigmoid(SWIGLU_ALPHA * gate) * (up + 1.0))
        a = jnp.concatenate(acts, axis=1).astype(jnp.bfloat16)
        y = jnp.dot(a, w2s_ref[...], preferred_element_type=jnp.float32) + b2_ref[0]
        y_ref[:rows, :] = y.astype(y_ref.dtype)
        if rows < y_ref.shape[0]:
            y_ref[rows:, :] = jnp.zeros((y_ref.shape[0] - rows, y_ref.shape[1]), y_ref.dtype)

    half = y_ref.shape[0] // 2
    active = j < nt_ref[0]
    pl.when(active & (rows_ref[j] > half))(lambda: experts_mlp(2 * half))
    pl.when(active & (rows_ref[j] <= half))(lambda: experts_mlp(half))


def _regroup_perm():
    src = np.arange(MXU_DIM)
    dst = np.where(src % 2 == 0, src // 2, LANES + src // 2)
    perm = np.zeros((MXU_DIM, MXU_DIM), np.float32)
    perm[src, dst] = 1.0
    return jnp.asarray(perm, jnp.bfloat16)


def _moe(tile_expert, n_tiles, tile_buf, tile_next, tile_rows, xs, w1, b1g, w2, b2):
    n_slots, W = xs.shape
    tm = MOE_TILE
    E, D, F2 = w1.shape
    F = w2.shape[1]
    xmap = lambda j, te, nt, *_: (jnp.minimum(j, nt[0] - 1), 0)
    emap = lambda j, te, *_: (te[j], 0, 0)
    return pl.pallas_call(
        _moe_kernel,
        grid_spec=pltpu.PrefetchScalarGridSpec(
            num_scalar_prefetch=5,
            grid=(n_slots // tm,),
            in_specs=[pl.BlockSpec((tm, W), xmap),
                      pl.BlockSpec(memory_space=pl.ANY),
                      pl.BlockSpec((1, 1, F2), emap),
                      pl.BlockSpec(memory_space=pl.ANY),
                      pl.BlockSpec((1, 1, D), emap),
                      pl.BlockSpec((MXU_DIM, MXU_DIM), lambda j, *_: (0, 0))],
            out_specs=pl.BlockSpec((tm, W), xmap),
            scratch_shapes=[pltpu.VMEM((2, D, F2), w1.dtype), pltpu.VMEM((2, F, D), w2.dtype),
                            pltpu.VMEM((D, F2), jnp.bfloat16), pltpu.VMEM((F, D), jnp.bfloat16),
                            pltpu.SemaphoreType.DMA((2, 2))]),
        out_shape=jax.ShapeDtypeStruct((n_slots, W), xs.dtype),
        input_output_aliases={5: 0},
        compiler_params=_params(("arbitrary",)),
        name="moe_experts",
    )(tile_expert, n_tiles, tile_buf, tile_next, tile_rows, xs, w1, b1g, w2, b2.reshape(E, 1, D),
      _regroup_perm())


def _combine_kernel(chunk_slot_ref, n_chunk_ref, ys_ref, ls_ref, w_ref, x1_ref, gt2_ref, g_ref,
                    o_ref, yl_ref, sem):
    i = pl.program_id(0)
    tc = x1_ref.shape[0]

    def pieces(tile, act):
        buf = tile % 2

        def copy(k):
            act(pltpu.make_async_copy(_chunk(ys_ref, chunk_slot_ref[tile * BLOCK_CHUNKS + k]),
                                      _chunk(yl_ref.at[buf], k * ROW_ALIGN), sem.at[buf]))

        _for_chunks(n_chunk_ref[tile], copy)

    def fetch(tile):
        yl_ref[tile % 2, TOP_K * tc:, :] = jnp.zeros((LOCAL_ROWS - TOP_K * tc, yl_ref.shape[2]), yl_ref.dtype)
        pieces(tile, lambda cp: cp.start())

    pl.when(i == 0)(lambda: fetch(i))
    pl.when(i + 1 < pl.num_programs(0))(lambda: fetch(i + 1))
    slot_id = lax.broadcasted_iota(jnp.int32, (tc, LOCAL_ROWS), 1)
    ls = ls_ref[...]
    w = w_ref[...]
    mix = jnp.zeros((tc, LOCAL_ROWS), jnp.float32)
    for k in range(TOP_K):
        mix = jnp.where(slot_id == ls[:, k:k + 1], w[:, k:k + 1], mix)
    pieces(i, lambda cp: cp.wait())
    f = jnp.dot(mix.astype(jnp.bfloat16), yl_ref[i % 2], preferred_element_type=jnp.float32)
    o_ref[...] = x1_ref[...] + gt2_ref[0] * (_rms(f, NORM_EPS) * g_ref[...])


def _combine(chunk_slot, n_chunks, ys, ls_t, topw_t, x1, mod3, g_post_ffn, seq):
    T, D = x1.shape
    tc = TOKEN_TILE
    per_b = seq // tc
    tile = pl.BlockSpec((tc, D), lambda i, *_: (i, 0))
    small = pl.BlockSpec((tc, TOP_K), lambda i, *_: (i, 0))
    return pl.pallas_call(
        _combine_kernel,
        grid_spec=pltpu.PrefetchScalarGridSpec(
            num_scalar_prefetch=2,
            grid=(T // tc,),
            in_specs=[pl.BlockSpec(memory_space=pl.ANY), small, small, tile,
                      pl.BlockSpec((1, 1, D), lambda i, *_: (i // per_b, 0, 5)),
                      pl.BlockSpec((1, D), lambda i, *_: (0, 0))],
            out_specs=tile,
            scratch_shapes=[pltpu.VMEM((2, LOCAL_ROWS, ys.shape[1]), ys.dtype),
                            pltpu.SemaphoreType.DMA((2,))]),
        out_shape=jax.ShapeDtypeStruct((T, D), jnp.float32),
        compiler_params=_params(("arbitrary",)),
        name="combine",
    )(chunk_slot, n_chunks, ys, ls_t, topw_t, x1, mod3, g_post_ffn)


def kernel(x, c, positions, w_ada, b_ada, g_pre_mix, w_in, lambda_q1, lambda_k1, lambda_q2, lambda_k2,
           g_subln, w_pool, pool_scale, w_proj_a, w_proj_b, w_out, g_post_mix, g_pre_ffn,
           w_router, b_router, w_exp1, b_exp1, w_exp2, b_exp2, g_post_ffn):
    B, S, D = x.shape
    T = B * S
    bf = jnp.bfloat16
    tabs = _rope_tables(positions)
    for l in range(w_ada.shape[0]):
        x2 = x.reshape(T, D)
        mod3 = _ada(c, w_ada[l], b_ada[l]).reshape(B, 1, N_MOD * D)
        n_qk = 2 * DIFF_WIDTH
        w_in_bf = jnp.concatenate([_permute_qk_columns(w_in[l][:, :n_qk]), w_in[l][:, n_qk:]], axis=1).astype(bf)
        z = _in_proj(x2, g_pre_mix[l][None], mod3, w_in_bf, tabs, S)
        lams = [v[l][None] for v in (lambda_q1, lambda_k1, lambda_q2, lambda_k2)]
        lam_init = 0.8 - 0.6 * math.exp(-0.3 * l)
        o = _diff_attention(z, lams, g_subln[l][None], lam_init, B, S)
        p = _pool(z, w_pool[l], pool_scale[l][None], B, S)
        x1, h2, ls, topw, cnt = _mix(
            x2, o, p, z, w_proj_a[l].astype(bf), w_proj_b[l].astype(bf), w_out[l].astype(bf),
            g_post_mix[l][None], mod3, g_pre_ffn[l][None], _split_bf16(w_router[l].T), b_router[l], S)

        i32 = jnp.int32
        n = cnt[:, :, 0].astype(i32)
        n_tok_tiles = n.shape[0]
        counts = jnp.sum(n, axis=0)
        padded = (counts + MOE_TILE - 1) // MOE_TILE * MOE_TILE
        gend = jnp.cumsum(padded).astype(i32)
        gstart = gend - padded
        local_end = jnp.cumsum(n, axis=1)
        slot_off = gstart[None, :] + jnp.cumsum(n, axis=0) - n
        chunk_row = jnp.arange(BLOCK_CHUNKS, dtype=i32) * ROW_ALIGN
        piece = jnp.sum(local_end[:, None, :] <= chunk_row[None, :, None], axis=2)
        in_piece = piece[:, :, None] == jnp.arange(N_EXPERTS, dtype=i32)[None, None, :]
        shift = jnp.sum(jnp.where(in_piece, (slot_off - (local_end - n))[:, None, :], 0), axis=2)
        chunk_slot = (shift + chunk_row[None, :]).reshape(-1).astype(i32)
        n_chunks = (local_end[:, -1] // ROW_ALIGN).astype(i32)
        steps_pad = -N_EXPERTS % n_tok_tiles
        group_pad_off = jnp.pad(gstart + counts, (0, steps_pad)).astype(i32)
        group_pad_len = jnp.pad(padded - counts, (0, steps_pad)).astype(i32)
        n_slots = -(-(T * TOP_K + n_tok_tiles * N_EXPERTS * ROW_ALIGN) // MOE_TILE) * MOE_TILE + N_EXPERTS * MOE_TILE
        tile_start = jnp.arange(n_slots // MOE_TILE, dtype=i32) * MOE_TILE
        tile_expert = jnp.minimum(jnp.sum(tile_start[:, None] >= gend[None, :], axis=1), N_EXPERTS - 1)
        n_tiles = gend[-1:] // MOE_TILE
        has_tiles = padded > 0
        expert_ids = jnp.arange(N_EXPERTS, dtype=i32)
        later = lax.cummin(jnp.where(has_tiles, expert_ids, N_EXPERTS), reverse=True)
        next_expert = jnp.concatenate([later[1:], jnp.full((1,), N_EXPERTS, i32)])
        next_expert = jnp.where(next_expert < N_EXPERTS, next_expert, -1)
        of_tile = tile_expert[:, None] == expert_ids[None, :]
        pick = lambda per_expert: jnp.sum(jnp.where(of_tile, per_expert[None, :], 0), axis=1).astype(i32)
        tile_buf = pick((jnp.cumsum(has_tiles.astype(i32)) - 1) % 2)
        tile_next = pick(next_expert)
        tile_rows = jnp.clip(pick(gstart + counts) - tile_start, 0, MOE_TILE).astype(i32)

        xs = _dispatch(chunk_slot, n_chunks, group_pad_off, group_pad_len, n_tiles, h2, ls, n_slots)
        half = D_FF
        b1g = b_exp1[l].reshape(N_EXPERTS, half // LANES, LANES, 2).transpose(0, 1, 3, 2)
        b1g = b1g.reshape(N_EXPERTS, 1, 2 * half)
        ys = _moe(tile_expert.astype(i32), n_tiles, tile_buf, tile_next, tile_rows, xs,
                  w_exp1[l], b1g, w_exp2[l], b_exp2[l])
        x = _combine(chunk_slot, n_chunks, ys, ls.T, topw.T, x1, mod3, g_post_ffn[l][None], S)
        x = x.reshape(B, S, D)
    return x
```
